```python
import jax, jax.numpy as jnp
from jax import lax
import numpy as np

D_MODEL = 1024
BATCH = 16
SEQ = 4096
DEPTH = 2

D_CONV = D_MODEL
CONV_WIDTH = 31
D_SGU = D_MODEL
SGU_GROUPS = 8
SGU_GROUP_DIM = D_SGU // SGU_GROUPS
CHUNK = 128
D_FF = ((8 * D_MODEL // 3 + 255) // 256) * 256
D_IN = 2 * D_CONV + 2 * D_SGU + 2 * D_MODEL
EPS = 1e-6

kernel_name = "hybrid_conformer_conv_gmlp_encoder"


def rmsnorm(x, g):
    xf = x.astype(jnp.float32)
    y = xf * lax.rsqrt(jnp.mean(xf * xf, axis=-1, keepdims=True) + EPS)
    return (y * g.astype(jnp.float32)).astype(x.dtype)


def layernorm(x, g, b):
    xf = x.astype(jnp.float32)
    mu = jnp.mean(xf, axis=-1, keepdims=True)
    var = jnp.mean(jnp.square(xf - mu), axis=-1, keepdims=True)
    y = (xf - mu) * lax.rsqrt(var + EPS)
    return (y * g.astype(jnp.float32) + b.astype(jnp.float32)).astype(x.dtype)


def depthwise_conv(x, w, b):
    pad = (CONV_WIDTH - 1) // 2
    y = lax.conv_general_dilated(
        x, w.astype(x.dtype), window_strides=(1,), padding=[(pad, pad)],
        dimension_numbers=("NWC", "WIO", "NWC"), feature_group_count=x.shape[-1])
    return y + b


def conformer_conv_branch(val, gate, conv_w, conv_b, ln_g, ln_b, w_out):
    c = val * jax.nn.sigmoid(gate)
    c = depthwise_conv(c, conv_w, conv_b)
    c = jax.nn.silu(layernorm(c, ln_g, ln_b))
    return c @ w_out


def spatial_gating_branch(u, v, ln_g, ln_b, w_s, b_s, w_out):
    bsz, seq, _ = v.shape
    n_chunks = seq // CHUNK
    vn = layernorm(v, ln_g, ln_b)
    vc = vn.reshape(bsz, n_chunks, CHUNK, SGU_GROUPS, SGU_GROUP_DIM)
    mixed = jnp.einsum("bcpgd,gqp->bcqgd", vc, w_s.astype(vc.dtype))
    mixed = mixed + jnp.transpose(b_s)[None, None, :, :, None]
    gated = u * mixed.reshape(bsz, seq, D_SGU)
    return gated @ w_out


def _fwd_setup_inputs(seed: int = 0) -> dict:
    key = jax.random.key(seed)
    ks = jax.random.split(key, 24)
    f32 = jnp.float32

    def nrm(k, shape, scale):
        return jax.random.normal(k, shape, f32) * scale

    L = DEPTH
    return {
        "x": jax.random.normal(ks[0], (BATCH, SEQ, D_MODEL), f32),
        "norm_mix": 1.0 + nrm(ks[1], (L, D_MODEL), 0.02),
        "w_in": nrm(ks[2], (L, D_MODEL, D_IN), D_MODEL ** -0.5),
        "gate_bias": nrm(ks[3], (L, 2 * D_MODEL), 0.02),
        "conv_w": nrm(ks[4], (L, CONV_WIDTH, 1, D_CONV), CONV_WIDTH ** -0.5),
        "conv_b": nrm(ks[5], (L, D_CONV), 0.02),
        "conv_ln_g": 1.0 + nrm(ks[6], (L, D_CONV), 0.02),
        "conv_ln_b": nrm(ks[7], (L, D_CONV), 0.02),
        "w_conv_out": nrm(ks[8], (L, D_CONV, D_MODEL), D_CONV ** -0.5),
        "sgu_ln_g": 1.0 + nrm(ks[9], (L, D_SGU), 0.02),
        "sgu_ln_b": nrm(ks[10], (L, D_SGU), 0.02),
        "w_spatial": nrm(ks[11], (L, SGU_GROUPS, CHUNK, CHUNK), CHUNK ** -0.5),
        "b_spatial": 1.0 + nrm(ks[12], (L, SGU_GROUPS, CHUNK), 0.02),
        "w_sgu_out": nrm(ks[13], (L, D_SGU, D_MODEL), D_SGU ** -0.5),
        "w_o": nrm(ks[14], (L, D_MODEL, D_MODEL), D_MODEL ** -0.5),
        "norm_ffn": 1.0 + nrm(ks[15], (L, D_MODEL), 0.02),
        "w_ffn_gate": nrm(ks[16], (L, D_MODEL, D_FF), D_MODEL ** -0.5),
        "w_ffn_up": nrm(ks[17], (L, D_MODEL, D_FF), D_MODEL ** -0.5),
        "w_ffn_down": nrm(ks[18], (L, D_FF, D_MODEL), D_FF ** -0.5),
        "norm_final": 1.0 + nrm(ks[19], (D_MODEL,), 0.02),
    }


def _fwd_reference(x, norm_mix, w_in, gate_bias, conv_w, conv_b, conv_ln_g, conv_ln_b,
              w_conv_out, sgu_ln_g, sgu_ln_b, w_spatial, b_spatial, w_sgu_out, w_o,
              norm_ffn, w_ffn_gate, w_ffn_up, w_ffn_down, norm_final):
    split_points = [D_CONV, 2 * D_CONV, 2 * D_CONV + D_SGU, 2 * D_CONV + 2 * D_SGU,
                    2 * D_CONV + 2 * D_SGU + D_MODEL]
    for l in range(DEPTH):
        h = rmsnorm(x, norm_mix[l])
        proj = h @ w_in[l]
        a_val, a_gate, u, v, g_a, g_b = jnp.split(proj, split_points, axis=-1)
        y_a = conformer_conv_branch(a_val, a_gate, conv_w[l], conv_b[l],
                                    conv_ln_g[l], conv_ln_b[l], w_conv_out[l])
        y_b = spatial_gating_branch(u, v, sgu_ln_g[l], sgu_ln_b[l],
                                    w_spatial[l], b_spatial[l], w_sgu_out[l])
        gb_a, gb_b = jnp.split(gate_bias[l], 2)
        merged = jax.nn.sigmoid(g_a + gb_a) * y_a + jax.nn.sigmoid(g_b + gb_b) * y_b
        x = x + merged @ w_o[l]
        h2 = rmsnorm(x, norm_ffn[l])
        x = x + (jax.nn.silu(h2 @ w_ffn_gate[l]) * (h2 @ w_ffn_up[l])) @ w_ffn_down[l]
    return rmsnorm(x, norm_final)


import jax as _jax
import jax.numpy as _jnp

TWIN_FORMAT = 'train_step'
FWD_PARAMS = ['x', 'norm_mix', 'w_in', 'gate_bias', 'conv_w', 'conv_b', 'conv_ln_g', 'conv_ln_b', 'w_conv_out', 'sgu_ln_g', 'sgu_ln_b', 'w_spatial', 'b_spatial', 'w_sgu_out', 'w_o', 'norm_ffn', 'w_ffn_gate', 'w_ffn_up', 'w_ffn_down', 'norm_final']
TWIN_WEIGHTS = ['norm_mix', 'w_in', 'gate_bias', 'conv_w', 'conv_b', 'conv_ln_g', 'conv_ln_b', 'w_conv_out', 'sgu_ln_g', 'sgu_ln_b', 'w_spatial', 'b_spatial', 'w_sgu_out', 'w_o', 'norm_ffn', 'w_ffn_gate', 'w_ffn_up', 'w_ffn_down', 'norm_final']
TWIN_DIFF_INPUT = 'x'
TWIN_INPUTS = ['x', 'norm_mix', 'w_in', 'gate_bias', 'conv_w', 'conv_b', 'conv_ln_g', 'conv_ln_b', 'w_conv_out', 'sgu_ln_g', 'sgu_ln_b', 'w_spatial', 'b_spatial', 'w_sgu_out', 'w_o', 'norm_ffn', 'w_ffn_gate', 'w_ffn_up', 'w_ffn_down', 'norm_final', 'loss_target', 'm_norm_mix', 'm_w_in', 'm_gate_bias', 'm_conv_w', 'm_conv_b', 'm_conv_ln_g', 'm_conv_ln_b', 'm_w_conv_out', 'm_sgu_ln_g', 'm_sgu_ln_b', 'm_w_spatial', 'm_b_spatial', 'm_w_sgu_out', 'm_w_o', 'm_norm_ffn', 'm_w_ffn_gate', 'm_w_ffn_up', 'm_w_ffn_down', 'm_norm_final', 'v_norm_mix', 'v_w_in', 'v_gate_bias', 'v_conv_w', 'v_conv_b', 'v_conv_ln_g', 'v_conv_ln_b', 'v_w_conv_out', 'v_sgu_ln_g', 'v_sgu_ln_b', 'v_w_spatial', 'v_b_spatial', 'v_w_sgu_out', 'v_w_o', 'v_norm_ffn', 'v_w_ffn_gate', 'v_w_ffn_up', 'v_w_ffn_down', 'v_norm_final']
TWIN_OUTPUTS = ['loss', 'grad_x', 'grad_norm_mix', 'grad_w_in', 'grad_gate_bias', 'grad_conv_w', 'grad_conv_b', 'grad_conv_ln_g', 'grad_conv_ln_b', 'grad_w_conv_out', 'grad_sgu_ln_g', 'grad_sgu_ln_b', 'grad_w_spatial', 'grad_b_spatial', 'grad_w_sgu_out', 'grad_w_o', 'grad_norm_ffn', 'grad_w_ffn_gate', 'grad_w_ffn_up', 'grad_w_ffn_down', 'grad_norm_final', 'delta_norm_mix', 'delta_w_in', 'delta_gate_bias', 'delta_conv_w', 'delta_conv_b', 'delta_conv_ln_g', 'delta_conv_ln_b', 'delta_w_conv_out', 'delta_sgu_ln_g', 'delta_sgu_ln_b', 'delta_w_spatial', 'delta_b_spatial', 'delta_w_sgu_out', 'delta_w_o', 'delta_norm_ffn', 'delta_w_ffn_gate', 'delta_w_ffn_up', 'delta_w_ffn_down', 'delta_norm_final', 'new_m_norm_mix', 'new_m_w_in', 'new_m_gate_bias', 'new_m_conv_w', 'new_m_conv_b', 'new_m_conv_ln_g', 'new_m_conv_ln_b', 'new_m_w_conv_out', 'new_m_sgu_ln_g', 'new_m_sgu_ln_b', 'new_m_w_spatial', 'new_m_b_spatial', 'new_m_w_sgu_out', 'new_m_w_o', 'new_m_norm_ffn', 'new_m_w_ffn_gate', 'new_m_w_ffn_up', 'new_m_w_ffn_down', 'new_m_norm_final', 'new_v_norm_mix', 'new_v_w_in', 'new_v_gate_bias', 'new_v_conv_w', 'new_v_conv_b', 'new_v_conv_ln_g', 'new_v_conv_ln_b', 'new_v_w_conv_out', 'new_v_sgu_ln_g', 'new_v_sgu_ln_b', 'new_v_w_spatial', 'new_v_b_spatial', 'new_v_w_sgu_out', 'new_v_w_o', 'new_v_norm_ffn', 'new_v_w_ffn_gate', 'new_v_w_ffn_up', 'new_v_w_ffn_down', 'new_v_norm_final']
TWIN_LEAF_KINDS = {'loss': 'loss', 'grad_x': 'grad_x', 'grad_norm_mix': 'grad_w', 'grad_w_in': 'grad_w', 'grad_gate_bias': 'grad_w', 'grad_conv_w': 'grad_w', 'grad_conv_b': 'grad_w', 'grad_conv_ln_g': 'grad_w', 'grad_conv_ln_b': 'grad_w', 'grad_w_conv_out': 'grad_w', 'grad_sgu_ln_g': 'grad_w', 'grad_sgu_ln_b': 'grad_w', 'grad_w_spatial': 'grad_w', 'grad_b_spatial': 'grad_w', 'grad_w_sgu_out': 'grad_w', 'grad_w_o': 'grad_w', 'grad_norm_ffn': 'grad_w', 'grad_w_ffn_gate': 'grad_w', 'grad_w_ffn_up': 'grad_w', 'grad_w_ffn_down': 'grad_w', 'grad_norm_final': 'grad_w', 'delta_norm_mix': 'delta_w', 'delta_w_in': 'delta_w', 'delta_gate_bias': 'delta_w', 'delta_conv_w': 'delta_w', 'delta_conv_b': 'delta_w', 'delta_conv_ln_g': 'delta_w', 'delta_conv_ln_b': 'delta_w', 'delta_w_conv_out': 'delta_w', 'delta_sgu_ln_g': 'delta_w', 'delta_sgu_ln_b': 'delta_w', 'delta_w_spatial': 'delta_w', 'delta_b_spatial': 'delta_w', 'delta_w_sgu_out': 'delta_w', 'delta_w_o': 'delta_w', 'delta_norm_ffn': 'delta_w', 'delta_w_ffn_gate': 'delta_w', 'delta_w_ffn_up': 'delta_w', 'delta_w_ffn_down': 'delta_w', 'delta_norm_final': 'delta_w', 'new_m_norm_mix': 'new_m', 'new_m_w_in': 'new_m', 'new_m_gate_bias': 'new_m', 'new_m_conv_w': 'new_m', 'new_m_conv_b': 'new_m', 'new_m_conv_ln_g': 'new_m', 'new_m_conv_ln_b': 'new_m', 'new_m_w_conv_out': 'new_m', 'new_m_sgu_ln_g': 'new_m', 'new_m_sgu_ln_b': 'new_m', 'new_m_w_spatial': 'new_m', 'new_m_b_spatial': 'new_m', 'new_m_w_sgu_out': 'new_m', 'new_m_w_o': 'new_m', 'new_m_norm_ffn': 'new_m', 'new_m_w_ffn_gate': 'new_m', 'new_m_w_ffn_up': 'new_m', 'new_m_w_ffn_down': 'new_m', 'new_m_norm_final': 'new_m', 'new_v_norm_mix': 'new_v', 'new_v_w_in': 'new_v', 'new_v_gate_bias': 'new_v', 'new_v_conv_w': 'new_v', 'new_v_conv_b': 'new_v', 'new_v_conv_ln_g': 'new_v', 'new_v_conv_ln_b': 'new_v', 'new_v_w_conv_out': 'new_v', 'new_v_sgu_ln_g': 'new_v', 'new_v_sgu_ln_b': 'new_v', 'new_v_w_spatial': 'new_v', 'new_v_b_spatial': 'new_v', 'new_v_w_sgu_out': 'new_v', 'new_v_w_o': 'new_v', 'new_v_norm_ffn': 'new_v', 'new_v_w_ffn_gate': 'new_v', 'new_v_w_ffn_up': 'new_v', 'new_v_w_ffn_down': 'new_v', 'new_v_norm_final': 'new_v'}


def _forward(args):
    return _fwd_reference(*[args[k] for k in FWD_PARAMS])


def _output_shape():
    out = _jax.eval_shape(lambda: _forward(_fwd_setup_inputs(0)))
    return out.shape, out.dtype

N_MICROBATCH = 1
ADAM_LR = 0.001
ADAM_B1 = 0.9
ADAM_B2 = 0.999
ADAM_EPS = 1e-08
ADAM_WD = 0.01
ADAM_STEP = 10
PER_EXAMPLE_BATCH_AXIS = {'x': 0, 'loss_target': 0}
SHARED_INPUTS = []
_WEIGHT_DTYPES = {'norm_mix': _jnp.float32, 'w_in': _jnp.float32, 'gate_bias': _jnp.float32, 'conv_w': _jnp.float32, 'conv_b': _jnp.float32, 'conv_ln_g': _jnp.float32, 'conv_ln_b': _jnp.float32, 'w_conv_out': _jnp.float32, 'sgu_ln_g': _jnp.float32, 'sgu_ln_b': _jnp.float32, 'w_spatial': _jnp.float32, 'b_spatial': _jnp.float32, 'w_sgu_out': _jnp.float32, 'w_o': _jnp.float32, 'norm_ffn': _jnp.float32, 'w_ffn_gate': _jnp.float32, 'w_ffn_up': _jnp.float32, 'w_ffn_down': _jnp.float32, 'norm_final': _jnp.float32}
MOMENT_SCALE = {'norm_mix': 2.135247e-01, 'w_in': 8.765460e-02, 'gate_bias': 5.034244e-02, 'conv_w': 6.826605e-02, 'conv_b': 1.241673e-01, 'conv_ln_g': 7.725395e-02, 'conv_ln_b': 6.595978e-02, 'w_conv_out': 6.606505e-02, 'sgu_ln_g': 1.069232e-01, 'sgu_ln_b': 1.176815e-01, 'w_spatial': 1.101370e-01, 'b_spatial': 1.128708e-01, 'w_sgu_out': 1.555602e-01, 'w_o': 1.686108e-01, 'norm_ffn': 1.498609e-01, 'w_ffn_gate': 6.433986e-02, 'w_ffn_up': 6.245124e-02, 'w_ffn_down': 1.033806e-01, 'norm_final': 6.383854e+01}


def _to_microbatches(a, axis):
    t = _jnp.moveaxis(a, axis, 0)
    t = t.reshape((N_MICROBATCH, t.shape[0] // N_MICROBATCH) + t.shape[1:])
    return _jnp.moveaxis(t, 1, axis + 1)


def setup_inputs(seed: int = 0) -> dict:
    inp = _fwd_setup_inputs(seed)
    key = _jax.random.fold_in(_jax.random.key(seed), 7919)
    shape, _ = _output_shape()
    out = dict(inp)
    out["loss_target"] = _jax.random.normal(_jax.random.fold_in(key, 0), shape, _jnp.float32)
    for i, name in enumerate(TWIN_WEIGHTS):
        w = inp[name].astype(_jnp.float32)
        if MOMENT_SCALE is None:
            s = _jnp.sqrt(_jnp.mean(_jnp.square(w)) + 1e-30)
        else:
            s = MOMENT_SCALE[name]
        km, kv = _jax.random.split(_jax.random.fold_in(key, i + 1))
        out[name] = w
        out["m_" + name] = s * _jax.random.normal(km, w.shape, _jnp.float32)
        out["v_" + name] = (s * s) * _jax.random.uniform(kv, w.shape, _jnp.float32, 0.5, 1.5)
    if N_MICROBATCH > 1:
        for name, axis in PER_EXAMPLE_BATCH_AXIS.items():
            out[name] = _to_microbatches(out[name], axis)
    return {'x': out['x'], 'norm_mix': out['norm_mix'], 'w_in': out['w_in'], 'gate_bias': out['gate_bias'], 'conv_w': out['conv_w'], 'conv_b': out['conv_b'], 'conv_ln_g': out['conv_ln_g'], 'conv_ln_b': out['conv_ln_b'], 'w_conv_out': out['w_conv_out'], 'sgu_ln_g': out['sgu_ln_g'], 'sgu_ln_b': out['sgu_ln_b'], 'w_spatial': out['w_spatial'], 'b_spatial': out['b_spatial'], 'w_sgu_out': out['w_sgu_out'], 'w_o': out['w_o'], 'norm_ffn': out['norm_ffn'], 'w_ffn_gate': out['w_ffn_gate'], 'w_ffn_up': out['w_ffn_up'], 'w_ffn_down': out['w_ffn_down'], 'norm_final': out['norm_final'], 'loss_target': out['loss_target'], 'm_norm_mix': out['m_norm_mix'], 'm_w_in': out['m_w_in'], 'm_gate_bias': out['m_gate_bias'], 'm_conv_w': out['m_conv_w'], 'm_conv_b': out['m_conv_b'], 'm_conv_ln_g': out['m_conv_ln_g'], 'm_conv_ln_b': out['m_conv_ln_b'], 'm_w_conv_out': out['m_w_conv_out'], 'm_sgu_ln_g': out['m_sgu_ln_g'], 'm_sgu_ln_b': out['m_sgu_ln_b'], 'm_w_spatial': out['m_w_spatial'], 'm_b_spatial': out['m_b_spatial'], 'm_w_sgu_out': out['m_w_sgu_out'], 'm_w_o': out['m_w_o'], 'm_norm_ffn': out['m_norm_ffn'], 'm_w_ffn_gate': out['m_w_ffn_gate'], 'm_w_ffn_up': out['m_w_ffn_up'], 'm_w_ffn_down': out['m_w_ffn_down'], 'm_norm_final': out['m_norm_final'], 'v_norm_mix': out['v_norm_mix'], 'v_w_in': out['v_w_in'], 'v_gate_bias': out['v_gate_bias'], 'v_conv_w': out['v_conv_w'], 'v_conv_b': out['v_conv_b'], 'v_conv_ln_g': out['v_conv_ln_g'], 'v_conv_ln_b': out['v_conv_ln_b'], 'v_w_conv_out': out['v_w_conv_out'], 'v_sgu_ln_g': out['v_sgu_ln_g'], 'v_sgu_ln_b': out['v_sgu_ln_b'], 'v_w_spatial': out['v_w_spatial'], 'v_b_spatial': out['v_b_spatial'], 'v_w_sgu_out': out['v_w_sgu_out'], 'v_w_o': out['v_w_o'], 'v_norm_ffn': out['v_norm_ffn'], 'v_w_ffn_gate': out['v_w_ffn_gate'], 'v_w_ffn_up': out['v_w_ffn_up'], 'v_w_ffn_down': out['v_w_ffn_down'], 'v_norm_final': out['v_norm_final']}


def _loss(weights, diff, rest, loss_target):
    with _jax.named_scope("forward"):
        args = {**rest, TWIN_DIFF_INPUT: diff, **{k: w.astype(_WEIGHT_DTYPES[k]) for k, w in weights.items()}}
        y = _forward(args)
    with _jax.named_scope("loss_head"):
        err = _jnp.square(y.astype(_jnp.float32) - loss_target)
        return 0.5 * _jnp.sum(_jnp.mean(err, axis=-1)) if err.ndim else 0.5 * err


def _adamw(w, g, m, v):
    m = ADAM_B1 * m + (1.0 - ADAM_B1) * g
    v = ADAM_B2 * v + (1.0 - ADAM_B2) * _jnp.square(g)
    m_hat = m / (1.0 - ADAM_B1 ** ADAM_STEP)
    v_hat = v / (1.0 - ADAM_B2 ** ADAM_STEP)
    delta = -ADAM_LR * (m_hat / (_jnp.sqrt(v_hat) + ADAM_EPS) + ADAM_WD * w)
    return delta, m, v


def reference(x, norm_mix, w_in, gate_bias, conv_w, conv_b, conv_ln_g, conv_ln_b, w_conv_out, sgu_ln_g, sgu_ln_b, w_spatial, b_spatial, w_sgu_out, w_o, norm_ffn, w_ffn_gate, w_ffn_up, w_ffn_down, norm_final, loss_target, m_norm_mix, m_w_in, m_gate_bias, m_conv_w, m_conv_b, m_conv_ln_g, m_conv_ln_b, m_w_conv_out, m_sgu_ln_g, m_sgu_ln_b, m_w_spatial, m_b_spatial, m_w_sgu_out, m_w_o, m_norm_ffn, m_w_ffn_gate, m_w_ffn_up, m_w_ffn_down, m_norm_final, v_norm_mix, v_w_in, v_gate_bias, v_conv_w, v_conv_b, v_conv_ln_g, v_conv_ln_b, v_w_conv_out, v_sgu_ln_g, v_sgu_ln_b, v_w_spatial, v_b_spatial, v_w_sgu_out, v_w_o, v_norm_ffn, v_w_ffn_gate, v_w_ffn_up, v_w_ffn_down, v_norm_final):
    given = dict(x=x, norm_mix=norm_mix, w_in=w_in, gate_bias=gate_bias, conv_w=conv_w, conv_b=conv_b, conv_ln_g=conv_ln_g, conv_ln_b=conv_ln_b, w_conv_out=w_conv_out, sgu_ln_g=sgu_ln_g, sgu_ln_b=sgu_ln_b, w_spatial=w_spatial, b_spatial=b_spatial, w_sgu_out=w_sgu_out, w_o=w_o, norm_ffn=norm_ffn, w_ffn_gate=w_ffn_gate, w_ffn_up=w_ffn_up, w_ffn_down=w_ffn_down, norm_final=norm_final, loss_target=loss_target, m_norm_mix=m_norm_mix, m_w_in=m_w_in, m_gate_bias=m_gate_bias, m_conv_w=m_conv_w, m_conv_b=m_conv_b, m_conv_ln_g=m_conv_ln_g, m_conv_ln_b=m_conv_ln_b, m_w_conv_out=m_w_conv_out, m_sgu_ln_g=m_sgu_ln_g, m_sgu_ln_b=m_sgu_ln_b, m_w_spatial=m_w_spatial, m_b_spatial=m_b_spatial, m_w_sgu_out=m_w_sgu_out, m_w_o=m_w_o, m_norm_ffn=m_norm_ffn, m_w_ffn_gate=m_w_ffn_gate, m_w_ffn_up=m_w_ffn_up, m_w_ffn_down=m_w_ffn_down, m_norm_final=m_norm_final, v_norm_mix=v_norm_mix, v_w_in=v_w_in, v_gate_bias=v_gate_bias, v_conv_w=v_conv_w, v_conv_b=v_conv_b, v_conv_ln_g=v_conv_ln_g, v_conv_ln_b=v_conv_ln_b, v_w_conv_out=v_w_conv_out, v_sgu_ln_g=v_sgu_ln_g, v_sgu_ln_b=v_sgu_ln_b, v_w_spatial=v_w_spatial, v_b_spatial=v_b_spatial, v_w_sgu_out=v_w_sgu_out, v_w_o=v_w_o, v_norm_ffn=v_norm_ffn, v_w_ffn_gate=v_w_ffn_gate, v_w_ffn_up=v_w_ffn_up, v_w_ffn_down=v_w_ffn_down, v_norm_final=v_norm_final)
    weights = {n: given[n] for n in TWIN_WEIGHTS}
    shared = {n: given[n] for n in SHARED_INPUTS}
    per_example = {n: given[n] for n in ['x']}
    grad_fn = _jax.value_and_grad(_loss, argnums=(0, 1))

    def one_microbatch(ex, loss_target):
        ex = dict(ex)
        diff = ex.pop(TWIN_DIFF_INPUT)
        return grad_fn(weights, diff, {**shared, **ex}, loss_target)

    if N_MICROBATCH == 1:
        loss, (grad_w, grad_x) = one_microbatch(per_example, given["loss_target"])
    else:
        def body(carry, xs):
            loss_sum, grad_sum = carry
            l_k, (gw_k, gx_k) = one_microbatch(xs[0], xs[1])
            with _jax.named_scope("update"):
                return (loss_sum + l_k, _jax.tree.map(_jnp.add, grad_sum, gw_k)), gx_k

        init = (_jnp.zeros((), _jnp.float32), _jax.tree.map(_jnp.zeros_like, weights))
        (loss, grad_w), grad_x = _jax.lax.scan(body, init, (per_example, given["loss_target"]))
    with _jax.named_scope("update"):
        delta_w, new_m, new_v = {}, {}, {}
        for n in TWIN_WEIGHTS:
            delta_w[n], new_m[n], new_v[n] = _adamw(weights[n], grad_w[n], given["m_" + n], given["v_" + n])
    return (loss, grad_x, *[grad_w[n] for n in TWIN_WEIGHTS], *[delta_w[n] for n in TWIN_WEIGHTS],
            *[new_m[n] for n in TWIN_WEIGHTS], *[new_v[n] for n in TWIN_WEIGHTS])
```

```python
import functools

import jax
import jax.numpy as jnp
from jax import lax
from jax.experimental import pallas as pl
from jax.experimental.pallas import tpu as pltpu

F32 = jnp.float32
BF16 = jnp.bfloat16
MESH_ID = pl.DeviceIdType.MESH

NDEV = 8
EPS = 1e-6
CONV_TAPS = 31
CONV_PAD = (CONV_TAPS - 1) // 2
CONV_TAPS_PADDED = 32
HALO = 16
CONV_ROWS = 64
LANES = 128
SUBLANES = 8
CHUNK = 128
GROUPS = 8
FF_PAD = 384
VMEM_LIMIT_BYTES = 56 * 1024 * 1024

ADAM_LR = 0.001
ADAM_B1 = 0.9
ADAM_B2 = 0.999
ADAM_EPS = 1e-08
ADAM_WD = 0.01
ADAM_STEP = 10

TILE_IN = 512
TILE_MIX = 256
TILE_FFN = 256
TILE_TN = 512
TILE_LOSS = 512

WEIGHT_NAMES = ['norm_mix', 'w_in', 'gate_bias', 'conv_w', 'conv_b', 'conv_ln_g', 'conv_ln_b', 'w_conv_out',
                'sgu_ln_g', 'sgu_ln_b', 'w_spatial', 'b_spatial', 'w_sgu_out', 'w_o', 'norm_ffn', 'w_ffn_gate',
                'w_ffn_up', 'w_ffn_down', 'norm_final']


def _sds(shape, dtype):
    return jax.ShapeDtypeStruct(tuple(shape), dtype)


def _params(*sem):
    return pltpu.CompilerParams(dimension_semantics=sem or None, vmem_limit_bytes=VMEM_LIMIT_BYTES)


def _nn(a, b):
    return jnp.dot(a, b, preferred_element_type=F32)


def _nt(a, b):
    return lax.dot_general(a, b, (((1,), (1,)), ((), ())), preferred_element_type=F32)


def _tn(a, b):
    return lax.dot_general(a, b, (((0,), (0,)), ((), ())), preferred_element_type=F32)


def _sig(v):
    return jax.nn.sigmoid(v)


def _fold(v):
    r, c = v.shape
    return jnp.sum(v.reshape(r // SUBLANES, SUBLANES, c), axis=0)


def _tile(tm, n, j=0):
    return pl.BlockSpec((tm, n), lambda i: (i, j))


def _row(n):
    return pl.BlockSpec((1, n), lambda i: (0, 0))


def _resident(shape):
    nd = len(shape)
    return pl.BlockSpec(tuple(shape), lambda *_: (0,) * nd)


def _weight(w, layer):
    n, _, r, c = w.shape
    return pl.BlockSpec((n, None, r, c), lambda *_: (0, layer, 0, 0), pipeline_mode=pl.Buffered(1))


def _my_coords():
    return lax.axis_index("x"), lax.axis_index("y"), lax.axis_index("c")


def _peer(rel):
    x, y, c = _my_coords()
    return (1 - x if rel & 4 else x, 1 - y if rel & 2 else y, 1 - c if rel & 1 else c)


def _slot(pos):
    return 4 * pos[0] + 2 * pos[1] + pos[2]


def _all_gather(arrays, name):
    n = len(arrays)
    chips = (4, 2, 6)

    def body(*refs):
        ins, outs = refs[:n], refs[n:2 * n]
        send_sems, recv_sems, local_sems = refs[2 * n:]
        me = _slot(_peer(0))

        def copy(j, sem, block_rel, to_rel, src=None):
            blk = outs[j].at[_slot(_peer(block_rel))]
            return pltpu.make_async_remote_copy(
                src_ref=blk if src is None else src, dst_ref=blk,
                send_sem=send_sems.at[j, sem], recv_sem=recv_sems.at[j, sem],
                device_id=_peer(to_rel), device_id_type=MESH_ID)

        started = []
        local = []
        for j in range(n):
            lc = pltpu.make_async_copy(ins[j], outs[j].at[me], local_sems.at[j])
            lc.start()
            local.append(lc)
            for rel in (1,) + chips:
                cp = copy(j, rel, 0, rel, src=ins[j])
                cp.start()
                started.append(cp)
        for j in range(n):
            for rel in chips:
                copy(j, rel, rel, 0).wait_recv()
                fwd = copy(j, rel ^ 1, rel, 1)
                fwd.start()
                started.append(fwd)
        for j in range(n):
            copy(j, 1, 1, 0).wait_recv()
            for rel in chips:
                copy(j, rel ^ 1, rel ^ 1, 0).wait_recv()
        for cp in started:
            cp.wait_send()
        for lc in local:
            lc.wait()

    any_spec = pl.BlockSpec(memory_space=pl.ANY)
    return pl.pallas_call(
        body, name=name,
        in_specs=[any_spec] * n, out_specs=[any_spec] * n,
        out_shape=[_sds((NDEV,) + a.shape, a.dtype) for a in arrays],
        scratch_shapes=[pltpu.SemaphoreType.DMA((n, NDEV)), pltpu.SemaphoreType.DMA((n, NDEV)),
                        pltpu.SemaphoreType.DMA((n,))],
    )(*arrays)


def _all_to_all(arrays, name):
    n = len(arrays)

    def body(*refs):
        ins, outs = refs[:n], refs[n:2 * n]
        send_sems, recv_sems, local_sems = refs[2 * n:]
        me = _slot(_peer(0))

        def copy(j, rel):
            return pltpu.make_async_remote_copy(
                src_ref=ins[j].at[_slot(_peer(rel))], dst_ref=outs[j].at[me],
                send_sem=send_sems.at[j, rel], recv_sem=recv_sems.at[j, rel],
                device_id=_peer(rel), device_id_type=MESH_ID)

        def arrival(j, rel):
            blk = outs[j].at[_slot(_peer(rel))]
            return pltpu.make_async_remote_copy(
                src_ref=blk, dst_ref=blk, send_sem=send_sems.at[j, rel], recv_sem=recv_sems.at[j, rel],
                device_id=_peer(rel), device_id_type=MESH_ID)

        started, local = [], []
        for j in range(n):
            lc = pltpu.make_async_copy(ins[j].at[me], outs[j].at[me], local_sems.at[j])
            lc.start()
            local.append(lc)
            for rel in range(1, NDEV):
                cp = copy(j, rel)
                cp.start()
                started.append(cp)
        for j in range(n):
            for rel in range(1, NDEV):
                arrival(j, rel).wait_recv()
        for cp in started:
            cp.wait_send()
        for lc in local:
            lc.wait()

    any_spec = pl.BlockSpec(memory_space=pl.ANY)
    return pl.pallas_call(
        body, name=name,
        in_specs=[any_spec] * n, out_specs=[any_spec] * n,
        out_shape=[_sds(a.shape, a.dtype) for a in arrays],
        scratch_shapes=[pltpu.SemaphoreType.DMA((n, NDEV)), pltpu.SemaphoreType.DMA((n, NDEV)),
                        pltpu.SemaphoreType.DMA((n,))],
    )(*arrays)


def _cast_pad(w, rows, cols, name):
    nl, r, c = w.shape

    def body(w_ref, o_ref):
        if (rows, cols) != (r, c):
            o_ref[...] = jnp.zeros(o_ref.shape, BF16)
        o_ref[0, 0:r, 0:c] = w_ref[0].astype(BF16)

    return pl.pallas_call(
        body, name=name, grid=(nl,),
        in_specs=[pl.BlockSpec((1, r, c), lambda i: (i, 0, 0))],
        out_specs=pl.BlockSpec((1, rows, cols), lambda i: (i, 0, 0)),
        out_shape=_sds((nl, rows, cols), BF16), compiler_params=_params("parallel"),
    )(w)


def _in_proj(x, gain, w, layer, tm, name):
    t, d = x.shape
    nb, _, _, bw = w.shape

    def body(x_ref, g_ref, w_ref, h_ref, p_ref):
        xv = x_ref[...]
        r = lax.rsqrt(jnp.mean(xv * xv, axis=-1, keepdims=True) + EPS)
        h = (xv * r * g_ref[...]).astype(BF16)
        h_ref[...] = h
        for k in range(nb):
            p_ref[:, k * bw:(k + 1) * bw] = _nn(h, w_ref[k]).astype(BF16)

    return pl.pallas_call(
        body, name=name, grid=(t // tm,),
        in_specs=[_tile(tm, d), _row(d), _weight(w, layer)],
        out_specs=[_tile(tm, d), _tile(tm, nb * bw)],
        out_shape=[_sds((t, d), BF16), _sds((t, nb * bw), BF16)],
        compiler_params=_params("parallel"),
    )(x, gain, w)


def _halo_specs(tm, t, d, col):
    nh, nhb = tm // HALO, t // HALO
    prev = pl.BlockSpec((HALO, d), lambda i: (jnp.maximum(i * nh - 1, 0), col))
    nxt = pl.BlockSpec((HALO, d), lambda i: (jnp.minimum((i + 1) * nh, nhb - 1), col))
    return prev, nxt


def _rolled_windows(buf_ref, r0, lanes):
    w = CONV_ROWS + 2 * HALO
    win = buf_ref[pl.ds(r0, w), lanes]
    return [win if b == 0 else pltpu.roll(win, shift=w - b, axis=0) for b in range(SUBLANES)]


def _tap(rolled, off):
    a, b = divmod(off, SUBLANES)
    return rolled[b][SUBLANES * a:SUBLANES * a + CONV_ROWS, :]


def _dwconv(buf_ref, w_ref, out_ref, tm, flip):
    nblk = out_ref.shape[1] // LANES

    def rows(j, carry):
        r0 = pl.multiple_of(j * CONV_ROWS, CONV_ROWS)
        for kb in range(nblk):
            lanes = slice(kb * LANES, (kb + 1) * LANES)
            rolled = _rolled_windows(buf_ref, r0, lanes)
            acc = jnp.zeros((CONV_ROWS, LANES), F32)
            for k in range(CONV_TAPS):
                off = (CONV_TAPS - k) if flip else (1 + k)
                acc = acc + w_ref[kb, k:k + 1, :] * _tap(rolled, off)
            out_ref[pl.ds(r0, CONV_ROWS), lanes] = acc
        return carry

    lax.fori_loop(0, tm // CONV_ROWS, rows, 0)


def _fill_glu_buffer(cbuf, av, ag, avp, agp, avn, agn, first, last, tm):
    c0p = avp[...].astype(F32) * _sig(agp[...].astype(F32))
    c0n = avn[...].astype(F32) * _sig(agn[...].astype(F32))
    cbuf[0:HALO, :] = jnp.where(first, 0.0, c0p)
    cbuf[HALO:HALO + tm, :] = av[...].astype(F32) * _sig(ag[...].astype(F32))
    cbuf[HALO + tm:HALO + tm + HALO, :] = jnp.where(last, 0.0, c0n)


def _layernorm_stats(v):
    mu = jnp.mean(v, axis=-1, keepdims=True)
    cen = v - mu
    rstd = lax.rsqrt(jnp.mean(cen * cen, axis=-1, keepdims=True) + EPS)
    return cen * rstd, rstd


def _spatial_mix(ws_ref, vn_ref, bias_ref, mixed_ref, tm):
    for ci in range(tm // CHUNK):
        rs = slice(ci * CHUNK, (ci + 1) * CHUNK)
        for g in range(GROUPS):
            ls = slice(g * LANES, (g + 1) * LANES)
            mixed_ref[rs, ls] = _nn(ws_ref[g], vn_ref[rs, ls]) + bias_ref[:, ls]


def _mixer_fwd(proj, x, w, small, layer, seq, tm, name):
    t, d = x.shape
    hp_v, hn_v = _halo_specs(tm, t, d, 0)
    hp_g, hn_g = _halo_specs(tm, t, d, 1)

    def body(av, ag, u_ref, v_ref, ga_ref, gb_ref, avp, agp, avn, agn, x_ref,
             cw_ref, cb_ref, lg_ref, lb_ref, wco_ref, sg_ref, sb_ref, ws_ref, bias_ref, wso_ref,
             gba_ref, gbb_ref, wo_ref,
             c1_ref, ya_ref, yb_ref, x1_ref, cbuf, c1f, vn_ref, mixed_ref):
        i = pl.program_id(0)
        first = (i * tm) % seq == 0
        last = ((i + 1) * tm) % seq == 0
        _fill_glu_buffer(cbuf, av, ag, avp, agp, avn, agn, first, last, tm)
        _dwconv(cbuf, cw_ref, c1f, tm, flip=False)
        c1 = c1f[...] + cb_ref[...]
        c1_ref[...] = c1.astype(BF16)
        c2hat, _ = _layernorm_stats(c1)
        c2 = c2hat * lg_ref[...] + lb_ref[...]
        c3 = (c2 * _sig(c2)).astype(BF16)
        ya = _nn(c3, wco_ref[...].reshape(d, d))
        ya_ref[...] = ya.astype(BF16)
        vhat, _ = _layernorm_stats(v_ref[...].astype(F32))
        vn_ref[...] = (vhat * sg_ref[...] + sb_ref[...]).astype(BF16)
        _spatial_mix(ws_ref, vn_ref, bias_ref, mixed_ref, tm)
        gated = (u_ref[...].astype(F32) * mixed_ref[...]).astype(BF16)
        yb = _nn(gated, wso_ref[...].reshape(d, d))
        yb_ref[...] = yb.astype(BF16)
        sa = _sig(ga_ref[...].astype(F32) + gba_ref[...])
        sb = _sig(gb_ref[...].astype(F32) + gbb_ref[...])
        merged = (sa * ya + sb * yb).astype(BF16)
        x1_ref[...] = x_ref[...] + _nn(merged, wo_ref[...].reshape(d, d))

    cols = [_tile(tm, d, j) for j in range(6)]
    return pl.pallas_call(
        body, name=name, grid=(t // tm,),
        in_specs=cols + [hp_v, hp_g, hn_v, hn_g, _tile(tm, d),
                         _weight(w['conv_w'], layer), _row(d), _row(d), _row(d),
                         _weight(w['w_conv_out'], layer), _row(d), _row(d),
                         _resident(small['ws'].shape), _resident(small['bias_full'].shape),
                         _weight(w['w_sgu_out'], layer), _row(d), _row(d), _weight(w['w_o'], layer)],
        out_specs=[_tile(tm, d)] * 4,
        out_shape=[_sds((t, d), BF16)] * 3 + [_sds((t, d), F32)],
        scratch_shapes=[pltpu.VMEM((tm + 2 * HALO, d), F32), pltpu.VMEM((tm, d), F32),
                        pltpu.VMEM((tm, d), BF16), pltpu.VMEM((tm, d), F32)],
        compiler_params=_params("parallel"),
    )(proj, proj, proj, proj, proj, proj, proj, proj, proj, proj, x,
      w['conv_w'], small['conv_b'], small['conv_ln_g'], small['conv_ln_b'], w['w_conv_out'],
      small['sgu_ln_g'], small['sgu_ln_b'], small['ws'], small['bias_full'], w['w_sgu_out'],
      small['gba'], small['gbb'], w['w_o'])


def _ffn_fwd(x1, gain, w, layer, tm, name):
    t, d = x1.shape
    nb = NDEV
    hid = nb * FF_PAD

    def body(x_ref, g_ref, wg_ref, wu_ref, wd_ref, h_ref, gg_ref, uu_ref, x2_ref):
        xv = x_ref[...]
        r = lax.rsqrt(jnp.mean(xv * xv, axis=-1, keepdims=True) + EPS)
        h = (xv * r * g_ref[...]).astype(BF16)
        h_ref[...] = h
        acc = xv
        for k in range(nb):
            cs = slice(k * FF_PAD, (k + 1) * FF_PAD)
            gk = _nn(h, wg_ref[k])
            uk = _nn(h, wu_ref[k])
            gg_ref[:, cs] = gk.astype(BF16)
            uu_ref[:, cs] = uk.astype(BF16)
            ak = (gk * _sig(gk) * uk).astype(BF16)
            acc = acc + _nn(ak, wd_ref[k])
        x2_ref[...] = acc

    return pl.pallas_call(
        body, name=name, grid=(t // tm,),
        in_specs=[_tile(tm, d), _row(d), _weight(w['w_ffn_gate'], layer),
                  _weight(w['w_ffn_up'], layer), _weight(w['w_ffn_down'], layer)],
        out_specs=[_tile(tm, d), _tile(tm, hid), _tile(tm, hid), _tile(tm, d)],
        out_shape=[_sds((t, d), BF16), _sds((t, hid), BF16), _sds((t, hid), BF16), _sds((t, d), F32)],
        compiler_params=_params("parallel"),
    )(x1, gain, w['w_ffn_gate'], w['w_ffn_up'], w['w_ffn_down'])


def _init_small(step, small_ref, acc_ref):
    @pl.when(step == 0)
    def _():
        small_ref[...] = jnp.zeros(small_ref.shape, F32)
        acc_ref[...] = jnp.zeros(acc_ref.shape, F32)


def _finish_small(step, nsteps, small_ref, acc_ref, nq):
    @pl.when(step == nsteps - 1)
    def _():
        for q in range(nq):
            small_ref[q:q + 1, :] = jnp.sum(acc_ref[q], axis=0, keepdims=True)


def _loss_bwd(x, gain, target, tm, name):
    t, d = x.shape
    nsteps = t // tm

    def body(x_ref, g_ref, t_ref, dx_ref, small_ref, acc_ref):
        i = pl.program_id(0)
        _init_small(i, small_ref, acc_ref)
        xv = x_ref[...]
        r = lax.rsqrt(jnp.mean(xv * xv, axis=-1, keepdims=True) + EPS)
        xhat = xv * r
        diff = xhat * g_ref[...] - t_ref[...]
        dy = diff * (1.0 / d)
        acc_ref[0] += _fold(dy * xhat)
        acc_ref[1] += _fold(diff * diff)
        dxhat = dy * g_ref[...]
        dx_ref[...] = r * (dxhat - xhat * jnp.mean(dxhat * xhat, axis=-1, keepdims=True))

        @pl.when(i == nsteps - 1)
        def _():
            small_ref[0:1, :] = jnp.sum(acc_ref[0], axis=0, keepdims=True)
            total = jnp.sum(acc_ref[1]) * (0.5 / d)
            small_ref[1:2, :] = jnp.full((1, d), total, F32)

    return pl.pallas_call(
        body, name=name, grid=(nsteps,),
        in_specs=[_tile(tm, d), _row(d), _tile(tm, d)],
        out_specs=[_tile(tm, d), _resident((SUBLANES, d))],
        out_shape=[_sds((t, d), F32), _sds((SUBLANES, d), F32)],
        scratch_shapes=[pltpu.VMEM((2, SUBLANES, d), F32)],
        compiler_params=_params("arbitrary"),
    )(x, gain, target)


def _ffn_bwd(dx2, x1, gain, gg, uu, w, layer, tm, name):
    t, d = x1.shape
    nb = NDEV
    hid = nb * FF_PAD
    nsteps = t // tm

    def body(dx_ref, x_ref, g_ref, gg_ref, uu_ref, wg_ref, wu_ref, wd_ref,
             a_ref, dg_ref, du_ref, dx1_ref, small_ref, acc_ref):
        i = pl.program_id(0)
        _init_small(i, small_ref, acc_ref)
        dxv = dx_ref[...]
        dxb = dxv.astype(BF16)
        dh = jnp.zeros((tm, d), F32)
        for k in range(nb):
            cs = slice(k * FF_PAD, (k + 1) * FF_PAD)
            gk = gg_ref[:, cs].astype(F32)
            uk = uu_ref[:, cs].astype(F32)
            sg = _sig(gk)
            silu = gk * sg
            a_ref[:, cs] = (silu * uk).astype(BF16)
            da = _nt(dxb, wd_ref[k])
            dgk = (da * uk * (sg * (1.0 + gk * (1.0 - sg)))).astype(BF16)
            duk = (da * silu).astype(BF16)
            dg_ref[:, cs] = dgk
            du_ref[:, cs] = duk
            dh = dh + _nt(dgk, wg_ref[k]) + _nt(duk, wu_ref[k])
        xv = x_ref[...]
        r = lax.rsqrt(jnp.mean(xv * xv, axis=-1, keepdims=True) + EPS)
        xhat = xv * r
        acc_ref[0] += _fold(dh * xhat)
        dxhat = dh * g_ref[...]
        dx1_ref[...] = dxv + r * (dxhat - xhat * jnp.mean(dxhat * xhat, axis=-1, keepdims=True))
        _finish_small(i, nsteps, small_ref, acc_ref, 1)

    return pl.pallas_call(
        body, name=name, grid=(nsteps,),
        in_specs=[_tile(tm, d), _tile(tm, d), _row(d), _tile(tm, hid), _tile(tm, hid),
                  _weight(w['w_ffn_gate'], layer), _weight(w['w_ffn_up'], layer), _weight(w['w_ffn_down'], layer)],
        out_specs=[_tile(tm, hid)] * 3 + [_tile(tm, d), _resident((SUBLANES, d))],
        out_shape=[_sds((t, hid), BF16)] * 3 + [_sds((t, d), F32), _sds((SUBLANES, d), F32)],
        scratch_shapes=[pltpu.VMEM((1, SUBLANES, d), F32)],
        compiler_params=_params("arbitrary"),
    )(dx2, x1, gain, gg, uu, w['w_ffn_gate'], w['w_ffn_up'], w['w_ffn_down'])


def _matmul_tn(a, b, a_blk, b_blk, stack, tm, name):
    t, ma = a.shape
    nb_ = b.shape[1]
    na, nb = ma // a_blk, nb_ // b_blk
    nsteps = t // tm
    if stack == 'b':
        out_shape, out_spec = (nb, ma, b_blk), pl.BlockSpec((None, a_blk, b_blk), lambda i, j, k: (j, i, 0))
    elif stack == 'a':
        out_shape, out_spec = (na, a_blk, nb_), pl.BlockSpec((None, a_blk, b_blk), lambda i, j, k: (i, 0, j))
    else:
        out_shape, out_spec = (ma, nb_), pl.BlockSpec((a_blk, b_blk), lambda i, j, k: (i, j))

    def body(a_ref, b_ref, o_ref, acc_ref):
        k = pl.program_id(2)

        @pl.when(k == 0)
        def _():
            acc_ref[...] = jnp.zeros(acc_ref.shape, F32)

        acc_ref[...] += _tn(a_ref[...].astype(BF16), b_ref[...].astype(BF16))

        @pl.when(k == nsteps - 1)
        def _():
            o_ref[...] = acc_ref[...].astype(BF16)

    return pl.pallas_call(
        body, name=name, grid=(na, nb, nsteps),
        in_specs=[pl.BlockSpec((tm, a_blk), lambda i, j, k: (k, i)), pl.BlockSpec((tm, b_blk), lambda i, j, k: (k, j))],
        out_specs=out_spec, out_shape=_sds(out_shape, BF16),
        scratch_shapes=[pltpu.VMEM((a_blk, b_blk), F32)],
        compiler_params=_params("parallel", "parallel", "arbitrary"),
    )(a, b)


def _mixer_bwd(dx1, proj, c1, ya, yb, w, small, layer, tm, name):
    t, d = dx1.shape
    nsteps = t // tm
    nq = 6

    def body(dx_ref, u_ref, v_ref, ga_ref, gb_ref, c1_ref, ya_ref, yb_ref,
             lg_ref, lb_ref, wco_ref, sg_ref, sb_ref, ws_ref, wst_ref, bias_ref, wso_ref, gba_ref, gbb_ref, wo_ref,
             sel_ref,
             dp_ref, dc1_ref, mg_ref, c3_ref, gt_ref, dya_ref, dyb_ref, small_ref, dws_ref, dbs_ref,
             acc_ref, vn_ref, mixed_ref, dmix_ref, dvn_ref, dbias_ref):
        i = pl.program_id(0)
        _init_small(i, small_ref, acc_ref)

        @pl.when(i == 0)
        def _():
            dws_ref[...] = jnp.zeros(dws_ref.shape, F32)
            dbs_ref[...] = jnp.zeros(dbs_ref.shape, F32)
            dbias_ref[...] = jnp.zeros(dbias_ref.shape, F32)

        dmerged = _nt(dx_ref[...].astype(BF16), wo_ref[...].reshape(d, d))
        ya = ya_ref[...].astype(F32)
        yb = yb_ref[...].astype(F32)
        sa = _sig(ga_ref[...].astype(F32) + gba_ref[...])
        sb = _sig(gb_ref[...].astype(F32) + gbb_ref[...])
        mg_ref[...] = (sa * ya + sb * yb).astype(BF16)
        dya = (dmerged * sa).astype(BF16)
        dyb = (dmerged * sb).astype(BF16)
        dya_ref[...] = dya
        dyb_ref[...] = dyb
        dga = dmerged * ya * (sa * (1.0 - sa))
        dgb = dmerged * yb * (sb * (1.0 - sb))
        acc_ref[0] += _fold(dga)
        acc_ref[1] += _fold(dgb)
        dp_ref[:, 0:2 * d] = jnp.zeros((tm, 2 * d), BF16)
        dp_ref[:, 4 * d:5 * d] = dga.astype(BF16)
        dp_ref[:, 5 * d:6 * d] = dgb.astype(BF16)
        c2hat, rstd = _layernorm_stats(c1_ref[...].astype(F32))
        c2 = c2hat * lg_ref[...] + lb_ref[...]
        s2 = _sig(c2)
        c3_ref[...] = (c2 * s2).astype(BF16)
        dc3 = _nt(dya, wco_ref[...].reshape(d, d))
        dc2 = dc3 * (s2 * (1.0 + c2 * (1.0 - s2)))
        acc_ref[2] += _fold(dc2 * c2hat)
        acc_ref[3] += _fold(dc2)
        dc2hat = dc2 * lg_ref[...]
        dc1_ref[...] = (rstd * (dc2hat - jnp.mean(dc2hat, axis=-1, keepdims=True)
                                - c2hat * jnp.mean(dc2hat * c2hat, axis=-1, keepdims=True))).astype(BF16)
        vhat, rstd_v = _layernorm_stats(v_ref[...].astype(F32))
        vn_ref[...] = (vhat * sg_ref[...] + sb_ref[...]).astype(BF16)
        _spatial_mix(ws_ref, vn_ref, bias_ref, mixed_ref, tm)
        u = u_ref[...].astype(F32)
        mixed = mixed_ref[...]
        gt_ref[...] = (u * mixed).astype(BF16)
        dgated = _nt(dyb, wso_ref[...].reshape(d, d))
        dp_ref[:, 2 * d:3 * d] = (dgated * mixed).astype(BF16)
        dmix_ref[...] = dgated * u
        for ci in range(tm // CHUNK):
            rs = slice(ci * CHUNK, (ci + 1) * CHUNK)
            dbias_ref[...] += dmix_ref[rs, :]
            for g in range(GROUPS):
                ls = slice(g * LANES, (g + 1) * LANES)
                dm = dmix_ref[rs, ls].astype(BF16)
                dws_ref[g] += _nt(dm, vn_ref[rs, ls])
                dvn_ref[rs, ls] = _nn(wst_ref[g], dm)
        dvn = dvn_ref[...]
        acc_ref[4] += _fold(dvn * vhat)
        acc_ref[5] += _fold(dvn)
        dvhat = dvn * sg_ref[...]
        dp_ref[:, 3 * d:4 * d] = (rstd_v * (dvhat - jnp.mean(dvhat, axis=-1, keepdims=True)
                                           - vhat * jnp.mean(dvhat * vhat, axis=-1, keepdims=True))).astype(BF16)
        _finish_small(i, nsteps, small_ref, acc_ref, nq)

        @pl.when(i == nsteps - 1)
        def _():
            db = dbias_ref[...]
            hi = db.astype(BF16)
            lo = (db - hi.astype(F32)).astype(BF16)
            dbs_ref[...] = _nt(sel_ref[...], hi) + _nt(sel_ref[...], lo)

    cols = [_tile(tm, d, j) for j in (2, 3, 4, 5)]
    return pl.pallas_call(
        body, name=name, grid=(nsteps,),
        in_specs=[_tile(tm, d)] + cols + [_tile(tm, d)] * 3 + [
            _row(d), _row(d), _weight(w['w_conv_out'], layer),
            _row(d), _row(d), _resident(small['ws'].shape), _resident(small['wst'].shape),
            _resident(small['bias_full'].shape), _weight(w['w_sgu_out'], layer), _row(d), _row(d),
            _weight(w['w_o'], layer), _resident(small['group_sel'].shape)],
        out_specs=[_tile(tm, 6 * d)] + [_tile(tm, d)] * 6 + [
            _resident((SUBLANES, d)), _resident((GROUPS, CHUNK, CHUNK)), _resident((GROUPS, CHUNK))],
        out_shape=[_sds((t, 6 * d), BF16)] + [_sds((t, d), BF16)] * 6 + [
            _sds((SUBLANES, d), F32), _sds((GROUPS, CHUNK, CHUNK), F32), _sds((GROUPS, CHUNK), F32)],
        scratch_shapes=[pltpu.VMEM((nq, SUBLANES, d), F32), pltpu.VMEM((tm, d), BF16), pltpu.VMEM((tm, d), F32),
                        pltpu.VMEM((tm, d), F32), pltpu.VMEM((tm, d), F32), pltpu.VMEM((CHUNK, d), F32)],
        compiler_params=_params("arbitrary"),
    )(dx1, proj, proj, proj, proj, c1, ya, yb,
      small['conv_ln_g'], small['conv_ln_b'], w['w_conv_out'], small['sgu_ln_g'], small['sgu_ln_b'],
      small['ws'], small['wst'], small['bias_full'], w['w_sgu_out'], small['gba'], small['gbb'], w['w_o'],
      small['group_sel'])


def _conv_bwd(dproj, dc1, proj, w, layer, seq, tm, name):
    t, d = dc1.shape
    nsteps = t // tm
    nblk = d // LANES
    hp_v, hn_v = _halo_specs(tm, t, d, 0)
    hp_g, hn_g = _halo_specs(tm, t, d, 1)
    hp_d, hn_d = _halo_specs(tm, t, d, 0)

    def body(dp_in, dc_ref, dcp, dcn, av, ag, avp, agp, avn, agn, cw_ref,
             dp_ref, dcw_ref, small_ref, acc_ref, cbuf, dbuf, dc0f, accw):
        del dp_in
        i = pl.program_id(0)
        _init_small(i, small_ref, acc_ref)

        @pl.when(i == 0)
        def _():
            accw[...] = jnp.zeros(accw.shape, F32)
            dcw_ref[...] = jnp.zeros(dcw_ref.shape, F32)

        first = (i * tm) % seq == 0
        last = ((i + 1) * tm) % seq == 0
        _fill_glu_buffer(cbuf, av, ag, avp, agp, avn, agn, first, last, tm)
        dc1v = dc_ref[...].astype(F32)
        dbuf[0:HALO, :] = jnp.where(first, 0.0, dcp[...].astype(F32))
        dbuf[HALO:HALO + tm, :] = dc1v
        dbuf[HALO + tm:HALO + tm + HALO, :] = jnp.where(last, 0.0, dcn[...].astype(F32))
        acc_ref[0] += _fold(dc1v)
        _dwconv(dbuf, cw_ref, dc0f, tm, flip=True)

        def rows(j, carry):
            r0 = pl.multiple_of(j * CONV_ROWS, CONV_ROWS)
            for kb in range(nblk):
                lanes = slice(kb * LANES, (kb + 1) * LANES)
                rolled = _rolled_windows(cbuf, r0, lanes)
                dv = dbuf[pl.ds(r0 + HALO, CONV_ROWS), lanes]
                for k in range(CONV_TAPS):
                    accw[kb, k] += _fold(dv * _tap(rolled, 1 + k))
            return carry

        lax.fori_loop(0, tm // CONV_ROWS, rows, 0)
        sg = _sig(ag[...].astype(F32))
        avv = av[...].astype(F32)
        dc0 = dc0f[...]
        dp_ref[:, 0:d] = (dc0 * sg).astype(BF16)
        dp_ref[:, d:2 * d] = (dc0 * avv * (sg * (1.0 - sg))).astype(BF16)
        _finish_small(i, nsteps, small_ref, acc_ref, 1)

        @pl.when(i == nsteps - 1)
        def _():
            for kb in range(nblk):
                dcw_ref[kb] = jnp.sum(accw[kb], axis=1)

    return pl.pallas_call(
        body, name=name, grid=(nsteps,),
        in_specs=[pl.BlockSpec(memory_space=pl.ANY), _tile(tm, d), hp_d, hn_d, _tile(tm, d, 0), _tile(tm, d, 1),
                  hp_v, hp_g, hn_v, hn_g, _weight(w['conv_w'], layer)],
        out_specs=[_tile(tm, 2 * d), _resident((nblk, CONV_TAPS_PADDED, LANES)), _resident((SUBLANES, d))],
        out_shape=[_sds(dproj.shape, BF16), _sds((nblk, CONV_TAPS_PADDED, LANES), F32), _sds((SUBLANES, d), F32)],
        scratch_shapes=[pltpu.VMEM((1, SUBLANES, d), F32), pltpu.VMEM((tm + 2 * HALO, d), F32),
                        pltpu.VMEM((tm + 2 * HALO, d), F32), pltpu.VMEM((tm, d), F32),
                        pltpu.VMEM((nblk, CONV_TAPS_PADDED, SUBLANES, LANES), F32)],
        input_output_aliases={0: 0},
        compiler_params=_params("arbitrary"),
    )(dproj, dc1, dc1, dc1, proj, proj, proj, proj, proj, proj, w['conv_w'])


def _in_proj_bwd(dproj, x, dx1, gain, w, layer, tm, name):
    t, d = x.shape
    nb, _, _, bw = w.shape
    nsteps = t // tm

    def body(dp_ref, x_ref, dx1_ref, g_ref, w_ref, dx_ref, small_ref, acc_ref):
        i = pl.program_id(0)
        _init_small(i, small_ref, acc_ref)
        dh = jnp.zeros((tm, d), F32)
        for k in range(nb):
            dh = dh + _nt(dp_ref[:, k * bw:(k + 1) * bw], w_ref[k])
        xv = x_ref[...]
        r = lax.rsqrt(jnp.mean(xv * xv, axis=-1, keepdims=True) + EPS)
        xhat = xv * r
        acc_ref[0] += _fold(dh * xhat)
        dxhat = dh * g_ref[...]
        dx_ref[...] = dx1_ref[...] + r * (dxhat - xhat * jnp.mean(dxhat * xhat, axis=-1, keepdims=True))
        _finish_small(i, nsteps, small_ref, acc_ref, 1)

    return pl.pallas_call(
        body, name=name, grid=(nsteps,),
        in_specs=[_tile(tm, nb * bw), _tile(tm, d), _tile(tm, d), _row(d), _weight(w, layer)],
        out_specs=[_tile(tm, d), _resident((SUBLANES, d))],
        out_shape=[_sds((t, d), F32), _sds((SUBLANES, d), F32)],
        scratch_shapes=[pltpu.VMEM((1, SUBLANES, d), F32)],
        compiler_params=_params("arbitrary"),
    )(dproj, x, dx1, gain, w)


def _adam(wv, g, mv, vv):
    m = ADAM_B1 * mv + (1.0 - ADAM_B1) * g
    v = ADAM_B2 * vv + (1.0 - ADAM_B2) * jnp.square(g)
    m_hat = m / (1.0 - ADAM_B1 ** ADAM_STEP)
    v_hat = v / (1.0 - ADAM_B2 ** ADAM_STEP)
    delta = -ADAM_LR * (m_hat / (jnp.sqrt(v_hat) + ADAM_EPS) + ADAM_WD * wv)
    return delta, m, v


def _adamw_layer(layer, w, m, v, parts, prev, nsplit, name):
    nl, r, c = w.shape
    npart, pr, pc = parts.shape
    rt, prt = r // nsplit, pr // nsplit

    def body(w_ref, m_ref, v_ref, p_ref, *rest):
        g_ref, d_ref, nm_ref, nv_ref = rest[-4:]
        g = p_ref[0, 0:rt, 0:c].astype(F32)
        for s in range(1, npart):
            g = g + p_ref[s, 0:rt, 0:c].astype(F32)
        delta, mn, vn = _adam(w_ref[0], g, m_ref[0], v_ref[0])
        g_ref[0] = g
        d_ref[0] = delta
        nm_ref[0] = mn
        nv_ref[0] = vn

    wspec = pl.BlockSpec((1, rt, c), lambda i: (layer, i, 0))
    pspec = pl.BlockSpec((npart, prt, pc), lambda i: (0, i, 0))
    in_specs = [wspec, wspec, wspec, pspec]
    args = [w, m, v, parts]
    aliases = {}
    if prev is not None:
        in_specs += [pl.BlockSpec(memory_space=pl.ANY)] * 4
        args += list(prev)
        aliases = {4 + q: q for q in range(4)}
    return pl.pallas_call(
        body, name=name, grid=(nsplit,),
        in_specs=in_specs, out_specs=[wspec] * 4, out_shape=[_sds(w.shape, F32)] * 4,
        input_output_aliases=aliases, compiler_params=_params("parallel"),
    )(*args)


def _adamw_small(vec_g, dws_g, dbs_g, params, moments_m, moments_v, rows):
    names = list(params)
    nper = len(names)

    def body(*refs):
        vec_ref, dws_ref, dbs_ref = refs[:3]
        w_refs = dict(zip(names, refs[3:3 + nper]))
        m_refs = dict(zip(names, refs[3 + nper:3 + 2 * nper]))
        v_refs = dict(zip(names, refs[3 + 2 * nper:3 + 3 * nper]))
        outs = refs[3 + 3 * nper:]
        o = {kind: dict(zip(names, outs[q * nper:(q + 1) * nper])) for q, kind in enumerate("gdmv")}

        def put(nm, idx, g):
            delta, mn, vn = _adam(w_refs[nm][idx], g, m_refs[nm][idx], v_refs[nm][idx])
            o["g"][nm][idx] = g
            o["d"][nm][idx] = delta
            o["m"][nm][idx] = mn
            o["v"][nm][idx] = vn

        def vec_row(rr):
            g = vec_ref[0, rr:rr + 1, :]
            for s in range(1, NDEV):
                g = g + vec_ref[s, rr:rr + 1, :]
            return g

        dd = vec_ref.shape[2]
        dws = dws_ref[0]
        dbs = dbs_ref[0]
        for s in range(1, NDEV):
            dws = dws + dws_ref[s]
            dbs = dbs + dbs_ref[s]
        put('w_spatial', (slice(None),) * 4, dws)
        put('b_spatial', (slice(None),) * 3, dbs)
        for nm, where in rows.items():
            for li, rr in enumerate(where):
                if nm == 'gate_bias':
                    put(nm, (slice(li, li + 1), slice(0, dd)), vec_row(rr[0]))
                    put(nm, (slice(li, li + 1), slice(dd, 2 * dd)), vec_row(rr[1]))
                else:
                    put(nm, (slice(li, li + 1), slice(None)), vec_row(rr))

    ins = [vec_g, dws_g, dbs_g] + [params[n] for n in names] + [moments_m[n] for n in names] + [moments_v[n] for n in names]
    out_shape = [_sds(params[n].shape, F32) for n in names] * 4
    res = pl.pallas_call(body, name="adamw_small", out_shape=out_shape, compiler_params=_params())(*ins)
    return {kind: dict(zip(names, res[q * nper:(q + 1) * nper])) for q, kind in enumerate("gdmv")}


def _prepare_weights(p):
    d = p['w_in'].shape[1]
    return {
        'w_in': _cast_pad(p['w_in'], d, p['w_in'].shape[2], "cast_w_in"),
        'w_conv_out': _cast_pad(p['w_conv_out'], p['w_conv_out'].shape[1], d, "cast_w_conv_out"),
        'w_sgu_out': _cast_pad(p['w_sgu_out'], p['w_sgu_out'].shape[1], d, "cast_w_sgu_out"),
        'w_o': _cast_pad(p['w_o'], p['w_o'].shape[1], d, "cast_w_o"),
        'w_ffn_gate': _cast_pad(p['w_ffn_gate'], d, FF_PAD, "cast_w_ffn_gate"),
        'w_ffn_up': _cast_pad(p['w_ffn_up'], d, FF_PAD, "cast_w_ffn_up"),
        'w_ffn_down': _cast_pad(p['w_ffn_down'], FF_PAD, d, "cast_w_ffn_down"),
        'conv_w': jnp.pad(p['conv_w'][:, :, 0, :], ((0, 0), (0, CONV_TAPS_PADDED - CONV_TAPS), (0, 0))),
    }


def _gather_weights(shards):
    names = list(shards)
    got = _all_gather([shards[n] for n in names], "gather_weights")
    return dict(zip(names, got))


def _layer_small(p, layer):
    d = p['norm_mix'].shape[1]
    ws = p['w_spatial'][layer]
    rows = {n: p[n][layer:layer + 1] for n in ('norm_mix', 'norm_ffn', 'conv_b', 'conv_ln_g', 'conv_ln_b',
                                               'sgu_ln_g', 'sgu_ln_b')}
    return {
        **rows,
        'ws': ws.astype(BF16), 'wst': jnp.swapaxes(ws, 1, 2).astype(BF16),
        'bias_full': jnp.repeat(p['b_spatial'][layer].T, LANES, axis=1),
        'gba': p['gate_bias'][layer:layer + 1, 0:d], 'gbb': p['gate_bias'][layer:layer + 1, d:2 * d],
        'group_sel': (jnp.arange(d)[None, :] // LANES == jnp.arange(GROUPS)[:, None]).astype(BF16),
    }


def _forward_backward(p, w, x, target, seq):
    nl = p['norm_mix'].shape[0]
    d = x.shape[1]
    smalls = [_layer_small(p, l) for l in range(nl)]
    saved = []
    for l in range(nl):
        h, proj = _in_proj(x, smalls[l]['norm_mix'], w['w_in'], l, TILE_IN, f"in_proj_{l}")
        c1, ya, yb, x1 = _mixer_fwd(proj, x, w, smalls[l], l, seq, TILE_MIX, f"mixer_fwd_{l}")
        h2, gg, uu, x2 = _ffn_fwd(x1, smalls[l]['norm_ffn'], w, l, TILE_FFN, f"ffn_fwd_{l}")
        saved.append(dict(x=x, h=h, proj=proj, c1=c1, ya=ya, yb=yb, x1=x1, h2=h2, gg=gg, uu=uu))
        x = x2
    dx, small_loss = _loss_bwd(x, p['norm_final'][None, :], target, TILE_LOSS, "loss_bwd")
    big, vec_blocks, dws_l, dbs_l = [None] * nl, [None] * nl, [None] * nl, [None] * nl
    for l in reversed(range(nl)):
        s = saved[l]
        act, dgg, duu, dx1, small_ffn = _ffn_bwd(dx, s['x1'], smalls[l]['norm_ffn'], s['gg'], s['uu'], w, l, TILE_FFN,
                                                 f"ffn_bwd_{l}")
        g_gate = _matmul_tn(s['h2'], dgg, d, FF_PAD, 'b', TILE_TN, f"dw_ffn_gate_{l}")
        g_up = _matmul_tn(s['h2'], duu, d, FF_PAD, 'b', TILE_TN, f"dw_ffn_up_{l}")
        g_down = _matmul_tn(act, dx, FF_PAD, d, 'a', TILE_TN, f"dw_ffn_down_{l}")
        (dproj, dc1, merged, c3, gated, dya, dyb, small_mix, dws, dbs) = _mixer_bwd(
            dx1, s['proj'], s['c1'], s['ya'], s['yb'], w, smalls[l], l, TILE_MIX, f"mixer_bwd_{l}")
        dproj, g_conv, small_conv = _conv_bwd(dproj, dc1, s['proj'], w, l, seq, TILE_MIX, f"conv_bwd_{l}")
        g_o = _matmul_tn(merged, dx1, d, d, None, TILE_TN, f"dw_o_{l}")
        g_co = _matmul_tn(c3, dya, d, d, None, TILE_TN, f"dw_conv_out_{l}")
        g_so = _matmul_tn(gated, dyb, d, d, None, TILE_TN, f"dw_sgu_out_{l}")
        g_in = _matmul_tn(s['h'], dproj, d, w['w_in'].shape[3], 'b', TILE_TN, f"dw_in_{l}")
        dx, small_in = _in_proj_bwd(dproj, s['x'], dx1, smalls[l]['norm_mix'], w['w_in'], l, TILE_IN, f"in_proj_bwd_{l}")
        rows = d // NDEV
        big[l] = {'w_in': g_in, 'w_conv_out': g_co.reshape(NDEV, rows, d), 'w_sgu_out': g_so.reshape(NDEV, rows, d),
                  'w_o': g_o.reshape(NDEV, rows, d), 'w_ffn_gate': g_gate, 'w_ffn_up': g_up, 'w_ffn_down': g_down,
                  'conv_w': g_conv}
        vec_blocks[l] = [small_in, small_mix, small_conv, small_ffn]
        dws_l[l], dbs_l[l] = dws, dbs
    vec = jnp.concatenate([b for l in range(nl) for b in vec_blocks[l]] + [small_loss], axis=0)
    return vec, jnp.stack(dws_l), jnp.stack(dbs_l), dx, big


def _vec_rows(nl):
    rows = {n: [] for n in ('norm_mix', 'gate_bias', 'conv_b', 'conv_ln_g', 'conv_ln_b', 'sgu_ln_g', 'sgu_ln_b', 'norm_ffn')}
    for l in range(nl):
        base = 4 * SUBLANES * l
        rows['norm_mix'].append(base)
        rows['gate_bias'].append((base + SUBLANES, base + SUBLANES + 1))
        rows['conv_ln_g'].append(base + SUBLANES + 2)
        rows['conv_ln_b'].append(base + SUBLANES + 3)
        rows['sgu_ln_g'].append(base + SUBLANES + 4)
        rows['sgu_ln_b'].append(base + SUBLANES + 5)
        rows['conv_b'].append(base + 2 * SUBLANES)
        rows['norm_ffn'].append(base + 3 * SUBLANES)
    rows['norm_final'] = [4 * SUBLANES * nl]
    return rows


def _train_step(p, m, v, x3, target3):
    nl = p['norm_mix'].shape[0]
    bsz, seq, d = x3.shape
    x = x3.reshape(bsz * seq, d)
    target = target3.reshape(bsz * seq, d)
    w = _gather_weights(_prepare_weights(p))
    vec, dws, dbs, dx, big = _forward_backward(p, w, x, target, seq)
    loss_row = 4 * SUBLANES * nl + 1
    loss = lax.psum(vec[loss_row, 0], ("x", "y", "c"))

    big_names = list(big[0])
    exchanged = _all_to_all([big[l][n] for l in range(nl) for n in big_names], "scatter_grads")
    vec_g, dws_g, dbs_g = _all_gather([vec, dws, dbs], "gather_small_grads")

    out = {kind: {} for kind in "gdmv"}
    splits = {'w_in': 4, 'w_conv_out': 1, 'w_sgu_out': 1, 'w_o': 1, 'w_ffn_gate': 4, 'w_ffn_up': 4, 'w_ffn_down': 1, 'conv_w': 1}
    for qi, n in enumerate(big_names):
        if n == 'conv_w':
            pad = ((0, 0), (0, CONV_TAPS_PADDED - CONV_TAPS), (0, 0))
            wl, ml, vl = (jnp.pad(a[n][:, :, 0, :], pad) for a in (p, m, v))
        else:
            wl, ml, vl = p[n], m[n], v[n]
        prev = None
        for l in range(nl):
            prev = _adamw_layer(l, wl, ml, vl, exchanged[l * len(big_names) + qi], prev, splits[n], f"adamw_{n}_{l}")
        for kind, arr in zip("gdmv", prev):
            out[kind][n] = arr[:, 0:CONV_TAPS, None, :] if n == 'conv_w' else arr
    small_names = ['norm_mix', 'gate_bias', 'conv_b', 'conv_ln_g', 'conv_ln_b', 'sgu_ln_g', 'sgu_ln_b', 'w_spatial',
                   'b_spatial', 'norm_ffn', 'norm_final']

    def two_d(a):
        return a[None, :] if a.ndim == 1 else a

    res = _adamw_small(vec_g, dws_g, dbs_g, {n: two_d(p[n]) for n in small_names}, {n: two_d(m[n]) for n in small_names},
                       {n: two_d(v[n]) for n in small_names}, _vec_rows(nl))
    for kind in "gdmv":
        for n in small_names:
            out[kind][n] = res[kind][n].reshape(p[n].shape)
    grad_x = dx.reshape(bsz, seq, d)
    return (loss, grad_x, *[out[kind][n] for kind in "gdmv" for n in WEIGHT_NAMES])


def kernel(x, norm_mix, w_in, gate_bias, conv_w, conv_b, conv_ln_g, conv_ln_b, w_conv_out, sgu_ln_g, sgu_ln_b, w_spatial, b_spatial, w_sgu_out, w_o, norm_ffn, w_ffn_gate, w_ffn_up, w_ffn_down, norm_final, loss_target, m_norm_mix, m_w_in, m_gate_bias, m_conv_w, m_conv_b, m_conv_ln_g, m_conv_ln_b, m_w_conv_out, m_sgu_ln_g, m_sgu_ln_b, m_w_spatial, m_b_spatial, m_w_sgu_out, m_w_o, m_norm_ffn, m_w_ffn_gate, m_w_ffn_up, m_w_ffn_down, m_norm_final, v_norm_mix, v_w_in, v_gate_bias, v_conv_w, v_conv_b, v_conv_ln_g, v_conv_ln_b, v_w_conv_out, v_sgu_ln_g, v_sgu_ln_b, v_w_spatial, v_b_spatial, v_w_sgu_out, v_w_o, v_norm_ffn, v_w_ffn_gate, v_w_ffn_up, v_w_ffn_down, v_norm_final):
    p = dict(zip(WEIGHT_NAMES, (norm_mix, w_in, gate_bias, conv_w, conv_b, conv_ln_g, conv_ln_b, w_conv_out, sgu_ln_g, sgu_ln_b, w_spatial, b_spatial, w_sgu_out, w_o, norm_ffn, w_ffn_gate, w_ffn_up, w_ffn_down, norm_final)))
    m = dict(zip(WEIGHT_NAMES, (m_norm_mix, m_w_in, m_gate_bias, m_conv_w, m_conv_b, m_conv_ln_g, m_conv_ln_b, m_w_conv_out, m_sgu_ln_g, m_sgu_ln_b, m_w_spatial, m_b_spatial, m_w_sgu_out, m_w_o, m_norm_ffn, m_w_ffn_gate, m_w_ffn_up, m_w_ffn_down, m_norm_final)))
    v = dict(zip(WEIGHT_NAMES, (v_norm_mix, v_w_in, v_gate_bias, v_conv_w, v_conv_b, v_conv_ln_g, v_conv_ln_b, v_w_conv_out, v_sgu_ln_g, v_sgu_ln_b, v_w_spatial, v_b_spatial, v_w_sgu_out, v_w_o, v_norm_ffn, v_w_ffn_gate, v_w_ffn_up, v_w_ffn_down, v_norm_final)))
    return _train_step(p, m, v, x, loss_target)
```

```python
import math

import jax
import jax.numpy as jnp
from jax import lax
from jax.experimental import pallas as pl
from jax.experimental.pallas import tpu as pltpu

F32 = jnp.float32
BF16 = jnp.bfloat16
MESH_ID = pl.DeviceIdType.MESH

NDEV = 8
EPS = 1e-6
CONV_TAPS = 31
CONV_TAPS_PADDED = 32
HALO = 16
CONV_ROWS = 64
LANES = 128
SUBLANES = 8
CHUNK = 128
GROUPS = 8
FF_PAD = 384
FF_PAIR = 2 * FF_PAD
TN_COLS = 512
VMEM_LIMIT_BYTES = 56 * 1024 * 1024

ADAM_LR = 0.001
ADAM_B1 = 0.9
ADAM_B2 = 0.999
ADAM_EPS = 1e-08
ADAM_WD = 0.01
ADAM_STEP = 10

TILE_IN = 512
TILE_MIX = 256
TILE_FFN_FWD = 1024
TILE_FFN_BWD = 512
TILE_TN = 1024
TILE_LOSS = 512

WEIGHT_NAMES = ['norm_mix', 'w_in', 'gate_bias', 'conv_w', 'conv_b', 'conv_ln_g', 'conv_ln_b', 'w_conv_out',
                'sgu_ln_g', 'sgu_ln_b', 'w_spatial', 'b_spatial', 'w_sgu_out', 'w_o', 'norm_ffn', 'w_ffn_gate',
                'w_ffn_up', 'w_ffn_down', 'norm_final']
MIXER_WEIGHTS = ['w_conv_out', 'w_sgu_out', 'w_o', 'conv_w']
FFN_WEIGHTS = ['w_ffn_gate', 'w_ffn_up', 'w_ffn_down']


def _sds(shape, dtype):
    return jax.ShapeDtypeStruct(tuple(shape), dtype)


def _params(*sem):
    return pltpu.CompilerParams(dimension_semantics=sem or None, vmem_limit_bytes=VMEM_LIMIT_BYTES)


def _nn(a, b):
    return jnp.dot(a, b, preferred_element_type=F32)


def _nt(a, b):
    return lax.dot_general(a, b, (((1,), (1,)), ((), ())), preferred_element_type=F32)


def _tn(a, b):
    return lax.dot_general(a, b, (((0,), (0,)), ((), ())), preferred_element_type=F32)


def _sig(v):
    return jax.nn.sigmoid(v)


def _fold(v):
    r, c = v.shape
    return jnp.sum(v.reshape(r // SUBLANES, SUBLANES, c), axis=0)


def _tile(tm, n, j=0):
    return pl.BlockSpec((tm, n), lambda i, *_: (i, j))


def _row(n):
    return pl.BlockSpec((1, n), lambda *_: (0, 0))


def _resident(shape):
    nd = len(shape)
    return pl.BlockSpec(tuple(shape), lambda *_: (0,) * nd)


def _weight(w):
    nd = w.ndim
    return pl.BlockSpec(tuple(w.shape), lambda *_: (0,) * nd, pipeline_mode=pl.Buffered(1))


def _peer(rel):
    x, y, c = lax.axis_index("x"), lax.axis_index("y"), lax.axis_index("c")
    return (1 - x if rel & 4 else x, 1 - y if rel & 2 else y, 1 - c if rel & 1 else c)


def _slot(pos):
    return 4 * pos[0] + 2 * pos[1] + pos[2]


class _Exchange:
    def __init__(self, arrays, layers=None):
        self.arrays = list(arrays)
        self.layers = list(layers) if layers is not None else [None] * len(self.arrays)

    def scratch(self):
        n = len(self.arrays)
        return [pltpu.SemaphoreType.DMA((n, NDEV)), pltpu.SemaphoreType.DMA((n, NDEV)), pltpu.SemaphoreType.DMA((n,))]

    def _src(self, ins, j):
        return ins[j] if self.layers[j] is None else ins[j].at[self.layers[j]]

    def _block_shape(self, j):
        a = self.arrays[j]
        return a.shape if self.layers[j] is None else a.shape[1:]


class _Gather(_Exchange):
    chips = (4, 2, 6)

    def out_shape(self):
        return [_sds((NDEV,) + tuple(self._block_shape(j)), a.dtype) for j, a in enumerate(self.arrays)]

    @staticmethod
    def _copy(outs, sems, j, sem, block_rel, to_rel, src=None):
        blk = outs[j].at[_slot(_peer(block_rel))]
        return pltpu.make_async_remote_copy(
            src_ref=blk if src is None else src, dst_ref=blk,
            send_sem=sems[0].at[j, sem], recv_sem=sems[1].at[j, sem],
            device_id=_peer(to_rel), device_id_type=MESH_ID)

    def _local(self, ins, outs, sems, j):
        return pltpu.make_async_copy(self._src(ins, j), outs[j].at[_slot(_peer(0))], sems[2].at[j])

    def start(self, ins, outs, sems):
        for j in range(len(self.arrays)):
            self._local(ins, outs, sems, j).start()
            for rel in (1,) + self.chips:
                self._copy(outs, sems, j, rel, 0, rel, src=self._src(ins, j)).start()

    def forward(self, ins, outs, sems):
        for j in range(len(self.arrays)):
            for rel in self.chips:
                self._copy(outs, sems, j, rel, rel, 0).wait_recv()
                self._copy(outs, sems, j, rel ^ 1, rel, 1).start()

    def finish(self, ins, outs, sems):
        for j in range(len(self.arrays)):
            self._copy(outs, sems, j, 1, 1, 0).wait_recv()
            for rel in self.chips:
                self._copy(outs, sems, j, rel ^ 1, rel ^ 1, 0).wait_recv()
        for j in range(len(self.arrays)):
            for rel in (1,) + self.chips:
                self._copy(outs, sems, j, rel, 0, rel, src=self._src(ins, j)).wait_send()
            for rel in self.chips:
                self._copy(outs, sems, j, rel ^ 1, rel, 1).wait_send()
            self._local(ins, outs, sems, j).wait()


class _Scatter(_Exchange):
    def out_shape(self):
        return [_sds(a.shape, a.dtype) for a in self.arrays]

    @staticmethod
    def _copy(ins, outs, sems, j, rel):
        return pltpu.make_async_remote_copy(
            src_ref=ins[j].at[_slot(_peer(rel))], dst_ref=outs[j].at[_slot(_peer(0))],
            send_sem=sems[0].at[j, rel], recv_sem=sems[1].at[j, rel],
            device_id=_peer(rel), device_id_type=MESH_ID)

    @staticmethod
    def _arrival(outs, sems, j, rel):
        blk = outs[j].at[_slot(_peer(rel))]
        return pltpu.make_async_remote_copy(
            src_ref=blk, dst_ref=blk, send_sem=sems[0].at[j, rel], recv_sem=sems[1].at[j, rel],
            device_id=_peer(rel), device_id_type=MESH_ID)

    @staticmethod
    def _local(ins, outs, sems, j):
        me = _slot(_peer(0))
        return pltpu.make_async_copy(ins[j].at[me], outs[j].at[me], sems[2].at[j])

    def start(self, ins, outs, sems):
        for j in range(len(self.arrays)):
            self._local(ins, outs, sems, j).start()
            for rel in range(1, NDEV):
                self._copy(ins, outs, sems, j, rel).start()

    def forward(self, ins, outs, sems):
        pass

    def finish(self, ins, outs, sems):
        for j in range(len(self.arrays)):
            for rel in range(1, NDEV):
                self._arrival(outs, sems, j, rel).wait_recv()
        for j in range(len(self.arrays)):
            for rel in range(1, NDEV):
                self._copy(ins, outs, sems, j, rel).wait_send()
            self._local(ins, outs, sems, j).wait()


def _call(body, *, name, args, in_specs, out_specs, out_shape, grid=(), scratch_shapes=(), semantics=(),
          aliases=None, comm=None):
    in_specs, out_specs, out_shape = list(in_specs), list(out_specs), list(out_shape)
    scratch_shapes = list(scratch_shapes)
    if comm is None:
        res = pl.pallas_call(
            body, name=name, grid=grid, in_specs=in_specs, out_specs=out_specs, out_shape=out_shape,
            scratch_shapes=scratch_shapes, input_output_aliases=aliases or {},
            compiler_params=_params(*semantics))(*args)
        return list(res), []
    n_in, n_out, n_scr, nc = len(in_specs), len(out_specs), len(scratch_shapes), len(comm.arrays)
    total = math.prod(grid)
    middle = min((total * 5) // 8, total - 1)

    def hosted(*refs):
        ins, cins = refs[:n_in], refs[n_in:n_in + nc]
        o0 = n_in + nc
        outs, couts = refs[o0:o0 + n_out], refs[o0 + n_out:o0 + n_out + nc]
        s0 = o0 + n_out + nc
        scr, sems = refs[s0:s0 + n_scr], refs[s0 + n_scr:]
        if total == 1:
            comm.start(cins, couts, sems)
            body(*ins, *outs, *scr)
            comm.forward(cins, couts, sems)
            comm.finish(cins, couts, sems)
            return
        step = 0
        for axis, size in enumerate(grid):
            step = step * size + pl.program_id(axis)
        pl.when(step == 0)(lambda: comm.start(cins, couts, sems))
        body(*ins, *outs, *scr)
        pl.when(step == middle)(lambda: comm.forward(cins, couts, sems))
        pl.when(step == total - 1)(lambda: comm.finish(cins, couts, sems))

    any_spec = pl.BlockSpec(memory_space=pl.ANY)
    res = pl.pallas_call(
        hosted, name=name, grid=grid,
        in_specs=in_specs + [any_spec] * nc, out_specs=out_specs + [any_spec] * nc,
        out_shape=out_shape + comm.out_shape(), scratch_shapes=scratch_shapes + comm.scratch(),
        input_output_aliases=aliases or {}, compiler_params=_params(*(("arbitrary",) * len(grid))),
    )(*args, *comm.arrays)
    return list(res[:n_out]), list(res[n_out:])


def _exchange_alone(comm, name):
    return _call(lambda: None, name=name, args=(), in_specs=(), out_specs=(), out_shape=(), comm=comm)[1]


def _cast_pad(w, rows, cols, name):
    nl, r, c = w.shape

    def body(w_ref, o_ref):
        if (rows, cols) != (r, c):
            o_ref[...] = jnp.zeros(o_ref.shape, BF16)
        o_ref[0, 0:r, 0:c] = w_ref[0].astype(BF16)

    return _call(body, name=name, grid=(nl,), args=(w,),
                 in_specs=[pl.BlockSpec((1, r, c), lambda i: (i, 0, 0))],
                 out_specs=[pl.BlockSpec((1, rows, cols), lambda i: (i, 0, 0))],
                 out_shape=[_sds((nl, rows, cols), BF16)], semantics=("parallel",))[0][0]


def _in_proj(x, gain, w, tm, name, comm):
    t, d = x.shape
    nb, _, bw = w.shape

    def body(x_ref, g_ref, w_ref, h_ref, p_ref):
        xv = x_ref[...]
        r = lax.rsqrt(jnp.mean(xv * xv, axis=-1, keepdims=True) + EPS)
        h = (xv * r * g_ref[...]).astype(BF16)
        h_ref[...] = h
        for k in range(nb):
            p_ref[:, k * bw:(k + 1) * bw] = _nn(h, w_ref[k]).astype(BF16)

    return _call(body, name=name, grid=(t // tm,), args=(x, gain, w),
                 in_specs=[_tile(tm, d), _row(d), _weight(w)],
                 out_specs=[_tile(tm, d), _tile(tm, nb * bw)],
                 out_shape=[_sds((t, d), BF16), _sds((t, nb * bw), BF16)],
                 semantics=("parallel",), comm=comm)


def _halo_specs(tm, t, d, col):
    nh, nhb = tm // HALO, t // HALO
    prev = pl.BlockSpec((HALO, d), lambda i: (jnp.maximum(i * nh - 1, 0), col))
    nxt = pl.BlockSpec((HALO, d), lambda i: (jnp.minimum((i + 1) * nh, nhb - 1), col))
    return prev, nxt


def _rolled_windows(buf_ref, r0, lanes):
    w = CONV_ROWS + 2 * HALO
    win = buf_ref[pl.ds(r0, w), lanes]
    return [win if b == 0 else pltpu.roll(win, shift=w - b, axis=0) for b in range(SUBLANES)]


def _tap(rolled, off):
    a, b = divmod(off, SUBLANES)
    return rolled[b][SUBLANES * a:SUBLANES * a + CONV_ROWS, :]


def _dwconv(buf_ref, w_ref, out_ref, tm, flip):
    nblk = out_ref.shape[1] // LANES

    def rows(j, carry):
        r0 = pl.multiple_of(j * CONV_ROWS, CONV_ROWS)
        for kb in range(nblk):
            lanes = slice(kb * LANES, (kb + 1) * LANES)
            rolled = _rolled_windows(buf_ref, r0, lanes)
            acc = jnp.zeros((CONV_ROWS, LANES), F32)
            for k in range(CONV_TAPS):
                off = (CONV_TAPS - k) if flip else (1 + k)
                acc = acc + w_ref[kb, k:k + 1, :] * _tap(rolled, off)
            out_ref[pl.ds(r0, CONV_ROWS), lanes] = acc
        return carry

    lax.fori_loop(0, tm // CONV_ROWS, rows, 0)


def _fill_glu_buffer(cbuf, av, ag, avp, agp, avn, agn, first, last, tm):
    c0p = avp[...].astype(F32) * _sig(agp[...].astype(F32))
    c0n = avn[...].astype(F32) * _sig(agn[...].astype(F32))
    cbuf[0:HALO, :] = jnp.where(first, 0.0, c0p)
    cbuf[HALO:HALO + tm, :] = av[...].astype(F32) * _sig(ag[...].astype(F32))
    cbuf[HALO + tm:HALO + tm + HALO, :] = jnp.where(last, 0.0, c0n)


def _layernorm_stats(v):
    mu = jnp.mean(v, axis=-1, keepdims=True)
    cen = v - mu
    rstd = lax.rsqrt(jnp.mean(cen * cen, axis=-1, keepdims=True) + EPS)
    return cen * rstd, rstd


def _spatial_mix(ws_ref, vn_ref, bias_ref, mixed_ref, tm):
    for ci in range(tm // CHUNK):
        rs = slice(ci * CHUNK, (ci + 1) * CHUNK)
        for g in range(GROUPS):
            ls = slice(g * LANES, (g + 1) * LANES)
            mixed_ref[rs, ls] = _nn(ws_ref[g], vn_ref[rs, ls]) + bias_ref[:, ls]


def _mixer_fwd(proj, x, w, small, seq, tm, name, comm):
    t, d = x.shape
    hp_v, hn_v = _halo_specs(tm, t, d, 0)
    hp_g, hn_g = _halo_specs(tm, t, d, 1)

    def body(av, ag, u_ref, v_ref, ga_ref, gb_ref, avp, agp, avn, agn, x_ref,
             cw_ref, cb_ref, lg_ref, lb_ref, wco_ref, sg_ref, sb_ref, ws_ref, bias_ref, wso_ref,
             gba_ref, gbb_ref, wo_ref,
             c1_ref, ya_ref, yb_ref, x1_ref, cbuf, c1f, vn_ref, mixed_ref):
        i = pl.program_id(0)
        first = (i * tm) % seq == 0
        last = ((i + 1) * tm) % seq == 0
        _fill_glu_buffer(cbuf, av, ag, avp, agp, avn, agn, first, last, tm)
        _dwconv(cbuf, cw_ref, c1f, tm, flip=False)
        c1 = c1f[...] + cb_ref[...]
        c1_ref[...] = c1.astype(BF16)
        c2hat, _ = _layernorm_stats(c1)
        c2 = c2hat * lg_ref[...] + lb_ref[...]
        c3 = (c2 * _sig(c2)).astype(BF16)
        ya = _nn(c3, wco_ref[...].reshape(d, d))
        ya_ref[...] = ya.astype(BF16)
        vhat, _ = _layernorm_stats(v_ref[...].astype(F32))
        vn_ref[...] = (vhat * sg_ref[...] + sb_ref[...]).astype(BF16)
        _spatial_mix(ws_ref, vn_ref, bias_ref, mixed_ref, tm)
        gated = (u_ref[...].astype(F32) * mixed_ref[...]).astype(BF16)
        yb = _nn(gated, wso_ref[...].reshape(d, d))
        yb_ref[...] = yb.astype(BF16)
        sa = _sig(ga_ref[...].astype(F32) + gba_ref[...])
        sb = _sig(gb_ref[...].astype(F32) + gbb_ref[...])
        merged = (sa * ya + sb * yb).astype(BF16)
        x1_ref[...] = x_ref[...] + _nn(merged, wo_ref[...].reshape(d, d))

    cols = [_tile(tm, d, j) for j in range(6)]
    return _call(
        body, name=name, grid=(t // tm,),
        args=(proj,) * 10 + (x, w['conv_w'], small['conv_b'], small['conv_ln_g'], small['conv_ln_b'], w['w_conv_out'],
                             small['sgu_ln_g'], small['sgu_ln_b'], small['ws'], small['bias_full'], w['w_sgu_out'],
                             small['gba'], small['gbb'], w['w_o']),
        in_specs=cols + [hp_v, hp_g, hn_v, hn_g, _tile(tm, d),
                         _weight(w['conv_w']), _row(d), _row(d), _row(d),
                         _weight(w['w_conv_out']), _row(d), _row(d),
                         _resident(small['ws'].shape), _resident(small['bias_full'].shape),
                         _weight(w['w_sgu_out']), _row(d), _row(d), _weight(w['w_o'])],
        out_specs=[_tile(tm, d)] * 4,
        out_shape=[_sds((t, d), BF16)] * 3 + [_sds((t, d), F32)],
        scratch_shapes=[pltpu.VMEM((tm + 2 * HALO, d), F32), pltpu.VMEM((tm, d), F32),
                        pltpu.VMEM((tm, d), BF16), pltpu.VMEM((tm, d), F32)],
        semantics=("parallel",), comm=comm)


def _pair_specs(w, tm):
    d = w['w_ffn_gate'].shape[1]
    up = pl.BlockSpec((2, d, FF_PAD), lambda i, k: (k, 0, 0))
    down = pl.BlockSpec((None, FF_PAIR, d), lambda i, k: (k, 0, 0))
    return [up, up, down]


def _ffn_fwd(x1, gain, w, tm, name, comm):
    t, d = x1.shape
    tm = min(tm, t)
    npair = NDEV // 2
    hid = NDEV * FF_PAD
    wd_pairs = w['w_ffn_down'].reshape(npair, FF_PAIR, d)

    def body(x_ref, g_ref, wg_ref, wu_ref, wd_ref, h_ref, gg_ref, uu_ref, x2_ref, hb_ref, acc_ref):
        k = pl.program_id(1)

        @pl.when(k == 0)
        def _():
            xv = x_ref[...]
            r = lax.rsqrt(jnp.mean(xv * xv, axis=-1, keepdims=True) + EPS)
            h = (xv * r * g_ref[...]).astype(BF16)
            hb_ref[...] = h
            h_ref[...] = h
            acc_ref[...] = xv

        h = hb_ref[...]
        gk = _nn(h, jnp.concatenate([wg_ref[0], wg_ref[1]], axis=1))
        uk = _nn(h, jnp.concatenate([wu_ref[0], wu_ref[1]], axis=1))
        gg_ref[...] = gk.astype(BF16)
        uu_ref[...] = uk.astype(BF16)
        ak = (gk * _sig(gk) * uk).astype(BF16)
        acc_ref[...] += _nn(ak, wd_ref[...])

        @pl.when(k == npair - 1)
        def _():
            x2_ref[...] = acc_ref[...]

    pair_cols = pl.BlockSpec((tm, FF_PAIR), lambda i, k: (i, k))
    return _call(
        body, name=name, grid=(t // tm, npair),
        args=(x1, gain, w['w_ffn_gate'], w['w_ffn_up'], wd_pairs),
        in_specs=[_tile(tm, d), _row(d)] + _pair_specs(w, tm),
        out_specs=[_tile(tm, d), pair_cols, pair_cols, _tile(tm, d)],
        out_shape=[_sds((t, d), BF16), _sds((t, hid), BF16), _sds((t, hid), BF16), _sds((t, d), F32)],
        scratch_shapes=[pltpu.VMEM((tm, d), BF16), pltpu.VMEM((tm, d), F32)],
        semantics=("parallel", "arbitrary"), comm=comm)


def _init_small(first, small_ref, acc_ref):
    @pl.when(first)
    def _():
        small_ref[...] = jnp.zeros(small_ref.shape, F32)
        acc_ref[...] = jnp.zeros(acc_ref.shape, F32)


def _finish_small(last, small_ref, acc_ref, nq):
    @pl.when(last)
    def _():
        for q in range(nq):
            small_ref[q:q + 1, :] = jnp.sum(acc_ref[q], axis=0, keepdims=True)


def _loss_bwd(x, gain, target, tm, name):
    t, d = x.shape
    nsteps = t // tm

    def body(x_ref, g_ref, t_ref, dx_ref, small_ref, acc_ref):
        i = pl.program_id(0)
        _init_small(i == 0, small_ref, acc_ref)
        xv = x_ref[...]
        r = lax.rsqrt(jnp.mean(xv * xv, axis=-1, keepdims=True) + EPS)
        xhat = xv * r
        diff = xhat * g_ref[...] - t_ref[...]
        dy = diff * (1.0 / d)
        acc_ref[0] += _fold(dy * xhat)
        acc_ref[1] += _fold(diff * diff)
        dxhat = dy * g_ref[...]
        dx_ref[...] = r * (dxhat - xhat * jnp.mean(dxhat * xhat, axis=-1, keepdims=True))

        @pl.when(i == nsteps - 1)
        def _():
            small_ref[0:1, :] = jnp.sum(acc_ref[0], axis=0, keepdims=True)
            total = jnp.sum(acc_ref[1]) * (0.5 / d)
            small_ref[1:2, :] = jnp.full((1, d), total, F32)

    return _call(body, name=name, grid=(nsteps,), args=(x, gain, target),
                 in_specs=[_tile(tm, d), _row(d), _tile(tm, d)],
                 out_specs=[_tile(tm, d), _resident((SUBLANES, d))],
                 out_shape=[_sds((t, d), F32), _sds((SUBLANES, d), F32)],
                 scratch_shapes=[pltpu.VMEM((2, SUBLANES, d), F32)], semantics=("arbitrary",))[0]


def _ffn_bwd(dx2, x1, gain, gg, uu, w, tm, name, comm):
    t, d = x1.shape
    npair = NDEV // 2
    hid = NDEV * FF_PAD
    nsteps = t // tm
    wd_pairs = w['w_ffn_down'].reshape(npair, FF_PAIR, d)

    def body(dx_ref, x_ref, g_ref, gg_ref, uu_ref, wg_ref, wu_ref, wd_ref,
             a_ref, dg_ref, du_ref, dx1_ref, small_ref, acc_ref, dxb_ref, dh_ref):
        i, k = pl.program_id(0), pl.program_id(1)
        _init_small((i == 0) & (k == 0), small_ref, acc_ref)

        @pl.when(k == 0)
        def _():
            dxb_ref[...] = dx_ref[...].astype(BF16)
            dh_ref[...] = jnp.zeros(dh_ref.shape, F32)

        gk = gg_ref[...].astype(F32)
        uk = uu_ref[...].astype(F32)
        sg = _sig(gk)
        silu = gk * sg
        a_ref[...] = (silu * uk).astype(BF16)
        da = _nt(dxb_ref[...], wd_ref[...])
        dgk = (da * uk * (sg * (1.0 + gk * (1.0 - sg)))).astype(BF16)
        duk = (da * silu).astype(BF16)
        dg_ref[...] = dgk
        du_ref[...] = duk
        dh_ref[...] += (_nt(dgk, jnp.concatenate([wg_ref[0], wg_ref[1]], axis=1))
                        + _nt(duk, jnp.concatenate([wu_ref[0], wu_ref[1]], axis=1)))

        @pl.when(k == npair - 1)
        def _():
            xv = x_ref[...]
            dh = dh_ref[...]
            r = lax.rsqrt(jnp.mean(xv * xv, axis=-1, keepdims=True) + EPS)
            xhat = xv * r
            acc_ref[0] += _fold(dh * xhat)
            dxhat = dh * g_ref[...]
            dx1_ref[...] = dx_ref[...] + r * (dxhat - xhat * jnp.mean(dxhat * xhat, axis=-1, keepdims=True))

        _finish_small((i == nsteps - 1) & (k == npair - 1), small_ref, acc_ref, 1)

    pair_cols = pl.BlockSpec((tm, FF_PAIR), lambda i, k: (i, k))
    return _call(
        body, name=name, grid=(nsteps, npair),
        args=(dx2, x1, gain, gg, uu, w['w_ffn_gate'], w['w_ffn_up'], wd_pairs),
        in_specs=[_tile(tm, d), _tile(tm, d), _row(d), pair_cols, pair_cols] + _pair_specs(w, tm),
        out_specs=[pair_cols] * 3 + [_tile(tm, d), _resident((SUBLANES, d))],
        out_shape=[_sds((t, hid), BF16)] * 3 + [_sds((t, d), F32), _sds((SUBLANES, d), F32)],
        scratch_shapes=[pltpu.VMEM((1, SUBLANES, d), F32), pltpu.VMEM((tm, d), BF16), pltpu.VMEM((tm, d), F32)],
        semantics=("arbitrary", "arbitrary"), comm=comm)


def _matmul_tn(a, b, a_blk, b_blk, stack, shard, tm, name, comm):
    t, ma = a.shape
    tm = min(tm, t)
    nb_ = b.shape[1]
    na, nb = ma // a_blk, nb_ // b_blk
    nsteps = t // tm
    cw = min(TN_COLS, b_blk)
    if stack == 'b':
        per = b_blk // shard
        out_shape, out_spec = (nb_ // shard, ma, shard), pl.BlockSpec((per, a_blk, shard), lambda i, j, k: (j, 0, 0))
    elif stack == 'a':
        per = a_blk // shard
        out_shape, out_spec = (ma // shard, shard, nb_), pl.BlockSpec((per, shard, b_blk), lambda i, j, k: (i, 0, 0))
    else:
        out_shape, out_spec = (ma, nb_), pl.BlockSpec((a_blk, b_blk), lambda i, j, k: (i, j))

    def body(a_ref, b_ref, o_ref, acc_ref):
        k = pl.program_id(2)

        @pl.when(k == 0)
        def _():
            acc_ref[...] = jnp.zeros(acc_ref.shape, F32)

        av = a_ref[...].astype(BF16)
        for c in range(0, b_blk, cw):
            acc_ref[:, c:c + cw] += _tn(av, b_ref[:, c:c + cw].astype(BF16))

        @pl.when(k == nsteps - 1)
        def _():
            if stack == 'b':
                for s in range(per):
                    o_ref[s] = acc_ref[:, s * shard:(s + 1) * shard].astype(BF16)
            elif stack == 'a':
                for s in range(per):
                    o_ref[s] = acc_ref[s * shard:(s + 1) * shard, :].astype(BF16)
            else:
                o_ref[...] = acc_ref[...].astype(BF16)

    res, got = _call(
        body, name=name, grid=(na, nb, nsteps), args=(a, b),
        in_specs=[pl.BlockSpec((tm, a_blk), lambda i, j, k: (k, i)), pl.BlockSpec((tm, b_blk), lambda i, j, k: (k, j))],
        out_specs=[out_spec], out_shape=[_sds(out_shape, BF16)],
        scratch_shapes=[pltpu.VMEM((a_blk, b_blk), F32)],
        semantics=("parallel", "parallel", "arbitrary"), comm=comm)
    return res[0], got


def _mixer_bwd(dx1, proj, c1, ya, yb, w, small, tm, name, comm):
    t, d = dx1.shape
    nsteps = t // tm
    nq = 6

    def body(dx_ref, u_ref, v_ref, ga_ref, gb_ref, c1_ref, ya_ref, yb_ref,
             lg_ref, lb_ref, wco_ref, sg_ref, sb_ref, ws_ref, wst_ref, bias_ref, wso_ref, gba_ref, gbb_ref, wo_ref,
             sel_ref,
             dp_ref, dc1_ref, mg_ref, c3_ref, gt_ref, dya_ref, dyb_ref, small_ref, dws_ref, dbs_ref,
             acc_ref, vn_ref, mixed_ref, dmix_ref, dvn_ref, dbias_ref):
        i = pl.program_id(0)
        _init_small(i == 0, small_ref, acc_ref)

        @pl.when(i == 0)
        def _():
            dws_ref[...] = jnp.zeros(dws_ref.shape, F32)
            dbs_ref[...] = jnp.zeros(dbs_ref.shape, F32)
            dbias_ref[...] = jnp.zeros(dbias_ref.shape, F32)

        dmerged = _nt(dx_ref[...].astype(BF16), wo_ref[...].reshape(d, d))
        ya = ya_ref[...].astype(F32)
        yb = yb_ref[...].astype(F32)
        sa = _sig(ga_ref[...].astype(F32) + gba_ref[...])
        sb = _sig(gb_ref[...].astype(F32) + gbb_ref[...])
        mg_ref[...] = (sa * ya + sb * yb).astype(BF16)
        dya = (dmerged * sa).astype(BF16)
        dyb = (dmerged * sb).astype(BF16)
        dya_ref[...] = dya
        dyb_ref[...] = dyb
        dga = dmerged * ya * (sa * (1.0 - sa))
        dgb = dmerged * yb * (sb * (1.0 - sb))
        acc_ref[0] += _fold(dga)
        acc_ref[1] += _fold(dgb)
        dp_ref[:, 0:2 * d] = jnp.zeros((tm, 2 * d), BF16)
        dp_ref[:, 4 * d:5 * d] = dga.astype(BF16)
        dp_ref[:, 5 * d:6 * d] = dgb.astype(BF16)
        c2hat, rstd = _layernorm_stats(c1_ref[...].astype(F32))
        c2 = c2hat * lg_ref[...] + lb_ref[...]
        s2 = _sig(c2)
        c3_ref[...] = (c2 * s2).astype(BF16)
        dc3 = _nt(dya, wco_ref[...].reshape(d, d))
        dc2 = dc3 * (s2 * (1.0 + c2 * (1.0 - s2)))
        acc_ref[2] += _fold(dc2 * c2hat)
        acc_ref[3] += _fold(dc2)
        dc2hat = dc2 * lg_ref[...]
        dc1_ref[...] = (rstd * (dc2hat - jnp.mean(dc2hat, axis=-1, keepdims=True)
                                - c2hat * jnp.mean(dc2hat * c2hat, axis=-1, keepdims=True))).astype(BF16)
        vhat, rstd_v = _layernorm_stats(v_ref[...].astype(F32))
        vn_ref[...] = (vhat * sg_ref[...] + sb_ref[...]).astype(BF16)
        _spatial_mix(ws_ref, vn_ref, bias_ref, mixed_ref, tm)
        u = u_ref[...].astype(F32)
        mixed = mixed_ref[...]
        gt_ref[...] = (u * mixed).astype(BF16)
        dgated = _nt(dyb, wso_ref[...].reshape(d, d))
        dp_ref[:, 2 * d:3 * d] = (dgated * mixed).astype(BF16)
        dmix_ref[...] = dgated * u
        for ci in range(tm // CHUNK):
            rs = slice(ci * CHUNK, (ci + 1) * CHUNK)
            dbias_ref[...] += dmix_ref[rs, :]
            for g in range(GROUPS):
                ls = slice(g * LANES, (g + 1) * LANES)
                dm = dmix_ref[rs, ls].astype(BF16)
                dws_ref[g] += _nt(dm, vn_ref[rs, ls])
                dvn_ref[rs, ls] = _nn(wst_ref[g], dm)
        dvn = dvn_ref[...]
        acc_ref[4] += _fold(dvn * vhat)
        acc_ref[5] += _fold(dvn)
        dvhat = dvn * sg_ref[...]
        dp_ref[:, 3 * d:4 * d] = (rstd_v * (dvhat - jnp.mean(dvhat, axis=-1, keepdims=True)
                                           - vhat * jnp.mean(dvhat * vhat, axis=-1, keepdims=True))).astype(BF16)
        _finish_small(i == nsteps - 1, small_ref, acc_ref, nq)

        @pl.when(i == nsteps - 1)
        def _():
            db = dbias_ref[...]
            hi = db.astype(BF16)
            lo = (db - hi.astype(F32)).astype(BF16)
            dbs_ref[...] = _nt(sel_ref[...], hi) + _nt(sel_ref[...], lo)

    cols = [_tile(tm, d, j) for j in (2, 3, 4, 5)]
    return _call(
        body, name=name, grid=(nsteps,),
        args=(dx1, proj, proj, proj, proj, c1, ya, yb,
              small['conv_ln_g'], small['conv_ln_b'], w['w_conv_out'], small['sgu_ln_g'], small['sgu_ln_b'],
              small['ws'], small['wst'], small['bias_full'], w['w_sgu_out'], small['gba'], small['gbb'], w['w_o'],
              small['group_sel']),
        in_specs=[_tile(tm, d)] + cols + [_tile(tm, d)] * 3 + [
            _row(d), _row(d), _weight(w['w_conv_out']),
            _row(d), _row(d), _resident(small['ws'].shape), _resident(small['wst'].shape),
            _resident(small['bias_full'].shape), _weight(w['w_sgu_out']), _row(d), _row(d),
            _weight(w['w_o']), _resident(small['group_sel'].shape)],
        out_specs=[_tile(tm, 6 * d)] + [_tile(tm, d)] * 6 + [
            _resident((SUBLANES, d)), _resident((GROUPS, CHUNK, CHUNK)), _resident((GROUPS, CHUNK))],
        out_shape=[_sds((t, 6 * d), BF16)] + [_sds((t, d), BF16)] * 6 + [
            _sds((SUBLANES, d), F32), _sds((GROUPS, CHUNK, CHUNK), F32), _sds((GROUPS, CHUNK), F32)],
        scratch_shapes=[pltpu.VMEM((nq, SUBLANES, d), F32), pltpu.VMEM((tm, d), BF16), pltpu.VMEM((tm, d), F32),
                        pltpu.VMEM((tm, d), F32), pltpu.VMEM((tm, d), F32), pltpu.VMEM((CHUNK, d), F32)],
        semantics=("arbitrary",), comm=comm)


def _conv_bwd(dproj, dc1, proj, w, seq, tm, name, comm):
    t, d = dc1.shape
    nsteps = t // tm
    nblk = d // LANES
    hp_v, hn_v = _halo_specs(tm, t, d, 0)
    hp_g, hn_g = _halo_specs(tm, t, d, 1)
    hp_d, hn_d = _halo_specs(tm, t, d, 0)

    def body(dp_in, dc_ref, dcp, dcn, av, ag, avp, agp, avn, agn, cw_ref,
             dp_ref, dcw_ref, small_ref, acc_ref, cbuf, dbuf, dc0f, accw):
        del dp_in
        i = pl.program_id(0)
        _init_small(i == 0, small_ref, acc_ref)

        @pl.when(i == 0)
        def _():
            accw[...] = jnp.zeros(accw.shape, F32)
            dcw_ref[...] = jnp.zeros(dcw_ref.shape, F32)

        first = (i * tm) % seq == 0
        last = ((i + 1) * tm) % seq == 0
        _fill_glu_buffer(cbuf, av, ag, avp, agp, avn, agn, first, last, tm)
        dc1v = dc_ref[...].astype(F32)
        dbuf[0:HALO, :] = jnp.where(first, 0.0, dcp[...].astype(F32))
        dbuf[HALO:HALO + tm, :] = dc1v
        dbuf[HALO + tm:HALO + tm + HALO, :] = jnp.where(last, 0.0, dcn[...].astype(F32))
        acc_ref[0] += _fold(dc1v)
        _dwconv(dbuf, cw_ref, dc0f, tm, flip=True)

        def rows(j, carry):
            r0 = pl.multiple_of(j * CONV_ROWS, CONV_ROWS)
            for kb in range(nblk):
                lanes = slice(kb * LANES, (kb + 1) * LANES)
                rolled = _rolled_windows(cbuf, r0, lanes)
                dv = dbuf[pl.ds(r0 + HALO, CONV_ROWS), lanes]
                for k in range(CONV_TAPS):
                    accw[kb, k] += _fold(dv * _tap(rolled, 1 + k))
            return carry

        lax.fori_loop(0, tm // CONV_ROWS, rows, 0)
        sg = _sig(ag[...].astype(F32))
        avv = av[...].astype(F32)
        dc0 = dc0f[...]
        dp_ref[:, 0:d] = (dc0 * sg).astype(BF16)
        dp_ref[:, d:2 * d] = (dc0 * avv * (sg * (1.0 - sg))).astype(BF16)
        _finish_small(i == nsteps - 1, small_ref, acc_ref, 1)

        @pl.when(i == nsteps - 1)
        def _():
            for kb in range(nblk):
                dcw_ref[kb] = jnp.sum(accw[kb], axis=1)

    return _call(
        body, name=name, grid=(nsteps,),
        args=(dproj, dc1, dc1, dc1, proj, proj, proj, proj, proj, proj, w['conv_w']),
        in_specs=[pl.BlockSpec(memory_space=pl.ANY), _tile(tm, d), hp_d, hn_d, _tile(tm, d, 0), _tile(tm, d, 1),
                  hp_v, hp_g, hn_v, hn_g, _weight(w['conv_w'])],
        out_specs=[_tile(tm, 2 * d), _resident((nblk, CONV_TAPS_PADDED, LANES)), _resident((SUBLANES, d))],
        out_shape=[_sds(dproj.shape, BF16), _sds((nblk, CONV_TAPS_PADDED, LANES), F32), _sds((SUBLANES, d), F32)],
        scratch_shapes=[pltpu.VMEM((1, SUBLANES, d), F32), pltpu.VMEM((tm + 2 * HALO, d), F32),
                        pltpu.VMEM((tm + 2 * HALO, d), F32), pltpu.VMEM((tm, d), F32),
                        pltpu.VMEM((nblk, CONV_TAPS_PADDED, SUBLANES, LANES), F32)],
        aliases={0: 0}, semantics=("arbitrary",), comm=comm)


def _in_proj_bwd(dproj, x, dx1, gain, w, tm, name, comm):
    t, d = x.shape
    nb, _, bw = w.shape
    nsteps = t // tm

    def body(dp_ref, x_ref, dx1_ref, g_ref, w_ref, dx_ref, small_ref, acc_ref):
        i = pl.program_id(0)
        _init_small(i == 0, small_ref, acc_ref)
        dh = jnp.zeros((tm, d), F32)
        for k in range(nb):
            dh = dh + _nt(dp_ref[:, k * bw:(k + 1) * bw], w_ref[k])
        xv = x_ref[...]
        r = lax.rsqrt(jnp.mean(xv * xv, axis=-1, keepdims=True) + EPS)
        xhat = xv * r
        acc_ref[0] += _fold(dh * xhat)
        dxhat = dh * g_ref[...]
        dx_ref[...] = dx1_ref[...] + r * (dxhat - xhat * jnp.mean(dxhat * xhat, axis=-1, keepdims=True))
        _finish_small(i == nsteps - 1, small_ref, acc_ref, 1)

    return _call(body, name=name, grid=(nsteps,), args=(dproj, x, dx1, gain, w),
                 in_specs=[_tile(tm, nb * bw), _tile(tm, d), _tile(tm, d), _row(d), _weight(w)],
                 out_specs=[_tile(tm, d), _resident((SUBLANES, d))],
                 out_shape=[_sds((t, d), F32), _sds((SUBLANES, d), F32)],
                 scratch_shapes=[pltpu.VMEM((1, SUBLANES, d), F32)], semantics=("arbitrary",), comm=comm)


def _adam(wv, g, mv, vv):
    m = ADAM_B1 * mv + (1.0 - ADAM_B1) * g
    v = ADAM_B2 * vv + (1.0 - ADAM_B2) * jnp.square(g)
    m_hat = m / (1.0 - ADAM_B1 ** ADAM_STEP)
    v_hat = v / (1.0 - ADAM_B2 ** ADAM_STEP)
    delta = -ADAM_LR * (m_hat / (jnp.sqrt(v_hat) + ADAM_EPS) + ADAM_WD * wv)
    return delta, m, v


def _adamw_layer(layer, w, m, v, parts, prev, nsplit, name):
    nl, r, c = w.shape
    npart, pr, pc = parts.shape
    rt, prt = r // nsplit, pr // nsplit

    def body(w_ref, m_ref, v_ref, p_ref, *rest):
        g_ref, d_ref, nm_ref, nv_ref = rest[-4:]
        g = p_ref[0, 0:rt, 0:c].astype(F32)
        for s in range(1, npart):
            g = g + p_ref[s, 0:rt, 0:c].astype(F32)
        delta, mn, vn = _adam(w_ref[0], g, m_ref[0], v_ref[0])
        g_ref[0] = g
        d_ref[0] = delta
        nm_ref[0] = mn
        nv_ref[0] = vn

    wspec = pl.BlockSpec((1, rt, c), lambda i: (layer, i, 0))
    pspec = pl.BlockSpec((npart, prt, pc), lambda i: (0, i, 0))
    in_specs = [wspec, wspec, wspec, pspec]
    args = [w, m, v, parts]
    aliases = {}
    if prev is not None:
        in_specs += [pl.BlockSpec(memory_space=pl.ANY)] * 4
        args += list(prev)
        aliases = {4 + q: q for q in range(4)}
    return _call(body, name=name, grid=(nsplit,), args=args, in_specs=in_specs, out_specs=[wspec] * 4,
                 out_shape=[_sds(w.shape, F32)] * 4, aliases=aliases, semantics=("parallel",))[0]


def _adamw_small(vec_g, dws_g, dbs_g, params, moments_m, moments_v, rows):
    names = list(params)
    nper = len(names)

    def body(*refs):
        vec_ref, dws_ref, dbs_ref = refs[:3]
        w_refs = dict(zip(names, refs[3:3 + nper]))
        m_refs = dict(zip(names, refs[3 + nper:3 + 2 * nper]))
        v_refs = dict(zip(names, refs[3 + 2 * nper:3 + 3 * nper]))
        outs = refs[3 + 3 * nper:]
        o = {kind: dict(zip(names, outs[q * nper:(q + 1) * nper])) for q, kind in enumerate("gdmv")}

        def put(nm, idx, g):
            delta, mn, vn = _adam(w_refs[nm][idx], g, m_refs[nm][idx], v_refs[nm][idx])
            o["g"][nm][idx] = g
            o["d"][nm][idx] = delta
            o["m"][nm][idx] = mn
            o["v"][nm][idx] = vn

        def vec_row(rr):
            g = vec_ref[0, rr:rr + 1, :]
            for s in range(1, NDEV):
                g = g + vec_ref[s, rr:rr + 1, :]
            return g

        dd = vec_ref.shape[2]
        dws = dws_ref[0]
        dbs = dbs_ref[0]
        for s in range(1, NDEV):
            dws = dws + dws_ref[s]
            dbs = dbs + dbs_ref[s]
        put('w_spatial', (slice(None),) * 4, dws)
        put('b_spatial', (slice(None),) * 3, dbs)
        for nm, where in rows.items():
            for li, rr in enumerate(where):
                if nm == 'gate_bias':
                    put(nm, (slice(li, li + 1), slice(0, dd)), vec_row(rr[0]))
                    put(nm, (slice(li, li + 1), slice(dd, 2 * dd)), vec_row(rr[1]))
                else:
                    put(nm, (slice(li, li + 1), slice(None)), vec_row(rr))

    ins = [vec_g, dws_g, dbs_g] + [params[n] for n in names] + [moments_m[n] for n in names] + [moments_v[n] for n in names]
    out_shape = [_sds(params[n].shape, F32) for n in names] * 4
    res = pl.pallas_call(body, name="adamw_small", out_shape=out_shape, compiler_params=_params())(*ins)
    return {kind: dict(zip(names, res[q * nper:(q + 1) * nper])) for q, kind in enumerate("gdmv")}


def _prepare_weights(p):
    d = p['w_in'].shape[1]
    return {
        'w_in': _cast_pad(p['w_in'], d, p['w_in'].shape[2], "cast_w_in"),
        'w_conv_out': _cast_pad(p['w_conv_out'], p['w_conv_out'].shape[1], d, "cast_w_conv_out"),
        'w_sgu_out': _cast_pad(p['w_sgu_out'], p['w_sgu_out'].shape[1], d, "cast_w_sgu_out"),
        'w_o': _cast_pad(p['w_o'], p['w_o'].shape[1], d, "cast_w_o"),
        'w_ffn_gate': _cast_pad(p['w_ffn_gate'], d, FF_PAD, "cast_w_ffn_gate"),
        'w_ffn_up': _cast_pad(p['w_ffn_up'], d, FF_PAD, "cast_w_ffn_up"),
        'w_ffn_down': _cast_pad(p['w_ffn_down'], FF_PAD, d, "cast_w_ffn_down"),
        'conv_w': jnp.pad(p['conv_w'][:, :, 0, :], ((0, 0), (0, CONV_TAPS_PADDED - CONV_TAPS), (0, 0))),
    }


def _gather_of(shards, names, layer):
    return _Gather([shards[n] for n in names], [layer] * len(names))


def _layer_small(p, layer):
    d = p['norm_mix'].shape[1]
    ws = p['w_spatial'][layer]
    rows = {n: p[n][layer:layer + 1] for n in ('norm_mix', 'norm_ffn', 'conv_b', 'conv_ln_g', 'conv_ln_b',
                                               'sgu_ln_g', 'sgu_ln_b')}
    return {
        **rows,
        'ws': ws.astype(BF16), 'wst': jnp.swapaxes(ws, 1, 2).astype(BF16),
        'bias_full': jnp.repeat(p['b_spatial'][layer].T, LANES, axis=1),
        'gba': p['gate_bias'][layer:layer + 1, 0:d], 'gbb': p['gate_bias'][layer:layer + 1, d:2 * d],
        'group_sel': (jnp.arange(d)[None, :] // LANES == jnp.arange(GROUPS)[:, None]).astype(BF16),
    }


class _GradQueue:
    def __init__(self):
        self.pending = []
        self.done = {}

    def push(self, key, array):
        self.pending.append((key, array))

    def take(self):
        keys = [k for k, _ in self.pending]
        comm = _Scatter([a for _, a in self.pending]) if self.pending else None
        self.pending = []
        return keys, comm

    def put(self, keys, arrays):
        self.done.update(zip(keys, arrays))


def _forward_backward(p, shards, x, target, seq):
    nl = p['norm_mix'].shape[0]
    d = x.shape[1]
    smalls = [_layer_small(p, l) for l in range(nl)]
    w_in = _exchange_alone(_gather_of(shards, ['w_in'], 0), "gather_w_in_0")[0]
    saved = []
    for l in range(nl):
        (h, proj), got = _in_proj(x, smalls[l]['norm_mix'], w_in, TILE_IN, f"in_proj_{l}",
                                  _gather_of(shards, MIXER_WEIGHTS, l))
        w = dict(zip(MIXER_WEIGHTS, got), w_in=w_in)
        (c1, ya, yb, x1), got = _mixer_fwd(proj, x, w, smalls[l], seq, TILE_MIX, f"mixer_fwd_{l}",
                                           _gather_of(shards, FFN_WEIGHTS, l))
        w.update(zip(FFN_WEIGHTS, got))
        nxt = _gather_of(shards, ['w_in'], l + 1) if l + 1 < nl else None
        (h2, gg, uu, x2), got = _ffn_fwd(x1, smalls[l]['norm_ffn'], w, TILE_FFN_FWD, f"ffn_fwd_{l}", nxt)
        saved.append(dict(x=x, h=h, proj=proj, c1=c1, ya=ya, yb=yb, x1=x1, h2=h2, gg=gg, uu=uu, w=w))
        x = x2
        if got:
            w_in = got[0]
    dx, small_loss = _loss_bwd(x, p['norm_final'][None, :], target, TILE_LOSS, "loss_bwd")
    queue = _GradQueue()
    vec_blocks, dws_l, dbs_l = [None] * nl, [None] * nl, [None] * nl
    rows = d // NDEV
    hid = NDEV * FF_PAD
    for l in reversed(range(nl)):
        s = saved[l]
        w = s['w']

        def hosted(fn, *args):
            keys, comm = queue.take()
            res, got = fn(*args, comm)
            queue.put(keys, got)
            return res

        def tn(key, a, b, a_blk, b_blk, stack, shard, reshape=None):
            keys, comm = queue.take()
            g, got = _matmul_tn(a, b, a_blk, b_blk, stack, shard, TILE_TN, f"dw_{key}_{l}", comm)
            queue.put(keys, got)
            queue.push((l, key), g if reshape is None else g.reshape(reshape))

        act, dgg, duu, dx1, small_ffn = hosted(_ffn_bwd, dx, s['x1'], smalls[l]['norm_ffn'], s['gg'], s['uu'], w,
                                               TILE_FFN_BWD, f"ffn_bwd_{l}")
        tn('w_ffn_gate', s['h2'], dgg, d, hid, 'b', FF_PAD)
        tn('w_ffn_up', s['h2'], duu, d, hid, 'b', FF_PAD)
        tn('w_ffn_down', act, dx, hid, d, 'a', FF_PAD)
        (dproj, dc1, merged, c3, gated, dya, dyb, small_mix, dws, dbs) = hosted(
            _mixer_bwd, dx1, s['proj'], s['c1'], s['ya'], s['yb'], w, smalls[l], TILE_MIX, f"mixer_bwd_{l}")
        dproj, g_conv, small_conv = hosted(_conv_bwd, dproj, dc1, s['proj'], w, seq, TILE_MIX, f"conv_bwd_{l}")
        queue.push((l, 'conv_w'), g_conv)
        tn('w_o', merged, dx1, d, d, None, None, (NDEV, rows, d))
        tn('w_conv_out', c3, dya, d, d, None, None, (NDEV, rows, d))
        tn('w_sgu_out', gated, dyb, d, d, None, None, (NDEV, rows, d))
        tn('w_in', s['h'], dproj, d, hid, 'b', w['w_in'].shape[2])
        dx, small_in = hosted(_in_proj_bwd, dproj, s['x'], dx1, smalls[l]['norm_mix'], w['w_in'], TILE_IN,
                              f"in_proj_bwd_{l}")
        vec_blocks[l] = [small_in, small_mix, small_conv, small_ffn]
        dws_l[l], dbs_l[l] = dws, dbs
    keys, comm = queue.take()
    if comm is not None:
        queue.put(keys, _exchange_alone(comm, "scatter_last_grads"))
    vec = jnp.concatenate([b for l in range(nl) for b in vec_blocks[l]] + [small_loss], axis=0)
    return vec, jnp.stack(dws_l), jnp.stack(dbs_l), dx, queue.done


def _vec_rows(nl):
    rows = {n: [] for n in ('norm_mix', 'gate_bias', 'conv_b', 'conv_ln_g', 'conv_ln_b', 'sgu_ln_g', 'sgu_ln_b', 'norm_ffn')}
    for l in range(nl):
        base = 4 * SUBLANES * l
        rows['norm_mix'].append(base)
        rows['gate_bias'].append((base + SUBLANES, base + SUBLANES + 1))
        rows['conv_ln_g'].append(base + SUBLANES + 2)
        rows['conv_ln_b'].append(base + SUBLANES + 3)
        rows['sgu_ln_g'].append(base + SUBLANES + 4)
        rows['sgu_ln_b'].append(base + SUBLANES + 5)
        rows['conv_b'].append(base + 2 * SUBLANES)
        rows['norm_ffn'].append(base + 3 * SUBLANES)
    rows['norm_final'] = [4 * SUBLANES * nl]
    return rows


def _train_step(p, m, v, x3, target3):
    nl = p['norm_mix'].shape[0]
    bsz, seq, d = x3.shape
    x = x3.reshape(bsz * seq, d)
    target = target3.reshape(bsz * seq, d)
    vec, dws, dbs, dx, exchanged = _forward_backward(p, _prepare_weights(p), x, target, seq)
    loss_row = 4 * SUBLANES * nl + 1
    loss = lax.psum(vec[loss_row, 0], ("x", "y", "c"))
    vec_g, dws_g, dbs_g = _exchange_alone(_Gather([vec, dws, dbs]), "gather_small_grads")

    out = {kind: {} for kind in "gdmv"}
    splits = {'w_in': 4, 'w_conv_out': 1, 'w_sgu_out': 1, 'w_o': 1, 'w_ffn_gate': 4, 'w_ffn_up': 4, 'w_ffn_down': 1, 'conv_w': 1}
    for n in splits:
        if n == 'conv_w':
            pad = ((0, 0), (0, CONV_TAPS_PADDED - CONV_TAPS), (0, 0))
            wl, ml, vl = (jnp.pad(a[n][:, :, 0, :], pad) for a in (p, m, v))
        else:
            wl, ml, vl = p[n], m[n], v[n]
        prev = None
        for l in range(nl):
            prev = _adamw_layer(l, wl, ml, vl, exchanged[(l, n)], prev, splits[n], f"adamw_{n}_{l}")
        for kind, arr in zip("gdmv", prev):
            out[kind][n] = arr[:, 0:CONV_TAPS, None, :] if n == 'conv_w' else arr
    small_names = ['norm_mix', 'gate_bias', 'conv_b', 'conv_ln_g', 'conv_ln_b', 'sgu_ln_g', 'sgu_ln_b', 'w_spatial',
                   'b_spatial', 'norm_ffn', 'norm_final']

    def two_d(a):
        return a[None, :] if a.ndim == 1 else a

    res = _adamw_small(vec_g, dws_g, dbs_g, {n: two_d(p[n]) for n in small_names}, {n: two_d(m[n]) for n in small_names},
                       {n: two_d(v[n]) for n in small_names}, _vec_rows(nl))
    for kind in "gdmv":
        for n in small_names:
            out[kind][n] = res[kind][n].reshape(p[n].shape)
    grad_x = dx.reshape(bsz, seq, d)
    return (loss, grad_x, *[out[kind][n] for kind in "gdmv" for n in WEIGHT_NAMES])


def kernel(x, norm_mix, w_in, gate_bias, conv_w, conv_b, conv_ln_g, conv_ln_b, w_conv_out, sgu_ln_g, sgu_ln_b, w_spatial, b_spatial, w_sgu_out, w_o, norm_ffn, w_ffn_gate, w_ffn_up, w_ffn_down, norm_final, loss_target, m_norm_mix, m_w_in, m_gate_bias, m_conv_w, m_conv_b, m_conv_ln_g, m_conv_ln_b, m_w_conv_out, m_sgu_ln_g, m_sgu_ln_b, m_w_spatial, m_b_spatial, m_w_sgu_out, m_w_o, m_norm_ffn, m_w_ffn_gate, m_w_ffn_up, m_w_ffn_down, m_norm_final, v_norm_mix, v_w_in, v_gate_bias, v_conv_w, v_conv_b, v_conv_ln_g, v_conv_ln_b, v_w_conv_out, v_sgu_ln_g, v_sgu_ln_b, v_w_spatial, v_b_spatial, v_w_sgu_out, v_w_o, v_norm_ffn, v_w_ffn_gate, v_w_ffn_up, v_w_ffn_down, v_norm_final):
    p = dict(zip(WEIGHT_NAMES, (norm_mix, w_in, gate_bias, conv_w, conv_b, conv_ln_g, conv_ln_b, w_conv_out, sgu_ln_g, sgu_ln_b, w_spatial, b_spatial, w_sgu_out, w_o, norm_ffn, w_ffn_gate, w_ffn_up, w_ffn_down, norm_final)))
    m = dict(zip(WEIGHT_NAMES, (m_norm_mix, m_w_in, m_gate_bias, m_conv_w, m_conv_b, m_conv_ln_g, m_conv_ln_b, m_w_conv_out, m_sgu_ln_g, m_sgu_ln_b, m_w_spatial, m_b_spatial, m_w_sgu_out, m_w_o, m_norm_ffn, m_w_ffn_gate, m_w_ffn_up, m_w_ffn_down, m_norm_final)))
    v = dict(zip(WEIGHT_NAMES, (v_norm_mix, v_w_in, v_gate_bias, v_conv_w, v_conv_b, v_conv_ln_g, v_conv_ln_b, v_w_conv_out, v_sgu_ln_g, v_sgu_ln_b, v_w_spatial, v_b_spatial, v_w_sgu_out, v_w_o, v_norm_ffn, v_w_ffn_gate, v_w_ffn_up, v_w_ffn_down, v_norm_final)))
    return _train_step(p, m, v, x, loss_target)
```

```python
import math

import jax
import jax.numpy as jnp
from jax import lax
from jax.experimental import pallas as pl
from jax.experimental.pallas import tpu as pltpu

F32 = jnp.float32
BF16 = jnp.bfloat16
MESH_ID = pl.DeviceIdType.MESH

NDEV = 8
EPS = 1e-6
CONV_TAPS = 31
CONV_TAPS_PADDED = 32
HALO = 16
CONV_ROWS = 64
LANES = 128
SUBLANES = 8
CHUNK = 128
GROUPS = 8
FF_PAD = 384
FF_PAIR = 2 * FF_PAD
TN_COLS = 512
VMEM_LIMIT_BYTES = 56 * 1024 * 1024

ADAM_LR = 0.001
ADAM_B1 = 0.9
ADAM_B2 = 0.999
ADAM_EPS = 1e-08
ADAM_WD = 0.01
ADAM_STEP = 10

TILE_IN = 512
TILE_MIX = 256
TILE_FFN_FWD = 1024
TILE_FFN_BWD = 512
TILE_TN = 1024
TILE_LOSS = 512

WEIGHT_NAMES = ['norm_mix', 'w_in', 'gate_bias', 'conv_w', 'conv_b', 'conv_ln_g', 'conv_ln_b', 'w_conv_out',
                'sgu_ln_g', 'sgu_ln_b', 'w_spatial', 'b_spatial', 'w_sgu_out', 'w_o', 'norm_ffn', 'w_ffn_gate',
                'w_ffn_up', 'w_ffn_down', 'norm_final']
MIXER_WEIGHTS = ['w_conv_out', 'w_sgu_out', 'w_o', 'conv_w']
FFN_WEIGHTS = ['w_ffn_gate', 'w_ffn_up', 'w_ffn_down']


def _sds(shape, dtype):
    return jax.ShapeDtypeStruct(tuple(shape), dtype)


def _params(*sem):
    return pltpu.CompilerParams(dimension_semantics=sem or None, vmem_limit_bytes=VMEM_LIMIT_BYTES)


def _nn(a, b):
    return jnp.dot(a, b, preferred_element_type=F32)


def _nt(a, b):
    return lax.dot_general(a, b, (((1,), (1,)), ((), ())), preferred_element_type=F32)


def _tn(a, b):
    return lax.dot_general(a, b, (((0,), (0,)), ((), ())), preferred_element_type=F32)


def _sig(v):
    return jax.nn.sigmoid(v)


def _fold(v):
    r, c = v.shape
    return jnp.sum(v.reshape(r // SUBLANES, SUBLANES, c), axis=0)


def _tile(tm, n, j=0):
    return pl.BlockSpec((tm, n), lambda i, *_: (i, j))


def _row(n):
    return pl.BlockSpec((1, n), lambda *_: (0, 0))


def _resident(shape):
    nd = len(shape)
    return pl.BlockSpec(tuple(shape), lambda *_: (0,) * nd)


def _weight(w):
    nd = w.ndim
    return pl.BlockSpec(tuple(w.shape), lambda *_: (0,) * nd, pipeline_mode=pl.Buffered(1))


def _peer(rel):
    x, y, c = lax.axis_index("x"), lax.axis_index("y"), lax.axis_index("c")
    return (1 - x if rel & 4 else x, 1 - y if rel & 2 else y, 1 - c if rel & 1 else c)


def _slot(pos):
    return 4 * pos[0] + 2 * pos[1] + pos[2]


class _Exchange:
    def __init__(self, arrays, layers=None):
        self.arrays = list(arrays)
        self.layers = list(layers) if layers is not None else [None] * len(self.arrays)

    def scratch(self):
        n = len(self.arrays)
        return [pltpu.SemaphoreType.DMA((n, NDEV)), pltpu.SemaphoreType.DMA((n, NDEV)), pltpu.SemaphoreType.DMA((n,))]

    def _src(self, ins, j):
        return ins[j] if self.layers[j] is None else ins[j].at[self.layers[j]]

    def _block_shape(self, j):
        a = self.arrays[j]
        return a.shape if self.layers[j] is None else a.shape[1:]


class _Gather(_Exchange):
    chips = (4, 2, 6)

    def out_shape(self):
        return [_sds((NDEV,) + tuple(self._block_shape(j)), a.dtype) for j, a in enumerate(self.arrays)]

    @staticmethod
    def _copy(outs, sems, j, sem, block_rel, to_rel, src=None):
        blk = outs[j].at[_slot(_peer(block_rel))]
        return pltpu.make_async_remote_copy(
            src_ref=blk if src is None else src, dst_ref=blk,
            send_sem=sems[0].at[j, sem], recv_sem=sems[1].at[j, sem],
            device_id=_peer(to_rel), device_id_type=MESH_ID)

    def _local(self, ins, outs, sems, j):
        return pltpu.make_async_copy(self._src(ins, j), outs[j].at[_slot(_peer(0))], sems[2].at[j])

    def start(self, ins, outs, sems):
        for j in range(len(self.arrays)):
            self._local(ins, outs, sems, j).start()
            for rel in (1,) + self.chips:
                self._copy(outs, sems, j, rel, 0, rel, src=self._src(ins, j)).start()

    def forward(self, ins, outs, sems):
        for j in range(len(self.arrays)):
            for rel in self.chips:
                self._copy(outs, sems, j, rel, rel, 0).wait_recv()
                self._copy(outs, sems, j, rel ^ 1, rel, 1).start()

    def finish(self, ins, outs, sems):
        for j in range(len(self.arrays)):
            self._copy(outs, sems, j, 1, 1, 0).wait_recv()
            for rel in self.chips:
                self._copy(outs, sems, j, rel ^ 1, rel ^ 1, 0).wait_recv()
        for j in range(len(self.arrays)):
            for rel in (1,) + self.chips:
                self._copy(outs, sems, j, rel, 0, rel, src=self._src(ins, j)).wait_send()
            for rel in self.chips:
                self._copy(outs, sems, j, rel ^ 1, rel, 1).wait_send()
            self._local(ins, outs, sems, j).wait()


class _Scatter(_Exchange):
    def out_shape(self):
        return [_sds(a.shape, a.dtype) for a in self.arrays]

    @staticmethod
    def _copy(ins, outs, sems, j, rel):
        return pltpu.make_async_remote_copy(
            src_ref=ins[j].at[_slot(_peer(rel))], dst_ref=outs[j].at[_slot(_peer(0))],
            send_sem=sems[0].at[j, rel], recv_sem=sems[1].at[j, rel],
            device_id=_peer(rel), device_id_type=MESH_ID)

    @staticmethod
    def _arrival(outs, sems, j, rel):
        blk = outs[j].at[_slot(_peer(rel))]
        return pltpu.make_async_remote_copy(
            src_ref=blk, dst_ref=blk, send_sem=sems[0].at[j, rel], recv_sem=sems[1].at[j, rel],
            device_id=_peer(rel), device_id_type=MESH_ID)

    @staticmethod
    def _local(ins, outs, sems, j):
        me = _slot(_peer(0))
        return pltpu.make_async_copy(ins[j].at[me], outs[j].at[me], sems[2].at[j])

    def start(self, ins, outs, sems):
        for j in range(len(self.arrays)):
            self._local(ins, outs, sems, j).start()
            for rel in range(1, NDEV):
                self._copy(ins, outs, sems, j, rel).start()

    def forward(self, ins, outs, sems):
        pass

    def finish(self, ins, outs, sems):
        for j in range(len(self.arrays)):
            for rel in range(1, NDEV):
                self._arrival(outs, sems, j, rel).wait_recv()
        for j in range(len(self.arrays)):
            for rel in range(1, NDEV):
                self._copy(ins, outs, sems, j, rel).wait_send()
            self._local(ins, outs, sems, j).wait()


class _Together:
    def __init__(self, parts):
        self.parts = [c for c in parts if c is not None]
        self.arrays = [a for c in self.parts for a in c.arrays]

    def out_shape(self):
        return [s for c in self.parts for s in c.out_shape()]

    def scratch(self):
        return [s for c in self.parts for s in c.scratch()]

    def _each(self, method, ins, outs, sems):
        at = 0
        for q, c in enumerate(self.parts):
            n = len(c.arrays)
            getattr(c, method)(ins[at:at + n], outs[at:at + n], sems[3 * q:3 * q + 3])
            at += n

    def start(self, ins, outs, sems):
        self._each("start", ins, outs, sems)

    def forward(self, ins, outs, sems):
        self._each("forward", ins, outs, sems)

    def finish(self, ins, outs, sems):
        self._each("finish", ins, outs, sems)


def _together(*parts):
    parts = [c for c in parts if c is not None]
    return _Together(parts) if parts else None


def _call(body, *, name, args, in_specs, out_specs, out_shape, grid=(), scratch_shapes=(), semantics=(),
          aliases=None, comm=None):
    in_specs, out_specs, out_shape = list(in_specs), list(out_specs), list(out_shape)
    scratch_shapes = list(scratch_shapes)
    if comm is None:
        res = pl.pallas_call(
            body, name=name, grid=grid, in_specs=in_specs, out_specs=out_specs, out_shape=out_shape,
            scratch_shapes=scratch_shapes, input_output_aliases=aliases or {},
            compiler_params=_params(*semantics))(*args)
        return list(res), []
    n_in, n_out, n_scr, nc = len(in_specs), len(out_specs), len(scratch_shapes), len(comm.arrays)
    total = math.prod(grid)
    middle = min((total * 5) // 8, total - 1)

    def hosted(*refs):
        ins, cins = refs[:n_in], refs[n_in:n_in + nc]
        o0 = n_in + nc
        outs, couts = refs[o0:o0 + n_out], refs[o0 + n_out:o0 + n_out + nc]
        s0 = o0 + n_out + nc
        scr, sems = refs[s0:s0 + n_scr], refs[s0 + n_scr:]
        if total == 1:
            comm.start(cins, couts, sems)
            body(*ins, *outs, *scr)
            comm.forward(cins, couts, sems)
            comm.finish(cins, couts, sems)
            return
        step = 0
        for axis, size in enumerate(grid):
            step = step * size + pl.program_id(axis)
        pl.when(step == 0)(lambda: comm.start(cins, couts, sems))
        body(*ins, *outs, *scr)
        pl.when(step == middle)(lambda: comm.forward(cins, couts, sems))
        pl.when(step == total - 1)(lambda: comm.finish(cins, couts, sems))

    any_spec = pl.BlockSpec(memory_space=pl.ANY)
    res = pl.pallas_call(
        hosted, name=name, grid=grid,
        in_specs=in_specs + [any_spec] * nc, out_specs=out_specs + [any_spec] * nc,
        out_shape=out_shape + comm.out_shape(), scratch_shapes=scratch_shapes + comm.scratch(),
        input_output_aliases=aliases or {}, compiler_params=_params(*(("arbitrary",) * len(grid))),
    )(*args, *comm.arrays)
    return list(res[:n_out]), list(res[n_out:])


def _exchange_alone(comm, name):
    return _call(lambda: None, name=name, args=(), in_specs=(), out_specs=(), out_shape=(), comm=comm)[1]


def _cast_pad(w, rows, cols, name):
    nl, r, c = w.shape

    def body(w_ref, o_ref):
        if (rows, cols) != (r, c):
            o_ref[...] = jnp.zeros(o_ref.shape, BF16)
        o_ref[0, 0:r, 0:c] = w_ref[0].astype(BF16)

    return _call(body, name=name, grid=(nl,), args=(w,),
                 in_specs=[pl.BlockSpec((1, r, c), lambda i: (i, 0, 0))],
                 out_specs=[pl.BlockSpec((1, rows, cols), lambda i: (i, 0, 0))],
                 out_shape=[_sds((nl, rows, cols), BF16)], semantics=("parallel",))[0][0]


def _in_proj(x, gain, w, tm, name, comm):
    t, d = x.shape
    nb, _, bw = w.shape

    def body(x_ref, g_ref, w_ref, h_ref, p_ref):
        xv = x_ref[...]
        r = lax.rsqrt(jnp.mean(xv * xv, axis=-1, keepdims=True) + EPS)
        h = (xv * r * g_ref[...]).astype(BF16)
        h_ref[...] = h
        for k in range(nb):
            p_ref[:, k * bw:(k + 1) * bw] = _nn(h, w_ref[k]).astype(BF16)

    return _call(body, name=name, grid=(t // tm,), args=(x, gain, w),
                 in_specs=[_tile(tm, d), _row(d), _weight(w)],
                 out_specs=[_tile(tm, d), _tile(tm, nb * bw)],
                 out_shape=[_sds((t, d), BF16), _sds((t, nb * bw), BF16)],
                 semantics=("parallel",), comm=comm)


def _halo_specs(tm, t, d, col):
    nh, nhb = tm // HALO, t // HALO
    prev = pl.BlockSpec((HALO, d), lambda i: (jnp.maximum(i * nh - 1, 0), col))
    nxt = pl.BlockSpec((HALO, d), lambda i: (jnp.minimum((i + 1) * nh, nhb - 1), col))
    return prev, nxt


def _shifted(buf_ref, kb, r0, off):
    return buf_ref[kb, pl.ds(r0 + off, CONV_ROWS), :]


def _dwconv(buf_ref, w_ref, out_ref, tm, flip):
    nblk = out_ref.shape[1] // LANES

    def rows(j, carry):
        r0 = pl.multiple_of(j * CONV_ROWS, CONV_ROWS)
        for kb in range(nblk):
            acc = jnp.zeros((CONV_ROWS, LANES), F32)
            for k in range(CONV_TAPS):
                off = (CONV_TAPS - k) if flip else (1 + k)
                acc = acc + w_ref[kb, k:k + 1, :] * _shifted(buf_ref, kb, r0, off)
            out_ref[pl.ds(r0, CONV_ROWS), kb * LANES:(kb + 1) * LANES] = acc
        return carry

    lax.fori_loop(0, tm // CONV_ROWS, rows, 0)


def _fill_halo_buffer(buf, prev, body, nxt, first, last, tm):
    prev = jnp.where(first, 0.0, prev)
    nxt = jnp.where(last, 0.0, nxt)
    for kb in range(buf.shape[0]):
        lanes = slice(kb * LANES, (kb + 1) * LANES)
        buf[kb, 0:HALO, :] = prev[:, lanes]
        buf[kb, HALO:HALO + tm, :] = body[:, lanes]
        buf[kb, HALO + tm:HALO + tm + HALO, :] = nxt[:, lanes]


def _fill_glu_buffer(cbuf, av, ag, avp, agp, avn, agn, first, last, tm):
    c0p = avp[...].astype(F32) * _sig(agp[...].astype(F32))
    c0n = avn[...].astype(F32) * _sig(agn[...].astype(F32))
    c0 = av[...].astype(F32) * _sig(ag[...].astype(F32))
    _fill_halo_buffer(cbuf, c0p, c0, c0n, first, last, tm)


def _layernorm_stats(v):
    mu = jnp.mean(v, axis=-1, keepdims=True)
    cen = v - mu
    rstd = lax.rsqrt(jnp.mean(cen * cen, axis=-1, keepdims=True) + EPS)
    return cen * rstd, rstd


def _spatial_mix(ws_ref, vn_ref, bias_ref, mixed_ref, tm):
    for ci in range(tm // CHUNK):
        rs = slice(ci * CHUNK, (ci + 1) * CHUNK)
        for g in range(GROUPS):
            ls = slice(g * LANES, (g + 1) * LANES)
            mixed_ref[rs, ls] = _nn(ws_ref[g], vn_ref[rs, ls]) + bias_ref[:, ls]


def _mixer_fwd(proj, x, w, small, seq, tm, name, comm):
    t, d = x.shape
    hp_v, hn_v = _halo_specs(tm, t, d, 0)
    hp_g, hn_g = _halo_specs(tm, t, d, 1)

    def body(av, ag, u_ref, v_ref, ga_ref, gb_ref, avp, agp, avn, agn, x_ref,
             cw_ref, cb_ref, lg_ref, lb_ref, wco_ref, sg_ref, sb_ref, ws_ref, bias_ref, wso_ref,
             gba_ref, gbb_ref, wo_ref,
             c1_ref, ya_ref, yb_ref, x1_ref, cbuf, c1f, vn_ref, mixed_ref):
        i = pl.program_id(0)
        first = (i * tm) % seq == 0
        last = ((i + 1) * tm) % seq == 0
        _fill_glu_buffer(cbuf, av, ag, avp, agp, avn, agn, first, last, tm)
        _dwconv(cbuf, cw_ref, c1f, tm, flip=False)
        c1 = c1f[...] + cb_ref[...]
        c1_ref[...] = c1.astype(BF16)
        c2hat, _ = _layernorm_stats(c1)
        c2 = c2hat * lg_ref[...] + lb_ref[...]
        c3 = (c2 * _sig(c2)).astype(BF16)
        ya = _nn(c3, wco_ref[...].reshape(d, d))
        ya_ref[...] = ya.astype(BF16)
        vhat, _ = _layernorm_stats(v_ref[...].astype(F32))
        vn_ref[...] = (vhat * sg_ref[...] + sb_ref[...]).astype(BF16)
        _spatial_mix(ws_ref, vn_ref, bias_ref, mixed_ref, tm)
        gated = (u_ref[...].astype(F32) * mixed_ref[...]).astype(BF16)
        yb = _nn(gated, wso_ref[...].reshape(d, d))
        yb_ref[...] = yb.astype(BF16)
        sa = _sig(ga_ref[...].astype(F32) + gba_ref[...])
        sb = _sig(gb_ref[...].astype(F32) + gbb_ref[...])
        merged = (sa * ya + sb * yb).astype(BF16)
        x1_ref[...] = x_ref[...] + _nn(merged, wo_ref[...].reshape(d, d))

    cols = [_tile(tm, d, j) for j in range(6)]
    return _call(
        body, name=name, grid=(t // tm,),
        args=(proj,) * 10 + (x, w['conv_w'], small['conv_b'], small['conv_ln_g'], small['conv_ln_b'], w['w_conv_out'],
                             small['sgu_ln_g'], small['sgu_ln_b'], small['ws'], small['bias_full'], w['w_sgu_out'],
                             small['gba'], small['gbb'], w['w_o']),
        in_specs=cols + [hp_v, hp_g, hn_v, hn_g, _tile(tm, d),
                         _weight(w['conv_w']), _row(d), _row(d), _row(d),
                         _weight(w['w_conv_out']), _row(d), _row(d),
                         _resident(small['ws'].shape), _resident(small['bias_full'].shape),
                         _weight(w['w_sgu_out']), _row(d), _row(d), _weight(w['w_o'])],
        out_specs=[_tile(tm, d)] * 4,
        out_shape=[_sds((t, d), BF16)] * 3 + [_sds((t, d), F32)],
        scratch_shapes=[pltpu.VMEM((d // LANES, tm + 2 * HALO, LANES), F32), pltpu.VMEM((tm, d), F32),
                        pltpu.VMEM((tm, d), BF16), pltpu.VMEM((tm, d), F32)],
        semantics=("parallel",), comm=comm)


def _pair_gate_up(wg, wu, name):
    n, d, c = wg.shape

    def body(wg_ref, wu_ref, o_ref):
        o_ref[:, 0:c] = wg_ref[0]
        o_ref[:, c:2 * c] = wg_ref[1]
        o_ref[:, 2 * c:3 * c] = wu_ref[0]
        o_ref[:, 3 * c:4 * c] = wu_ref[1]

    pair = pl.BlockSpec((2, d, c), lambda k: (k, 0, 0))
    return _call(body, name=name, grid=(n // 2,), args=(wg, wu), in_specs=[pair, pair],
                 out_specs=[pl.BlockSpec((None, d, 4 * c), lambda k: (k, 0, 0))],
                 out_shape=[_sds((n // 2, d, 4 * c), wg.dtype)], semantics=("parallel",))[0][0]


def _pair_specs(w, tm):
    d = w['w_gate_up'].shape[1]
    up = pl.BlockSpec((None, d, 2 * FF_PAIR), lambda i, k: (k, 0, 0))
    down = pl.BlockSpec((None, FF_PAIR, d), lambda i, k: (k, 0, 0))
    return [up, down]


def _ffn_fwd(x1, gain, w, tm, name, comm):
    t, d = x1.shape
    tm = min(tm, t)
    npair = NDEV // 2
    hid = NDEV * FF_PAD
    wd_pairs = w['w_ffn_down'].reshape(npair, FF_PAIR, d)

    def body(x_ref, g_ref, wgu_ref, wd_ref, h_ref, gg_ref, uu_ref, x2_ref, hb_ref, acc_ref):
        k = pl.program_id(1)

        @pl.when(k == 0)
        def _():
            xv = x_ref[...]
            r = lax.rsqrt(jnp.mean(xv * xv, axis=-1, keepdims=True) + EPS)
            h = (xv * r * g_ref[...]).astype(BF16)
            hb_ref[...] = h
            h_ref[...] = h
            acc_ref[...] = xv

        gu = _nn(hb_ref[...], wgu_ref[...])
        gk = gu[:, 0:FF_PAIR]
        uk = gu[:, FF_PAIR:2 * FF_PAIR]
        gg_ref[...] = gk.astype(BF16)
        uu_ref[...] = uk.astype(BF16)
        ak = (gk * _sig(gk) * uk).astype(BF16)
        acc_ref[...] += _nn(ak, wd_ref[...])

        @pl.when(k == npair - 1)
        def _():
            x2_ref[...] = acc_ref[...]

    pair_cols = pl.BlockSpec((tm, FF_PAIR), lambda i, k: (i, k))
    return _call(
        body, name=name, grid=(t // tm, npair),
        args=(x1, gain, w['w_gate_up'], wd_pairs),
        in_specs=[_tile(tm, d), _row(d)] + _pair_specs(w, tm),
        out_specs=[_tile(tm, d), pair_cols, pair_cols, _tile(tm, d)],
        out_shape=[_sds((t, d), BF16), _sds((t, hid), BF16), _sds((t, hid), BF16), _sds((t, d), F32)],
        scratch_shapes=[pltpu.VMEM((tm, d), BF16), pltpu.VMEM((tm, d), F32)],
        semantics=("parallel", "arbitrary"), comm=comm)


def _init_small(first, small_ref, acc_ref):
    @pl.when(first)
    def _():
        small_ref[...] = jnp.zeros(small_ref.shape, F32)
        acc_ref[...] = jnp.zeros(acc_ref.shape, F32)


def _finish_small(last, small_ref, acc_ref, nq):
    @pl.when(last)
    def _():
        for q in range(nq):
            small_ref[q:q + 1, :] = jnp.sum(acc_ref[q], axis=0, keepdims=True)


def _loss_bwd(x, gain, target, tm, name):
    t, d = x.shape
    nsteps = t // tm

    def body(x_ref, g_ref, t_ref, dx_ref, small_ref, acc_ref):
        i = pl.program_id(0)
        _init_small(i == 0, small_ref, acc_ref)
        xv = x_ref[...]
        r = lax.rsqrt(jnp.mean(xv * xv, axis=-1, keepdims=True) + EPS)
        xhat = xv * r
        diff = xhat * g_ref[...] - t_ref[...]
        dy = diff * (1.0 / d)
        acc_ref[0] += _fold(dy * xhat)
        acc_ref[1] += _fold(diff * diff)
        dxhat = dy * g_ref[...]
        dx_ref[...] = r * (dxhat - xhat * jnp.mean(dxhat * xhat, axis=-1, keepdims=True))

        @pl.when(i == nsteps - 1)
        def _():
            small_ref[0:1, :] = jnp.sum(acc_ref[0], axis=0, keepdims=True)
            total = jnp.sum(acc_ref[1]) * (0.5 / d)
            small_ref[1:2, :] = jnp.full((1, d), total, F32)

    return _call(body, name=name, grid=(nsteps,), args=(x, gain, target),
                 in_specs=[_tile(tm, d), _row(d), _tile(tm, d)],
                 out_specs=[_tile(tm, d), _resident((SUBLANES, d))],
                 out_shape=[_sds((t, d), F32), _sds((SUBLANES, d), F32)],
                 scratch_shapes=[pltpu.VMEM((2, SUBLANES, d), F32)], semantics=("arbitrary",))[0]


def _ffn_bwd(dx2, x1, gain, gg, uu, w, tm, name, comm):
    t, d = x1.shape
    npair = NDEV // 2
    hid = NDEV * FF_PAD
    nsteps = t // tm
    wd_pairs = w['w_ffn_down'].reshape(npair, FF_PAIR, d)

    def body(dx_ref, x_ref, g_ref, gg_ref, uu_ref, wgu_ref, wd_ref,
             a_ref, dg_ref, du_ref, dx1_ref, small_ref, acc_ref, dxb_ref, dh_ref, dgu_ref):
        i, k = pl.program_id(0), pl.program_id(1)
        _init_small((i == 0) & (k == 0), small_ref, acc_ref)

        @pl.when(k == 0)
        def _():
            dxb_ref[...] = dx_ref[...].astype(BF16)
            dh_ref[...] = jnp.zeros(dh_ref.shape, F32)

        gk = gg_ref[...].astype(F32)
        uk = uu_ref[...].astype(F32)
        sg = _sig(gk)
        silu = gk * sg
        a_ref[...] = (silu * uk).astype(BF16)
        da = _nt(dxb_ref[...], wd_ref[...])
        dgk = (da * uk * (sg * (1.0 + gk * (1.0 - sg)))).astype(BF16)
        duk = (da * silu).astype(BF16)
        dg_ref[...] = dgk
        du_ref[...] = duk
        dgu_ref[:, 0:FF_PAIR] = dgk
        dgu_ref[:, FF_PAIR:2 * FF_PAIR] = duk
        dh_ref[...] += _nt(dgu_ref[...], wgu_ref[...])

        @pl.when(k == npair - 1)
        def _():
            xv = x_ref[...]
            dh = dh_ref[...]
            r = lax.rsqrt(jnp.mean(xv * xv, axis=-1, keepdims=True) + EPS)
            xhat = xv * r
            acc_ref[0] += _fold(dh * xhat)
            dxhat = dh * g_ref[...]
            dx1_ref[...] = dx_ref[...] + r * (dxhat - xhat * jnp.mean(dxhat * xhat, axis=-1, keepdims=True))

        _finish_small((i == nsteps - 1) & (k == npair - 1), small_ref, acc_ref, 1)

    pair_cols = pl.BlockSpec((tm, FF_PAIR), lambda i, k: (i, k))
    return _call(
        body, name=name, grid=(nsteps, npair),
        args=(dx2, x1, gain, gg, uu, w['w_gate_up'], wd_pairs),
        in_specs=[_tile(tm, d), _tile(tm, d), _row(d), pair_cols, pair_cols] + _pair_specs(w, tm),
        out_specs=[pair_cols] * 3 + [_tile(tm, d), _resident((SUBLANES, d))],
        out_shape=[_sds((t, hid), BF16)] * 3 + [_sds((t, d), F32), _sds((SUBLANES, d), F32)],
        scratch_shapes=[pltpu.VMEM((1, SUBLANES, d), F32), pltpu.VMEM((tm, d), BF16), pltpu.VMEM((tm, d), F32),
                        pltpu.VMEM((tm, 2 * FF_PAIR), BF16)],
        semantics=("arbitrary", "arbitrary"), comm=comm)


def _matmul_tn(a, b, a_blk, b_blk, stack, shard, tm, name, comm):
    t, ma = a.shape
    tm = min(tm, t)
    nb_ = b.shape[1]
    na, nb = ma // a_blk, nb_ // b_blk
    nsteps = t // tm
    cw = min(TN_COLS, b_blk)
    if stack == 'b':
        per = b_blk // shard
        out_shape, out_spec = (nb_ // shard, ma, shard), pl.BlockSpec((per, a_blk, shard), lambda i, j, k: (j, 0, 0))
    elif stack == 'a':
        per = a_blk // shard
        out_shape, out_spec = (ma // shard, shard, nb_), pl.BlockSpec((per, shard, b_blk), lambda i, j, k: (i, 0, 0))
    else:
        out_shape, out_spec = (ma, nb_), pl.BlockSpec((a_blk, b_blk), lambda i, j, k: (i, j))

    def body(a_ref, b_ref, o_ref, acc_ref):
        k = pl.program_id(2)

        @pl.when(k == 0)
        def _():
            acc_ref[...] = jnp.zeros(acc_ref.shape, F32)

        av = a_ref[...].astype(BF16)
        for c in range(0, b_blk, cw):
            acc_ref[:, c:c + cw] += _tn(av, b_ref[:, c:c + cw].astype(BF16))

        @pl.when(k == nsteps - 1)
        def _():
            if stack == 'b':
                for s in range(per):
                    o_ref[s] = acc_ref[:, s * shard:(s + 1) * shard].astype(BF16)
            elif stack == 'a':
                for s in range(per):
                    o_ref[s] = acc_ref[s * shard:(s + 1) * shard, :].astype(BF16)
            else:
                o_ref[...] = acc_ref[...].astype(BF16)

    res, got = _call(
        body, name=name, grid=(na, nb, nsteps), args=(a, b),
        in_specs=[pl.BlockSpec((tm, a_blk), lambda i, j, k: (k, i)), pl.BlockSpec((tm, b_blk), lambda i, j, k: (k, j))],
        out_specs=[out_spec], out_shape=[_sds(out_shape, BF16)],
        scratch_shapes=[pltpu.VMEM((a_blk, b_blk), F32)],
        semantics=("parallel", "parallel", "arbitrary"), comm=comm)
    return res[0], got


def _mixer_bwd(dx1, proj, c1, ya, yb, w, small, tm, name, comm):
    t, d = dx1.shape
    nsteps = t // tm
    nq = 6

    def body(dx_ref, u_ref, v_ref, ga_ref, gb_ref, c1_ref, ya_ref, yb_ref,
             lg_ref, lb_ref, wco_ref, sg_ref, sb_ref, ws_ref, wst_ref, bias_ref, wso_ref, gba_ref, gbb_ref, wo_ref,
             sel_ref,
             dp_ref, dc1_ref, mg_ref, c3_ref, gt_ref, dya_ref, dyb_ref, small_ref, dws_ref, dbs_ref,
             acc_ref, vn_ref, mixed_ref, dmix_ref, dvn_ref, dbias_ref):
        i = pl.program_id(0)
        _init_small(i == 0, small_ref, acc_ref)

        @pl.when(i == 0)
        def _():
            dws_ref[...] = jnp.zeros(dws_ref.shape, F32)
            dbs_ref[...] = jnp.zeros(dbs_ref.shape, F32)
            dbias_ref[...] = jnp.zeros(dbias_ref.shape, F32)

        dmerged = _nt(dx_ref[...].astype(BF16), wo_ref[...].reshape(d, d))
        ya = ya_ref[...].astype(F32)
        yb = yb_ref[...].astype(F32)
        sa = _sig(ga_ref[...].astype(F32) + gba_ref[...])
        sb = _sig(gb_ref[...].astype(F32) + gbb_ref[...])
        mg_ref[...] = (sa * ya + sb * yb).astype(BF16)
        dya = (dmerged * sa).astype(BF16)
        dyb = (dmerged * sb).astype(BF16)
        dya_ref[...] = dya
        dyb_ref[...] = dyb
        dga = dmerged * ya * (sa * (1.0 - sa))
        dgb = dmerged * yb * (sb * (1.0 - sb))
        acc_ref[0] += _fold(dga)
        acc_ref[1] += _fold(dgb)
        dp_ref[:, 0:2 * d] = jnp.zeros((tm, 2 * d), BF16)
        dp_ref[:, 4 * d:5 * d] = dga.astype(BF16)
        dp_ref[:, 5 * d:6 * d] = dgb.astype(BF16)
        c2hat, rstd = _layernorm_stats(c1_ref[...].astype(F32))
        c2 = c2hat * lg_ref[...] + lb_ref[...]
        s2 = _sig(c2)
        c3_ref[...] = (c2 * s2).astype(BF16)
        dc3 = _nt(dya, wco_ref[...].reshape(d, d))
        dc2 = dc3 * (s2 * (1.0 + c2 * (1.0 - s2)))
        acc_ref[2] += _fold(dc2 * c2hat)
        acc_ref[3] += _fold(dc2)
        dc2hat = dc2 * lg_ref[...]
        dc1_ref[...] = (rstd * (dc2hat - jnp.mean(dc2hat, axis=-1, keepdims=True)
                                - c2hat * jnp.mean(dc2hat * c2hat, axis=-1, keepdims=True))).astype(BF16)
        vhat, rstd_v = _layernorm_stats(v_ref[...].astype(F32))
        vn_ref[...] = (vhat * sg_ref[...] + sb_ref[...]).astype(BF16)
        _spatial_mix(ws_ref, vn_ref, bias_ref, mixed_ref, tm)
        u = u_ref[...].astype(F32)
        mixed = mixed_ref[...]
        gt_ref[...] = (u * mixed).astype(BF16)
        dgated = _nt(dyb, wso_ref[...].reshape(d, d))
        dp_ref[:, 2 * d:3 * d] = (dgated * mixed).astype(BF16)
        dmix_ref[...] = dgated * u
        for ci in range(tm // CHUNK):
            rs = slice(ci * CHUNK, (ci + 1) * CHUNK)
            dbias_ref[...] += dmix_ref[rs, :]
            for g in range(GROUPS):
                ls = slice(g * LANES, (g + 1) * LANES)
                dm = dmix_ref[rs, ls].astype(BF16)
                dws_ref[g] += _nt(dm, vn_ref[rs, ls])
                dvn_ref[rs, ls] = _nn(wst_ref[g], dm)
        dvn = dvn_ref[...]
        acc_ref[4] += _fold(dvn * vhat)
        acc_ref[5] += _fold(dvn)
        dvhat = dvn * sg_ref[...]
        dp_ref[:, 3 * d:4 * d] = (rstd_v * (dvhat - jnp.mean(dvhat, axis=-1, keepdims=True)
                                           - vhat * jnp.mean(dvhat * vhat, axis=-1, keepdims=True))).astype(BF16)
        _finish_small(i == nsteps - 1, small_ref, acc_ref, nq)

        @pl.when(i == nsteps - 1)
        def _():
            db = dbias_ref[...]
            hi = db.astype(BF16)
            lo = (db - hi.astype(F32)).astype(BF16)
            dbs_ref[...] = _nt(sel_ref[...], hi) + _nt(sel_ref[...], lo)

    cols = [_tile(tm, d, j) for j in (2, 3, 4, 5)]
    return _call(
        body, name=name, grid=(nsteps,),
        args=(dx1, proj, proj, proj, proj, c1, ya, yb,
              small['conv_ln_g'], small['conv_ln_b'], w['w_conv_out'], small['sgu_ln_g'], small['sgu_ln_b'],
              small['ws'], small['wst'], small['bias_full'], w['w_sgu_out'], small['gba'], small['gbb'], w['w_o'],
              small['group_sel']),
        in_specs=[_tile(tm, d)] + cols + [_tile(tm, d)] * 3 + [
            _row(d), _row(d), _weight(w['w_conv_out']),
            _row(d), _row(d), _resident(small['ws'].shape), _resident(small['wst'].shape),
            _resident(small['bias_full'].shape), _weight(w['w_sgu_out']), _row(d), _row(d),
            _weight(w['w_o']), _resident(small['group_sel'].shape)],
        out_specs=[_tile(tm, 6 * d)] + [_tile(tm, d)] * 6 + [
            _resident((SUBLANES, d)), _resident((GROUPS, CHUNK, CHUNK)), _resident((GROUPS, CHUNK))],
        out_shape=[_sds((t, 6 * d), BF16)] + [_sds((t, d), BF16)] * 6 + [
            _sds((SUBLANES, d), F32), _sds((GROUPS, CHUNK, CHUNK), F32), _sds((GROUPS, CHUNK), F32)],
        scratch_shapes=[pltpu.VMEM((nq, SUBLANES, d), F32), pltpu.VMEM((tm, d), BF16), pltpu.VMEM((tm, d), F32),
                        pltpu.VMEM((tm, d), F32), pltpu.VMEM((tm, d), F32), pltpu.VMEM((CHUNK, d), F32)],
        semantics=("arbitrary",), comm=comm)


def _conv_bwd(dproj, dc1, proj, w, seq, tm, name, comm):
    t, d = dc1.shape
    nsteps = t // tm
    nblk = d // LANES
    hp_v, hn_v = _halo_specs(tm, t, d, 0)
    hp_g, hn_g = _halo_specs(tm, t, d, 1)
    hp_d, hn_d = _halo_specs(tm, t, d, 0)

    def body(dp_in, dc_ref, dcp, dcn, av, ag, avp, agp, avn, agn, cw_ref,
             dp_ref, dcw_ref, small_ref, acc_ref, cbuf, dbuf, dc0f, accw):
        del dp_in
        i = pl.program_id(0)
        _init_small(i == 0, small_ref, acc_ref)

        @pl.when(i == 0)
        def _():
            accw[...] = jnp.zeros(accw.shape, F32)
            dcw_ref[...] = jnp.zeros(dcw_ref.shape, F32)

        first = (i * tm) % seq == 0
        last = ((i + 1) * tm) % seq == 0
        _fill_glu_buffer(cbuf, av, ag, avp, agp, avn, agn, first, last, tm)
        dc1v = dc_ref[...].astype(F32)
        _fill_halo_buffer(dbuf, dcp[...].astype(F32), dc1v, dcn[...].astype(F32), first, last, tm)
        acc_ref[0] += _fold(dc1v)
        _dwconv(dbuf, cw_ref, dc0f, tm, flip=True)

        def rows(j, carry):
            r0 = pl.multiple_of(j * CONV_ROWS, CONV_ROWS)
            for kb in range(nblk):
                dv = dbuf[kb, pl.ds(r0 + HALO, CONV_ROWS), :]
                for k in range(CONV_TAPS):
                    accw[kb, k] += _fold(dv * _shifted(cbuf, kb, r0, 1 + k))
            return carry

        lax.fori_loop(0, tm // CONV_ROWS, rows, 0)
        sg = _sig(ag[...].astype(F32))
        avv = av[...].astype(F32)
        dc0 = dc0f[...]
        dp_ref[:, 0:d] = (dc0 * sg).astype(BF16)
        dp_ref[:, d:2 * d] = (dc0 * avv * (sg * (1.0 - sg))).astype(BF16)
        _finish_small(i == nsteps - 1, small_ref, acc_ref, 1)

        @pl.when(i == nsteps - 1)
        def _():
            for kb in range(nblk):
                dcw_ref[kb] = jnp.sum(accw[kb], axis=1)

    return _call(
        body, name=name, grid=(nsteps,),
        args=(dproj, dc1, dc1, dc1, proj, proj, proj, proj, proj, proj, w['conv_w']),
        in_specs=[pl.BlockSpec(memory_space=pl.ANY), _tile(tm, d), hp_d, hn_d, _tile(tm, d, 0), _tile(tm, d, 1),
                  hp_v, hp_g, hn_v, hn_g, _weight(w['conv_w'])],
        out_specs=[_tile(tm, 2 * d), _resident((nblk, CONV_TAPS_PADDED, LANES)), _resident((SUBLANES, d))],
        out_shape=[_sds(dproj.shape, BF16), _sds((nblk, CONV_TAPS_PADDED, LANES), F32), _sds((SUBLANES, d), F32)],
        scratch_shapes=[pltpu.VMEM((1, SUBLANES, d), F32), pltpu.VMEM((nblk, tm + 2 * HALO, LANES), F32),
                        pltpu.VMEM((nblk, tm + 2 * HALO, LANES), F32), pltpu.VMEM((tm, d), F32),
                        pltpu.VMEM((nblk, CONV_TAPS_PADDED, SUBLANES, LANES), F32)],
        aliases={0: 0}, semantics=("arbitrary",), comm=comm)


def _in_proj_bwd(dproj, x, dx1, gain, w, tm, name, comm):
    t, d = x.shape
    nb, _, bw = w.shape
    nsteps = t // tm

    def body(dp_ref, x_ref, dx1_ref, g_ref, w_ref, dx_ref, small_ref, acc_ref):
        i = pl.program_id(0)
        _init_small(i == 0, small_ref, acc_ref)
        dh = jnp.zeros((tm, d), F32)
        for k in range(nb):
            dh = dh + _nt(dp_ref[:, k * bw:(k + 1) * bw], w_ref[k])
        xv = x_ref[...]
        r = lax.rsqrt(jnp.mean(xv * xv, axis=-1, keepdims=True) + EPS)
        xhat = xv * r
        acc_ref[0] += _fold(dh * xhat)
        dxhat = dh * g_ref[...]
        dx_ref[...] = dx1_ref[...] + r * (dxhat - xhat * jnp.mean(dxhat * xhat, axis=-1, keepdims=True))
        _finish_small(i == nsteps - 1, small_ref, acc_ref, 1)

    return _call(body, name=name, grid=(nsteps,), args=(dproj, x, dx1, gain, w),
                 in_specs=[_tile(tm, nb * bw), _tile(tm, d), _tile(tm, d), _row(d), _weight(w)],
                 out_specs=[_tile(tm, d), _resident((SUBLANES, d))],
                 out_shape=[_sds((t, d), F32), _sds((SUBLANES, d), F32)],
                 scratch_shapes=[pltpu.VMEM((1, SUBLANES, d), F32)], semantics=("arbitrary",), comm=comm)


def _adam(wv, g, mv, vv):
    m = ADAM_B1 * mv + (1.0 - ADAM_B1) * g
    v = ADAM_B2 * vv + (1.0 - ADAM_B2) * jnp.square(g)
    m_hat = m / (1.0 - ADAM_B1 ** ADAM_STEP)
    v_hat = v / (1.0 - ADAM_B2 ** ADAM_STEP)
    delta = -ADAM_LR * (m_hat / (jnp.sqrt(v_hat) + ADAM_EPS) + ADAM_WD * wv)
    return delta, m, v


def _adamw_layer(layer, w, m, v, parts, prev, nsplit, name):
    nl, r, c = w.shape
    npart, pr, pc = parts.shape
    rt, prt = r // nsplit, pr // nsplit

    def body(w_ref, m_ref, v_ref, p_ref, *rest):
        g_ref, d_ref, nm_ref, nv_ref = rest[-4:]
        g = p_ref[0, 0:rt, 0:c].astype(F32)
        for s in range(1, npart):
            g = g + p_ref[s, 0:rt, 0:c].astype(F32)
        delta, mn, vn = _adam(w_ref[0], g, m_ref[0], v_ref[0])
        g_ref[0] = g
        d_ref[0] = delta
        nm_ref[0] = mn
        nv_ref[0] = vn

    wspec = pl.BlockSpec((1, rt, c), lambda i: (layer, i, 0))
    pspec = pl.BlockSpec((npart, prt, pc), lambda i: (0, i, 0))
    in_specs = [wspec, wspec, wspec, pspec]
    args = [w, m, v, parts]
    aliases = {}
    if prev is not None:
        in_specs += [pl.BlockSpec(memory_space=pl.ANY)] * 4
        args += list(prev)
        aliases = {4 + q: q for q in range(4)}
    return _call(body, name=name, grid=(nsplit,), args=args, in_specs=in_specs, out_specs=[wspec] * 4,
                 out_shape=[_sds(w.shape, F32)] * 4, aliases=aliases, semantics=("parallel",))[0]


VEC_ROWS = {'norm_mix': 0, 'gate_bias': (SUBLANES, SUBLANES + 1), 'conv_ln_g': SUBLANES + 2, 'conv_ln_b': SUBLANES + 3,
            'sgu_ln_g': SUBLANES + 4, 'sgu_ln_b': SUBLANES + 5, 'conv_b': 2 * SUBLANES, 'norm_ffn': 3 * SUBLANES}
VEC_ROWS_PER_LAYER = 4 * SUBLANES


def _adamw_small(gathered, params, moments_m, moments_v):
    names = list(params)
    nper = len(names)
    nl = len(gathered)

    def body(*refs):
        g_refs = [refs[3 * l:3 * l + 3] for l in range(nl)]
        rest = refs[3 * nl:]
        w_refs = dict(zip(names, rest[0:nper]))
        m_refs = dict(zip(names, rest[nper:2 * nper]))
        v_refs = dict(zip(names, rest[2 * nper:3 * nper]))
        outs = rest[3 * nper:]
        o = {kind: dict(zip(names, outs[q * nper:(q + 1) * nper])) for q, kind in enumerate("gdmv")}

        def put(nm, idx, g):
            delta, mn, vn = _adam(w_refs[nm][idx], g, m_refs[nm][idx], v_refs[nm][idx])
            o["g"][nm][idx] = g
            o["d"][nm][idx] = delta
            o["m"][nm][idx] = mn
            o["v"][nm][idx] = vn

        def total(ref, *idx):
            g = ref[(0, *idx)]
            for s in range(1, NDEV):
                g = g + ref[(s, *idx)]
            return g

        for l, (vec_ref, dws_ref, dbs_ref) in enumerate(g_refs):
            dd = vec_ref.shape[2]
            put('w_spatial', (l,), total(dws_ref))
            put('b_spatial', (l,), total(dbs_ref))
            for nm, rr in VEC_ROWS.items():
                if nm == 'gate_bias':
                    put(nm, (slice(l, l + 1), slice(0, dd)), total(vec_ref, slice(rr[0], rr[0] + 1)))
                    put(nm, (slice(l, l + 1), slice(dd, 2 * dd)), total(vec_ref, slice(rr[1], rr[1] + 1)))
                else:
                    put(nm, (slice(l, l + 1), slice(None)), total(vec_ref, slice(rr, rr + 1)))
        last = g_refs[nl - 1][0]
        put('norm_final', (slice(0, 1), slice(None)), total(last, slice(VEC_ROWS_PER_LAYER, VEC_ROWS_PER_LAYER + 1)))

    ins = [a for g in gathered for a in g] + [params[n] for n in names] + [moments_m[n] for n in names] + [moments_v[n] for n in names]
    out_shape = [_sds(params[n].shape, F32) for n in names] * 4
    res = pl.pallas_call(body, name="adamw_small", out_shape=out_shape, compiler_params=_params())(*ins)
    return {kind: dict(zip(names, res[q * nper:(q + 1) * nper])) for q, kind in enumerate("gdmv")}


def _prepare_weights(p):
    d = p['w_in'].shape[1]
    return {
        'w_in': _cast_pad(p['w_in'], d, p['w_in'].shape[2], "cast_w_in"),
        'w_conv_out': _cast_pad(p['w_conv_out'], p['w_conv_out'].shape[1], d, "cast_w_conv_out"),
        'w_sgu_out': _cast_pad(p['w_sgu_out'], p['w_sgu_out'].shape[1], d, "cast_w_sgu_out"),
        'w_o': _cast_pad(p['w_o'], p['w_o'].shape[1], d, "cast_w_o"),
        'w_ffn_gate': _cast_pad(p['w_ffn_gate'], d, FF_PAD, "cast_w_ffn_gate"),
        'w_ffn_up': _cast_pad(p['w_ffn_up'], d, FF_PAD, "cast_w_ffn_up"),
        'w_ffn_down': _cast_pad(p['w_ffn_down'], FF_PAD, d, "cast_w_ffn_down"),
        'conv_w': jnp.pad(p['conv_w'][:, :, 0, :], ((0, 0), (0, CONV_TAPS_PADDED - CONV_TAPS), (0, 0))),
    }


def _gather_of(shards, names, layer):
    return _Gather([shards[n] for n in names], [layer] * len(names))


def _layer_small(p, layer):
    d = p['norm_mix'].shape[1]
    ws = p['w_spatial'][layer]
    rows = {n: p[n][layer:layer + 1] for n in ('norm_mix', 'norm_ffn', 'conv_b', 'conv_ln_g', 'conv_ln_b',
                                               'sgu_ln_g', 'sgu_ln_b')}
    return {
        **rows,
        'ws': ws.astype(BF16), 'wst': jnp.swapaxes(ws, 1, 2).astype(BF16),
        'bias_full': jnp.repeat(p['b_spatial'][layer].T, LANES, axis=1),
        'gba': p['gate_bias'][layer:layer + 1, 0:d], 'gbb': p['gate_bias'][layer:layer + 1, d:2 * d],
        'group_sel': (jnp.arange(d)[None, :] // LANES == jnp.arange(GROUPS)[:, None]).astype(BF16),
    }


class _GradQueue:
    def __init__(self):
        self.pending = []
        self.done = {}

    def push(self, key, array):
        self.pending.append((key, array))

    def take(self):
        keys = [k for k, _ in self.pending]
        comm = _Scatter([a for _, a in self.pending]) if self.pending else None
        self.pending = []
        return keys, comm

    def put(self, keys, arrays):
        self.done.update(zip(keys, arrays))


def _forward_backward(p, shards, x, target, seq):
    nl = p['norm_mix'].shape[0]
    d = x.shape[1]
    smalls = [_layer_small(p, l) for l in range(nl)]
    w_in = _exchange_alone(_gather_of(shards, ['w_in'], 0), "gather_w_in_0")[0]
    saved = []
    for l in range(nl):
        (h, proj), got = _in_proj(x, smalls[l]['norm_mix'], w_in, TILE_IN, f"in_proj_{l}",
                                  _gather_of(shards, MIXER_WEIGHTS, l))
        w = dict(zip(MIXER_WEIGHTS, got), w_in=w_in)
        (c1, ya, yb, x1), got = _mixer_fwd(proj, x, w, smalls[l], seq, TILE_MIX, f"mixer_fwd_{l}",
                                           _gather_of(shards, FFN_WEIGHTS, l))
        w.update(zip(FFN_WEIGHTS, got))
        w['w_gate_up'] = _pair_gate_up(w['w_ffn_gate'], w['w_ffn_up'], f"pair_gate_up_{l}")
        nxt = _gather_of(shards, ['w_in'], l + 1) if l + 1 < nl else None
        (h2, gg, uu, x2), got = _ffn_fwd(x1, smalls[l]['norm_ffn'], w, TILE_FFN_FWD, f"ffn_fwd_{l}", nxt)
        saved.append(dict(x=x, h=h, proj=proj, c1=c1, ya=ya, yb=yb, x1=x1, h2=h2, gg=gg, uu=uu, w=w))
        x = x2
        if got:
            w_in = got[0]
    dx, small_loss = _loss_bwd(x, p['norm_final'][None, :], target, TILE_LOSS, "loss_bwd")
    queue = _GradQueue()
    small_gathered = [None] * nl
    small_pending = None
    rows = d // NDEV
    hid = NDEV * FF_PAD
    for l in reversed(range(nl)):
        s = saved[l]
        w = s['w']

        def hosted(fn, *args, extra=None):
            keys, comm = queue.take()
            res, got = fn(*args, _together(comm, extra))
            queue.put(keys, got[:len(keys)])
            return res, got[len(keys):]

        def tn(key, a, b, a_blk, b_blk, stack, shard, reshape=None, host=False):
            keys, comm = queue.take() if host else ([], None)
            g, got = _matmul_tn(a, b, a_blk, b_blk, stack, shard, TILE_TN, f"dw_{key}_{l}", comm)
            queue.put(keys, got)
            queue.push((l, key), g if reshape is None else g.reshape(reshape))

        extra = small_pending[1] if small_pending else None
        (act, dgg, duu, dx1, small_ffn), got = hosted(_ffn_bwd, dx, s['x1'], smalls[l]['norm_ffn'], s['gg'], s['uu'], w,
                                                      TILE_FFN_BWD, f"ffn_bwd_{l}", extra=extra)
        if small_pending:
            small_gathered[small_pending[0]] = got
            small_pending = None
        tn('w_ffn_gate', s['h2'], dgg, d, hid, 'b', FF_PAD)
        tn('w_ffn_up', s['h2'], duu, d, hid, 'b', FF_PAD)
        (dproj, dc1, merged, c3, gated, dya, dyb, small_mix, dws, dbs), _ = hosted(
            _mixer_bwd, dx1, s['proj'], s['c1'], s['ya'], s['yb'], w, smalls[l], TILE_MIX, f"mixer_bwd_{l}")
        tn('w_ffn_down', act, dx, hid, d, 'a', FF_PAD)
        (dproj, g_conv, small_conv), _ = hosted(_conv_bwd, dproj, dc1, s['proj'], w, seq, TILE_MIX, f"conv_bwd_{l}")
        queue.push((l, 'conv_w'), g_conv)
        tn('w_o', merged, dx1, d, d, None, None, (NDEV, rows, d))
        tn('w_conv_out', c3, dya, d, d, None, None, (NDEV, rows, d))
        tn('w_sgu_out', gated, dyb, d, d, None, None, (NDEV, rows, d))
        tn('w_in', s['h'], dproj, d, hid, 'b', w['w_in'].shape[2], host=True)
        if l == 0:
            (dx, small_in), _ = hosted(_in_proj_bwd, dproj, s['x'], dx1, smalls[l]['norm_mix'], w['w_in'], TILE_IN,
                                       f"in_proj_bwd_{l}")
        else:
            (dx, small_in), _ = _in_proj_bwd(dproj, s['x'], dx1, smalls[l]['norm_mix'], w['w_in'], TILE_IN,
                                             f"in_proj_bwd_{l}", None)
        blocks = [small_in, small_mix, small_conv, small_ffn] + ([small_loss] if l == nl - 1 else [])
        small_pending = (l, _Gather([jnp.concatenate(blocks, axis=0), dws, dbs]))
    keys, comm = queue.take()
    last = _exchange_alone(_together(comm, small_pending[1]), "exchange_last_grads")
    queue.put(keys, last[:len(keys)])
    small_gathered[small_pending[0]] = last[len(keys):]
    return small_gathered, small_loss, dx, queue.done


def _train_step(p, m, v, x3, target3):
    nl = p['norm_mix'].shape[0]
    bsz, seq, d = x3.shape
    x = x3.reshape(bsz * seq, d)
    target = target3.reshape(bsz * seq, d)
    small_gathered, small_loss, dx, exchanged = _forward_backward(p, _prepare_weights(p), x, target, seq)
    loss = lax.psum(small_loss[1, 0], ("x", "y", "c"))

    out = {kind: {} for kind in "gdmv"}
    splits = {'w_in': 4, 'w_conv_out': 1, 'w_sgu_out': 1, 'w_o': 1, 'w_ffn_gate': 4, 'w_ffn_up': 4, 'w_ffn_down': 1, 'conv_w': 1}
    for n in splits:
        if n == 'conv_w':
            pad = ((0, 0), (0, CONV_TAPS_PADDED - CONV_TAPS), (0, 0))
            wl, ml, vl = (jnp.pad(a[n][:, :, 0, :], pad) for a in (p, m, v))
        else:
            wl, ml, vl = p[n], m[n], v[n]
        prev = None
        for l in range(nl):
            prev = _adamw_layer(l, wl, ml, vl, exchanged[(l, n)], prev, splits[n], f"adamw_{n}_{l}")
        for kind, arr in zip("gdmv", prev):
            out[kind][n] = arr[:, 0:CONV_TAPS, None, :] if n == 'conv_w' else arr
    small_names = ['norm_mix', 'gate_bias', 'conv_b', 'conv_ln_g', 'conv_ln_b', 'sgu_ln_g', 'sgu_ln_b', 'w_spatial',
                   'b_spatial', 'norm_ffn', 'norm_final']

    def two_d(a):
        return a[None, :] if a.ndim == 1 else a

    res = _adamw_small(small_gathered, {n: two_d(p[n]) for n in small_names}, {n: two_d(m[n]) for n in small_names},
                       {n: two_d(v[n]) for n in small_names})
    for kind in "gdmv":
        for n in small_names:
            out[kind][n] = res[kind][n].reshape(p[n].shape)
    grad_x = dx.reshape(bsz, seq, d)
    return (loss, grad_x, *[out[kind][n] for kind in "gdmv" for n in WEIGHT_NAMES])


def kernel(x, norm_mix, w_in, gate_bias, conv_w, conv_b, conv_ln_g, conv_ln_b, w_conv_out, sgu_ln_g, sgu_ln_b, w_spatial, b_spatial, w_sgu_out, w_o, norm_ffn, w_ffn_gate, w_ffn_up, w_ffn_down, norm_final, loss_target, m_norm_mix, m_w_in, m_gate_bias, m_conv_w, m_conv_b, m_conv_ln_g, m_conv_ln_b, m_w_conv_out, m_sgu_ln_g, m_sgu_ln_b, m_w_spatial, m_b_spatial, m_w_sgu_out, m_w_o, m_norm_ffn, m_w_ffn_gate, m_w_ffn_up, m_w_ffn_down, m_norm_final, v_norm_mix, v_w_in, v_gate_bias, v_conv_w, v_conv_b, v_conv_ln_g, v_conv_ln_b, v_w_conv_out, v_sgu_ln_g, v_sgu_ln_b, v_w_spatial, v_b_spatial, v_w_sgu_out, v_w_o, v_norm_ffn, v_w_ffn_gate, v_w_ffn_up, v_w_ffn_down, v_norm_final):
    p = dict(zip(WEIGHT_NAMES, (norm_mix, w_in, gate_bias, conv_w, conv_b, conv_ln_g, conv_ln_b, w_conv_out, sgu_ln_g, sgu_ln_b, w_spatial, b_spatial, w_sgu_out, w_o, norm_ffn, w_ffn_gate, w_ffn_up, w_ffn_down, norm_final)))
    m = dict(zip(WEIGHT_NAMES, (m_norm_mix, m_w_in, m_gate_bias, m_conv_w, m_conv_b, m_conv_ln_g, m_conv_ln_b, m_w_conv_out, m_sgu_ln_g, m_sgu_ln_b, m_w_spatial, m_b_spatial, m_w_sgu_out, m_w_o, m_norm_ffn, m_w_ffn_gate, m_w_ffn_up, m_w_ffn_down, m_norm_final)))
    v = dict(zip(WEIGHT_NAMES, (v_norm_mix, v_w_in, v_gate_bias, v_conv_w, v_conv_b, v_conv_ln_g, v_conv_ln_b, v_w_conv_out, v_sgu_ln_g, v_sgu_ln_b, v_w_spatial, v_b_spatial, v_w_sgu_out, v_w_o, v_norm_ffn, v_w_ffn_gate, v_w_ffn_up, v_w_ffn_down, v_norm_final)))
    return _train_step(p, m, v, x, loss_target)
```

```python
import math

import jax
import jax.numpy as jnp
from jax import lax
from jax.experimental import pallas as pl
from jax.experimental.pallas import tpu as pltpu

F32 = jnp.float32
BF16 = jnp.bfloat16
MESH_ID = pl.DeviceIdType.MESH

NDEV = 8
EPS = 1e-6
CONV_TAPS = 31
CONV_TAPS_PADDED = 32
HALO = 16
CONV_ROWS = 128
CONV_UNROLL_MAX = 4
LANES = 128
SUBLANES = 8
CHUNK = 128
GROUPS = 8
FF_PAD = 384
FF_PAIR = 2 * FF_PAD
TN_COLS = 512
VMEM_LIMIT_BYTES = 56 * 1024 * 1024

ADAM_LR = 0.001
ADAM_B1 = 0.9
ADAM_B2 = 0.999
ADAM_EPS = 1e-08
ADAM_WD = 0.01
ADAM_STEP = 10

TILE_IN = 512
TILE_MIX = 256
TILE_FFN_FWD = 1024
TILE_FFN_BWD = 512
TILE_TN = 1024
TILE_LOSS = 512

WEIGHT_NAMES = ['norm_mix', 'w_in', 'gate_bias', 'conv_w', 'conv_b', 'conv_ln_g', 'conv_ln_b', 'w_conv_out',
                'sgu_ln_g', 'sgu_ln_b', 'w_spatial', 'b_spatial', 'w_sgu_out', 'w_o', 'norm_ffn', 'w_ffn_gate',
                'w_ffn_up', 'w_ffn_down', 'norm_final']
MIXER_WEIGHTS = ['w_conv_out', 'w_sgu_out', 'w_o', 'conv_w']
FFN_WEIGHTS = ['w_ffn_gate', 'w_ffn_up', 'w_ffn_down']


def _sds(shape, dtype):
    return jax.ShapeDtypeStruct(tuple(shape), dtype)


def _params(*sem):
    return pltpu.CompilerParams(dimension_semantics=sem or None, vmem_limit_bytes=VMEM_LIMIT_BYTES)


def _nn(a, b):
    return jnp.dot(a, b, preferred_element_type=F32)


def _nt(a, b):
    return lax.dot_general(a, b, (((1,), (1,)), ((), ())), preferred_element_type=F32)


def _tn(a, b):
    return lax.dot_general(a, b, (((0,), (0,)), ((), ())), preferred_element_type=F32)


def _sig(v):
    return jax.nn.sigmoid(v)


def _fold(v):
    r, c = v.shape
    return jnp.sum(v.reshape(r // SUBLANES, SUBLANES, c), axis=0)


def _tile(tm, n, j=0):
    return pl.BlockSpec((tm, n), lambda i, *_: (i, j))


def _row(n):
    return pl.BlockSpec((1, n), lambda *_: (0, 0))


def _resident(shape):
    nd = len(shape)
    return pl.BlockSpec(tuple(shape), lambda *_: (0,) * nd)


def _weight(w):
    nd = w.ndim
    return pl.BlockSpec(tuple(w.shape), lambda *_: (0,) * nd, pipeline_mode=pl.Buffered(1))


def _peer(rel):
    x, y, c = lax.axis_index("x"), lax.axis_index("y"), lax.axis_index("c")
    return (1 - x if rel & 4 else x, 1 - y if rel & 2 else y, 1 - c if rel & 1 else c)


def _slot(pos):
    return 4 * pos[0] + 2 * pos[1] + pos[2]


class _Exchange:
    def __init__(self, arrays, layers=None):
        self.arrays = list(arrays)
        self.layers = list(layers) if layers is not None else [None] * len(self.arrays)

    def scratch(self):
        n = len(self.arrays)
        return [pltpu.SemaphoreType.DMA((n, NDEV)), pltpu.SemaphoreType.DMA((n, NDEV)), pltpu.SemaphoreType.DMA((n,))]

    def _src(self, ins, j):
        return ins[j] if self.layers[j] is None else ins[j].at[self.layers[j]]

    def _block_shape(self, j):
        a = self.arrays[j]
        return a.shape if self.layers[j] is None else a.shape[1:]


class _Gather(_Exchange):
    chips = (4, 2, 6)

    def out_shape(self):
        return [_sds((NDEV,) + tuple(self._block_shape(j)), a.dtype) for j, a in enumerate(self.arrays)]

    @staticmethod
    def _copy(outs, sems, j, sem, block_rel, to_rel, src=None):
        blk = outs[j].at[_slot(_peer(block_rel))]
        return pltpu.make_async_remote_copy(
            src_ref=blk if src is None else src, dst_ref=blk,
            send_sem=sems[0].at[j, sem], recv_sem=sems[1].at[j, sem],
            device_id=_peer(to_rel), device_id_type=MESH_ID)

    def _local(self, ins, outs, sems, j):
        return pltpu.make_async_copy(self._src(ins, j), outs[j].at[_slot(_peer(0))], sems[2].at[j])

    def start(self, ins, outs, sems):
        for j in range(len(self.arrays)):
            self._local(ins, outs, sems, j).start()
            for rel in (1,) + self.chips:
                self._copy(outs, sems, j, rel, 0, rel, src=self._src(ins, j)).start()

    def forward(self, ins, outs, sems):
        for j in range(len(self.arrays)):
            for rel in self.chips:
                self._copy(outs, sems, j, rel, rel, 0).wait_recv()
                self._copy(outs, sems, j, rel ^ 1, rel, 1).start()

    def finish(self, ins, outs, sems):
        for j in range(len(self.arrays)):
            self._copy(outs, sems, j, 1, 1, 0).wait_recv()
            for rel in self.chips:
                self._copy(outs, sems, j, rel ^ 1, rel ^ 1, 0).wait_recv()
        for j in range(len(self.arrays)):
            for rel in (1,) + self.chips:
                self._copy(outs, sems, j, rel, 0, rel, src=self._src(ins, j)).wait_send()
            for rel in self.chips:
                self._copy(outs, sems, j, rel ^ 1, rel, 1).wait_send()
            self._local(ins, outs, sems, j).wait()


class _Scatter(_Exchange):
    def out_shape(self):
        return [_sds(a.shape, a.dtype) for a in self.arrays]

    @staticmethod
    def _copy(ins, outs, sems, j, rel):
        return pltpu.make_async_remote_copy(
            src_ref=ins[j].at[_slot(_peer(rel))], dst_ref=outs[j].at[_slot(_peer(0))],
            send_sem=sems[0].at[j, rel], recv_sem=sems[1].at[j, rel],
            device_id=_peer(rel), device_id_type=MESH_ID)

    @staticmethod
    def _arrival(outs, sems, j, rel):
        blk = outs[j].at[_slot(_peer(rel))]
        return pltpu.make_async_remote_copy(
            src_ref=blk, dst_ref=blk, send_sem=sems[0].at[j, rel], recv_sem=sems[1].at[j, rel],
            device_id=_peer(rel), device_id_type=MESH_ID)

    @staticmethod
    def _local(ins, outs, sems, j):
        me = _slot(_peer(0))
        return pltpu.make_async_copy(ins[j].at[me], outs[j].at[me], sems[2].at[j])

    def start(self, ins, outs, sems):
        for j in range(len(self.arrays)):
            self._local(ins, outs, sems, j).start()
            for rel in range(1, NDEV):
                self._copy(ins, outs, sems, j, rel).start()

    def forward(self, ins, outs, sems):
        pass

    def finish(self, ins, outs, sems):
        for j in range(len(self.arrays)):
            for rel in range(1, NDEV):
                self._arrival(outs, sems, j, rel).wait_recv()
        for j in range(len(self.arrays)):
            for rel in range(1, NDEV):
                self._copy(ins, outs, sems, j, rel).wait_send()
            self._local(ins, outs, sems, j).wait()


class _Together:
    def __init__(self, parts):
        self.parts = [c for c in parts if c is not None]
        self.arrays = [a for c in self.parts for a in c.arrays]

    def out_shape(self):
        return [s for c in self.parts for s in c.out_shape()]

    def scratch(self):
        return [s for c in self.parts for s in c.scratch()]

    def _each(self, method, ins, outs, sems):
        at = 0
        for q, c in enumerate(self.parts):
            n = len(c.arrays)
            getattr(c, method)(ins[at:at + n], outs[at:at + n], sems[3 * q:3 * q + 3])
            at += n

    def start(self, ins, outs, sems):
        self._each("start", ins, outs, sems)

    def forward(self, ins, outs, sems):
        self._each("forward", ins, outs, sems)

    def finish(self, ins, outs, sems):
        self._each("finish", ins, outs, sems)


def _together(*parts):
    parts = [c for c in parts if c is not None]
    return _Together(parts) if parts else None


def _call(body, *, name, args, in_specs, out_specs, out_shape, grid=(), scratch_shapes=(), semantics=(),
          aliases=None, comm=None):
    in_specs, out_specs, out_shape = list(in_specs), list(out_specs), list(out_shape)
    scratch_shapes = list(scratch_shapes)
    if comm is None:
        res = pl.pallas_call(
            body, name=name, grid=grid, in_specs=in_specs, out_specs=out_specs, out_shape=out_shape,
            scratch_shapes=scratch_shapes, input_output_aliases=aliases or {},
            compiler_params=_params(*semantics))(*args)
        return list(res), []
    n_in, n_out, n_scr, nc = len(in_specs), len(out_specs), len(scratch_shapes), len(comm.arrays)
    total = math.prod(grid)
    middle = min((total * 5) // 8, total - 1)

    def hosted(*refs):
        ins, cins = refs[:n_in], refs[n_in:n_in + nc]
        o0 = n_in + nc
        outs, couts = refs[o0:o0 + n_out], refs[o0 + n_out:o0 + n_out + nc]
        s0 = o0 + n_out + nc
        scr, sems = refs[s0:s0 + n_scr], refs[s0 + n_scr:]
        if total == 1:
            comm.start(cins, couts, sems)
            body(*ins, *outs, *scr)
            comm.forward(cins, couts, sems)
            comm.finish(cins, couts, sems)
            return
        step = 0
        for axis, size in enumerate(grid):
            step = step * size + pl.program_id(axis)
        pl.when(step == 0)(lambda: comm.start(cins, couts, sems))
        body(*ins, *outs, *scr)
        pl.when(step == middle)(lambda: comm.forward(cins, couts, sems))
        pl.when(step == total - 1)(lambda: comm.finish(cins, couts, sems))

    any_spec = pl.BlockSpec(memory_space=pl.ANY)
    res = pl.pallas_call(
        hosted, name=name, grid=grid,
        in_specs=in_specs + [any_spec] * nc, out_specs=out_specs + [any_spec] * nc,
        out_shape=out_shape + comm.out_shape(), scratch_shapes=scratch_shapes + comm.scratch(),
        input_output_aliases=aliases or {}, compiler_params=_params(*(("arbitrary",) * len(grid))),
    )(*args, *comm.arrays)
    return list(res[:n_out]), list(res[n_out:])


def _exchange_alone(comm, name):
    return _call(lambda: None, name=name, args=(), in_specs=(), out_specs=(), out_shape=(), comm=comm)[1]


def _cast_pad(w, rows, cols, name):
    nl, r, c = w.shape

    def body(w_ref, o_ref):
        if (rows, cols) != (r, c):
            o_ref[...] = jnp.zeros(o_ref.shape, BF16)
        o_ref[0, 0:r, 0:c] = w_ref[0].astype(BF16)

    return _call(body, name=name, grid=(nl,), args=(w,),
                 in_specs=[pl.BlockSpec((1, r, c), lambda i: (i, 0, 0))],
                 out_specs=[pl.BlockSpec((1, rows, cols), lambda i: (i, 0, 0))],
                 out_shape=[_sds((nl, rows, cols), BF16)], semantics=("parallel",))[0][0]


def _in_proj(x, gain, w, tm, name, comm):
    t, d = x.shape
    nb, _, bw = w.shape

    def body(x_ref, g_ref, w_ref, h_ref, p_ref):
        xv = x_ref[...]
        r = lax.rsqrt(jnp.mean(xv * xv, axis=-1, keepdims=True) + EPS)
        h = (xv * r * g_ref[...]).astype(BF16)
        h_ref[...] = h
        for k in range(nb):
            p_ref[:, k * bw:(k + 1) * bw] = _nn(h, w_ref[k]).astype(BF16)

    return _call(body, name=name, grid=(t // tm,), args=(x, gain, w),
                 in_specs=[_tile(tm, d), _row(d), _weight(w)],
                 out_specs=[_tile(tm, d), _tile(tm, nb * bw)],
                 out_shape=[_sds((t, d), BF16), _sds((t, nb * bw), BF16)],
                 semantics=("parallel",), comm=comm)


def _halo_specs(tm, t, d, col):
    nh, nhb = tm // HALO, t // HALO
    prev = pl.BlockSpec((HALO, d), lambda i: (jnp.maximum(i * nh - 1, 0), col))
    nxt = pl.BlockSpec((HALO, d), lambda i: (jnp.minimum((i + 1) * nh, nhb - 1), col))
    return prev, nxt


def _row_chunks(tm, rows):
    n = tm // CONV_ROWS
    if n <= CONV_UNROLL_MAX:
        for j in range(n):
            rows(j * CONV_ROWS)
    else:
        def step(j, carry):
            rows(pl.multiple_of(j * CONV_ROWS, CONV_ROWS))
            return carry

        lax.fori_loop(0, n, step, 0)


def _shifted(buf_ref, kb, r0, off):
    return buf_ref[kb, pl.ds(r0 + off, CONV_ROWS), :]


def _dwconv(buf_ref, w_ref, out_ref, tm, flip):
    nblk = out_ref.shape[1] // LANES

    def rows(r0):
        for kb in range(nblk):
            acc = jnp.zeros((CONV_ROWS, LANES), F32)
            for k in range(CONV_TAPS):
                off = (CONV_TAPS - k) if flip else (1 + k)
                acc = acc + w_ref[kb, k:k + 1, :] * _shifted(buf_ref, kb, r0, off)
            out_ref[pl.ds(r0, CONV_ROWS), kb * LANES:(kb + 1) * LANES] = acc

    _row_chunks(tm, rows)


def _fill_halo_buffer(buf, prev, body, nxt, first, last, tm):
    prev = jnp.where(first, 0.0, prev)
    nxt = jnp.where(last, 0.0, nxt)
    for kb in range(buf.shape[0]):
        lanes = slice(kb * LANES, (kb + 1) * LANES)
        buf[kb, 0:HALO, :] = prev[:, lanes]
        buf[kb, HALO:HALO + tm, :] = body[:, lanes]
        buf[kb, HALO + tm:HALO + tm + HALO, :] = nxt[:, lanes]


def _fill_glu_buffer(cbuf, av, ag, avp, agp, avn, agn, first, last, tm):
    c0p = avp[...].astype(F32) * _sig(agp[...].astype(F32))
    c0n = avn[...].astype(F32) * _sig(agn[...].astype(F32))
    c0 = av[...].astype(F32) * _sig(ag[...].astype(F32))
    _fill_halo_buffer(cbuf, c0p, c0, c0n, first, last, tm)


def _layernorm_stats(v):
    mu = jnp.mean(v, axis=-1, keepdims=True)
    cen = v - mu
    rstd = lax.rsqrt(jnp.mean(cen * cen, axis=-1, keepdims=True) + EPS)
    return cen * rstd, rstd


def _spatial_mix(ws_ref, vn_ref, bias_ref, mixed_ref, tm):
    for ci in range(tm // CHUNK):
        rs = slice(ci * CHUNK, (ci + 1) * CHUNK)
        for g in range(GROUPS):
            ls = slice(g * LANES, (g + 1) * LANES)
            mixed_ref[rs, ls] = _nn(ws_ref[g], vn_ref[rs, ls]) + bias_ref[:, ls]


def _mixer_fwd(proj, x, w, small, seq, tm, name, comm):
    t, d = x.shape
    hp_v, hn_v = _halo_specs(tm, t, d, 0)
    hp_g, hn_g = _halo_specs(tm, t, d, 1)

    def body(av, ag, u_ref, v_ref, ga_ref, gb_ref, avp, agp, avn, agn, x_ref,
             cw_ref, cb_ref, lg_ref, lb_ref, wco_ref, sg_ref, sb_ref, ws_ref, bias_ref, wso_ref,
             gba_ref, gbb_ref, wo_ref,
             c1_ref, ya_ref, yb_ref, x1_ref, cbuf, c1f, vn_ref, mixed_ref):
        i = pl.program_id(0)
        first = (i * tm) % seq == 0
        last = ((i + 1) * tm) % seq == 0
        _fill_glu_buffer(cbuf, av, ag, avp, agp, avn, agn, first, last, tm)
        _dwconv(cbuf, cw_ref, c1f, tm, flip=False)
        c1 = c1f[...] + cb_ref[...]
        c1_ref[...] = c1.astype(BF16)
        c2hat, _ = _layernorm_stats(c1)
        c2 = c2hat * lg_ref[...] + lb_ref[...]
        c3 = (c2 * _sig(c2)).astype(BF16)
        ya = _nn(c3, wco_ref[...].reshape(d, d))
        ya_ref[...] = ya.astype(BF16)
        vhat, _ = _layernorm_stats(v_ref[...].astype(F32))
        vn_ref[...] = (vhat * sg_ref[...] + sb_ref[...]).astype(BF16)
        _spatial_mix(ws_ref, vn_ref, bias_ref, mixed_ref, tm)
        gated = (u_ref[...].astype(F32) * mixed_ref[...]).astype(BF16)
        yb = _nn(gated, wso_ref[...].reshape(d, d))
        yb_ref[...] = yb.astype(BF16)
        sa = _sig(ga_ref[...].astype(F32) + gba_ref[...])
        sb = _sig(gb_ref[...].astype(F32) + gbb_ref[...])
        merged = (sa * ya + sb * yb).astype(BF16)
        x1_ref[...] = x_ref[...] + _nn(merged, wo_ref[...].reshape(d, d))

    cols = [_tile(tm, d, j) for j in range(6)]
    return _call(
        body, name=name, grid=(t // tm,),
        args=(proj,) * 10 + (x, w['conv_w'], small['conv_b'], small['conv_ln_g'], small['conv_ln_b'], w['w_conv_out'],
                             small['sgu_ln_g'], small['sgu_ln_b'], small['ws'], small['bias_full'], w['w_sgu_out'],
                             small['gba'], small['gbb'], w['w_o']),
        in_specs=cols + [hp_v, hp_g, hn_v, hn_g, _tile(tm, d),
                         _weight(w['conv_w']), _row(d), _row(d), _row(d),
                         _weight(w['w_conv_out']), _row(d), _row(d),
                         _resident(small['ws'].shape), _resident(small['bias_full'].shape),
                         _weight(w['w_sgu_out']), _row(d), _row(d), _weight(w['w_o'])],
        out_specs=[_tile(tm, d)] * 4,
        out_shape=[_sds((t, d), BF16)] * 3 + [_sds((t, d), F32)],
        scratch_shapes=[pltpu.VMEM((d // LANES, tm + 2 * HALO, LANES), F32), pltpu.VMEM((tm, d), F32),
                        pltpu.VMEM((tm, d), BF16), pltpu.VMEM((tm, d), F32)],
        semantics=("parallel",), comm=comm)


def _pair_gate_up(wg, wu, name):
    n, d, c = wg.shape

    def body(wg_ref, wu_ref, o_ref):
        o_ref[:, 0:c] = wg_ref[0]
        o_ref[:, c:2 * c] = wg_ref[1]
        o_ref[:, 2 * c:3 * c] = wu_ref[0]
        o_ref[:, 3 * c:4 * c] = wu_ref[1]

    pair = pl.BlockSpec((2, d, c), lambda k: (k, 0, 0))
    return _call(body, name=name, grid=(n // 2,), args=(wg, wu), in_specs=[pair, pair],
                 out_specs=[pl.BlockSpec((None, d, 4 * c), lambda k: (k, 0, 0))],
                 out_shape=[_sds((n // 2, d, 4 * c), wg.dtype)], semantics=("parallel",))[0][0]


def _pair_specs(w, tm):
    d = w['w_gate_up'].shape[1]
    up = pl.BlockSpec((None, d, 2 * FF_PAIR), lambda i, k: (k, 0, 0))
    down = pl.BlockSpec((None, FF_PAIR, d), lambda i, k: (k, 0, 0))
    return [up, down]


def _ffn_fwd(x1, gain, w, tm, name, comm):
    t, d = x1.shape
    tm = min(tm, t)
    npair = NDEV // 2
    hid = NDEV * FF_PAD
    wd_pairs = w['w_ffn_down'].reshape(npair, FF_PAIR, d)

    def body(x_ref, g_ref, wgu_ref, wd_ref, h_ref, gg_ref, uu_ref, x2_ref, hb_ref, acc_ref):
        k = pl.program_id(1)

        @pl.when(k == 0)
        def _():
            xv = x_ref[...]
            r = lax.rsqrt(jnp.mean(xv * xv, axis=-1, keepdims=True) + EPS)
            h = (xv * r * g_ref[...]).astype(BF16)
            hb_ref[...] = h
            h_ref[...] = h
            acc_ref[...] = xv

        gu = _nn(hb_ref[...], wgu_ref[...])
        gk = gu[:, 0:FF_PAIR]
        uk = gu[:, FF_PAIR:2 * FF_PAIR]
        gg_ref[...] = gk.astype(BF16)
        uu_ref[...] = uk.astype(BF16)
        ak = (gk * _sig(gk) * uk).astype(BF16)
        acc_ref[...] += _nn(ak, wd_ref[...])

        @pl.when(k == npair - 1)
        def _():
            x2_ref[...] = acc_ref[...]

    pair_cols = pl.BlockSpec((tm, FF_PAIR), lambda i, k: (i, k))
    return _call(
        body, name=name, grid=(t // tm, npair),
        args=(x1, gain, w['w_gate_up'], wd_pairs),
        in_specs=[_tile(tm, d), _row(d)] + _pair_specs(w, tm),
        out_specs=[_tile(tm, d), pair_cols, pair_cols, _tile(tm, d)],
        out_shape=[_sds((t, d), BF16), _sds((t, hid), BF16), _sds((t, hid), BF16), _sds((t, d), F32)],
        scratch_shapes=[pltpu.VMEM((tm, d), BF16), pltpu.VMEM((tm, d), F32)],
        semantics=("parallel", "arbitrary"), comm=comm)


def _init_small(first, small_ref, acc_ref):
    @pl.when(first)
    def _():
        small_ref[...] = jnp.zeros(small_ref.shape, F32)
        acc_ref[...] = jnp.zeros(acc_ref.shape, F32)


def _finish_small(last, small_ref, acc_ref, nq):
    @pl.when(last)
    def _():
        for q in range(nq):
            small_ref[q:q + 1, :] = jnp.sum(acc_ref[q], axis=0, keepdims=True)


def _loss_bwd(x, gain, target, tm, name):
    t, d = x.shape
    nsteps = t // tm

    def body(x_ref, g_ref, t_ref, dx_ref, small_ref, acc_ref):
        i = pl.program_id(0)
        _init_small(i == 0, small_ref, acc_ref)
        xv = x_ref[...]
        r = lax.rsqrt(jnp.mean(xv * xv, axis=-1, keepdims=True) + EPS)
        xhat = xv * r
        diff = xhat * g_ref[...] - t_ref[...]
        dy = diff * (1.0 / d)
        acc_ref[0] += _fold(dy * xhat)
        acc_ref[1] += _fold(diff * diff)
        dxhat = dy * g_ref[...]
        dx_ref[...] = r * (dxhat - xhat * jnp.mean(dxhat * xhat, axis=-1, keepdims=True))

        @pl.when(i == nsteps - 1)
        def _():
            small_ref[0:1, :] = jnp.sum(acc_ref[0], axis=0, keepdims=True)
            total = jnp.sum(acc_ref[1]) * (0.5 / d)
            small_ref[1:2, :] = jnp.full((1, d), total, F32)

    return _call(body, name=name, grid=(nsteps,), args=(x, gain, target),
                 in_specs=[_tile(tm, d), _row(d), _tile(tm, d)],
                 out_specs=[_tile(tm, d), _resident((SUBLANES, d))],
                 out_shape=[_sds((t, d), F32), _sds((SUBLANES, d), F32)],
                 scratch_shapes=[pltpu.VMEM((2, SUBLANES, d), F32)], semantics=("arbitrary",))[0]


def _ffn_bwd(dx2, x1, gain, gg, uu, w, tm, name, comm):
    t, d = x1.shape
    npair = NDEV // 2
    hid = NDEV * FF_PAD
    nsteps = t // tm
    wd_pairs = w['w_ffn_down'].reshape(npair, FF_PAIR, d)

    def body(dx_ref, x_ref, g_ref, gg_ref, uu_ref, wgu_ref, wd_ref,
             a_ref, dg_ref, du_ref, dx1_ref, small_ref, acc_ref, dxb_ref, dh_ref, dgu_ref):
        i, k = pl.program_id(0), pl.program_id(1)
        _init_small((i == 0) & (k == 0), small_ref, acc_ref)

        @pl.when(k == 0)
        def _():
            dxb_ref[...] = dx_ref[...].astype(BF16)
            dh_ref[...] = jnp.zeros(dh_ref.shape, F32)

        gk = gg_ref[...].astype(F32)
        uk = uu_ref[...].astype(F32)
        sg = _sig(gk)
        silu = gk * sg
        a_ref[...] = (silu * uk).astype(BF16)
        da = _nt(dxb_ref[...], wd_ref[...])
        dgk = (da * uk * (sg * (1.0 + gk * (1.0 - sg)))).astype(BF16)
        duk = (da * silu).astype(BF16)
        dg_ref[...] = dgk
        du_ref[...] = duk
        dgu_ref[:, 0:FF_PAIR] = dgk
        dgu_ref[:, FF_PAIR:2 * FF_PAIR] = duk
        dh_ref[...] += _nt(dgu_ref[...], wgu_ref[...])

        @pl.when(k == npair - 1)
        def _():
            xv = x_ref[...]
            dh = dh_ref[...]
            r = lax.rsqrt(jnp.mean(xv * xv, axis=-1, keepdims=True) + EPS)
            xhat = xv * r
            acc_ref[0] += _fold(dh * xhat)
            dxhat = dh * g_ref[...]
            dx1_ref[...] = dx_ref[...] + r * (dxhat - xhat * jnp.mean(dxhat * xhat, axis=-1, keepdims=True))

        _finish_small((i == nsteps - 1) & (k == npair - 1), small_ref, acc_ref, 1)

    pair_cols = pl.BlockSpec((tm, FF_PAIR), lambda i, k: (i, k))
    return _call(
        body, name=name, grid=(nsteps, npair),
        args=(dx2, x1, gain, gg, uu, w['w_gate_up'], wd_pairs),
        in_specs=[_tile(tm, d), _tile(tm, d), _row(d), pair_cols, pair_cols] + _pair_specs(w, tm),
        out_specs=[pair_cols] * 3 + [_tile(tm, d), _resident((SUBLANES, d))],
        out_shape=[_sds((t, hid), BF16)] * 3 + [_sds((t, d), F32), _sds((SUBLANES, d), F32)],
        scratch_shapes=[pltpu.VMEM((1, SUBLANES, d), F32), pltpu.VMEM((tm, d), BF16), pltpu.VMEM((tm, d), F32),
                        pltpu.VMEM((tm, 2 * FF_PAIR), BF16)],
        semantics=("arbitrary", "arbitrary"), comm=comm)


def _matmul_tn(a, b, a_blk, b_blk, stack, shard, tm, name, comm):
    t, ma = a.shape
    tm = min(tm, t)
    nb_ = b.shape[1]
    na, nb = ma // a_blk, nb_ // b_blk
    nsteps = t // tm
    cw = min(TN_COLS, b_blk)
    if stack == 'b':
        per = b_blk // shard
        out_shape, out_spec = (nb_ // shard, ma, shard), pl.BlockSpec((per, a_blk, shard), lambda i, j, k: (j, 0, 0))
    elif stack == 'a':
        per = a_blk // shard
        out_shape, out_spec = (ma // shard, shard, nb_), pl.BlockSpec((per, shard, b_blk), lambda i, j, k: (i, 0, 0))
    else:
        out_shape, out_spec = (ma, nb_), pl.BlockSpec((a_blk, b_blk), lambda i, j, k: (i, j))

    def body(a_ref, b_ref, o_ref, acc_ref):
        k = pl.program_id(2)

        @pl.when(k == 0)
        def _():
            acc_ref[...] = jnp.zeros(acc_ref.shape, F32)

        av = a_ref[...].astype(BF16)
        for c in range(0, b_blk, cw):
            acc_ref[:, c:c + cw] += _tn(av, b_ref[:, c:c + cw].astype(BF16))

        @pl.when(k == nsteps - 1)
        def _():
            if stack == 'b':
                for s in range(per):
                    o_ref[s] = acc_ref[:, s * shard:(s + 1) * shard].astype(BF16)
            elif stack == 'a':
                for s in range(per):
                    o_ref[s] = acc_ref[s * shard:(s + 1) * shard, :].astype(BF16)
            else:
                o_ref[...] = acc_ref[...].astype(BF16)

    res, got = _call(
        body, name=name, grid=(na, nb, nsteps), args=(a, b),
        in_specs=[pl.BlockSpec((tm, a_blk), lambda i, j, k: (k, i)), pl.BlockSpec((tm, b_blk), lambda i, j, k: (k, j))],
        out_specs=[out_spec], out_shape=[_sds(out_shape, BF16)],
        scratch_shapes=[pltpu.VMEM((a_blk, b_blk), F32)],
        semantics=("parallel", "parallel", "arbitrary"), comm=comm)
    return res[0], got


def _mixer_bwd(dx1, proj, c1, ya, yb, w, small, tm, name, comm):
    t, d = dx1.shape
    nsteps = t // tm
    nq = 6

    def body(dx_ref, u_ref, v_ref, ga_ref, gb_ref, c1_ref, ya_ref, yb_ref,
             lg_ref, lb_ref, wco_ref, sg_ref, sb_ref, ws_ref, wst_ref, bias_ref, wso_ref, gba_ref, gbb_ref, wo_ref,
             sel_ref,
             dp_ref, dc1_ref, mg_ref, c3_ref, gt_ref, dya_ref, dyb_ref, small_ref, dws_ref, dbs_ref,
             acc_ref, vn_ref, mixed_ref, dmix_ref, dvn_ref, dbias_ref):
        i = pl.program_id(0)
        _init_small(i == 0, small_ref, acc_ref)

        @pl.when(i == 0)
        def _():
            dws_ref[...] = jnp.zeros(dws_ref.shape, F32)
            dbs_ref[...] = jnp.zeros(dbs_ref.shape, F32)
            dbias_ref[...] = jnp.zeros(dbias_ref.shape, F32)

        dmerged = _nt(dx_ref[...].astype(BF16), wo_ref[...].reshape(d, d))
        ya = ya_ref[...].astype(F32)
        yb = yb_ref[...].astype(F32)
        sa = _sig(ga_ref[...].astype(F32) + gba_ref[...])
        sb = _sig(gb_ref[...].astype(F32) + gbb_ref[...])
        mg_ref[...] = (sa * ya + sb * yb).astype(BF16)
        dya = (dmerged * sa).astype(BF16)
        dyb = (dmerged * sb).astype(BF16)
        dya_ref[...] = dya
        dyb_ref[...] = dyb
        dga = dmerged * ya * (sa * (1.0 - sa))
        dgb = dmerged * yb * (sb * (1.0 - sb))
        acc_ref[0] += _fold(dga)
        acc_ref[1] += _fold(dgb)
        dp_ref[:, 0:2 * d] = jnp.zeros((tm, 2 * d), BF16)
        dp_ref[:, 4 * d:5 * d] = dga.astype(BF16)
        dp_ref[:, 5 * d:6 * d] = dgb.astype(BF16)
        c2hat, rstd = _layernorm_stats(c1_ref[...].astype(F32))
        c2 = c2hat * lg_ref[...] + lb_ref[...]
        s2 = _sig(c2)
        c3_ref[...] = (c2 * s2).astype(BF16)
        dc3 = _nt(dya, wco_ref[...].reshape(d, d))
        dc2 = dc3 * (s2 * (1.0 + c2 * (1.0 - s2)))
        acc_ref[2] += _fold(dc2 * c2hat)
        acc_ref[3] += _fold(dc2)
        dc2hat = dc2 * lg_ref[...]
        dc1_ref[...] = (rstd * (dc2hat - jnp.mean(dc2hat, axis=-1, keepdims=True)
                                - c2hat * jnp.mean(dc2hat * c2hat, axis=-1, keepdims=True))).astype(BF16)
        vhat, rstd_v = _layernorm_stats(v_ref[...].astype(F32))
        vn_ref[...] = (vhat * sg_ref[...] + sb_ref[...]).astype(BF16)
        _spatial_mix(ws_ref, vn_ref, bias_ref, mixed_ref, tm)
        u = u_ref[...].astype(F32)
        mixed = mixed_ref[...]
        gt_ref[...] = (u * mixed).astype(BF16)
        dgated = _nt(dyb, wso_ref[...].reshape(d, d))
        dp_ref[:, 2 * d:3 * d] = (dgated * mixed).astype(BF16)
        dmix_ref[...] = dgated * u
        for ci in range(tm // CHUNK):
            rs = slice(ci * CHUNK, (ci + 1) * CHUNK)
            dbias_ref[...] += dmix_ref[rs, :]
            for g in range(GROUPS):
                ls = slice(g * LANES, (g + 1) * LANES)
                dm = dmix_ref[rs, ls].astype(BF16)
                dws_ref[g] += _nt(dm, vn_ref[rs, ls])
                dvn_ref[rs, ls] = _nn(wst_ref[g], dm)
        dvn = dvn_ref[...]
        acc_ref[4] += _fold(dvn * vhat)
        acc_ref[5] += _fold(dvn)
        dvhat = dvn * sg_ref[...]
        dp_ref[:, 3 * d:4 * d] = (rstd_v * (dvhat - jnp.mean(dvhat, axis=-1, keepdims=True)
                                           - vhat * jnp.mean(dvhat * vhat, axis=-1, keepdims=True))).astype(BF16)
        _finish_small(i == nsteps - 1, small_ref, acc_ref, nq)

        @pl.when(i == nsteps - 1)
        def _():
            db = dbias_ref[...]
            hi = db.astype(BF16)
            lo = (db - hi.astype(F32)).astype(BF16)
            dbs_ref[...] = _nt(sel_ref[...], hi) + _nt(sel_ref[...], lo)

    cols = [_tile(tm, d, j) for j in (2, 3, 4, 5)]
    return _call(
        body, name=name, grid=(nsteps,),
        args=(dx1, proj, proj, proj, proj, c1, ya, yb,
              small['conv_ln_g'], small['conv_ln_b'], w['w_conv_out'], small['sgu_ln_g'], small['sgu_ln_b'],
              small['ws'], small['wst'], small['bias_full'], w['w_sgu_out'], small['gba'], small['gbb'], w['w_o'],
              small['group_sel']),
        in_specs=[_tile(tm, d)] + cols + [_tile(tm, d)] * 3 + [
            _row(d), _row(d), _weight(w['w_conv_out']),
            _row(d), _row(d), _resident(small['ws'].shape), _resident(small['wst'].shape),
            _resident(small['bias_full'].shape), _weight(w['w_sgu_out']), _row(d), _row(d),
            _weight(w['w_o']), _resident(small['group_sel'].shape)],
        out_specs=[_tile(tm, 6 * d)] + [_tile(tm, d)] * 6 + [
            _resident((SUBLANES, d)), _resident((GROUPS, CHUNK, CHUNK)), _resident((GROUPS, CHUNK))],
        out_shape=[_sds((t, 6 * d), BF16)] + [_sds((t, d), BF16)] * 6 + [
            _sds((SUBLANES, d), F32), _sds((GROUPS, CHUNK, CHUNK), F32), _sds((GROUPS, CHUNK), F32)],
        scratch_shapes=[pltpu.VMEM((nq, SUBLANES, d), F32), pltpu.VMEM((tm, d), BF16), pltpu.VMEM((tm, d), F32),
                        pltpu.VMEM((tm, d), F32), pltpu.VMEM((tm, d), F32), pltpu.VMEM((CHUNK, d), F32)],
        semantics=("arbitrary",), comm=comm)


def _conv_bwd(dproj, dc1, proj, w, pairs, seq, tm, name, comm):
    t, d = dc1.shape
    nsteps = t // tm
    nblk = d // LANES
    npairs = len(pairs)
    hp_v, hn_v = _halo_specs(tm, t, d, 0)
    hp_g, hn_g = _halo_specs(tm, t, d, 1)
    hp_d, hn_d = _halo_specs(tm, t, d, 0)

    def body(*refs):
        (dp_in, dc_ref, dcp, dcn, av, ag, avp, agp, avn, agn, cw_ref), refs = refs[:11], refs[11:]
        ab_refs, refs = refs[:2 * npairs], refs[2 * npairs:]
        (dp_ref, dcw_ref, small_ref), refs = refs[:3], refs[3:]
        grad_refs, refs = refs[:npairs], refs[npairs:]
        acc_ref, cbuf, dbuf, dc0f, accw, gacc = refs
        del dp_in
        i = pl.program_id(0)
        _init_small(i == 0, small_ref, acc_ref)

        @pl.when(i == 0)
        def _():
            accw[...] = jnp.zeros(accw.shape, F32)
            dcw_ref[...] = jnp.zeros(dcw_ref.shape, F32)
            gacc[...] = jnp.zeros(gacc.shape, F32)

        first = (i * tm) % seq == 0
        last = ((i + 1) * tm) % seq == 0
        _fill_glu_buffer(cbuf, av, ag, avp, agp, avn, agn, first, last, tm)
        dc1v = dc_ref[...].astype(F32)
        _fill_halo_buffer(dbuf, dcp[...].astype(F32), dc1v, dcn[...].astype(F32), first, last, tm)
        acc_ref[0] += _fold(dc1v)
        _dwconv(dbuf, cw_ref, dc0f, tm, flip=True)

        def rows(r0):
            for kb in range(nblk):
                dv = dbuf[kb, pl.ds(r0 + HALO, CONV_ROWS), :]
                for k in range(CONV_TAPS):
                    accw[kb, k] += _fold(dv * _shifted(cbuf, kb, r0, 1 + k))

        _row_chunks(tm, rows)
        for q in range(npairs):
            aq = ab_refs[2 * q][...].astype(BF16)
            for c in range(0, d, TN_COLS):
                gacc[q, :, c:c + TN_COLS] += _tn(aq, ab_refs[2 * q + 1][:, c:c + TN_COLS].astype(BF16))
        sg = _sig(ag[...].astype(F32))
        avv = av[...].astype(F32)
        dc0 = dc0f[...]
        dp_ref[:, 0:d] = (dc0 * sg).astype(BF16)
        dp_ref[:, d:2 * d] = (dc0 * avv * (sg * (1.0 - sg))).astype(BF16)
        _finish_small(i == nsteps - 1, small_ref, acc_ref, 1)

        @pl.when(i == nsteps - 1)
        def _():
            for kb in range(nblk):
                dcw_ref[kb] = jnp.sum(accw[kb], axis=1)
            for q in range(npairs):
                grad_refs[q][...] = gacc[q].astype(BF16)

    return _call(
        body, name=name, grid=(nsteps,),
        args=(dproj, dc1, dc1, dc1, proj, proj, proj, proj, proj, proj, w['conv_w']) + tuple(a for ab in pairs for a in ab),
        in_specs=[pl.BlockSpec(memory_space=pl.ANY), _tile(tm, d), hp_d, hn_d, _tile(tm, d, 0), _tile(tm, d, 1),
                  hp_v, hp_g, hn_v, hn_g, _weight(w['conv_w'])] + [_tile(tm, d)] * (2 * npairs),
        out_specs=[_tile(tm, 2 * d), _resident((nblk, CONV_TAPS_PADDED, LANES)), _resident((SUBLANES, d))]
        + [_resident((d, d))] * npairs,
        out_shape=[_sds(dproj.shape, BF16), _sds((nblk, CONV_TAPS_PADDED, LANES), F32), _sds((SUBLANES, d), F32)]
        + [_sds((d, d), BF16)] * npairs,
        scratch_shapes=[pltpu.VMEM((1, SUBLANES, d), F32), pltpu.VMEM((nblk, tm + 2 * HALO, LANES), F32),
                        pltpu.VMEM((nblk, tm + 2 * HALO, LANES), F32), pltpu.VMEM((tm, d), F32),
                        pltpu.VMEM((nblk, CONV_TAPS_PADDED, SUBLANES, LANES), F32),
                        pltpu.VMEM((npairs, d, d), F32)],
        aliases={0: 0}, semantics=("arbitrary",), comm=comm)


def _in_proj_bwd(dproj, x, dx1, gain, w, tm, name, comm):
    t, d = x.shape
    nb, _, bw = w.shape
    nsteps = t // tm

    def body(dp_ref, x_ref, dx1_ref, g_ref, w_ref, dx_ref, small_ref, acc_ref):
        i = pl.program_id(0)
        _init_small(i == 0, small_ref, acc_ref)
        dh = jnp.zeros((tm, d), F32)
        for k in range(nb):
            dh = dh + _nt(dp_ref[:, k * bw:(k + 1) * bw], w_ref[k])
        xv = x_ref[...]
        r = lax.rsqrt(jnp.mean(xv * xv, axis=-1, keepdims=True) + EPS)
        xhat = xv * r
        acc_ref[0] += _fold(dh * xhat)
        dxhat = dh * g_ref[...]
        dx_ref[...] = dx1_ref[...] + r * (dxhat - xhat * jnp.mean(dxhat * xhat, axis=-1, keepdims=True))
        _finish_small(i == nsteps - 1, small_ref, acc_ref, 1)

    return _call(body, name=name, grid=(nsteps,), args=(dproj, x, dx1, gain, w),
                 in_specs=[_tile(tm, nb * bw), _tile(tm, d), _tile(tm, d), _row(d), _weight(w)],
                 out_specs=[_tile(tm, d), _resident((SUBLANES, d))],
                 out_shape=[_sds((t, d), F32), _sds((SUBLANES, d), F32)],
                 scratch_shapes=[pltpu.VMEM((1, SUBLANES, d), F32)], semantics=("arbitrary",), comm=comm)


def _adam(wv, g, mv, vv):
    m = ADAM_B1 * mv + (1.0 - ADAM_B1) * g
    v = ADAM_B2 * vv + (1.0 - ADAM_B2) * jnp.square(g)
    m_hat = m / (1.0 - ADAM_B1 ** ADAM_STEP)
    v_hat = v / (1.0 - ADAM_B2 ** ADAM_STEP)
    delta = -ADAM_LR * (m_hat / (jnp.sqrt(v_hat) + ADAM_EPS) + ADAM_WD * wv)
    return delta, m, v


def _adamw_layer(layer, w, m, v, parts, prev, nsplit, name):
    nl, r, c = w.shape
    npart, pr, pc = parts.shape
    rt, prt = r // nsplit, pr // nsplit

    def body(w_ref, m_ref, v_ref, p_ref, *rest):
        g_ref, d_ref, nm_ref, nv_ref = rest[-4:]
        g = p_ref[0, 0:rt, 0:c].astype(F32)
        for s in range(1, npart):
            g = g + p_ref[s, 0:rt, 0:c].astype(F32)
        delta, mn, vn = _adam(w_ref[0], g, m_ref[0], v_ref[0])
        g_ref[0] = g
        d_ref[0] = delta
        nm_ref[0] = mn
        nv_ref[0] = vn

    wspec = pl.BlockSpec((1, rt, c), lambda i: (layer, i, 0))
    pspec = pl.BlockSpec((npart, prt, pc), lambda i: (0, i, 0))
    in_specs = [wspec, wspec, wspec, pspec]
    args = [w, m, v, parts]
    aliases = {}
    if prev is not None:
        in_specs += [pl.BlockSpec(memory_space=pl.ANY)] * 4
        args += list(prev)
        aliases = {4 + q: q for q in range(4)}
    return _call(body, name=name, grid=(nsplit,), args=args, in_specs=in_specs, out_specs=[wspec] * 4,
                 out_shape=[_sds(w.shape, F32)] * 4, aliases=aliases, semantics=("parallel",))[0]


VEC_ROWS = {'norm_mix': 0, 'gate_bias': (SUBLANES, SUBLANES + 1), 'conv_ln_g': SUBLANES + 2, 'conv_ln_b': SUBLANES + 3,
            'sgu_ln_g': SUBLANES + 4, 'sgu_ln_b': SUBLANES + 5, 'conv_b': 2 * SUBLANES, 'norm_ffn': 3 * SUBLANES}
VEC_ROWS_PER_LAYER = 4 * SUBLANES


def _adamw_small(gathered, params, moments_m, moments_v):
    names = list(params)
    nper = len(names)
    nl = len(gathered)

    def body(*refs):
        g_refs = [refs[3 * l:3 * l + 3] for l in range(nl)]
        rest = refs[3 * nl:]
        w_refs = dict(zip(names, rest[0:nper]))
        m_refs = dict(zip(names, rest[nper:2 * nper]))
        v_refs = dict(zip(names, rest[2 * nper:3 * nper]))
        outs = rest[3 * nper:]
        o = {kind: dict(zip(names, outs[q * nper:(q + 1) * nper])) for q, kind in enumerate("gdmv")}

        def put(nm, idx, g):
            delta, mn, vn = _adam(w_refs[nm][idx], g, m_refs[nm][idx], v_refs[nm][idx])
            o["g"][nm][idx] = g
            o["d"][nm][idx] = delta
            o["m"][nm][idx] = mn
            o["v"][nm][idx] = vn

        def total(ref, *idx):
            g = ref[(0, *idx)]
            for s in range(1, NDEV):
                g = g + ref[(s, *idx)]
            return g

        for l, (vec_ref, dws_ref, dbs_ref) in enumerate(g_refs):
            dd = vec_ref.shape[2]
            put('w_spatial', (l,), total(dws_ref))
            put('b_spatial', (l,), total(dbs_ref))
            for nm, rr in VEC_ROWS.items():
                if nm == 'gate_bias':
                    put(nm, (slice(l, l + 1), slice(0, dd)), total(vec_ref, slice(rr[0], rr[0] + 1)))
                    put(nm, (slice(l, l + 1), slice(dd, 2 * dd)), total(vec_ref, slice(rr[1], rr[1] + 1)))
                else:
                    put(nm, (slice(l, l + 1), slice(None)), total(vec_ref, slice(rr, rr + 1)))
        last = g_refs[nl - 1][0]
        put('norm_final', (slice(0, 1), slice(None)), total(last, slice(VEC_ROWS_PER_LAYER, VEC_ROWS_PER_LAYER + 1)))

    ins = [a for g in gathered for a in g] + [params[n] for n in names] + [moments_m[n] for n in names] + [moments_v[n] for n in names]
    out_shape = [_sds(params[n].shape, F32) for n in names] * 4
    res = pl.pallas_call(body, name="adamw_small", out_shape=out_shape, compiler_params=_params())(*ins)
    return {kind: dict(zip(names, res[q * nper:(q + 1) * nper])) for q, kind in enumerate("gdmv")}


def _prepare_weights(p):
    d = p['w_in'].shape[1]
    return {
        'w_in': _cast_pad(p['w_in'], d, p['w_in'].shape[2], "cast_w_in"),
        'w_conv_out': _cast_pad(p['w_conv_out'], p['w_conv_out'].shape[1], d, "cast_w_conv_out"),
        'w_sgu_out': _cast_pad(p['w_sgu_out'], p['w_sgu_out'].shape[1], d, "cast_w_sgu_out"),
        'w_o': _cast_pad(p['w_o'], p['w_o'].shape[1], d, "cast_w_o"),
        'w_ffn_gate': _cast_pad(p['w_ffn_gate'], d, FF_PAD, "cast_w_ffn_gate"),
        'w_ffn_up': _cast_pad(p['w_ffn_up'], d, FF_PAD, "cast_w_ffn_up"),
        'w_ffn_down': _cast_pad(p['w_ffn_down'], FF_PAD, d, "cast_w_ffn_down"),
        'conv_w': jnp.pad(p['conv_w'][:, :, 0, :], ((0, 0), (0, CONV_TAPS_PADDED - CONV_TAPS), (0, 0))),
    }


def _gather_of(shards, names, layer):
    return _Gather([shards[n] for n in names], [layer] * len(names))


def _layer_small(p, layer):
    d = p['norm_mix'].shape[1]
    ws = p['w_spatial'][layer]
    rows = {n: p[n][layer:layer + 1] for n in ('norm_mix', 'norm_ffn', 'conv_b', 'conv_ln_g', 'conv_ln_b',
                                               'sgu_ln_g', 'sgu_ln_b')}
    return {
        **rows,
        'ws': ws.astype(BF16), 'wst': jnp.swapaxes(ws, 1, 2).astype(BF16),
        'bias_full': jnp.repeat(p['b_spatial'][layer].T, LANES, axis=1),
        'gba': p['gate_bias'][layer:layer + 1, 0:d], 'gbb': p['gate_bias'][layer:layer + 1, d:2 * d],
        'group_sel': (jnp.arange(d)[None, :] // LANES == jnp.arange(GROUPS)[:, None]).astype(BF16),
    }


class _GradQueue:
    def __init__(self):
        self.pending = []
        self.done = {}

    def push(self, key, array):
        self.pending.append((key, array))

    def take(self):
        keys = [k for k, _ in self.pending]
        comm = _Scatter([a for _, a in self.pending]) if self.pending else None
        self.pending = []
        return keys, comm

    def put(self, keys, arrays):
        self.done.update(zip(keys, arrays))


def _forward_backward(p, shards, x, target, seq):
    nl = p['norm_mix'].shape[0]
    d = x.shape[1]
    smalls = [_layer_small(p, l) for l in range(nl)]
    w_in = _exchange_alone(_gather_of(shards, ['w_in'], 0), "gather_w_in_0")[0]
    saved = []
    for l in range(nl):
        (h, proj), got = _in_proj(x, smalls[l]['norm_mix'], w_in, TILE_IN, f"in_proj_{l}",
                                  _gather_of(shards, MIXER_WEIGHTS, l))
        w = dict(zip(MIXER_WEIGHTS, got), w_in=w_in)
        (c1, ya, yb, x1), got = _mixer_fwd(proj, x, w, smalls[l], seq, TILE_MIX, f"mixer_fwd_{l}",
                                           _gather_of(shards, FFN_WEIGHTS, l))
        w.update(zip(FFN_WEIGHTS, got))
        w['w_gate_up'] = _pair_gate_up(w['w_ffn_gate'], w['w_ffn_up'], f"pair_gate_up_{l}")
        nxt = _gather_of(shards, ['w_in'], l + 1) if l + 1 < nl else None
        (h2, gg, uu, x2), got = _ffn_fwd(x1, smalls[l]['norm_ffn'], w, TILE_FFN_FWD, f"ffn_fwd_{l}", nxt)
        saved.append(dict(x=x, h=h, proj=proj, c1=c1, ya=ya, yb=yb, x1=x1, h2=h2, gg=gg, uu=uu, w=w))
        x = x2
        if got:
            w_in = got[0]
    dx, small_loss = _loss_bwd(x, p['norm_final'][None, :], target, TILE_LOSS, "loss_bwd")
    queue = _GradQueue()
    small_gathered = [None] * nl
    small_pending = None
    rows = d // NDEV
    hid = NDEV * FF_PAD
    for l in reversed(range(nl)):
        s = saved[l]
        w = s['w']

        def hosted(fn, *args, extra=None):
            keys, comm = queue.take()
            res, got = fn(*args, _together(comm, extra))
            queue.put(keys, got[:len(keys)])
            return res, got[len(keys):]

        def tn(key, a, b, a_blk, b_blk, stack, shard, reshape=None, host=False):
            keys, comm = queue.take() if host else ([], None)
            g, got = _matmul_tn(a, b, a_blk, b_blk, stack, shard, TILE_TN, f"dw_{key}_{l}", comm)
            queue.put(keys, got)
            queue.push((l, key), g if reshape is None else g.reshape(reshape))

        extra = small_pending[1] if small_pending else None
        (act, dgg, duu, dx1, small_ffn), got = hosted(_ffn_bwd, dx, s['x1'], smalls[l]['norm_ffn'], s['gg'], s['uu'], w,
                                                      TILE_FFN_BWD, f"ffn_bwd_{l}", extra=extra)
        if small_pending:
            small_gathered[small_pending[0]] = got
            small_pending = None
        tn('w_ffn_gate', s['h2'], dgg, d, hid, 'b', FF_PAD)
        tn('w_ffn_up', s['h2'], duu, d, hid, 'b', FF_PAD)
        (dproj, dc1, merged, c3, gated, dya, dyb, small_mix, dws, dbs), _ = hosted(
            _mixer_bwd, dx1, s['proj'], s['c1'], s['ya'], s['yb'], w, smalls[l], TILE_MIX, f"mixer_bwd_{l}")
        tn('w_ffn_down', act, dx, hid, d, 'a', FF_PAD)
        (dproj, g_conv, small_conv, g_o, g_co, g_so), _ = hosted(
            _conv_bwd, dproj, dc1, s['proj'], w, [(merged, dx1), (c3, dya), (gated, dyb)], seq, TILE_MIX,
            f"conv_bwd_{l}")
        queue.push((l, 'conv_w'), g_conv)
        queue.push((l, 'w_o'), g_o.reshape(NDEV, rows, d))
        queue.push((l, 'w_conv_out'), g_co.reshape(NDEV, rows, d))
        queue.push((l, 'w_sgu_out'), g_so.reshape(NDEV, rows, d))
        tn('w_in', s['h'], dproj, d, hid, 'b', w['w_in'].shape[2], host=True)
        if l == 0:
            (dx, small_in), _ = hosted(_in_proj_bwd, dproj, s['x'], dx1, smalls[l]['norm_mix'], w['w_in'], TILE_IN,
                                       f"in_proj_bwd_{l}")
        else:
            (dx, small_in), _ = _in_proj_bwd(dproj, s['x'], dx1, smalls[l]['norm_mix'], w['w_in'], TILE_IN,
                                             f"in_proj_bwd_{l}", None)
        blocks = [small_in, small_mix, small_conv, small_ffn] + ([small_loss] if l == nl - 1 else [])
        small_pending = (l, _Gather([jnp.concatenate(blocks, axis=0), dws, dbs]))
    keys, comm = queue.take()
    last = _exchange_alone(_together(comm, small_pending[1]), "exchange_last_grads")
    queue.put(keys, last[:len(keys)])
    small_gathered[small_pending[0]] = last[len(keys):]
    return small_gathered, small_loss, dx, queue.done


def _train_step(p, m, v, x3, target3):
    nl = p['norm_mix'].shape[0]
    bsz, seq, d = x3.shape
    x = x3.reshape(bsz * seq, d)
    target = target3.reshape(bsz * seq, d)
    small_gathered, small_loss, dx, exchanged = _forward_backward(p, _prepare_weights(p), x, target, seq)
    loss = lax.psum(small_loss[1, 0], ("x", "y", "c"))

    out = {kind: {} for kind in "gdmv"}
    splits = {'w_in': 4, 'w_conv_out': 1, 'w_sgu_out': 1, 'w_o': 1, 'w_ffn_gate': 4, 'w_ffn_up': 4, 'w_ffn_down': 1, 'conv_w': 1}
    for n in splits:
        if n == 'conv_w':
            pad = ((0, 0), (0, CONV_TAPS_PADDED - CONV_TAPS), (0, 0))
            wl, ml, vl = (jnp.pad(a[n][:, :, 0, :], pad) for a in (p, m, v))
        else:
            wl, ml, vl = p[n], m[n], v[n]
        prev = None
        for l in range(nl):
            prev = _adamw_layer(l, wl, ml, vl, exchanged[(l, n)], prev, splits[n], f"adamw_{n}_{l}")
        for kind, arr in zip("gdmv", prev):
            out[kind][n] = arr[:, 0:CONV_TAPS, None, :] if n == 'conv_w' else arr
    small_names = ['norm_mix', 'gate_bias', 'conv_b', 'conv_ln_g', 'conv_ln_b', 'sgu_ln_g', 'sgu_ln_b', 'w_spatial',
                   'b_spatial', 'norm_ffn', 'norm_final']

    def two_d(a):
        return a[None, :] if a.ndim == 1 else a

    res = _adamw_small(small_gathered, {n: two_d(p[n]) for n in small_names}, {n: two_d(m[n]) for n in small_names},
                       {n: two_d(v[n]) for n in small_names})
    for kind in "gdmv":
        for n in small_names:
            out[kind][n] = res[kind][n].reshape(p[n].shape)
    grad_x = dx.reshape(bsz, seq, d)
    return (loss, grad_x, *[out[kind][n] for kind in "gdmv" for n in WEIGHT_NAMES])


def kernel(x, norm_mix, w_in, gate_bias, conv_w, conv_b, conv_ln_g, conv_ln_b, w_conv_out, sgu_ln_g, sgu_ln_b, w_spatial, b_spatial, w_sgu_out, w_o, norm_ffn, w_ffn_gate, w_ffn_up, w_ffn_down, norm_final, loss_target, m_norm_mix, m_w_in, m_gate_bias, m_conv_w, m_conv_b, m_conv_ln_g, m_conv_ln_b, m_w_conv_out, m_sgu_ln_g, m_sgu_ln_b, m_w_spatial, m_b_spatial, m_w_sgu_out, m_w_o, m_norm_ffn, m_w_ffn_gate, m_w_ffn_up, m_w_ffn_down, m_norm_final, v_norm_mix, v_w_in, v_gate_bias, v_conv_w, v_conv_b, v_conv_ln_g, v_conv_ln_b, v_w_conv_out, v_sgu_ln_g, v_sgu_ln_b, v_w_spatial, v_b_spatial, v_w_sgu_out, v_w_o, v_norm_ffn, v_w_ffn_gate, v_w_ffn_up, v_w_ffn_down, v_norm_final):
    p = dict(zip(WEIGHT_NAMES, (norm_mix, w_in, gate_bias, conv_w, conv_b, conv_ln_g, conv_ln_b, w_conv_out, sgu_ln_g, sgu_ln_b, w_spatial, b_spatial, w_sgu_out, w_o, norm_ffn, w_ffn_gate, w_ffn_up, w_ffn_down, norm_final)))
    m = dict(zip(WEIGHT_NAMES, (m_norm_mix, m_w_in, m_gate_bias, m_conv_w, m_conv_b, m_conv_ln_g, m_conv_ln_b, m_w_conv_out, m_sgu_ln_g, m_sgu_ln_b, m_w_spatial, m_b_spatial, m_w_sgu_out, m_w_o, m_norm_ffn, m_w_ffn_gate, m_w_ffn_up, m_w_ffn_down, m_norm_final)))
    v = dict(zip(WEIGHT_NAMES, (v_norm_mix, v_w_in, v_gate_bias, v_conv_w, v_conv_b, v_conv_ln_g, v_conv_ln_b, v_w_conv_out, v_sgu_ln_g, v_sgu_ln_b, v_w_spatial, v_b_spatial, v_w_sgu_out, v_w_o, v_norm_ffn, v_w_ffn_gate, v_w_ffn_up, v_w_ffn_down, v_norm_final)))
    return _train_step(p, m, v, x, loss_target)
```

```python
import math

import jax
import jax.numpy as jnp
from jax import lax
from jax.experimental import pallas as pl
from jax.experimental.pallas import tpu as pltpu

F32 = jnp.float32
BF16 = jnp.bfloat16
MESH_ID = pl.DeviceIdType.MESH

NDEV = 8
EPS = 1e-6
CONV_TAPS = 31
CONV_TAPS_PADDED = 32
HALO = 16
CONV_ROWS = 128
CONV_ROWS_LOOP = 64
LANES = 128
SUBLANES = 8
CHUNK = 128
GROUPS = 8
FF_PAD = 384
FF_PAIR = 2 * FF_PAD
TN_COLS = 512
VMEM_LIMIT_BYTES = 56 * 1024 * 1024

ADAM_LR = 0.001
ADAM_B1 = 0.9
ADAM_B2 = 0.999
ADAM_EPS = 1e-08
ADAM_WD = 0.01
ADAM_STEP = 10

TILE_IN = 512
TILE_MIX = 256
TILE_FFN_FWD = 1024
TILE_FFN_BWD = 512
TILE_TN = 1024
TILE_LOSS = 512

WEIGHT_NAMES = ['norm_mix', 'w_in', 'gate_bias', 'conv_w', 'conv_b', 'conv_ln_g', 'conv_ln_b', 'w_conv_out',
                'sgu_ln_g', 'sgu_ln_b', 'w_spatial', 'b_spatial', 'w_sgu_out', 'w_o', 'norm_ffn', 'w_ffn_gate',
                'w_ffn_up', 'w_ffn_down', 'norm_final']
MIXER_WEIGHTS = ['w_conv_out', 'w_sgu_out', 'w_o', 'conv_w']
FFN_WEIGHTS = ['w_ffn_gate', 'w_ffn_up', 'w_ffn_down']


def _sds(shape, dtype):
    return jax.ShapeDtypeStruct(tuple(shape), dtype)


def _params(*sem):
    return pltpu.CompilerParams(dimension_semantics=sem or None, vmem_limit_bytes=VMEM_LIMIT_BYTES)


def _nn(a, b):
    return jnp.dot(a, b, preferred_element_type=F32)


def _nt(a, b):
    return lax.dot_general(a, b, (((1,), (1,)), ((), ())), preferred_element_type=F32)


def _tn(a, b):
    return lax.dot_general(a, b, (((0,), (0,)), ((), ())), preferred_element_type=F32)


def _sig(v):
    return jax.nn.sigmoid(v)


def _fold(v):
    r, c = v.shape
    return jnp.sum(v.reshape(r // SUBLANES, SUBLANES, c), axis=0)


def _tile(tm, n, j=0):
    return pl.BlockSpec((tm, n), lambda i, *_: (i, j))


def _row(n):
    return pl.BlockSpec((1, n), lambda *_: (0, 0))


def _resident(shape):
    nd = len(shape)
    return pl.BlockSpec(tuple(shape), lambda *_: (0,) * nd)


def _weight(w):
    nd = w.ndim
    return pl.BlockSpec(tuple(w.shape), lambda *_: (0,) * nd, pipeline_mode=pl.Buffered(1))


def _peer(rel):
    x, y, c = lax.axis_index("x"), lax.axis_index("y"), lax.axis_index("c")
    return (1 - x if rel & 4 else x, 1 - y if rel & 2 else y, 1 - c if rel & 1 else c)


def _slot(pos):
    return 4 * pos[0] + 2 * pos[1] + pos[2]


class _Exchange:
    def __init__(self, arrays, layers=None):
        self.arrays = list(arrays)
        self.layers = list(layers) if layers is not None else [None] * len(self.arrays)

    def scratch(self):
        n = len(self.arrays)
        return [pltpu.SemaphoreType.DMA((n, NDEV)), pltpu.SemaphoreType.DMA((n, NDEV)), pltpu.SemaphoreType.DMA((n,))]

    def _src(self, ins, j):
        return ins[j] if self.layers[j] is None else ins[j].at[self.layers[j]]

    def _block_shape(self, j):
        a = self.arrays[j]
        return a.shape if self.layers[j] is None else a.shape[1:]


class _Gather(_Exchange):
    chips = (4, 2, 6)

    def out_shape(self):
        return [_sds((NDEV,) + tuple(self._block_shape(j)), a.dtype) for j, a in enumerate(self.arrays)]

    @staticmethod
    def _copy(outs, sems, j, sem, block_rel, to_rel, src=None):
        blk = outs[j].at[_slot(_peer(block_rel))]
        return pltpu.make_async_remote_copy(
            src_ref=blk if src is None else src, dst_ref=blk,
            send_sem=sems[0].at[j, sem], recv_sem=sems[1].at[j, sem],
            device_id=_peer(to_rel), device_id_type=MESH_ID)

    def _local(self, ins, outs, sems, j):
        return pltpu.make_async_copy(self._src(ins, j), outs[j].at[_slot(_peer(0))], sems[2].at[j])

    def start(self, ins, outs, sems):
        for j in range(len(self.arrays)):
            self._local(ins, outs, sems, j).start()
            for rel in (1,) + self.chips:
                self._copy(outs, sems, j, rel, 0, rel, src=self._src(ins, j)).start()

    def forward(self, ins, outs, sems):
        for j in range(len(self.arrays)):
            for rel in self.chips:
                self._copy(outs, sems, j, rel, rel, 0).wait_recv()
                self._copy(outs, sems, j, rel ^ 1, rel, 1).start()

    def finish(self, ins, outs, sems):
        for j in range(len(self.arrays)):
            self._copy(outs, sems, j, 1, 1, 0).wait_recv()
            for rel in self.chips:
                self._copy(outs, sems, j, rel ^ 1, rel ^ 1, 0).wait_recv()
        for j in range(len(self.arrays)):
            for rel in (1,) + self.chips:
                self._copy(outs, sems, j, rel, 0, rel, src=self._src(ins, j)).wait_send()
            for rel in self.chips:
                self._copy(outs, sems, j, rel ^ 1, rel, 1).wait_send()
            self._local(ins, outs, sems, j).wait()


class _Scatter(_Exchange):
    def out_shape(self):
        return [_sds(a.shape, a.dtype) for a in self.arrays]

    @staticmethod
    def _copy(ins, outs, sems, j, rel):
        return pltpu.make_async_remote_copy(
            src_ref=ins[j].at[_slot(_peer(rel))], dst_ref=outs[j].at[_slot(_peer(0))],
            send_sem=sems[0].at[j, rel], recv_sem=sems[1].at[j, rel],
            device_id=_peer(rel), device_id_type=MESH_ID)

    @staticmethod
    def _arrival(outs, sems, j, rel):
        blk = outs[j].at[_slot(_peer(rel))]
        return pltpu.make_async_remote_copy(
            src_ref=blk, dst_ref=blk, send_sem=sems[0].at[j, rel], recv_sem=sems[1].at[j, rel],
            device_id=_peer(rel), device_id_type=MESH_ID)

    @staticmethod
    def _local(ins, outs, sems, j):
        me = _slot(_peer(0))
        return pltpu.make_async_copy(ins[j].at[me], outs[j].at[me], sems[2].at[j])

    def start(self, ins, outs, sems):
        for j in range(len(self.arrays)):
            self._local(ins, outs, sems, j).start()
            for rel in range(1, NDEV):
                self._copy(ins, outs, sems, j, rel).start()

    def forward(self, ins, outs, sems):
        pass

    def finish(self, ins, outs, sems):
        for j in range(len(self.arrays)):
            for rel in range(1, NDEV):
                self._arrival(outs, sems, j, rel).wait_recv()
        for j in range(len(self.arrays)):
            for rel in range(1, NDEV):
                self._copy(ins, outs, sems, j, rel).wait_send()
            self._local(ins, outs, sems, j).wait()


class _Together:
    def __init__(self, parts):
        self.parts = [c for c in parts if c is not None]
        self.arrays = [a for c in self.parts for a in c.arrays]

    def out_shape(self):
        return [s for c in self.parts for s in c.out_shape()]

    def scratch(self):
        return [s for c in self.parts for s in c.scratch()]

    def _each(self, method, ins, outs, sems):
        at = 0
        for q, c in enumerate(self.parts):
            n = len(c.arrays)
            getattr(c, method)(ins[at:at + n], outs[at:at + n], sems[3 * q:3 * q + 3])
            at += n

    def start(self, ins, outs, sems):
        self._each("start", ins, outs, sems)

    def forward(self, ins, outs, sems):
        self._each("forward", ins, outs, sems)

    def finish(self, ins, outs, sems):
        self._each("finish", ins, outs, sems)


def _together(*parts):
    parts = [c for c in parts if c is not None]
    return _Together(parts) if parts else None


def _call(body, *, name, args, in_specs, out_specs, out_shape, grid=(), scratch_shapes=(), semantics=(),
          aliases=None, comm=None):
    in_specs, out_specs, out_shape = list(in_specs), list(out_specs), list(out_shape)
    scratch_shapes = list(scratch_shapes)
    if comm is None:
        res = pl.pallas_call(
            body, name=name, grid=grid, in_specs=in_specs, out_specs=out_specs, out_shape=out_shape,
            scratch_shapes=scratch_shapes, input_output_aliases=aliases or {},
            compiler_params=_params(*semantics))(*args)
        return list(res), []
    n_in, n_out, n_scr, nc = len(in_specs), len(out_specs), len(scratch_shapes), len(comm.arrays)
    total = math.prod(grid)
    middle = min((total * 5) // 8, total - 1)

    def hosted(*refs):
        ins, cins = refs[:n_in], refs[n_in:n_in + nc]
        o0 = n_in + nc
        outs, couts = refs[o0:o0 + n_out], refs[o0 + n_out:o0 + n_out + nc]
        s0 = o0 + n_out + nc
        scr, sems = refs[s0:s0 + n_scr], refs[s0 + n_scr:]
        if total == 1:
            comm.start(cins, couts, sems)
            body(*ins, *outs, *scr)
            comm.forward(cins, couts, sems)
            comm.finish(cins, couts, sems)
            return
        step = 0
        for axis, size in enumerate(grid):
            step = step * size + pl.program_id(axis)
        pl.when(step == 0)(lambda: comm.start(cins, couts, sems))
        body(*ins, *outs, *scr)
        pl.when(step == middle)(lambda: comm.forward(cins, couts, sems))
        pl.when(step == total - 1)(lambda: comm.finish(cins, couts, sems))

    any_spec = pl.BlockSpec(memory_space=pl.ANY)
    res = pl.pallas_call(
        hosted, name=name, grid=grid,
        in_specs=in_specs + [any_spec] * nc, out_specs=out_specs + [any_spec] * nc,
        out_shape=out_shape + comm.out_shape(), scratch_shapes=scratch_shapes + comm.scratch(),
        input_output_aliases=aliases or {}, compiler_params=_params(*(("arbitrary",) * len(grid))),
    )(*args, *comm.arrays)
    return list(res[:n_out]), list(res[n_out:])


def _exchange_alone(comm, name):
    return _call(lambda: None, name=name, args=(), in_specs=(), out_specs=(), out_shape=(), comm=comm)[1]


def _cast_pad(w, rows, cols, name):
    nl, r, c = w.shape

    def body(w_ref, o_ref):
        if (rows, cols) != (r, c):
            o_ref[...] = jnp.zeros(o_ref.shape, BF16)
        o_ref[0, 0:r, 0:c] = w_ref[0].astype(BF16)

    return _call(body, name=name, grid=(nl,), args=(w,),
                 in_specs=[pl.BlockSpec((1, r, c), lambda i: (i, 0, 0))],
                 out_specs=[pl.BlockSpec((1, rows, cols), lambda i: (i, 0, 0))],
                 out_shape=[_sds((nl, rows, cols), BF16)], semantics=("parallel",))[0][0]


def _in_proj(x, gain, w, tm, name, comm):
    t, d = x.shape
    nb, _, bw = w.shape

    def body(x_ref, g_ref, w_ref, h_ref, p_ref):
        xv = x_ref[...]
        r = lax.rsqrt(jnp.mean(xv * xv, axis=-1, keepdims=True) + EPS)
        h = (xv * r * g_ref[...]).astype(BF16)
        h_ref[...] = h
        for k in range(nb):
            p_ref[:, k * bw:(k + 1) * bw] = _nn(h, w_ref[k]).astype(BF16)

    return _call(body, name=name, grid=(t // tm,), args=(x, gain, w),
                 in_specs=[_tile(tm, d), _row(d), _weight(w)],
                 out_specs=[_tile(tm, d), _tile(tm, nb * bw)],
                 out_shape=[_sds((t, d), BF16), _sds((t, nb * bw), BF16)],
                 semantics=("parallel",), comm=comm)


def _halo_specs(tm, t, d, col):
    nh, nhb = tm // HALO, t // HALO
    prev = pl.BlockSpec((HALO, d), lambda i: (jnp.maximum(i * nh - 1, 0), col))
    nxt = pl.BlockSpec((HALO, d), lambda i: (jnp.minimum((i + 1) * nh, nhb - 1), col))
    return prev, nxt


def _row_chunks(tm, rc, unroll, rows):
    n = tm // rc
    if unroll:
        for j in range(n):
            rows(j * rc)
    else:
        def step(j, carry):
            rows(pl.multiple_of(j * rc, rc))
            return carry

        lax.fori_loop(0, n, step, 0)


def _shifted(buf_ref, kb, r0, off, rc):
    return buf_ref[kb, pl.ds(r0 + off, rc), :]


def _dwconv(buf_ref, w_ref, out_ref, tm, flip, rc, unroll):
    nblk = out_ref.shape[1] // LANES

    def rows(r0):
        for kb in range(nblk):
            acc = jnp.zeros((rc, LANES), F32)
            for k in range(CONV_TAPS):
                off = (CONV_TAPS - k) if flip else (1 + k)
                acc = acc + w_ref[kb, k:k + 1, :] * _shifted(buf_ref, kb, r0, off, rc)
            out_ref[pl.ds(r0, rc), kb * LANES:(kb + 1) * LANES] = acc

    _row_chunks(tm, rc, unroll, rows)


def _fill_halo_buffer(buf, prev, body, nxt, first, last, tm):
    prev = jnp.where(first, 0.0, prev)
    nxt = jnp.where(last, 0.0, nxt)
    for kb in range(buf.shape[0]):
        lanes = slice(kb * LANES, (kb + 1) * LANES)
        buf[kb, 0:HALO, :] = prev[:, lanes]
        buf[kb, HALO:HALO + tm, :] = body[:, lanes]
        buf[kb, HALO + tm:HALO + tm + HALO, :] = nxt[:, lanes]


def _fill_glu_buffer(cbuf, av, ag, avp, agp, avn, agn, first, last, tm):
    c0p = avp[...].astype(F32) * _sig(agp[...].astype(F32))
    c0n = avn[...].astype(F32) * _sig(agn[...].astype(F32))
    c0 = av[...].astype(F32) * _sig(ag[...].astype(F32))
    _fill_halo_buffer(cbuf, c0p, c0, c0n, first, last, tm)


def _layernorm_stats(v):
    mu = jnp.mean(v, axis=-1, keepdims=True)
    cen = v - mu
    rstd = lax.rsqrt(jnp.mean(cen * cen, axis=-1, keepdims=True) + EPS)
    return cen * rstd, rstd


def _spatial_mix(ws_ref, vn_ref, bias_ref, mixed_ref, tm):
    for ci in range(tm // CHUNK):
        rs = slice(ci * CHUNK, (ci + 1) * CHUNK)
        for g in range(GROUPS):
            ls = slice(g * LANES, (g + 1) * LANES)
            mixed_ref[rs, ls] = _nn(ws_ref[g], vn_ref[rs, ls]) + bias_ref[:, ls]


def _mixer_fwd(proj, x, w, small, seq, tm, name, comm):
    t, d = x.shape
    hp_v, hn_v = _halo_specs(tm, t, d, 0)
    hp_g, hn_g = _halo_specs(tm, t, d, 1)

    def body(av, ag, u_ref, v_ref, ga_ref, gb_ref, avp, agp, avn, agn, x_ref,
             cw_ref, cb_ref, lg_ref, lb_ref, wco_ref, sg_ref, sb_ref, ws_ref, bias_ref, wso_ref,
             gba_ref, gbb_ref, wo_ref,
             c1_ref, ya_ref, yb_ref, x1_ref, cbuf, c1f, vn_ref, mixed_ref):
        i = pl.program_id(0)
        first = (i * tm) % seq == 0
        last = ((i + 1) * tm) % seq == 0
        _fill_glu_buffer(cbuf, av, ag, avp, agp, avn, agn, first, last, tm)
        _dwconv(cbuf, cw_ref, c1f, tm, flip=False, rc=CONV_ROWS_LOOP, unroll=False)
        c1 = c1f[...] + cb_ref[...]
        c1_ref[...] = c1.astype(BF16)
        c2hat, _ = _layernorm_stats(c1)
        c2 = c2hat * lg_ref[...] + lb_ref[...]
        c3 = (c2 * _sig(c2)).astype(BF16)
        ya = _nn(c3, wco_ref[...].reshape(d, d))
        ya_ref[...] = ya.astype(BF16)
        vhat, _ = _layernorm_stats(v_ref[...].astype(F32))
        vn_ref[...] = (vhat * sg_ref[...] + sb_ref[...]).astype(BF16)
        _spatial_mix(ws_ref, vn_ref, bias_ref, mixed_ref, tm)
        gated = (u_ref[...].astype(F32) * mixed_ref[...]).astype(BF16)
        yb = _nn(gated, wso_ref[...].reshape(d, d))
        yb_ref[...] = yb.astype(BF16)
        sa = _sig(ga_ref[...].astype(F32) + gba_ref[...])
        sb = _sig(gb_ref[...].astype(F32) + gbb_ref[...])
        merged = (sa * ya + sb * yb).astype(BF16)
        x1_ref[...] = x_ref[...] + _nn(merged, wo_ref[...].reshape(d, d))

    cols = [_tile(tm, d, j) for j in range(6)]
    return _call(
        body, name=name, grid=(t // tm,),
        args=(proj,) * 10 + (x, w['conv_w'], small['conv_b'], small['conv_ln_g'], small['conv_ln_b'], w['w_conv_out'],
                             small['sgu_ln_g'], small['sgu_ln_b'], small['ws'], small['bias_full'], w['w_sgu_out'],
                             small['gba'], small['gbb'], w['w_o']),
        in_specs=cols + [hp_v, hp_g, hn_v, hn_g, _tile(tm, d),
                         _weight(w['conv_w']), _row(d), _row(d), _row(d),
                         _weight(w['w_conv_out']), _row(d), _row(d),
                         _resident(small['ws'].shape), _resident(small['bias_full'].shape),
                         _weight(w['w_sgu_out']), _row(d), _row(d), _weight(w['w_o'])],
        out_specs=[_tile(tm, d)] * 4,
        out_shape=[_sds((t, d), BF16)] * 3 + [_sds((t, d), F32)],
        scratch_shapes=[pltpu.VMEM((d // LANES, tm + 2 * HALO, LANES), F32), pltpu.VMEM((tm, d), F32),
                        pltpu.VMEM((tm, d), BF16), pltpu.VMEM((tm, d), F32)],
        semantics=("parallel",), comm=comm)


def _pair_gate_up(wg, wu, name):
    n, c, d = wg.shape

    def body(wg_ref, wu_ref, o_ref):
        o_ref[0:c, :] = wg_ref[0]
        o_ref[c:2 * c, :] = wg_ref[1]
        o_ref[2 * c:3 * c, :] = wu_ref[0]
        o_ref[3 * c:4 * c, :] = wu_ref[1]

    pair = pl.BlockSpec((2, c, d), lambda k: (k, 0, 0))
    return _call(body, name=name, grid=(n // 2,), args=(wg, wu), in_specs=[pair, pair],
                 out_specs=[pl.BlockSpec((None, 4 * c, d), lambda k: (k, 0, 0))],
                 out_shape=[_sds((n // 2, 4 * c, d), wg.dtype)], semantics=("parallel",))[0][0]


def _pair_specs(w, tm):
    d = w['w_gate_up'].shape[2]
    up = pl.BlockSpec((None, 2 * FF_PAIR, d), lambda i, k: (k, 0, 0))
    down = pl.BlockSpec((None, FF_PAIR, d), lambda i, k: (k, 0, 0))
    return [up, down]


def _ffn_fwd(x1, gain, w, tm, name, comm):
    t, d = x1.shape
    tm = min(tm, t)
    npair = NDEV // 2
    hid = NDEV * FF_PAD
    wd_pairs = w['w_ffn_down'].reshape(npair, FF_PAIR, d)

    def body(x_ref, g_ref, wgu_ref, wd_ref, h_ref, gg_ref, uu_ref, x2_ref, hb_ref, acc_ref):
        k = pl.program_id(1)

        @pl.when(k == 0)
        def _():
            xv = x_ref[...]
            r = lax.rsqrt(jnp.mean(xv * xv, axis=-1, keepdims=True) + EPS)
            h = (xv * r * g_ref[...]).astype(BF16)
            hb_ref[...] = h
            h_ref[...] = h
            acc_ref[...] = xv

        gu = _nt(hb_ref[...], wgu_ref[...])
        gk = gu[:, 0:FF_PAIR]
        uk = gu[:, FF_PAIR:2 * FF_PAIR]
        gg_ref[...] = gk.astype(BF16)
        uu_ref[...] = uk.astype(BF16)
        ak = (gk * _sig(gk) * uk).astype(BF16)
        acc_ref[...] += _nn(ak, wd_ref[...])

        @pl.when(k == npair - 1)
        def _():
            x2_ref[...] = acc_ref[...]

    pair_cols = pl.BlockSpec((tm, FF_PAIR), lambda i, k: (i, k))
    return _call(
        body, name=name, grid=(t // tm, npair),
        args=(x1, gain, w['w_gate_up'], wd_pairs),
        in_specs=[_tile(tm, d), _row(d)] + _pair_specs(w, tm),
        out_specs=[_tile(tm, d), pair_cols, pair_cols, _tile(tm, d)],
        out_shape=[_sds((t, d), BF16), _sds((t, hid), BF16), _sds((t, hid), BF16), _sds((t, d), F32)],
        scratch_shapes=[pltpu.VMEM((tm, d), BF16), pltpu.VMEM((tm, d), F32)],
        semantics=("parallel", "arbitrary"), comm=comm)


def _init_small(first, small_ref, acc_ref):
    @pl.when(first)
    def _():
        small_ref[...] = jnp.zeros(small_ref.shape, F32)
        acc_ref[...] = jnp.zeros(acc_ref.shape, F32)


def _finish_small(last, small_ref, acc_ref, nq):
    @pl.when(last)
    def _():
        for q in range(nq):
            small_ref[q:q + 1, :] = jnp.sum(acc_ref[q], axis=0, keepdims=True)


def _loss_bwd(x, gain, target, tm, name):
    t, d = x.shape
    nsteps = t // tm

    def body(x_ref, g_ref, t_ref, dx_ref, small_ref, acc_ref):
        i = pl.program_id(0)
        _init_small(i == 0, small_ref, acc_ref)
        xv = x_ref[...]
        r = lax.rsqrt(jnp.mean(xv * xv, axis=-1, keepdims=True) + EPS)
        xhat = xv * r
        diff = xhat * g_ref[...] - t_ref[...]
        dy = diff * (1.0 / d)
        acc_ref[0] += _fold(dy * xhat)
        acc_ref[1] += _fold(diff * diff)
        dxhat = dy * g_ref[...]
        dx_ref[...] = r * (dxhat - xhat * jnp.mean(dxhat * xhat, axis=-1, keepdims=True))

        @pl.when(i == nsteps - 1)
        def _():
            small_ref[0:1, :] = jnp.sum(acc_ref[0], axis=0, keepdims=True)
            total = jnp.sum(acc_ref[1]) * (0.5 / d)
            small_ref[1:2, :] = jnp.full((1, d), total, F32)

    return _call(body, name=name, grid=(nsteps,), args=(x, gain, target),
                 in_specs=[_tile(tm, d), _row(d), _tile(tm, d)],
                 out_specs=[_tile(tm, d), _resident((SUBLANES, d))],
                 out_shape=[_sds((t, d), F32), _sds((SUBLANES, d), F32)],
                 scratch_shapes=[pltpu.VMEM((2, SUBLANES, d), F32)], semantics=("arbitrary",))[0]


def _ffn_bwd(dx2, x1, gain, gg, uu, w, tm, name, comm):
    t, d = x1.shape
    npair = NDEV // 2
    hid = NDEV * FF_PAD
    nsteps = t // tm
    wd_pairs = w['w_ffn_down'].reshape(npair, FF_PAIR, d)

    def body(dx_ref, x_ref, g_ref, gg_ref, uu_ref, wgu_ref, wd_ref,
             a_ref, dg_ref, du_ref, dx1_ref, small_ref, acc_ref, dxb_ref, dh_ref, dgu_ref):
        i, k = pl.program_id(0), pl.program_id(1)
        _init_small((i == 0) & (k == 0), small_ref, acc_ref)

        @pl.when(k == 0)
        def _():
            dxb_ref[...] = dx_ref[...].astype(BF16)
            dh_ref[...] = jnp.zeros(dh_ref.shape, F32)

        gk = gg_ref[...].astype(F32)
        uk = uu_ref[...].astype(F32)
        sg = _sig(gk)
        silu = gk * sg
        a_ref[...] = (silu * uk).astype(BF16)
        da = _nt(dxb_ref[...], wd_ref[...])
        dgk = (da * uk * (sg * (1.0 + gk * (1.0 - sg)))).astype(BF16)
        duk = (da * silu).astype(BF16)
        dg_ref[...] = dgk
        du_ref[...] = duk
        dgu_ref[:, 0:FF_PAIR] = dgk
        dgu_ref[:, FF_PAIR:2 * FF_PAIR] = duk
        dh_ref[...] += _nn(dgu_ref[...], wgu_ref[...])

        @pl.when(k == npair - 1)
        def _():
            xv = x_ref[...]
            dh = dh_ref[...]
            r = lax.rsqrt(jnp.mean(xv * xv, axis=-1, keepdims=True) + EPS)
            xhat = xv * r
            acc_ref[0] += _fold(dh * xhat)
            dxhat = dh * g_ref[...]
            dx1_ref[...] = dx_ref[...] + r * (dxhat - xhat * jnp.mean(dxhat * xhat, axis=-1, keepdims=True))

        _finish_small((i == nsteps - 1) & (k == npair - 1), small_ref, acc_ref, 1)

    pair_cols = pl.BlockSpec((tm, FF_PAIR), lambda i, k: (i, k))
    return _call(
        body, name=name, grid=(nsteps, npair),
        args=(dx2, x1, gain, gg, uu, w['w_gate_up'], wd_pairs),
        in_specs=[_tile(tm, d), _tile(tm, d), _row(d), pair_cols, pair_cols] + _pair_specs(w, tm),
        out_specs=[pair_cols] * 3 + [_tile(tm, d), _resident((SUBLANES, d))],
        out_shape=[_sds((t, hid), BF16)] * 3 + [_sds((t, d), F32), _sds((SUBLANES, d), F32)],
        scratch_shapes=[pltpu.VMEM((1, SUBLANES, d), F32), pltpu.VMEM((tm, d), BF16), pltpu.VMEM((tm, d), F32),
                        pltpu.VMEM((tm, 2 * FF_PAIR), BF16)],
        semantics=("arbitrary", "arbitrary"), comm=comm)


def _matmul_tn(a, b, a_blk, b_blk, stack, shard, tm, name, comm):
    t, ma = a.shape
    tm = min(tm, t)
    nb_ = b.shape[1]
    na, nb = ma // a_blk, nb_ // b_blk
    nsteps = t // tm
    cw = min(TN_COLS, b_blk)
    if stack == 'b':
        per = b_blk // shard
        out_shape, out_spec = (nb_ // shard, ma, shard), pl.BlockSpec((per, a_blk, shard), lambda i, j, k: (j, 0, 0))
    elif stack == 'a':
        per = a_blk // shard
        out_shape, out_spec = (ma // shard, shard, nb_), pl.BlockSpec((per, shard, b_blk), lambda i, j, k: (i, 0, 0))
    else:
        out_shape, out_spec = (ma, nb_), pl.BlockSpec((a_blk, b_blk), lambda i, j, k: (i, j))

    def body(a_ref, b_ref, o_ref, acc_ref):
        k = pl.program_id(2)

        @pl.when(k == 0)
        def _():
            acc_ref[...] = jnp.zeros(acc_ref.shape, F32)

        av = a_ref[...].astype(BF16)
        for c in range(0, b_blk, cw):
            acc_ref[:, c:c + cw] += _tn(av, b_ref[:, c:c + cw].astype(BF16))

        @pl.when(k == nsteps - 1)
        def _():
            if stack == 'b':
                for s in range(per):
                    o_ref[s] = acc_ref[:, s * shard:(s + 1) * shard].astype(BF16)
            elif stack == 'a':
                for s in range(per):
                    o_ref[s] = acc_ref[s * shard:(s + 1) * shard, :].astype(BF16)
            else:
                o_ref[...] = acc_ref[...].astype(BF16)

    res, got = _call(
        body, name=name, grid=(na, nb, nsteps), args=(a, b),
        in_specs=[pl.BlockSpec((tm, a_blk), lambda i, j, k: (k, i)), pl.BlockSpec((tm, b_blk), lambda i, j, k: (k, j))],
        out_specs=[out_spec], out_shape=[_sds(out_shape, BF16)],
        scratch_shapes=[pltpu.VMEM((a_blk, b_blk), F32)],
        semantics=("parallel", "parallel", "arbitrary"), comm=comm)
    return res[0], got


def _mixer_bwd(dx1, proj, c1, ya, yb, w, small, tm, name, comm):
    t, d = dx1.shape
    nsteps = t // tm
    nq = 6

    def body(dx_ref, u_ref, v_ref, ga_ref, gb_ref, c1_ref, ya_ref, yb_ref,
             lg_ref, lb_ref, wco_ref, sg_ref, sb_ref, ws_ref, wst_ref, bias_ref, wso_ref, gba_ref, gbb_ref, wo_ref,
             sel_ref,
             dp_ref, dc1_ref, mg_ref, c3_ref, gt_ref, dya_ref, dyb_ref, small_ref, dws_ref, dbs_ref,
             acc_ref, vn_ref, mixed_ref, dmix_ref, dvn_ref, dbias_ref):
        i = pl.program_id(0)
        _init_small(i == 0, small_ref, acc_ref)

        @pl.when(i == 0)
        def _():
            dws_ref[...] = jnp.zeros(dws_ref.shape, F32)
            dbs_ref[...] = jnp.zeros(dbs_ref.shape, F32)
            dbias_ref[...] = jnp.zeros(dbias_ref.shape, F32)

        dmerged = _nt(dx_ref[...].astype(BF16), wo_ref[...].reshape(d, d))
        ya = ya_ref[...].astype(F32)
        yb = yb_ref[...].astype(F32)
        sa = _sig(ga_ref[...].astype(F32) + gba_ref[...])
        sb = _sig(gb_ref[...].astype(F32) + gbb_ref[...])
        mg_ref[...] = (sa * ya + sb * yb).astype(BF16)
        dya = (dmerged * sa).astype(BF16)
        dyb = (dmerged * sb).astype(BF16)
        dya_ref[...] = dya
        dyb_ref[...] = dyb
        dga = dmerged * ya * (sa * (1.0 - sa))
        dgb = dmerged * yb * (sb * (1.0 - sb))
        acc_ref[0] += _fold(dga)
        acc_ref[1] += _fold(dgb)
        dp_ref[:, 0:2 * d] = jnp.zeros((tm, 2 * d), BF16)
        dp_ref[:, 4 * d:5 * d] = dga.astype(BF16)
        dp_ref[:, 5 * d:6 * d] = dgb.astype(BF16)
        c2hat, rstd = _layernorm_stats(c1_ref[...].astype(F32))
        c2 = c2hat * lg_ref[...] + lb_ref[...]
        s2 = _sig(c2)
        c3_ref[...] = (c2 * s2).astype(BF16)
        dc3 = _nt(dya, wco_ref[...].reshape(d, d))
        dc2 = dc3 * (s2 * (1.0 + c2 * (1.0 - s2)))
        acc_ref[2] += _fold(dc2 * c2hat)
        acc_ref[3] += _fold(dc2)
        dc2hat = dc2 * lg_ref[...]
        dc1_ref[...] = (rstd * (dc2hat - jnp.mean(dc2hat, axis=-1, keepdims=True)
                                - c2hat * jnp.mean(dc2hat * c2hat, axis=-1, keepdims=True))).astype(BF16)
        vhat, rstd_v = _layernorm_stats(v_ref[...].astype(F32))
        vn_ref[...] = (vhat * sg_ref[...] + sb_ref[...]).astype(BF16)
        _spatial_mix(ws_ref, vn_ref, bias_ref, mixed_ref, tm)
        u = u_ref[...].astype(F32)
        mixed = mixed_ref[...]
        gt_ref[...] = (u * mixed).astype(BF16)
        dgated = _nt(dyb, wso_ref[...].reshape(d, d))
        dp_ref[:, 2 * d:3 * d] = (dgated * mixed).astype(BF16)
        dmix_ref[...] = dgated * u
        for ci in range(tm // CHUNK):
            rs = slice(ci * CHUNK, (ci + 1) * CHUNK)
            dbias_ref[...] += dmix_ref[rs, :]
            for g in range(GROUPS):
                ls = slice(g * LANES, (g + 1) * LANES)
                dm = dmix_ref[rs, ls].astype(BF16)
                dws_ref[g] += _nt(dm, vn_ref[rs, ls])
                dvn_ref[rs, ls] = _nn(wst_ref[g], dm)
        dvn = dvn_ref[...]
        acc_ref[4] += _fold(dvn * vhat)
        acc_ref[5] += _fold(dvn)
        dvhat = dvn * sg_ref[...]
        dp_ref[:, 3 * d:4 * d] = (rstd_v * (dvhat - jnp.mean(dvhat, axis=-1, keepdims=True)
                                           - vhat * jnp.mean(dvhat * vhat, axis=-1, keepdims=True))).astype(BF16)
        _finish_small(i == nsteps - 1, small_ref, acc_ref, nq)

        @pl.when(i == nsteps - 1)
        def _():
            db = dbias_ref[...]
            hi = db.astype(BF16)
            lo = (db - hi.astype(F32)).astype(BF16)
            dbs_ref[...] = _nt(sel_ref[...], hi) + _nt(sel_ref[...], lo)

    cols = [_tile(tm, d, j) for j in (2, 3, 4, 5)]
    return _call(
        body, name=name, grid=(nsteps,),
        args=(dx1, proj, proj, proj, proj, c1, ya, yb,
              small['conv_ln_g'], small['conv_ln_b'], w['w_conv_out'], small['sgu_ln_g'], small['sgu_ln_b'],
              small['ws'], small['wst'], small['bias_full'], w['w_sgu_out'], small['gba'], small['gbb'], w['w_o'],
              small['group_sel']),
        in_specs=[_tile(tm, d)] + cols + [_tile(tm, d)] * 3 + [
            _row(d), _row(d), _weight(w['w_conv_out']),
            _row(d), _row(d), _resident(small['ws'].shape), _resident(small['wst'].shape),
            _resident(small['bias_full'].shape), _weight(w['w_sgu_out']), _row(d), _row(d),
            _weight(w['w_o']), _resident(small['group_sel'].shape)],
        out_specs=[_tile(tm, 6 * d)] + [_tile(tm, d)] * 6 + [
            _resident((SUBLANES, d)), _resident((GROUPS, CHUNK, CHUNK)), _resident((GROUPS, CHUNK))],
        out_shape=[_sds((t, 6 * d), BF16)] + [_sds((t, d), BF16)] * 6 + [
            _sds((SUBLANES, d), F32), _sds((GROUPS, CHUNK, CHUNK), F32), _sds((GROUPS, CHUNK), F32)],
        scratch_shapes=[pltpu.VMEM((nq, SUBLANES, d), F32), pltpu.VMEM((tm, d), BF16), pltpu.VMEM((tm, d), F32),
                        pltpu.VMEM((tm, d), F32), pltpu.VMEM((tm, d), F32), pltpu.VMEM((CHUNK, d), F32)],
        semantics=("arbitrary",), comm=comm)


def _conv_bwd(dproj, dc1, proj, w, pairs, seq, tm, name, comm):
    t, d = dc1.shape
    nsteps = t // tm
    nblk = d // LANES
    npairs = len(pairs)
    hp_v, hn_v = _halo_specs(tm, t, d, 0)
    hp_g, hn_g = _halo_specs(tm, t, d, 1)
    hp_d, hn_d = _halo_specs(tm, t, d, 0)

    def body(*refs):
        (dp_in, dc_ref, dcp, dcn, av, ag, avp, agp, avn, agn, cw_ref), refs = refs[:11], refs[11:]
        ab_refs, refs = refs[:2 * npairs], refs[2 * npairs:]
        (dp_ref, dcw_ref, small_ref), refs = refs[:3], refs[3:]
        grad_refs, refs = refs[:npairs], refs[npairs:]
        acc_ref, cbuf, dbuf, dc0f, accw, gacc = refs
        del dp_in
        i = pl.program_id(0)
        _init_small(i == 0, small_ref, acc_ref)

        @pl.when(i == 0)
        def _():
            accw[...] = jnp.zeros(accw.shape, F32)
            dcw_ref[...] = jnp.zeros(dcw_ref.shape, F32)
            gacc[...] = jnp.zeros(gacc.shape, F32)

        first = (i * tm) % seq == 0
        last = ((i + 1) * tm) % seq == 0
        _fill_glu_buffer(cbuf, av, ag, avp, agp, avn, agn, first, last, tm)
        dc1v = dc_ref[...].astype(F32)
        _fill_halo_buffer(dbuf, dcp[...].astype(F32), dc1v, dcn[...].astype(F32), first, last, tm)
        acc_ref[0] += _fold(dc1v)
        _dwconv(dbuf, cw_ref, dc0f, tm, flip=True, rc=CONV_ROWS, unroll=True)

        def rows(r0):
            for kb in range(nblk):
                dv = dbuf[kb, r0 + HALO:r0 + HALO + CONV_ROWS, :]
                for k in range(CONV_TAPS):
                    accw[kb, k] += _fold(dv * _shifted(cbuf, kb, r0, 1 + k, CONV_ROWS))

        _row_chunks(tm, CONV_ROWS, True, rows)
        for q in range(npairs):
            aq = ab_refs[2 * q][...].astype(BF16)
            for c in range(0, d, TN_COLS):
                gacc[q, :, c:c + TN_COLS] += _tn(aq, ab_refs[2 * q + 1][:, c:c + TN_COLS].astype(BF16))
        sg = _sig(ag[...].astype(F32))
        avv = av[...].astype(F32)
        dc0 = dc0f[...]
        dp_ref[:, 0:d] = (dc0 * sg).astype(BF16)
        dp_ref[:, d:2 * d] = (dc0 * avv * (sg * (1.0 - sg))).astype(BF16)
        _finish_small(i == nsteps - 1, small_ref, acc_ref, 1)

        @pl.when(i == nsteps - 1)
        def _():
            for kb in range(nblk):
                dcw_ref[kb] = jnp.sum(accw[kb], axis=1)
            for q in range(npairs):
                grad_refs[q][...] = gacc[q].astype(BF16)

    return _call(
        body, name=name, grid=(nsteps,),
        args=(dproj, dc1, dc1, dc1, proj, proj, proj, proj, proj, proj, w['conv_w']) + tuple(a for ab in pairs for a in ab),
        in_specs=[pl.BlockSpec(memory_space=pl.ANY), _tile(tm, d), hp_d, hn_d, _tile(tm, d, 0), _tile(tm, d, 1),
                  hp_v, hp_g, hn_v, hn_g, _weight(w['conv_w'])] + [_tile(tm, d)] * (2 * npairs),
        out_specs=[_tile(tm, 2 * d), _resident((nblk, CONV_TAPS_PADDED, LANES)), _resident((SUBLANES, d))]
        + [_resident((d, d))] * npairs,
        out_shape=[_sds(dproj.shape, BF16), _sds((nblk, CONV_TAPS_PADDED, LANES), F32), _sds((SUBLANES, d), F32)]
        + [_sds((d, d), BF16)] * npairs,
        scratch_shapes=[pltpu.VMEM((1, SUBLANES, d), F32), pltpu.VMEM((nblk, tm + 2 * HALO, LANES), F32),
                        pltpu.VMEM((nblk, tm + 2 * HALO, LANES), F32), pltpu.VMEM((tm, d), F32),
                        pltpu.VMEM((nblk, CONV_TAPS_PADDED, SUBLANES, LANES), F32),
                        pltpu.VMEM((npairs, d, d), F32)],
        aliases={0: 0}, semantics=("arbitrary",), comm=comm)


def _in_proj_bwd(dproj, x, dx1, gain, w, tm, name, comm):
    t, d = x.shape
    nb, _, bw = w.shape
    nsteps = t // tm

    def body(dp_ref, x_ref, dx1_ref, g_ref, w_ref, dx_ref, small_ref, acc_ref):
        i = pl.program_id(0)
        _init_small(i == 0, small_ref, acc_ref)
        dh = jnp.zeros((tm, d), F32)
        for k in range(nb):
            dh = dh + _nt(dp_ref[:, k * bw:(k + 1) * bw], w_ref[k])
        xv = x_ref[...]
        r = lax.rsqrt(jnp.mean(xv * xv, axis=-1, keepdims=True) + EPS)
        xhat = xv * r
        acc_ref[0] += _fold(dh * xhat)
        dxhat = dh * g_ref[...]
        dx_ref[...] = dx1_ref[...] + r * (dxhat - xhat * jnp.mean(dxhat * xhat, axis=-1, keepdims=True))
        _finish_small(i == nsteps - 1, small_ref, acc_ref, 1)

    return _call(body, name=name, grid=(nsteps,), args=(dproj, x, dx1, gain, w),
                 in_specs=[_tile(tm, nb * bw), _tile(tm, d), _tile(tm, d), _row(d), _weight(w)],
                 out_specs=[_tile(tm, d), _resident((SUBLANES, d))],
                 out_shape=[_sds((t, d), F32), _sds((SUBLANES, d), F32)],
                 scratch_shapes=[pltpu.VMEM((1, SUBLANES, d), F32)], semantics=("arbitrary",), comm=comm)


def _adam(wv, g, mv, vv):
    m = ADAM_B1 * mv + (1.0 - ADAM_B1) * g
    v = ADAM_B2 * vv + (1.0 - ADAM_B2) * jnp.square(g)
    m_hat = m / (1.0 - ADAM_B1 ** ADAM_STEP)
    v_hat = v / (1.0 - ADAM_B2 ** ADAM_STEP)
    delta = -ADAM_LR * (m_hat / (jnp.sqrt(v_hat) + ADAM_EPS) + ADAM_WD * wv)
    return delta, m, v


def _adamw_layer(layer, w, m, v, parts, prev, nsplit, name):
    nl, r, c = w.shape
    npart, pr, pc = parts.shape
    rt, prt = r // nsplit, pr // nsplit

    def body(w_ref, m_ref, v_ref, p_ref, *rest):
        g_ref, d_ref, nm_ref, nv_ref = rest[-4:]
        g = p_ref[0, 0:rt, 0:c].astype(F32)
        for s in range(1, npart):
            g = g + p_ref[s, 0:rt, 0:c].astype(F32)
        delta, mn, vn = _adam(w_ref[0], g, m_ref[0], v_ref[0])
        g_ref[0] = g
        d_ref[0] = delta
        nm_ref[0] = mn
        nv_ref[0] = vn

    wspec = pl.BlockSpec((1, rt, c), lambda i: (layer, i, 0))
    pspec = pl.BlockSpec((npart, prt, pc), lambda i: (0, i, 0))
    in_specs = [wspec, wspec, wspec, pspec]
    args = [w, m, v, parts]
    aliases = {}
    if prev is not None:
        in_specs += [pl.BlockSpec(memory_space=pl.ANY)] * 4
        args += list(prev)
        aliases = {4 + q: q for q in range(4)}
    return _call(body, name=name, grid=(nsplit,), args=args, in_specs=in_specs, out_specs=[wspec] * 4,
                 out_shape=[_sds(w.shape, F32)] * 4, aliases=aliases, semantics=("parallel",))[0]


VEC_ROWS = {'gate_bias': (0, 1), 'conv_ln_g': 2, 'conv_ln_b': 3, 'sgu_ln_g': 4, 'sgu_ln_b': 5,
            'conv_b': SUBLANES, 'norm_ffn': 2 * SUBLANES}
FINAL_ROW = 3 * SUBLANES
LOSS_ROW = 3 * SUBLANES + 1


def _adamw_small(gathered, params, moments_m, moments_v):
    names = list(params)
    nper = len(names)
    nl = len(gathered)

    def body(*refs):
        g_refs = [refs[4 * l:4 * l + 4] for l in range(nl)]
        rest = refs[4 * nl:]
        w_refs = dict(zip(names, rest[0:nper]))
        m_refs = dict(zip(names, rest[nper:2 * nper]))
        v_refs = dict(zip(names, rest[2 * nper:3 * nper]))
        outs = rest[3 * nper:]
        loss_ref = outs[4 * nper]
        o = {kind: dict(zip(names, outs[q * nper:(q + 1) * nper])) for q, kind in enumerate("gdmv")}

        def put(nm, idx, g):
            delta, mn, vn = _adam(w_refs[nm][idx], g, m_refs[nm][idx], v_refs[nm][idx])
            o["g"][nm][idx] = g
            o["d"][nm][idx] = delta
            o["m"][nm][idx] = mn
            o["v"][nm][idx] = vn

        def total(ref, *idx):
            g = ref[(0, *idx)]
            for s in range(1, NDEV):
                g = g + ref[(s, *idx)]
            return g

        for l, (vec_ref, dws_ref, dbs_ref, vin_ref) in enumerate(g_refs):
            dd = vec_ref.shape[2]
            put('w_spatial', (l,), total(dws_ref))
            put('b_spatial', (l,), total(dbs_ref))
            put('norm_mix', (slice(l, l + 1), slice(None)), total(vin_ref, slice(0, 1)))
            for nm, rr in VEC_ROWS.items():
                if nm == 'gate_bias':
                    put(nm, (slice(l, l + 1), slice(0, dd)), total(vec_ref, slice(rr[0], rr[0] + 1)))
                    put(nm, (slice(l, l + 1), slice(dd, 2 * dd)), total(vec_ref, slice(rr[1], rr[1] + 1)))
                else:
                    put(nm, (slice(l, l + 1), slice(None)), total(vec_ref, slice(rr, rr + 1)))
        last = g_refs[nl - 1][0]
        put('norm_final', (slice(0, 1), slice(None)), total(last, slice(FINAL_ROW, FINAL_ROW + 1)))
        loss_ref[...] = total(last, slice(LOSS_ROW, LOSS_ROW + 1))

    ins = [a for g in gathered for a in g] + [params[n] for n in names] + [moments_m[n] for n in names] + [moments_v[n] for n in names]
    out_shape = [_sds(params[n].shape, F32) for n in names] * 4 + [_sds((1, gathered[0][0].shape[2]), F32)]
    res = pl.pallas_call(body, name="adamw_small", out_shape=out_shape, compiler_params=_params())(*ins)
    return {kind: dict(zip(names, res[q * nper:(q + 1) * nper])) for q, kind in enumerate("gdmv")}, res[4 * nper]


def _hidden_major(a):
    return jnp.swapaxes(a, 1, 2)


def _prepare_weights(p):
    d = p['w_in'].shape[1]
    return {
        'w_in': _cast_pad(p['w_in'], d, p['w_in'].shape[2], "cast_w_in"),
        'w_conv_out': _cast_pad(p['w_conv_out'], p['w_conv_out'].shape[1], d, "cast_w_conv_out"),
        'w_sgu_out': _cast_pad(p['w_sgu_out'], p['w_sgu_out'].shape[1], d, "cast_w_sgu_out"),
        'w_o': _cast_pad(p['w_o'], p['w_o'].shape[1], d, "cast_w_o"),
        'w_ffn_gate': _cast_pad(_hidden_major(p['w_ffn_gate']), FF_PAD, d, "cast_w_ffn_gate"),
        'w_ffn_up': _cast_pad(_hidden_major(p['w_ffn_up']), FF_PAD, d, "cast_w_ffn_up"),
        'w_ffn_down': _cast_pad(p['w_ffn_down'], FF_PAD, d, "cast_w_ffn_down"),
        'conv_w': jnp.pad(p['conv_w'][:, :, 0, :], ((0, 0), (0, CONV_TAPS_PADDED - CONV_TAPS), (0, 0))),
    }


def _gather_of(shards, names, layer):
    return _Gather([shards[n] for n in names], [layer] * len(names))


def _layer_small(p, layer):
    d = p['norm_mix'].shape[1]
    ws = p['w_spatial'][layer]
    rows = {n: p[n][layer:layer + 1] for n in ('norm_mix', 'norm_ffn', 'conv_b', 'conv_ln_g', 'conv_ln_b',
                                               'sgu_ln_g', 'sgu_ln_b')}
    return {
        **rows,
        'ws': ws.astype(BF16), 'wst': jnp.swapaxes(ws, 1, 2).astype(BF16),
        'bias_full': jnp.repeat(p['b_spatial'][layer].T, LANES, axis=1),
        'gba': p['gate_bias'][layer:layer + 1, 0:d], 'gbb': p['gate_bias'][layer:layer + 1, d:2 * d],
        'group_sel': (jnp.arange(d)[None, :] // LANES == jnp.arange(GROUPS)[:, None]).astype(BF16),
    }


class _GradQueue:
    def __init__(self):
        self.pending = []
        self.done = {}

    def push(self, key, array):
        self.pending.append((key, array))

    def take(self):
        keys = [k for k, _ in self.pending]
        comm = _Scatter([a for _, a in self.pending]) if self.pending else None
        self.pending = []
        return keys, comm

    def put(self, keys, arrays):
        self.done.update(zip(keys, arrays))


def _forward_backward(p, shards, x, target, seq):
    nl = p['norm_mix'].shape[0]
    d = x.shape[1]
    smalls = [_layer_small(p, l) for l in range(nl)]
    w_in = _exchange_alone(_gather_of(shards, ['w_in'], 0), "gather_w_in_0")[0]
    saved = []
    for l in range(nl):
        (h, proj), got = _in_proj(x, smalls[l]['norm_mix'], w_in, TILE_IN, f"in_proj_{l}",
                                  _gather_of(shards, MIXER_WEIGHTS, l))
        w = dict(zip(MIXER_WEIGHTS, got), w_in=w_in)
        (c1, ya, yb, x1), got = _mixer_fwd(proj, x, w, smalls[l], seq, TILE_MIX, f"mixer_fwd_{l}",
                                           _gather_of(shards, FFN_WEIGHTS, l))
        w.update(zip(FFN_WEIGHTS, got))
        w['w_gate_up'] = _pair_gate_up(w['w_ffn_gate'], w['w_ffn_up'], f"pair_gate_up_{l}")
        nxt = _gather_of(shards, ['w_in'], l + 1) if l + 1 < nl else None
        (h2, gg, uu, x2), got = _ffn_fwd(x1, smalls[l]['norm_ffn'], w, TILE_FFN_FWD, f"ffn_fwd_{l}", nxt)
        saved.append(dict(x=x, h=h, proj=proj, c1=c1, ya=ya, yb=yb, x1=x1, h2=h2, gg=gg, uu=uu, w=w))
        x = x2
        if got:
            w_in = got[0]
    dx, small_loss = _loss_bwd(x, p['norm_final'][None, :], target, TILE_LOSS, "loss_bwd")
    queue = _GradQueue()
    small_gathered = [None] * nl
    small_pending = None
    rows = d // NDEV
    hid = NDEV * FF_PAD
    for l in reversed(range(nl)):
        s = saved[l]
        w = s['w']

        def hosted(fn, *args, extra=None):
            keys, comm = queue.take()
            res, got = fn(*args, _together(comm, extra))
            queue.put(keys, got[:len(keys)])
            return res, got[len(keys):]

        def tn(key, a, b, a_blk, b_blk, stack, shard, reshape=None, host=False):
            keys, comm = queue.take() if host else ([], None)
            g, got = _matmul_tn(a, b, a_blk, b_blk, stack, shard, TILE_TN, f"dw_{key}_{l}", comm)
            queue.put(keys, got)
            queue.push((l, key), g if reshape is None else g.reshape(reshape))

        extra = small_pending[1] if small_pending else None
        (act, dgg, duu, dx1, small_ffn), got = hosted(_ffn_bwd, dx, s['x1'], smalls[l]['norm_ffn'], s['gg'], s['uu'], w,
                                                      TILE_FFN_BWD, f"ffn_bwd_{l}", extra=extra)
        if small_pending:
            small_gathered[small_pending[0]] = got
            small_pending = None
        tn('w_ffn_gate', dgg, s['h2'], hid, d, 'a', FF_PAD)
        tn('w_ffn_up', duu, s['h2'], hid, d, 'a', FF_PAD)
        (dproj, dc1, merged, c3, gated, dya, dyb, small_mix, dws, dbs), _ = hosted(
            _mixer_bwd, dx1, s['proj'], s['c1'], s['ya'], s['yb'], w, smalls[l], TILE_MIX, f"mixer_bwd_{l}")
        tn('w_ffn_down', act, dx, hid, d, 'a', FF_PAD)
        (dproj, g_conv, small_conv, g_o, g_co, g_so), _ = hosted(
            _conv_bwd, dproj, dc1, s['proj'], w, [(merged, dx1), (c3, dya), (gated, dyb)], seq, TILE_MIX,
            f"conv_bwd_{l}")
        queue.push((l, 'conv_w'), g_conv)
        queue.push((l, 'w_o'), g_o.reshape(NDEV, rows, d))
        queue.push((l, 'w_conv_out'), g_co.reshape(NDEV, rows, d))
        queue.push((l, 'w_sgu_out'), g_so.reshape(NDEV, rows, d))
        tn('w_in', s['h'], dproj, d, hid, 'b', w['w_in'].shape[2], host=True)
        blocks = [small_mix, small_conv, small_ffn] + ([small_loss] if l == nl - 1 else [])
        small_main = [jnp.concatenate(blocks, axis=0), dws, dbs]
        if l == 0:
            (dx, small_in), small_main_gathered = hosted(
                _in_proj_bwd, dproj, s['x'], dx1, smalls[l]['norm_mix'], w['w_in'], TILE_IN, f"in_proj_bwd_{l}",
                extra=_Gather(small_main))
        else:
            (dx, small_in), _ = _in_proj_bwd(dproj, s['x'], dx1, smalls[l]['norm_mix'], w['w_in'], TILE_IN,
                                             f"in_proj_bwd_{l}", None)
        small_pending = (l, _Gather(small_main + [small_in]))
    keys, comm = queue.take()
    last = _exchange_alone(_together(comm, _Gather([small_in])), "exchange_last_grads")
    queue.put(keys, last[:len(keys)])
    small_gathered[0] = small_main_gathered + last[len(keys):]
    return small_gathered, dx, queue.done


def _train_step(p, m, v, x3, target3):
    nl = p['norm_mix'].shape[0]
    bsz, seq, d = x3.shape
    x = x3.reshape(bsz * seq, d)
    target = target3.reshape(bsz * seq, d)
    small_gathered, dx, exchanged = _forward_backward(p, _prepare_weights(p), x, target, seq)

    out = {kind: {} for kind in "gdmv"}
    splits = {'w_in': 4, 'w_conv_out': 1, 'w_sgu_out': 1, 'w_o': 1, 'w_ffn_gate': 1, 'w_ffn_up': 1, 'w_ffn_down': 1, 'conv_w': 1}
    for n in splits:
        if n == 'conv_w':
            pad = ((0, 0), (0, CONV_TAPS_PADDED - CONV_TAPS), (0, 0))
            wl, ml, vl = (jnp.pad(a[n][:, :, 0, :], pad) for a in (p, m, v))
        elif n in ('w_ffn_gate', 'w_ffn_up'):
            wl, ml, vl = (_hidden_major(a[n]) for a in (p, m, v))
        else:
            wl, ml, vl = p[n], m[n], v[n]
        prev = None
        for l in range(nl):
            prev = _adamw_layer(l, wl, ml, vl, exchanged[(l, n)], prev, splits[n], f"adamw_{n}_{l}")
        for kind, arr in zip("gdmv", prev):
            if n == 'conv_w':
                arr = arr[:, 0:CONV_TAPS, None, :]
            elif n in ('w_ffn_gate', 'w_ffn_up'):
                arr = _hidden_major(arr)
            out[kind][n] = arr
    small_names = ['norm_mix', 'gate_bias', 'conv_b', 'conv_ln_g', 'conv_ln_b', 'sgu_ln_g', 'sgu_ln_b', 'w_spatial',
                   'b_spatial', 'norm_ffn', 'norm_final']

    def two_d(a):
        return a[None, :] if a.ndim == 1 else a

    res, loss_row = _adamw_small(small_gathered, {n: two_d(p[n]) for n in small_names},
                                 {n: two_d(m[n]) for n in small_names}, {n: two_d(v[n]) for n in small_names})
    loss = loss_row[0, 0]
    for kind in "gdmv":
        for n in small_names:
            out[kind][n] = res[kind][n].reshape(p[n].shape)
    grad_x = dx.reshape(bsz, seq, d)
    return (loss, grad_x, *[out[kind][n] for kind in "gdmv" for n in WEIGHT_NAMES])


def kernel(x, norm_mix, w_in, gate_bias, conv_w, conv_b, conv_ln_g, conv_ln_b, w_conv_out, sgu_ln_g, sgu_ln_b, w_spatial, b_spatial, w_sgu_out, w_o, norm_ffn, w_ffn_gate, w_ffn_up, w_ffn_down, norm_final, loss_target, m_norm_mix, m_w_in, m_gate_bias, m_conv_w, m_conv_b, m_conv_ln_g, m_conv_ln_b, m_w_conv_out, m_sgu_ln_g, m_sgu_ln_b, m_w_spatial, m_b_spatial, m_w_sgu_out, m_w_o, m_norm_ffn, m_w_ffn_gate, m_w_ffn_up, m_w_ffn_down, m_norm_final, v_norm_mix, v_w_in, v_gate_bias, v_conv_w, v_conv_b, v_conv_ln_g, v_conv_ln_b, v_w_conv_out, v_sgu_ln_g, v_sgu_ln_b, v_w_spatial, v_b_spatial, v_w_sgu_out, v_w_o, v_norm_ffn, v_w_ffn_gate, v_w_ffn_up, v_w_ffn_down, v_norm_final):
    p = dict(zip(WEIGHT_NAMES, (norm_mix, w_in, gate_bias, conv_w, conv_b, conv_ln_g, conv_ln_b, w_conv_out, sgu_ln_g, sgu_ln_b, w_spatial, b_spatial, w_sgu_out, w_o, norm_ffn, w_ffn_gate, w_ffn_up, w_ffn_down, norm_final)))
    m = dict(zip(WEIGHT_NAMES, (m_norm_mix, m_w_in, m_gate_bias, m_conv_w, m_conv_b, m_conv_ln_g, m_conv_ln_b, m_w_conv_out, m_sgu_ln_g, m_sgu_ln_b, m_w_spatial, m_b_spatial, m_w_sgu_out, m_w_o, m_norm_ffn, m_w_ffn_gate, m_w_ffn_up, m_w_ffn_down, m_norm_final)))
    v = dict(zip(WEIGHT_NAMES, (v_norm_mix, v_w_in, v_gate_bias, v_conv_w, v_conv_b, v_conv_ln_g, v_conv_ln_b, v_w_conv_out, v_sgu_ln_g, v_sgu_ln_b, v_w_spatial, v_b_spatial, v_w_sgu_out, v_w_o, v_norm_ffn, v_w_ffn_gate, v_w_ffn_up, v_w_ffn_down, v_norm_final)))
    return _train_step(p, m, v, x, loss_target)
```

```python
import math

import jax
import jax.numpy as jnp
from jax import lax
from jax.experimental import pallas as pl
from jax.experimental.pallas import tpu as pltpu

F32 = jnp.float32
BF16 = jnp.bfloat16
MESH_ID = pl.DeviceIdType.MESH

NDEV = 8
EPS = 1e-6
CONV_TAPS = 31
CONV_TAPS_PADDED = 32
HALO = 16
CONV_ROWS = 128
CONV_ROWS_LOOP = 64
LANES = 128
SUBLANES = 8
CHUNK = 128
GROUPS = 8
FF_PAD = 384
FF_PAIR = 2 * FF_PAD
TN_COLS = 512
VMEM_LIMIT_BYTES = 56 * 1024 * 1024
VMEM_LIMIT_FFN_BWD = 62 * 1024 * 1024

ADAM_LR = 0.001
ADAM_B1 = 0.9
ADAM_B2 = 0.999
ADAM_EPS = 1e-08
ADAM_WD = 0.01
ADAM_STEP = 10

TILE_IN = 512
TILE_MIX = 256
TILE_FFN_FWD = 1024
TILE_FFN_BWD = 1024
TILE_TN = 1024
TILE_LOSS = 512

WEIGHT_NAMES = ['norm_mix', 'w_in', 'gate_bias', 'conv_w', 'conv_b', 'conv_ln_g', 'conv_ln_b', 'w_conv_out',
                'sgu_ln_g', 'sgu_ln_b', 'w_spatial', 'b_spatial', 'w_sgu_out', 'w_o', 'norm_ffn', 'w_ffn_gate',
                'w_ffn_up', 'w_ffn_down', 'norm_final']
MIXER_WEIGHTS = ['w_conv_out', 'w_sgu_out', 'w_o', 'conv_w']
FFN_WEIGHTS = ['w_ffn_gate', 'w_ffn_up', 'w_ffn_down']


def _sds(shape, dtype):
    return jax.ShapeDtypeStruct(tuple(shape), dtype)


def _params(*sem, vmem=VMEM_LIMIT_BYTES):
    return pltpu.CompilerParams(dimension_semantics=sem or None, vmem_limit_bytes=vmem)


def _nn(a, b):
    return jnp.dot(a, b, preferred_element_type=F32)


def _nt(a, b):
    return lax.dot_general(a, b, (((1,), (1,)), ((), ())), preferred_element_type=F32)


def _tn(a, b):
    return lax.dot_general(a, b, (((0,), (0,)), ((), ())), preferred_element_type=F32)


def _sig(v):
    return jax.nn.sigmoid(v)


def _fold(v):
    r, c = v.shape
    return jnp.sum(v.reshape(r // SUBLANES, SUBLANES, c), axis=0)


def _tile(tm, n, j=0):
    return pl.BlockSpec((tm, n), lambda i, *_: (i, j))


def _row(n):
    return pl.BlockSpec((1, n), lambda *_: (0, 0))


def _resident(shape):
    nd = len(shape)
    return pl.BlockSpec(tuple(shape), lambda *_: (0,) * nd)


def _weight(w):
    nd = w.ndim
    return pl.BlockSpec(tuple(w.shape), lambda *_: (0,) * nd, pipeline_mode=pl.Buffered(1))


def _peer(rel):
    x, y, c = lax.axis_index("x"), lax.axis_index("y"), lax.axis_index("c")
    return (1 - x if rel & 4 else x, 1 - y if rel & 2 else y, 1 - c if rel & 1 else c)


def _slot(pos):
    return 4 * pos[0] + 2 * pos[1] + pos[2]


class _Exchange:
    def __init__(self, arrays, layers=None):
        self.arrays = list(arrays)
        self.layers = list(layers) if layers is not None else [None] * len(self.arrays)

    def scratch(self):
        n = len(self.arrays)
        return [pltpu.SemaphoreType.DMA((n, NDEV)), pltpu.SemaphoreType.DMA((n, NDEV)), pltpu.SemaphoreType.DMA((n,))]

    def _src(self, ins, j):
        return ins[j] if self.layers[j] is None else ins[j].at[self.layers[j]]

    def _block_shape(self, j):
        a = self.arrays[j]
        return a.shape if self.layers[j] is None else a.shape[1:]


class _Gather(_Exchange):
    chips = (4, 2, 6)

    def out_shape(self):
        return [_sds((NDEV,) + tuple(self._block_shape(j)), a.dtype) for j, a in enumerate(self.arrays)]

    @staticmethod
    def _copy(outs, sems, j, sem, block_rel, to_rel, src=None):
        blk = outs[j].at[_slot(_peer(block_rel))]
        return pltpu.make_async_remote_copy(
            src_ref=blk if src is None else src, dst_ref=blk,
            send_sem=sems[0].at[j, sem], recv_sem=sems[1].at[j, sem],
            device_id=_peer(to_rel), device_id_type=MESH_ID)

    def _local(self, ins, outs, sems, j):
        return pltpu.make_async_copy(self._src(ins, j), outs[j].at[_slot(_peer(0))], sems[2].at[j])

    def start(self, ins, outs, sems):
        for j in range(len(self.arrays)):
            self._local(ins, outs, sems, j).start()
            for rel in (1,) + self.chips:
                self._copy(outs, sems, j, rel, 0, rel, src=self._src(ins, j)).start()

    def forward(self, ins, outs, sems):
        for j in range(len(self.arrays)):
            for rel in self.chips:
                self._copy(outs, sems, j, rel, rel, 0).wait_recv()
                self._copy(outs, sems, j, rel ^ 1, rel, 1).start()

    def finish(self, ins, outs, sems):
        for j in range(len(self.arrays)):
            self._copy(outs, sems, j, 1, 1, 0).wait_recv()
            for rel in self.chips:
                self._copy(outs, sems, j, rel ^ 1, rel ^ 1, 0).wait_recv()
        for j in range(len(self.arrays)):
            for rel in (1,) + self.chips:
                self._copy(outs, sems, j, rel, 0, rel, src=self._src(ins, j)).wait_send()
            for rel in self.chips:
                self._copy(outs, sems, j, rel ^ 1, rel, 1).wait_send()
            self._local(ins, outs, sems, j).wait()


class _Scatter(_Exchange):
    def out_shape(self):
        return [_sds(a.shape, a.dtype) for a in self.arrays]

    @staticmethod
    def _copy(ins, outs, sems, j, rel):
        return pltpu.make_async_remote_copy(
            src_ref=ins[j].at[_slot(_peer(rel))], dst_ref=outs[j].at[_slot(_peer(0))],
            send_sem=sems[0].at[j, rel], recv_sem=sems[1].at[j, rel],
            device_id=_peer(rel), device_id_type=MESH_ID)

    @staticmethod
    def _arrival(outs, sems, j, rel):
        blk = outs[j].at[_slot(_peer(rel))]
        return pltpu.make_async_remote_copy(
            src_ref=blk, dst_ref=blk, send_sem=sems[0].at[j, rel], recv_sem=sems[1].at[j, rel],
            device_id=_peer(rel), device_id_type=MESH_ID)

    @staticmethod
    def _local(ins, outs, sems, j):
        me = _slot(_peer(0))
        return pltpu.make_async_copy(ins[j].at[me], outs[j].at[me], sems[2].at[j])

    def start(self, ins, outs, sems):
        for j in range(len(self.arrays)):
            self._local(ins, outs, sems, j).start()
            for rel in range(1, NDEV):
                self._copy(ins, outs, sems, j, rel).start()

    def forward(self, ins, outs, sems):
        pass

    def finish(self, ins, outs, sems):
        for j in range(len(self.arrays)):
            for rel in range(1, NDEV):
                self._arrival(outs, sems, j, rel).wait_recv()
        for j in range(len(self.arrays)):
            for rel in range(1, NDEV):
                self._copy(ins, outs, sems, j, rel).wait_send()
            self._local(ins, outs, sems, j).wait()


class _Together:
    def __init__(self, parts):
        self.parts = [c for c in parts if c is not None]
        self.arrays = [a for c in self.parts for a in c.arrays]

    def out_shape(self):
        return [s for c in self.parts for s in c.out_shape()]

    def scratch(self):
        return [s for c in self.parts for s in c.scratch()]

    def _each(self, method, ins, outs, sems):
        at = 0
        for q, c in enumerate(self.parts):
            n = len(c.arrays)
            getattr(c, method)(ins[at:at + n], outs[at:at + n], sems[3 * q:3 * q + 3])
            at += n

    def start(self, ins, outs, sems):
        self._each("start", ins, outs, sems)

    def forward(self, ins, outs, sems):
        self._each("forward", ins, outs, sems)

    def finish(self, ins, outs, sems):
        self._each("finish", ins, outs, sems)


def _together(*parts):
    parts = [c for c in parts if c is not None]
    return _Together(parts) if parts else None


def _call(body, *, name, args, in_specs, out_specs, out_shape, grid=(), scratch_shapes=(), semantics=(),
          aliases=None, comm=None, vmem=VMEM_LIMIT_BYTES):
    in_specs, out_specs, out_shape = list(in_specs), list(out_specs), list(out_shape)
    scratch_shapes = list(scratch_shapes)
    if comm is None:
        res = pl.pallas_call(
            body, name=name, grid=grid, in_specs=in_specs, out_specs=out_specs, out_shape=out_shape,
            scratch_shapes=scratch_shapes, input_output_aliases=aliases or {},
            compiler_params=_params(*semantics, vmem=vmem))(*args)
        return list(res), []
    n_in, n_out, n_scr, nc = len(in_specs), len(out_specs), len(scratch_shapes), len(comm.arrays)
    total = math.prod(grid)
    middle = min((total * 5) // 8, total - 1)

    def hosted(*refs):
        ins, cins = refs[:n_in], refs[n_in:n_in + nc]
        o0 = n_in + nc
        outs, couts = refs[o0:o0 + n_out], refs[o0 + n_out:o0 + n_out + nc]
        s0 = o0 + n_out + nc
        scr, sems = refs[s0:s0 + n_scr], refs[s0 + n_scr:]
        if total == 1:
            comm.start(cins, couts, sems)
            body(*ins, *outs, *scr)
            comm.forward(cins, couts, sems)
            comm.finish(cins, couts, sems)
            return
        step = 0
        for axis, size in enumerate(grid):
            step = step * size + pl.program_id(axis)
        pl.when(step == 0)(lambda: comm.start(cins, couts, sems))
        body(*ins, *outs, *scr)
        pl.when(step == middle)(lambda: comm.forward(cins, couts, sems))
        pl.when(step == total - 1)(lambda: comm.finish(cins, couts, sems))

    any_spec = pl.BlockSpec(memory_space=pl.ANY)
    res = pl.pallas_call(
        hosted, name=name, grid=grid,
        in_specs=in_specs + [any_spec] * nc, out_specs=out_specs + [any_spec] * nc,
        out_shape=out_shape + comm.out_shape(), scratch_shapes=scratch_shapes + comm.scratch(),
        input_output_aliases=aliases or {}, compiler_params=_params(*(("arbitrary",) * len(grid)), vmem=vmem),
    )(*args, *comm.arrays)
    return list(res[:n_out]), list(res[n_out:])


def _exchange_alone(comm, name):
    return _call(lambda: None, name=name, args=(), in_specs=(), out_specs=(), out_shape=(), comm=comm)[1]


def _cast_pad(w, rows, cols, name):
    nl, r, c = w.shape

    def body(w_ref, o_ref):
        if (rows, cols) != (r, c):
            o_ref[...] = jnp.zeros(o_ref.shape, BF16)
        o_ref[0, 0:r, 0:c] = w_ref[0].astype(BF16)

    return _call(body, name=name, grid=(nl,), args=(w,),
                 in_specs=[pl.BlockSpec((1, r, c), lambda i: (i, 0, 0))],
                 out_specs=[pl.BlockSpec((1, rows, cols), lambda i: (i, 0, 0))],
                 out_shape=[_sds((nl, rows, cols), BF16)], semantics=("parallel",))[0][0]


def _in_proj(x, gain, w, tm, name, comm):
    t, d = x.shape
    nb, _, bw = w.shape

    def body(x_ref, g_ref, w_ref, h_ref, p_ref):
        xv = x_ref[...]
        r = lax.rsqrt(jnp.mean(xv * xv, axis=-1, keepdims=True) + EPS)
        h = (xv * r * g_ref[...]).astype(BF16)
        h_ref[...] = h
        for k in range(nb):
            p_ref[:, k * bw:(k + 1) * bw] = _nn(h, w_ref[k]).astype(BF16)

    return _call(body, name=name, grid=(t // tm,), args=(x, gain, w),
                 in_specs=[_tile(tm, d), _row(d), _weight(w)],
                 out_specs=[_tile(tm, d), _tile(tm, nb * bw)],
                 out_shape=[_sds((t, d), BF16), _sds((t, nb * bw), BF16)],
                 semantics=("parallel",), comm=comm)


def _halo_specs(tm, t, d, col):
    nh, nhb = tm // HALO, t // HALO
    prev = pl.BlockSpec((HALO, d), lambda i: (jnp.maximum(i * nh - 1, 0), col))
    nxt = pl.BlockSpec((HALO, d), lambda i: (jnp.minimum((i + 1) * nh, nhb - 1), col))
    return prev, nxt


def _row_chunks(tm, rc, unroll, rows):
    n = tm // rc
    if unroll:
        for j in range(n):
            rows(j * rc)
    else:
        def step(j, carry):
            rows(pl.multiple_of(j * rc, rc))
            return carry

        lax.fori_loop(0, n, step, 0)


def _shifted(buf_ref, kb, r0, off, rc):
    return buf_ref[kb, pl.ds(r0 + off, rc), :]


def _dwconv(buf_ref, w_ref, out_ref, tm, flip, rc, unroll):
    nblk = out_ref.shape[1] // LANES

    def rows(r0):
        for kb in range(nblk):
            acc = jnp.zeros((rc, LANES), F32)
            for k in range(CONV_TAPS):
                off = (CONV_TAPS - k) if flip else (1 + k)
                acc = acc + w_ref[kb, k:k + 1, :] * _shifted(buf_ref, kb, r0, off, rc)
            out_ref[pl.ds(r0, rc), kb * LANES:(kb + 1) * LANES] = acc

    _row_chunks(tm, rc, unroll, rows)


def _fill_halo_buffer(buf, prev, body, nxt, first, last, tm):
    prev = jnp.where(first, 0.0, prev)
    nxt = jnp.where(last, 0.0, nxt)
    for kb in range(buf.shape[0]):
        lanes = slice(kb * LANES, (kb + 1) * LANES)
        buf[kb, 0:HALO, :] = prev[:, lanes]
        buf[kb, HALO:HALO + tm, :] = body[:, lanes]
        buf[kb, HALO + tm:HALO + tm + HALO, :] = nxt[:, lanes]


def _fill_glu_buffer(cbuf, av, ag, avp, agp, avn, agn, first, last, tm):
    c0p = avp[...].astype(F32) * _sig(agp[...].astype(F32))
    c0n = avn[...].astype(F32) * _sig(agn[...].astype(F32))
    c0 = av[...].astype(F32) * _sig(ag[...].astype(F32))
    _fill_halo_buffer(cbuf, c0p, c0, c0n, first, last, tm)


def _layernorm_stats(v):
    mu = jnp.mean(v, axis=-1, keepdims=True)
    cen = v - mu
    rstd = lax.rsqrt(jnp.mean(cen * cen, axis=-1, keepdims=True) + EPS)
    return cen * rstd, rstd


def _spatial_mix(ws_ref, vn_ref, bias_ref, mixed_ref, tm):
    for ci in range(tm // CHUNK):
        rs = slice(ci * CHUNK, (ci + 1) * CHUNK)
        for g in range(GROUPS):
            ls = slice(g * LANES, (g + 1) * LANES)
            mixed_ref[rs, ls] = _nn(ws_ref[g], vn_ref[rs, ls]) + bias_ref[:, ls]


def _mixer_fwd(proj, x, w, small, seq, tm, name, comm):
    t, d = x.shape
    hp_v, hn_v = _halo_specs(tm, t, d, 0)
    hp_g, hn_g = _halo_specs(tm, t, d, 1)

    def body(av, ag, u_ref, v_ref, ga_ref, gb_ref, avp, agp, avn, agn, x_ref,
             cw_ref, cb_ref, lg_ref, lb_ref, wco_ref, sg_ref, sb_ref, ws_ref, bias_ref, wso_ref,
             gba_ref, gbb_ref, wo_ref,
             c1_ref, ya_ref, yb_ref, x1_ref, cbuf, c1f, vn_ref, mixed_ref):
        i = pl.program_id(0)
        first = (i * tm) % seq == 0
        last = ((i + 1) * tm) % seq == 0
        _fill_glu_buffer(cbuf, av, ag, avp, agp, avn, agn, first, last, tm)
        _dwconv(cbuf, cw_ref, c1f, tm, flip=False, rc=CONV_ROWS_LOOP, unroll=False)
        c1 = c1f[...] + cb_ref[...]
        c1_ref[...] = c1.astype(BF16)
        c2hat, _ = _layernorm_stats(c1)
        c2 = c2hat * lg_ref[...] + lb_ref[...]
        c3 = (c2 * _sig(c2)).astype(BF16)
        ya = _nn(c3, wco_ref[...].reshape(d, d))
        ya_ref[...] = ya.astype(BF16)
        vhat, _ = _layernorm_stats(v_ref[...].astype(F32))
        vn_ref[...] = (vhat * sg_ref[...] + sb_ref[...]).astype(BF16)
        _spatial_mix(ws_ref, vn_ref, bias_ref, mixed_ref, tm)
        gated = (u_ref[...].astype(F32) * mixed_ref[...]).astype(BF16)
        yb = _nn(gated, wso_ref[...].reshape(d, d))
        yb_ref[...] = yb.astype(BF16)
        sa = _sig(ga_ref[...].astype(F32) + gba_ref[...])
        sb = _sig(gb_ref[...].astype(F32) + gbb_ref[...])
        merged = (sa * ya + sb * yb).astype(BF16)
        x1_ref[...] = x_ref[...] + _nn(merged, wo_ref[...].reshape(d, d))

    cols = [_tile(tm, d, j) for j in range(6)]
    return _call(
        body, name=name, grid=(t // tm,),
        args=(proj,) * 10 + (x, w['conv_w'], small['conv_b'], small['conv_ln_g'], small['conv_ln_b'], w['w_conv_out'],
                             small['sgu_ln_g'], small['sgu_ln_b'], small['ws'], small['bias_full'], w['w_sgu_out'],
                             small['gba'], small['gbb'], w['w_o']),
        in_specs=cols + [hp_v, hp_g, hn_v, hn_g, _tile(tm, d),
                         _weight(w['conv_w']), _row(d), _row(d), _row(d),
                         _weight(w['w_conv_out']), _row(d), _row(d),
                         _resident(small['ws'].shape), _resident(small['bias_full'].shape),
                         _weight(w['w_sgu_out']), _row(d), _row(d), _weight(w['w_o'])],
        out_specs=[_tile(tm, d)] * 4,
        out_shape=[_sds((t, d), BF16)] * 3 + [_sds((t, d), F32)],
        scratch_shapes=[pltpu.VMEM((d // LANES, tm + 2 * HALO, LANES), F32), pltpu.VMEM((tm, d), F32),
                        pltpu.VMEM((tm, d), BF16), pltpu.VMEM((tm, d), F32)],
        semantics=("parallel",), comm=comm)


def _pair_gate_up(wg, wu, name):
    n, c, d = wg.shape

    def body(wg_ref, wu_ref, o_ref):
        o_ref[0:c, :] = wg_ref[0]
        o_ref[c:2 * c, :] = wg_ref[1]
        o_ref[2 * c:3 * c, :] = wu_ref[0]
        o_ref[3 * c:4 * c, :] = wu_ref[1]

    pair = pl.BlockSpec((2, c, d), lambda k: (k, 0, 0))
    return _call(body, name=name, grid=(n // 2,), args=(wg, wu), in_specs=[pair, pair],
                 out_specs=[pl.BlockSpec((None, 4 * c, d), lambda k: (k, 0, 0))],
                 out_shape=[_sds((n // 2, 4 * c, d), wg.dtype)], semantics=("parallel",))[0][0]


def _pair_specs(w, tm):
    d = w['w_gate_up'].shape[2]
    up = pl.BlockSpec((None, 2 * FF_PAIR, d), lambda i, k: (k, 0, 0))
    down = pl.BlockSpec((None, FF_PAIR, d), lambda i, k: (k, 0, 0))
    return [up, down]


def _ffn_fwd(x1, gain, w, tm, name, comm):
    t, d = x1.shape
    tm = min(tm, t)
    npair = NDEV // 2
    hid = NDEV * FF_PAD
    wd_pairs = w['w_ffn_down'].reshape(npair, FF_PAIR, d)

    def body(x_ref, g_ref, wgu_ref, wd_ref, h_ref, gg_ref, uu_ref, x2_ref, hb_ref, acc_ref):
        k = pl.program_id(1)

        @pl.when(k == 0)
        def _():
            xv = x_ref[...]
            r = lax.rsqrt(jnp.mean(xv * xv, axis=-1, keepdims=True) + EPS)
            h = (xv * r * g_ref[...]).astype(BF16)
            hb_ref[...] = h
            h_ref[...] = h
            acc_ref[...] = xv

        gu = _nt(hb_ref[...], wgu_ref[...])
        gk = gu[:, 0:FF_PAIR]
        uk = gu[:, FF_PAIR:2 * FF_PAIR]
        gg_ref[...] = gk.astype(BF16)
        uu_ref[...] = uk.astype(BF16)
        ak = (gk * _sig(gk) * uk).astype(BF16)
        acc_ref[...] += _nn(ak, wd_ref[...])

        @pl.when(k == npair - 1)
        def _():
            x2_ref[...] = acc_ref[...]

    pair_cols = pl.BlockSpec((tm, FF_PAIR), lambda i, k: (i, k))
    return _call(
        body, name=name, grid=(t // tm, npair),
        args=(x1, gain, w['w_gate_up'], wd_pairs),
        in_specs=[_tile(tm, d), _row(d)] + _pair_specs(w, tm),
        out_specs=[_tile(tm, d), pair_cols, pair_cols, _tile(tm, d)],
        out_shape=[_sds((t, d), BF16), _sds((t, hid), BF16), _sds((t, hid), BF16), _sds((t, d), F32)],
        scratch_shapes=[pltpu.VMEM((tm, d), BF16), pltpu.VMEM((tm, d), F32)],
        semantics=("parallel", "arbitrary"), comm=comm)


def _init_small(first, small_ref, acc_ref):
    @pl.when(first)
    def _():
        small_ref[...] = jnp.zeros(small_ref.shape, F32)
        acc_ref[...] = jnp.zeros(acc_ref.shape, F32)


def _finish_small(last, small_ref, acc_ref, nq):
    @pl.when(last)
    def _():
        for q in range(nq):
            small_ref[q:q + 1, :] = jnp.sum(acc_ref[q], axis=0, keepdims=True)


def _loss_bwd(x, gain, target, tm, name):
    t, d = x.shape
    nsteps = t // tm

    def body(x_ref, g_ref, t_ref, dx_ref, small_ref, acc_ref):
        i = pl.program_id(0)
        _init_small(i == 0, small_ref, acc_ref)
        xv = x_ref[...]
        r = lax.rsqrt(jnp.mean(xv * xv, axis=-1, keepdims=True) + EPS)
        xhat = xv * r
        diff = xhat * g_ref[...] - t_ref[...]
        dy = diff * (1.0 / d)
        acc_ref[0] += _fold(dy * xhat)
        acc_ref[1] += _fold(diff * diff)
        dxhat = dy * g_ref[...]
        dx_ref[...] = r * (dxhat - xhat * jnp.mean(dxhat * xhat, axis=-1, keepdims=True))

        @pl.when(i == nsteps - 1)
        def _():
            small_ref[0:1, :] = jnp.sum(acc_ref[0], axis=0, keepdims=True)
            total = jnp.sum(acc_ref[1]) * (0.5 / d)
            small_ref[1:2, :] = jnp.full((1, d), total, F32)

    return _call(body, name=name, grid=(nsteps,), args=(x, gain, target),
                 in_specs=[_tile(tm, d), _row(d), _tile(tm, d)],
                 out_specs=[_tile(tm, d), _resident((SUBLANES, d))],
                 out_shape=[_sds((t, d), F32), _sds((SUBLANES, d), F32)],
                 scratch_shapes=[pltpu.VMEM((2, SUBLANES, d), F32)], semantics=("arbitrary",))[0]


def _ffn_bwd(dx2, x1, gain, gg, uu, w, tm, name, comm):
    t, d = x1.shape
    tm = min(tm, t)
    npair = NDEV // 2
    hid = NDEV * FF_PAD
    nsteps = t // tm
    wd_pairs = w['w_ffn_down'].reshape(npair, FF_PAIR, d)

    def body(dx_ref, x_ref, g_ref, gg_ref, uu_ref, wgu_ref, wd_ref,
             a_ref, dg_ref, du_ref, dx1_ref, small_ref, acc_ref, dxb_ref, dh_ref, dgu_ref):
        i, k = pl.program_id(0), pl.program_id(1)
        _init_small((i == 0) & (k == 0), small_ref, acc_ref)

        @pl.when(k == 0)
        def _():
            dxb_ref[...] = dx_ref[...].astype(BF16)
            dh_ref[...] = jnp.zeros(dh_ref.shape, F32)

        gk = gg_ref[...].astype(F32)
        uk = uu_ref[...].astype(F32)
        sg = _sig(gk)
        silu = gk * sg
        a_ref[...] = (silu * uk).astype(BF16)
        da = _nt(dxb_ref[...], wd_ref[...])
        dgk = (da * uk * (sg * (1.0 + gk * (1.0 - sg)))).astype(BF16)
        duk = (da * silu).astype(BF16)
        dg_ref[...] = dgk
        du_ref[...] = duk
        dgu_ref[:, 0:FF_PAIR] = dgk
        dgu_ref[:, FF_PAIR:2 * FF_PAIR] = duk
        dh_ref[...] += _nn(dgu_ref[...], wgu_ref[...])

        @pl.when(k == npair - 1)
        def _():
            xv = x_ref[...]
            dh = dh_ref[...]
            r = lax.rsqrt(jnp.mean(xv * xv, axis=-1, keepdims=True) + EPS)
            xhat = xv * r
            acc_ref[0] += _fold(dh * xhat)
            dxhat = dh * g_ref[...]
            dx1_ref[...] = dx_ref[...] + r * (dxhat - xhat * jnp.mean(dxhat * xhat, axis=-1, keepdims=True))

        _finish_small((i == nsteps - 1) & (k == npair - 1), small_ref, acc_ref, 1)

    pair_cols = pl.BlockSpec((tm, FF_PAIR), lambda i, k: (i, k))
    row_tile = pl.BlockSpec((tm, d), lambda i, k: (i, 0), pipeline_mode=pl.Buffered(1))
    return _call(
        body, name=name, grid=(nsteps, npair),
        args=(dx2, x1, gain, gg, uu, w['w_gate_up'], wd_pairs),
        in_specs=[row_tile, row_tile, _row(d), pair_cols, pair_cols] + _pair_specs(w, tm),
        out_specs=[pair_cols] * 3 + [_tile(tm, d), _resident((SUBLANES, d))],
        out_shape=[_sds((t, hid), BF16)] * 3 + [_sds((t, d), F32), _sds((SUBLANES, d), F32)],
        scratch_shapes=[pltpu.VMEM((1, SUBLANES, d), F32), pltpu.VMEM((tm, d), BF16), pltpu.VMEM((tm, d), F32),
                        pltpu.VMEM((tm, 2 * FF_PAIR), BF16)],
        semantics=("arbitrary", "arbitrary"), comm=comm, vmem=VMEM_LIMIT_FFN_BWD)


def _matmul_tn(a, b, a_blk, b_blk, stack, shard, tm, name, comm):
    t, ma = a.shape
    tm = min(tm, t)
    nb_ = b.shape[1]
    na, nb = ma // a_blk, nb_ // b_blk
    nsteps = t // tm
    cw = min(TN_COLS, b_blk)
    if stack == 'b':
        per = b_blk // shard
        out_shape, out_spec = (nb_ // shard, ma, shard), pl.BlockSpec((per, a_blk, shard), lambda i, j, k: (j, 0, 0))
    elif stack == 'a':
        per = a_blk // shard
        out_shape, out_spec = (ma // shard, shard, nb_), pl.BlockSpec((per, shard, b_blk), lambda i, j, k: (i, 0, 0))
    else:
        out_shape, out_spec = (ma, nb_), pl.BlockSpec((a_blk, b_blk), lambda i, j, k: (i, j))

    def body(a_ref, b_ref, o_ref, acc_ref):
        k = pl.program_id(2)

        @pl.when(k == 0)
        def _():
            acc_ref[...] = jnp.zeros(acc_ref.shape, F32)

        av = a_ref[...].astype(BF16)
        for c in range(0, b_blk, cw):
            acc_ref[:, c:c + cw] += _tn(av, b_ref[:, c:c + cw].astype(BF16))

        @pl.when(k == nsteps - 1)
        def _():
            if stack == 'b':
                for s in range(per):
                    o_ref[s] = acc_ref[:, s * shard:(s + 1) * shard].astype(BF16)
            elif stack == 'a':
                for s in range(per):
                    o_ref[s] = acc_ref[s * shard:(s + 1) * shard, :].astype(BF16)
            else:
                o_ref[...] = acc_ref[...].astype(BF16)

    res, got = _call(
        body, name=name, grid=(na, nb, nsteps), args=(a, b),
        in_specs=[pl.BlockSpec((tm, a_blk), lambda i, j, k: (k, i)), pl.BlockSpec((tm, b_blk), lambda i, j, k: (k, j))],
        out_specs=[out_spec], out_shape=[_sds(out_shape, BF16)],
        scratch_shapes=[pltpu.VMEM((a_blk, b_blk), F32)],
        semantics=("parallel", "parallel", "arbitrary"), comm=comm)
    return res[0], got


def _mixer_bwd(dx1, proj, c1, ya, yb, w, small, tm, name, comm):
    t, d = dx1.shape
    nsteps = t // tm
    nq = 6

    def body(dx_ref, u_ref, v_ref, ga_ref, gb_ref, c1_ref, ya_ref, yb_ref,
             lg_ref, lb_ref, wco_ref, sg_ref, sb_ref, ws_ref, wst_ref, bias_ref, wso_ref, gba_ref, gbb_ref, wo_ref,
             sel_ref,
             dp_ref, dc1_ref, mg_ref, c3_ref, gt_ref, dya_ref, dyb_ref, small_ref, dws_ref, dbs_ref,
             acc_ref, vn_ref, mixed_ref, dmix_ref, dvn_ref, dbias_ref):
        i = pl.program_id(0)
        _init_small(i == 0, small_ref, acc_ref)

        @pl.when(i == 0)
        def _():
            dws_ref[...] = jnp.zeros(dws_ref.shape, F32)
            dbs_ref[...] = jnp.zeros(dbs_ref.shape, F32)
            dbias_ref[...] = jnp.zeros(dbias_ref.shape, F32)

        dmerged = _nt(dx_ref[...].astype(BF16), wo_ref[...].reshape(d, d))
        ya = ya_ref[...].astype(F32)
        yb = yb_ref[...].astype(F32)
        sa = _sig(ga_ref[...].astype(F32) + gba_ref[...])
        sb = _sig(gb_ref[...].astype(F32) + gbb_ref[...])
        mg_ref[...] = (sa * ya + sb * yb).astype(BF16)
        dya = (dmerged * sa).astype(BF16)
        dyb = (dmerged * sb).astype(BF16)
        dya_ref[...] = dya
        dyb_ref[...] = dyb
        dga = dmerged * ya * (sa * (1.0 - sa))
        dgb = dmerged * yb * (sb * (1.0 - sb))
        acc_ref[0] += _fold(dga)
        acc_ref[1] += _fold(dgb)
        dp_ref[:, 0:2 * d] = jnp.zeros((tm, 2 * d), BF16)
        dp_ref[:, 4 * d:5 * d] = dga.astype(BF16)
        dp_ref[:, 5 * d:6 * d] = dgb.astype(BF16)
        c2hat, rstd = _layernorm_stats(c1_ref[...].astype(F32))
        c2 = c2hat * lg_ref[...] + lb_ref[...]
        s2 = _sig(c2)
        c3_ref[...] = (c2 * s2).astype(BF16)
        dc3 = _nt(dya, wco_ref[...].reshape(d, d))
        dc2 = dc3 * (s2 * (1.0 + c2 * (1.0 - s2)))
        acc_ref[2] += _fold(dc2 * c2hat)
        acc_ref[3] += _fold(dc2)
        dc2hat = dc2 * lg_ref[...]
        dc1_ref[...] = (rstd * (dc2hat - jnp.mean(dc2hat, axis=-1, keepdims=True)
                                - c2hat * jnp.mean(dc2hat * c2hat, axis=-1, keepdims=True))).astype(BF16)
        vhat, rstd_v = _layernorm_stats(v_ref[...].astype(F32))
        vn_ref[...] = (vhat * sg_ref[...] + sb_ref[...]).astype(BF16)
        _spatial_mix(ws_ref, vn_ref, bias_ref, mixed_ref, tm)
        u = u_ref[...].astype(F32)
        mixed = mixed_ref[...]
        gt_ref[...] = (u * mixed).astype(BF16)
        dgated = _nt(dyb, wso_ref[...].reshape(d, d))
        dp_ref[:, 2 * d:3 * d] = (dgated * mixed).astype(BF16)
        dmix_ref[...] = dgated * u
        for ci in range(tm // CHUNK):
            rs = slice(ci * CHUNK, (ci + 1) * CHUNK)
            dbias_ref[...] += dmix_ref[rs, :]
            for g in range(GROUPS):
                ls = slice(g * LANES, (g + 1) * LANES)
                dm = dmix_ref[rs, ls].astype(BF16)
                dws_ref[g] += _nt(dm, vn_ref[rs, ls])
                dvn_ref[rs, ls] = _nn(wst_ref[g], dm)
        dvn = dvn_ref[...]
        acc_ref[4] += _fold(dvn * vhat)
        acc_ref[5] += _fold(dvn)
        dvhat = dvn * sg_ref[...]
        dp_ref[:, 3 * d:4 * d] = (rstd_v * (dvhat - jnp.mean(dvhat, axis=-1, keepdims=True)
                                           - vhat * jnp.mean(dvhat * vhat, axis=-1, keepdims=True))).astype(BF16)
        _finish_small(i == nsteps - 1, small_ref, acc_ref, nq)

        @pl.when(i == nsteps - 1)
        def _():
            db = dbias_ref[...]
            hi = db.astype(BF16)
            lo = (db - hi.astype(F32)).astype(BF16)
            dbs_ref[...] = _nt(sel_ref[...], hi) + _nt(sel_ref[...], lo)

    cols = [_tile(tm, d, j) for j in (2, 3, 4, 5)]
    return _call(
        body, name=name, grid=(nsteps,),
        args=(dx1, proj, proj, proj, proj, c1, ya, yb,
              small['conv_ln_g'], small['conv_ln_b'], w['w_conv_out'], small['sgu_ln_g'], small['sgu_ln_b'],
              small['ws'], small['wst'], small['bias_full'], w['w_sgu_out'], small['gba'], small['gbb'], w['w_o'],
              small['group_sel']),
        in_specs=[_tile(tm, d)] + cols + [_tile(tm, d)] * 3 + [
            _row(d), _row(d), _weight(w['w_conv_out']),
            _row(d), _row(d), _resident(small['ws'].shape), _resident(small['wst'].shape),
            _resident(small['bias_full'].shape), _weight(w['w_sgu_out']), _row(d), _row(d),
            _weight(w['w_o']), _resident(small['group_sel'].shape)],
        out_specs=[_tile(tm, 6 * d)] + [_tile(tm, d)] * 6 + [
            _resident((SUBLANES, d)), _resident((GROUPS, CHUNK, CHUNK)), _resident((GROUPS, CHUNK))],
        out_shape=[_sds((t, 6 * d), BF16)] + [_sds((t, d), BF16)] * 6 + [
            _sds((SUBLANES, d), F32), _sds((GROUPS, CHUNK, CHUNK), F32), _sds((GROUPS, CHUNK), F32)],
        scratch_shapes=[pltpu.VMEM((nq, SUBLANES, d), F32), pltpu.VMEM((tm, d), BF16), pltpu.VMEM((tm, d), F32),
                        pltpu.VMEM((tm, d), F32), pltpu.VMEM((tm, d), F32), pltpu.VMEM((CHUNK, d), F32)],
        semantics=("arbitrary",), comm=comm)


def _conv_bwd(dproj, dc1, proj, w, pairs, seq, tm, name, comm):
    t, d = dc1.shape
    nsteps = t // tm
    nblk = d // LANES
    npairs = len(pairs)
    hp_v, hn_v = _halo_specs(tm, t, d, 0)
    hp_g, hn_g = _halo_specs(tm, t, d, 1)
    hp_d, hn_d = _halo_specs(tm, t, d, 0)

    def body(*refs):
        (dp_in, dc_ref, dcp, dcn, av, ag, avp, agp, avn, agn, cw_ref), refs = refs[:11], refs[11:]
        ab_refs, refs = refs[:2 * npairs], refs[2 * npairs:]
        (dp_ref, dcw_ref, small_ref), refs = refs[:3], refs[3:]
        grad_refs, refs = refs[:npairs], refs[npairs:]
        acc_ref, cbuf, dbuf, dc0f, accw, gacc = refs
        del dp_in
        i = pl.program_id(0)
        _init_small(i == 0, small_ref, acc_ref)

        @pl.when(i == 0)
        def _():
            accw[...] = jnp.zeros(accw.shape, F32)
            dcw_ref[...] = jnp.zeros(dcw_ref.shape, F32)
            gacc[...] = jnp.zeros(gacc.shape, F32)

        first = (i * tm) % seq == 0
        last = ((i + 1) * tm) % seq == 0
        _fill_glu_buffer(cbuf, av, ag, avp, agp, avn, agn, first, last, tm)
        dc1v = dc_ref[...].astype(F32)
        _fill_halo_buffer(dbuf, dcp[...].astype(F32), dc1v, dcn[...].astype(F32), first, last, tm)
        acc_ref[0] += _fold(dc1v)
        _dwconv(dbuf, cw_ref, dc0f, tm, flip=True, rc=CONV_ROWS, unroll=True)

        def rows(r0):
            for kb in range(nblk):
                dv = dbuf[kb, r0 + HALO:r0 + HALO + CONV_ROWS, :]
                for k in range(CONV_TAPS):
                    accw[kb, k] += _fold(dv * _shifted(cbuf, kb, r0, 1 + k, CONV_ROWS))

        _row_chunks(tm, CONV_ROWS, True, rows)
        for q in range(npairs):
            aq = ab_refs[2 * q][...].astype(BF16)
            for c in range(0, d, TN_COLS):
                gacc[q, :, c:c + TN_COLS] += _tn(aq, ab_refs[2 * q + 1][:, c:c + TN_COLS].astype(BF16))
        sg = _sig(ag[...].astype(F32))
        avv = av[...].astype(F32)
        dc0 = dc0f[...]
        dp_ref[:, 0:d] = (dc0 * sg).astype(BF16)
        dp_ref[:, d:2 * d] = (dc0 * avv * (sg * (1.0 - sg))).astype(BF16)
        _finish_small(i == nsteps - 1, small_ref, acc_ref, 1)

        @pl.when(i == nsteps - 1)
        def _():
            for kb in range(nblk):
                dcw_ref[kb] = jnp.sum(accw[kb], axis=1)
            for q in range(npairs):
                grad_refs[q][...] = gacc[q].astype(BF16)

    return _call(
        body, name=name, grid=(nsteps,),
        args=(dproj, dc1, dc1, dc1, proj, proj, proj, proj, proj, proj, w['conv_w']) + tuple(a for ab in pairs for a in ab),
        in_specs=[pl.BlockSpec(memory_space=pl.ANY), _tile(tm, d), hp_d, hn_d, _tile(tm, d, 0), _tile(tm, d, 1),
                  hp_v, hp_g, hn_v, hn_g, _weight(w['conv_w'])] + [_tile(tm, d)] * (2 * npairs),
        out_specs=[_tile(tm, 2 * d), _resident((nblk, CONV_TAPS_PADDED, LANES)), _resident((SUBLANES, d))]
        + [_resident((d, d))] * npairs,
        out_shape=[_sds(dproj.shape, BF16), _sds((nblk, CONV_TAPS_PADDED, LANES), F32), _sds((SUBLANES, d), F32)]
        + [_sds((d, d), BF16)] * npairs,
        scratch_shapes=[pltpu.VMEM((1, SUBLANES, d), F32), pltpu.VMEM((nblk, tm + 2 * HALO, LANES), F32),
                        pltpu.VMEM((nblk, tm + 2 * HALO, LANES), F32), pltpu.VMEM((tm, d), F32),
                        pltpu.VMEM((nblk, CONV_TAPS_PADDED, SUBLANES, LANES), F32),
                        pltpu.VMEM((npairs, d, d), F32)],
        aliases={0: 0}, semantics=("arbitrary",), comm=comm)


def _in_proj_bwd(dproj, x, dx1, gain, w, tm, name, comm):
    t, d = x.shape
    nb, _, bw = w.shape
    nsteps = t // tm

    def body(dp_ref, x_ref, dx1_ref, g_ref, w_ref, dx_ref, small_ref, acc_ref):
        i = pl.program_id(0)
        _init_small(i == 0, small_ref, acc_ref)
        dh = jnp.zeros((tm, d), F32)
        for k in range(nb):
            dh = dh + _nt(dp_ref[:, k * bw:(k + 1) * bw], w_ref[k])
        xv = x_ref[...]
        r = lax.rsqrt(jnp.mean(xv * xv, axis=-1, keepdims=True) + EPS)
        xhat = xv * r
        acc_ref[0] += _fold(dh * xhat)
        dxhat = dh * g_ref[...]
        dx_ref[...] = dx1_ref[...] + r * (dxhat - xhat * jnp.mean(dxhat * xhat, axis=-1, keepdims=True))
        _finish_small(i == nsteps - 1, small_ref, acc_ref, 1)

    return _call(body, name=name, grid=(nsteps,), args=(dproj, x, dx1, gain, w),
                 in_specs=[_tile(tm, nb * bw), _tile(tm, d), _tile(tm, d), _row(d), _weight(w)],
                 out_specs=[_tile(tm, d), _resident((SUBLANES, d))],
                 out_shape=[_sds((t, d), F32), _sds((SUBLANES, d), F32)],
                 scratch_shapes=[pltpu.VMEM((1, SUBLANES, d), F32)], semantics=("arbitrary",), comm=comm)


def _adam(wv, g, mv, vv):
    m = ADAM_B1 * mv + (1.0 - ADAM_B1) * g
    v = ADAM_B2 * vv + (1.0 - ADAM_B2) * jnp.square(g)
    m_hat = m / (1.0 - ADAM_B1 ** ADAM_STEP)
    v_hat = v / (1.0 - ADAM_B2 ** ADAM_STEP)
    delta = -ADAM_LR * (m_hat / (jnp.sqrt(v_hat) + ADAM_EPS) + ADAM_WD * wv)
    return delta, m, v


def _adamw_layer(layer, w, m, v, parts, prev, nsplit, name):
    nl, r, c = w.shape
    npart, pr, pc = parts.shape
    rt, prt = r // nsplit, pr // nsplit

    def body(w_ref, m_ref, v_ref, p_ref, *rest):
        g_ref, d_ref, nm_ref, nv_ref = rest[-4:]
        g = p_ref[0, 0:rt, 0:c].astype(F32)
        for s in range(1, npart):
            g = g + p_ref[s, 0:rt, 0:c].astype(F32)
        delta, mn, vn = _adam(w_ref[0], g, m_ref[0], v_ref[0])
        g_ref[0] = g
        d_ref[0] = delta
        nm_ref[0] = mn
        nv_ref[0] = vn

    wspec = pl.BlockSpec((1, rt, c), lambda i: (layer, i, 0))
    pspec = pl.BlockSpec((npart, prt, pc), lambda i: (0, i, 0))
    in_specs = [wspec, wspec, wspec, pspec]
    args = [w, m, v, parts]
    aliases = {}
    if prev is not None:
        in_specs += [pl.BlockSpec(memory_space=pl.ANY)] * 4
        args += list(prev)
        aliases = {4 + q: q for q in range(4)}
    return _call(body, name=name, grid=(nsplit,), args=args, in_specs=in_specs, out_specs=[wspec] * 4,
                 out_shape=[_sds(w.shape, F32)] * 4, aliases=aliases, semantics=("parallel",))[0]


VEC_ROWS = {'gate_bias': (0, 1), 'conv_ln_g': 2, 'conv_ln_b': 3, 'sgu_ln_g': 4, 'sgu_ln_b': 5,
            'conv_b': SUBLANES, 'norm_ffn': 2 * SUBLANES}
FINAL_ROW = 3 * SUBLANES
LOSS_ROW = 3 * SUBLANES + 1


def _adamw_small(gathered, params, moments_m, moments_v):
    names = list(params)
    nper = len(names)
    nl = len(gathered)

    def body(*refs):
        g_refs = [refs[4 * l:4 * l + 4] for l in range(nl)]
        rest = refs[4 * nl:]
        w_refs = dict(zip(names, rest[0:nper]))
        m_refs = dict(zip(names, rest[nper:2 * nper]))
        v_refs = dict(zip(names, rest[2 * nper:3 * nper]))
        outs = rest[3 * nper:]
        loss_ref = outs[4 * nper]
        o = {kind: dict(zip(names, outs[q * nper:(q + 1) * nper])) for q, kind in enumerate("gdmv")}

        def put(nm, idx, g):
            delta, mn, vn = _adam(w_refs[nm][idx], g, m_refs[nm][idx], v_refs[nm][idx])
            o["g"][nm][idx] = g
            o["d"][nm][idx] = delta
            o["m"][nm][idx] = mn
            o["v"][nm][idx] = vn

        def total(ref, *idx):
            g = ref[(0, *idx)]
            for s in range(1, NDEV):
                g = g + ref[(s, *idx)]
            return g

        for l, (vec_ref, dws_ref, dbs_ref, vin_ref) in enumerate(g_refs):
            dd = vec_ref.shape[2]
            put('w_spatial', (l,), total(dws_ref))
            put('b_spatial', (l,), total(dbs_ref))
            put('norm_mix', (slice(l, l + 1), slice(None)), total(vin_ref, slice(0, 1)))
            for nm, rr in VEC_ROWS.items():
                if nm == 'gate_bias':
                    put(nm, (slice(l, l + 1), slice(0, dd)), total(vec_ref, slice(rr[0], rr[0] + 1)))
                    put(nm, (slice(l, l + 1), slice(dd, 2 * dd)), total(vec_ref, slice(rr[1], rr[1] + 1)))
                else:
                    put(nm, (slice(l, l + 1), slice(None)), total(vec_ref, slice(rr, rr + 1)))
        last = g_refs[nl - 1][0]
        put('norm_final', (slice(0, 1), slice(None)), total(last, slice(FINAL_ROW, FINAL_ROW + 1)))
        loss_ref[...] = total(last, slice(LOSS_ROW, LOSS_ROW + 1))

    ins = [a for g in gathered for a in g] + [params[n] for n in names] + [moments_m[n] for n in names] + [moments_v[n] for n in names]
    out_shape = [_sds(params[n].shape, F32) for n in names] * 4 + [_sds((1, gathered[0][0].shape[2]), F32)]
    res = pl.pallas_call(body, name="adamw_small", out_shape=out_shape, compiler_params=_params())(*ins)
    return {kind: dict(zip(names, res[q * nper:(q + 1) * nper])) for q, kind in enumerate("gdmv")}, res[4 * nper]


def _hidden_major(a):
    return jnp.swapaxes(a, 1, 2)


def _prepare_weights(p):
    d = p['w_in'].shape[1]
    return {
        'w_in': _cast_pad(p['w_in'], d, p['w_in'].shape[2], "cast_w_in"),
        'w_conv_out': _cast_pad(p['w_conv_out'], p['w_conv_out'].shape[1], d, "cast_w_conv_out"),
        'w_sgu_out': _cast_pad(p['w_sgu_out'], p['w_sgu_out'].shape[1], d, "cast_w_sgu_out"),
        'w_o': _cast_pad(p['w_o'], p['w_o'].shape[1], d, "cast_w_o"),
        'w_ffn_gate': _cast_pad(_hidden_major(p['w_ffn_gate']), FF_PAD, d, "cast_w_ffn_gate"),
        'w_ffn_up': _cast_pad(_hidden_major(p['w_ffn_up']), FF_PAD, d, "cast_w_ffn_up"),
        'w_ffn_down': _cast_pad(p['w_ffn_down'], FF_PAD, d, "cast_w_ffn_down"),
        'conv_w': jnp.pad(p['conv_w'][:, :, 0, :], ((0, 0), (0, CONV_TAPS_PADDED - CONV_TAPS), (0, 0))),
    }


def _gather_of(shards, names, layer):
    return _Gather([shards[n] for n in names], [layer] * len(names))


def _layer_small(p, layer):
    d = p['norm_mix'].shape[1]
    ws = p['w_spatial'][layer]
    rows = {n: p[n][layer:layer + 1] for n in ('norm_mix', 'norm_ffn', 'conv_b', 'conv_ln_g', 'conv_ln_b',
                                               'sgu_ln_g', 'sgu_ln_b')}
    return {
        **rows,
        'ws': ws.astype(BF16), 'wst': jnp.swapaxes(ws, 1, 2).astype(BF16),
        'bias_full': jnp.repeat(p['b_spatial'][layer].T, LANES, axis=1),
        'gba': p['gate_bias'][layer:layer + 1, 0:d], 'gbb': p['gate_bias'][layer:layer + 1, d:2 * d],
        'group_sel': (jnp.arange(d)[None, :] // LANES == jnp.arange(GROUPS)[:, None]).astype(BF16),
    }


class _GradQueue:
    def __init__(self):
        self.pending = []
        self.done = {}

    def push(self, key, array):
        self.pending.append((key, array))

    def take(self):
        keys = [k for k, _ in self.pending]
        comm = _Scatter([a for _, a in self.pending]) if self.pending else None
        self.pending = []
        return keys, comm

    def put(self, keys, arrays):
        self.done.update(zip(keys, arrays))


def _forward_backward(p, shards, x, target, seq):
    nl = p['norm_mix'].shape[0]
    d = x.shape[1]
    smalls = [_layer_small(p, l) for l in range(nl)]
    w_in = _exchange_alone(_gather_of(shards, ['w_in'], 0), "gather_w_in_0")[0]
    saved = []
    for l in range(nl):
        (h, proj), got = _in_proj(x, smalls[l]['norm_mix'], w_in, TILE_IN, f"in_proj_{l}",
                                  _gather_of(shards, MIXER_WEIGHTS, l))
        w = dict(zip(MIXER_WEIGHTS, got), w_in=w_in)
        (c1, ya, yb, x1), got = _mixer_fwd(proj, x, w, smalls[l], seq, TILE_MIX, f"mixer_fwd_{l}",
                                           _gather_of(shards, FFN_WEIGHTS, l))
        w.update(zip(FFN_WEIGHTS, got))
        w['w_gate_up'] = _pair_gate_up(w['w_ffn_gate'], w['w_ffn_up'], f"pair_gate_up_{l}")
        nxt = _gather_of(shards, ['w_in'], l + 1) if l + 1 < nl else None
        (h2, gg, uu, x2), got = _ffn_fwd(x1, smalls[l]['norm_ffn'], w, TILE_FFN_FWD, f"ffn_fwd_{l}", nxt)
        saved.append(dict(x=x, h=h, proj=proj, c1=c1, ya=ya, yb=yb, x1=x1, h2=h2, gg=gg, uu=uu, w=w))
        x = x2
        if got:
            w_in = got[0]
    dx, small_loss = _loss_bwd(x, p['norm_final'][None, :], target, TILE_LOSS, "loss_bwd")
    queue = _GradQueue()
    small_gathered = [None] * nl
    small_pending = None
    rows = d // NDEV
    hid = NDEV * FF_PAD
    for l in reversed(range(nl)):
        s = saved[l]
        w = s['w']

        def hosted(fn, *args, extra=None):
            keys, comm = queue.take()
            res, got = fn(*args, _together(comm, extra))
            queue.put(keys, got[:len(keys)])
            return res, got[len(keys):]

        def tn(key, a, b, a_blk, b_blk, stack, shard, reshape=None, host=False, extra=None):
            keys, comm = queue.take() if host else ([], None)
            g, got = _matmul_tn(a, b, a_blk, b_blk, stack, shard, TILE_TN, f"dw_{key}_{l}", _together(comm, extra))
            queue.put(keys, got[:len(keys)])
            queue.push((l, key), g if reshape is None else g.reshape(reshape))
            return got[len(keys):]

        (act, dgg, duu, dx1, small_ffn), _ = hosted(_ffn_bwd, dx, s['x1'], smalls[l]['norm_ffn'], s['gg'], s['uu'], w,
                                                    TILE_FFN_BWD, f"ffn_bwd_{l}")
        tn('w_ffn_gate', dgg, s['h2'], hid, d, 'a', FF_PAD)
        tn('w_ffn_up', duu, s['h2'], hid, d, 'a', FF_PAD)
        (dproj, dc1, merged, c3, gated, dya, dyb, small_mix, dws, dbs), _ = hosted(
            _mixer_bwd, dx1, s['proj'], s['c1'], s['ya'], s['yb'], w, smalls[l], TILE_MIX, f"mixer_bwd_{l}")
        tn('w_ffn_down', act, dx, hid, d, 'a', FF_PAD)
        extra = small_pending[1] if small_pending else None
        (dproj, g_conv, small_conv, g_o, g_co, g_so), got = hosted(
            _conv_bwd, dproj, dc1, s['proj'], w, [(merged, dx1), (c3, dya), (gated, dyb)], seq, TILE_MIX,
            f"conv_bwd_{l}", extra=extra)
        if small_pending:
            small_gathered[small_pending[0]] = got
            small_pending = None
        queue.push((l, 'conv_w'), g_conv)
        queue.push((l, 'w_o'), g_o.reshape(NDEV, rows, d))
        queue.push((l, 'w_conv_out'), g_co.reshape(NDEV, rows, d))
        queue.push((l, 'w_sgu_out'), g_so.reshape(NDEV, rows, d))
        blocks = [small_mix, small_conv, small_ffn] + ([small_loss] if l == nl - 1 else [])
        small_main = [jnp.concatenate(blocks, axis=0), dws, dbs]
        small_main_gathered = tn('w_in', s['h'], dproj, d, hid, 'b', w['w_in'].shape[2], host=True,
                                 extra=_Gather(small_main) if l == 0 else None)
        if l == 0:
            (dx, small_in), _ = hosted(_in_proj_bwd, dproj, s['x'], dx1, smalls[l]['norm_mix'], w['w_in'], TILE_IN,
                                       f"in_proj_bwd_{l}")
        else:
            (dx, small_in), _ = _in_proj_bwd(dproj, s['x'], dx1, smalls[l]['norm_mix'], w['w_in'], TILE_IN,
                                             f"in_proj_bwd_{l}", None)
        small_pending = (l, _Gather(small_main + [small_in]))
    keys, comm = queue.take()
    last = _exchange_alone(_together(comm, _Gather([small_in])), "exchange_last_grads")
    queue.put(keys, last[:len(keys)])
    small_gathered[0] = small_main_gathered + last[len(keys):]
    return small_gathered, dx, queue.done


def _train_step(p, m, v, x3, target3):
    nl = p['norm_mix'].shape[0]
    bsz, seq, d = x3.shape
    x = x3.reshape(bsz * seq, d)
    target = target3.reshape(bsz * seq, d)
    small_gathered, dx, exchanged = _forward_backward(p, _prepare_weights(p), x, target, seq)

    out = {kind: {} for kind in "gdmv"}
    splits = {'w_in': 4, 'w_conv_out': 1, 'w_sgu_out': 1, 'w_o': 1, 'w_ffn_gate': 1, 'w_ffn_up': 1, 'w_ffn_down': 1, 'conv_w': 1}
    for n in splits:
        if n == 'conv_w':
            pad = ((0, 0), (0, CONV_TAPS_PADDED - CONV_TAPS), (0, 0))
            wl, ml, vl = (jnp.pad(a[n][:, :, 0, :], pad) for a in (p, m, v))
        elif n in ('w_ffn_gate', 'w_ffn_up'):
            wl, ml, vl = (_hidden_major(a[n]) for a in (p, m, v))
        else:
            wl, ml, vl = p[n], m[n], v[n]
        prev = None
        for l in range(nl):
            prev = _adamw_layer(l, wl, ml, vl, exchanged[(l, n)], prev, splits[n], f"adamw_{n}_{l}")
        for kind, arr in zip("gdmv", prev):
            if n == 'conv_w':
                arr = arr[:, 0:CONV_TAPS, None, :]
            elif n in ('w_ffn_gate', 'w_ffn_up'):
                arr = _hidden_major(arr)
            out[kind][n] = arr
    small_names = ['norm_mix', 'gate_bias', 'conv_b', 'conv_ln_g', 'conv_ln_b', 'sgu_ln_g', 'sgu_ln_b', 'w_spatial',
                   'b_spatial', 'norm_ffn', 'norm_final']

    def two_d(a):
        return a[None, :] if a.ndim == 1 else a

    res, loss_row = _adamw_small(small_gathered, {n: two_d(p[n]) for n in small_names},
                                 {n: two_d(m[n]) for n in small_names}, {n: two_d(v[n]) for n in small_names})
    loss = loss_row[0, 0]
    for kind in "gdmv":
        for n in small_names:
            out[kind][n] = res[kind][n].reshape(p[n].shape)
    grad_x = dx.reshape(bsz, seq, d)
    return (loss, grad_x, *[out[kind][n] for kind in "gdmv" for n in WEIGHT_NAMES])


def kernel(x, norm_mix, w_in, gate_bias, conv_w, conv_b, conv_ln_g, conv_ln_b, w_conv_out, sgu_ln_g, sgu_ln_b, w_spatial, b_spatial, w_sgu_out, w_o, norm_ffn, w_ffn_gate, w_ffn_up, w_ffn_down, norm_final, loss_target, m_norm_mix, m_w_in, m_gate_bias, m_conv_w, m_conv_b, m_conv_ln_g, m_conv_ln_b, m_w_conv_out, m_sgu_ln_g, m_sgu_ln_b, m_w_spatial, m_b_spatial, m_w_sgu_out, m_w_o, m_norm_ffn, m_w_ffn_gate, m_w_ffn_up, m_w_ffn_down, m_norm_final, v_norm_mix, v_w_in, v_gate_bias, v_conv_w, v_conv_b, v_conv_ln_g, v_conv_ln_b, v_w_conv_out, v_sgu_ln_g, v_sgu_ln_b, v_w_spatial, v_b_spatial, v_w_sgu_out, v_w_o, v_norm_ffn, v_w_ffn_gate, v_w_ffn_up, v_w_ffn_down, v_norm_final):
    p = dict(zip(WEIGHT_NAMES, (norm_mix, w_in, gate_bias, conv_w, conv_b, conv_ln_g, conv_ln_b, w_conv_out, sgu_ln_g, sgu_ln_b, w_spatial, b_spatial, w_sgu_out, w_o, norm_ffn, w_ffn_gate, w_ffn_up, w_ffn_down, norm_final)))
    m = dict(zip(WEIGHT_NAMES, (m_norm_mix, m_w_in, m_gate_bias, m_conv_w, m_conv_b, m_conv_ln_g, m_conv_ln_b, m_w_conv_out, m_sgu_ln_g, m_sgu_ln_b, m_w_spatial, m_b_spatial, m_w_sgu_out, m_w_o, m_norm_ffn, m_w_ffn_gate, m_w_ffn_up, m_w_ffn_down, m_norm_final)))
    v = dict(zip(WEIGHT_NAMES, (v_norm_mix, v_w_in, v_gate_bias, v_conv_w, v_conv_b, v_conv_ln_g, v_conv_ln_b, v_w_conv_out, v_sgu_ln_g, v_sgu_ln_b, v_w_spatial, v_b_spatial, v_w_sgu_out, v_w_o, v_norm_ffn, v_w_ffn_gate, v_w_ffn_up, v_w_ffn_down, v_norm_final)))
    return _train_step(p, m, v, x, loss_target)
```

```python
import math

import jax
import jax.numpy as jnp
from jax import lax
from jax.experimental import pallas as pl
from jax.experimental.pallas import tpu as pltpu

F32 = jnp.float32
BF16 = jnp.bfloat16
MESH_ID = pl.DeviceIdType.MESH

NDEV = 8
EPS = 1e-6
CONV_TAPS = 31
CONV_TAPS_PADDED = 32
HALO = 16
CONV_ROWS = 128
CONV_ROWS_LOOP = 64
LANES = 128
SUBLANES = 8
CHUNK = 128
GROUPS = 8
FF_PAD = 384
FF_PAIR = 2 * FF_PAD
TN_COLS = 512
VMEM_LIMIT_BYTES = 56 * 1024 * 1024
VMEM_LIMIT_FFN_BWD = 62 * 1024 * 1024

ADAM_LR = 0.001
ADAM_B1 = 0.9
ADAM_B2 = 0.999
ADAM_EPS = 1e-08
ADAM_WD = 0.01
ADAM_STEP = 10

TILE_IN = 512
TILE_MIX = 256
TILE_FFN_FWD = 1024
TILE_FFN_BWD = 1024
TILE_TN = 1024
TILE_TN_SWIGLU = 512
TILE_LOSS = 512

WEIGHT_NAMES = ['norm_mix', 'w_in', 'gate_bias', 'conv_w', 'conv_b', 'conv_ln_g', 'conv_ln_b', 'w_conv_out',
                'sgu_ln_g', 'sgu_ln_b', 'w_spatial', 'b_spatial', 'w_sgu_out', 'w_o', 'norm_ffn', 'w_ffn_gate',
                'w_ffn_up', 'w_ffn_down', 'norm_final']
MIXER_WEIGHTS = ['w_conv_out', 'w_sgu_out', 'w_o', 'conv_w']
FFN_WEIGHTS = ['w_ffn_gate', 'w_ffn_up', 'w_ffn_down']


def _sds(shape, dtype):
    return jax.ShapeDtypeStruct(tuple(shape), dtype)


def _params(*sem, vmem=VMEM_LIMIT_BYTES):
    return pltpu.CompilerParams(dimension_semantics=sem or None, vmem_limit_bytes=vmem)


def _nn(a, b):
    return jnp.dot(a, b, preferred_element_type=F32)


def _nt(a, b):
    return lax.dot_general(a, b, (((1,), (1,)), ((), ())), preferred_element_type=F32)


def _tn(a, b):
    return lax.dot_general(a, b, (((0,), (0,)), ((), ())), preferred_element_type=F32)


def _sig(v):
    return jax.nn.sigmoid(v)


def _fold(v):
    r, c = v.shape
    return jnp.sum(v.reshape(r // SUBLANES, SUBLANES, c), axis=0)


def _tile(tm, n, j=0):
    return pl.BlockSpec((tm, n), lambda i, *_: (i, j))


def _row(n):
    return pl.BlockSpec((1, n), lambda *_: (0, 0))


def _resident(shape):
    nd = len(shape)
    return pl.BlockSpec(tuple(shape), lambda *_: (0,) * nd)


def _weight(w):
    nd = w.ndim
    return pl.BlockSpec(tuple(w.shape), lambda *_: (0,) * nd, pipeline_mode=pl.Buffered(1))


def _peer(rel):
    x, y, c = lax.axis_index("x"), lax.axis_index("y"), lax.axis_index("c")
    return (1 - x if rel & 4 else x, 1 - y if rel & 2 else y, 1 - c if rel & 1 else c)


def _slot(pos):
    return 4 * pos[0] + 2 * pos[1] + pos[2]


class _Exchange:
    def __init__(self, arrays, layers=None):
        self.arrays = list(arrays)
        self.layers = list(layers) if layers is not None else [None] * len(self.arrays)

    def scratch(self):
        n = len(self.arrays)
        return [pltpu.SemaphoreType.DMA((n, NDEV)), pltpu.SemaphoreType.DMA((n, NDEV)), pltpu.SemaphoreType.DMA((n,))]

    def _src(self, ins, j):
        return ins[j] if self.layers[j] is None else ins[j].at[self.layers[j]]

    def _block_shape(self, j):
        a = self.arrays[j]
        return a.shape if self.layers[j] is None else a.shape[1:]


class _Gather(_Exchange):
    chips = (4, 2, 6)

    def out_shape(self):
        return [_sds((NDEV,) + tuple(self._block_shape(j)), a.dtype) for j, a in enumerate(self.arrays)]

    @staticmethod
    def _copy(outs, sems, j, sem, block_rel, to_rel, src=None):
        blk = outs[j].at[_slot(_peer(block_rel))]
        return pltpu.make_async_remote_copy(
            src_ref=blk if src is None else src, dst_ref=blk,
            send_sem=sems[0].at[j, sem], recv_sem=sems[1].at[j, sem],
            device_id=_peer(to_rel), device_id_type=MESH_ID)

    def _local(self, ins, outs, sems, j):
        return pltpu.make_async_copy(self._src(ins, j), outs[j].at[_slot(_peer(0))], sems[2].at[j])

    def start(self, ins, outs, sems):
        for j in range(len(self.arrays)):
            self._local(ins, outs, sems, j).start()
            for rel in (1,) + self.chips:
                self._copy(outs, sems, j, rel, 0, rel, src=self._src(ins, j)).start()

    def forward(self, ins, outs, sems):
        for j in range(len(self.arrays)):
            for rel in self.chips:
                self._copy(outs, sems, j, rel, rel, 0).wait_recv()
                self._copy(outs, sems, j, rel ^ 1, rel, 1).start()

    def finish(self, ins, outs, sems):
        for j in range(len(self.arrays)):
            self._copy(outs, sems, j, 1, 1, 0).wait_recv()
            for rel in self.chips:
                self._copy(outs, sems, j, rel ^ 1, rel ^ 1, 0).wait_recv()
        for j in range(len(self.arrays)):
            for rel in (1,) + self.chips:
                self._copy(outs, sems, j, rel, 0, rel, src=self._src(ins, j)).wait_send()
            for rel in self.chips:
                self._copy(outs, sems, j, rel ^ 1, rel, 1).wait_send()
            self._local(ins, outs, sems, j).wait()


class _Scatter(_Exchange):
    def out_shape(self):
        return [_sds(a.shape, a.dtype) for a in self.arrays]

    @staticmethod
    def _copy(ins, outs, sems, j, rel):
        return pltpu.make_async_remote_copy(
            src_ref=ins[j].at[_slot(_peer(rel))], dst_ref=outs[j].at[_slot(_peer(0))],
            send_sem=sems[0].at[j, rel], recv_sem=sems[1].at[j, rel],
            device_id=_peer(rel), device_id_type=MESH_ID)

    @staticmethod
    def _arrival(outs, sems, j, rel):
        blk = outs[j].at[_slot(_peer(rel))]
        return pltpu.make_async_remote_copy(
            src_ref=blk, dst_ref=blk, send_sem=sems[0].at[j, rel], recv_sem=sems[1].at[j, rel],
            device_id=_peer(rel), device_id_type=MESH_ID)

    @staticmethod
    def _local(ins, outs, sems, j):
        me = _slot(_peer(0))
        return pltpu.make_async_copy(ins[j].at[me], outs[j].at[me], sems[2].at[j])

    def start(self, ins, outs, sems):
        for j in range(len(self.arrays)):
            self._local(ins, outs, sems, j).start()
            for rel in range(1, NDEV):
                self._copy(ins, outs, sems, j, rel).start()

    def forward(self, ins, outs, sems):
        pass

    def finish(self, ins, outs, sems):
        for j in range(len(self.arrays)):
            for rel in range(1, NDEV):
                self._arrival(outs, sems, j, rel).wait_recv()
        for j in range(len(self.arrays)):
            for rel in range(1, NDEV):
                self._copy(ins, outs, sems, j, rel).wait_send()
            self._local(ins, outs, sems, j).wait()


class _Together:
    def __init__(self, parts):
        self.parts = [c for c in parts if c is not None]
        self.arrays = [a for c in self.parts for a in c.arrays]

    def out_shape(self):
        return [s for c in self.parts for s in c.out_shape()]

    def scratch(self):
        return [s for c in self.parts for s in c.scratch()]

    def _each(self, method, ins, outs, sems):
        at = 0
        for q, c in enumerate(self.parts):
            n = len(c.arrays)
            getattr(c, method)(ins[at:at + n], outs[at:at + n], sems[3 * q:3 * q + 3])
            at += n

    def start(self, ins, outs, sems):
        self._each("start", ins, outs, sems)

    def forward(self, ins, outs, sems):
        self._each("forward", ins, outs, sems)

    def finish(self, ins, outs, sems):
        self._each("finish", ins, outs, sems)


def _together(*parts):
    parts = [c for c in parts if c is not None]
    return _Together(parts) if parts else None


def _call(body, *, name, args, in_specs, out_specs, out_shape, grid=(), scratch_shapes=(), semantics=(),
          aliases=None, comm=None, vmem=VMEM_LIMIT_BYTES):
    in_specs, out_specs, out_shape = list(in_specs), list(out_specs), list(out_shape)
    scratch_shapes = list(scratch_shapes)
    if comm is None:
        res = pl.pallas_call(
            body, name=name, grid=grid, in_specs=in_specs, out_specs=out_specs, out_shape=out_shape,
            scratch_shapes=scratch_shapes, input_output_aliases=aliases or {},
            compiler_params=_params(*semantics, vmem=vmem))(*args)
        return list(res), []
    n_in, n_out, n_scr, nc = len(in_specs), len(out_specs), len(scratch_shapes), len(comm.arrays)
    total = math.prod(grid)
    middle = min((total * 5) // 8, total - 1)

    def hosted(*refs):
        ins, cins = refs[:n_in], refs[n_in:n_in + nc]
        o0 = n_in + nc
        outs, couts = refs[o0:o0 + n_out], refs[o0 + n_out:o0 + n_out + nc]
        s0 = o0 + n_out + nc
        scr, sems = refs[s0:s0 + n_scr], refs[s0 + n_scr:]
        if total == 1:
            comm.start(cins, couts, sems)
            body(*ins, *outs, *scr)
            comm.forward(cins, couts, sems)
            comm.finish(cins, couts, sems)
            return
        step = 0
        for axis, size in enumerate(grid):
            step = step * size + pl.program_id(axis)
        pl.when(step == 0)(lambda: comm.start(cins, couts, sems))
        body(*ins, *outs, *scr)
        pl.when(step == middle)(lambda: comm.forward(cins, couts, sems))
        pl.when(step == total - 1)(lambda: comm.finish(cins, couts, sems))

    any_spec = pl.BlockSpec(memory_space=pl.ANY)
    res = pl.pallas_call(
        hosted, name=name, grid=grid,
        in_specs=in_specs + [any_spec] * nc, out_specs=out_specs + [any_spec] * nc,
        out_shape=out_shape + comm.out_shape(), scratch_shapes=scratch_shapes + comm.scratch(),
        input_output_aliases=aliases or {}, compiler_params=_params(*(("arbitrary",) * len(grid)), vmem=vmem),
    )(*args, *comm.arrays)
    return list(res[:n_out]), list(res[n_out:])


def _exchange_alone(comm, name):
    return _call(lambda: None, name=name, args=(), in_specs=(), out_specs=(), out_shape=(), comm=comm)[1]


def _cast_pad(w, rows, cols, name):
    nl, r, c = w.shape

    def body(w_ref, o_ref):
        if (rows, cols) != (r, c):
            o_ref[...] = jnp.zeros(o_ref.shape, BF16)
        o_ref[0, 0:r, 0:c] = w_ref[0].astype(BF16)

    return _call(body, name=name, grid=(nl,), args=(w,),
                 in_specs=[pl.BlockSpec((1, r, c), lambda i: (i, 0, 0))],
                 out_specs=[pl.BlockSpec((1, rows, cols), lambda i: (i, 0, 0))],
                 out_shape=[_sds((nl, rows, cols), BF16)], semantics=("parallel",))[0][0]


def _in_proj(x, gain, w, tm, name, comm):
    t, d = x.shape
    nb, _, bw = w.shape

    def body(x_ref, g_ref, w_ref, h_ref, p_ref):
        xv = x_ref[...]
        r = lax.rsqrt(jnp.mean(xv * xv, axis=-1, keepdims=True) + EPS)
        h = (xv * r * g_ref[...]).astype(BF16)
        h_ref[...] = h
        for k in range(nb):
            p_ref[:, k * bw:(k + 1) * bw] = _nn(h, w_ref[k]).astype(BF16)

    return _call(body, name=name, grid=(t // tm,), args=(x, gain, w),
                 in_specs=[_tile(tm, d), _row(d), _weight(w)],
                 out_specs=[_tile(tm, d), _tile(tm, nb * bw)],
                 out_shape=[_sds((t, d), BF16), _sds((t, nb * bw), BF16)],
                 semantics=("parallel",), comm=comm)


def _halo_specs(tm, t, d, col):
    nh, nhb = tm // HALO, t // HALO
    prev = pl.BlockSpec((HALO, d), lambda i: (jnp.maximum(i * nh - 1, 0), col))
    nxt = pl.BlockSpec((HALO, d), lambda i: (jnp.minimum((i + 1) * nh, nhb - 1), col))
    return prev, nxt


def _row_chunks(tm, rc, unroll, rows):
    n = tm // rc
    if unroll:
        for j in range(n):
            rows(j * rc)
    else:
        def step(j, carry):
            rows(pl.multiple_of(j * rc, rc))
            return carry

        lax.fori_loop(0, n, step, 0)


def _shifted(buf_ref, kb, r0, off, rc):
    return buf_ref[kb, pl.ds(r0 + off, rc), :]


def _dwconv(buf_ref, w_ref, out_ref, tm, flip, rc, unroll):
    nblk = out_ref.shape[1] // LANES

    def rows(r0):
        for kb in range(nblk):
            acc = jnp.zeros((rc, LANES), F32)
            for k in range(CONV_TAPS):
                off = (CONV_TAPS - k) if flip else (1 + k)
                acc = acc + w_ref[kb, k:k + 1, :] * _shifted(buf_ref, kb, r0, off, rc)
            out_ref[pl.ds(r0, rc), kb * LANES:(kb + 1) * LANES] = acc

    _row_chunks(tm, rc, unroll, rows)


def _fill_halo_buffer(buf, prev, body, nxt, first, last, tm):
    prev = jnp.where(first, 0.0, prev)
    nxt = jnp.where(last, 0.0, nxt)
    for kb in range(buf.shape[0]):
        lanes = slice(kb * LANES, (kb + 1) * LANES)
        buf[kb, 0:HALO, :] = prev[:, lanes]
        buf[kb, HALO:HALO + tm, :] = body[:, lanes]
        buf[kb, HALO + tm:HALO + tm + HALO, :] = nxt[:, lanes]


def _fill_glu_buffer(cbuf, av, ag, avp, agp, avn, agn, first, last, tm):
    c0p = avp[...].astype(F32) * _sig(agp[...].astype(F32))
    c0n = avn[...].astype(F32) * _sig(agn[...].astype(F32))
    c0 = av[...].astype(F32) * _sig(ag[...].astype(F32))
    _fill_halo_buffer(cbuf, c0p, c0, c0n, first, last, tm)


def _layernorm_stats(v):
    mu = jnp.mean(v, axis=-1, keepdims=True)
    cen = v - mu
    rstd = lax.rsqrt(jnp.mean(cen * cen, axis=-1, keepdims=True) + EPS)
    return cen * rstd, rstd


def _spatial_mix(ws_ref, vn_ref, bias_ref, mixed_ref, tm):
    for ci in range(tm // CHUNK):
        rs = slice(ci * CHUNK, (ci + 1) * CHUNK)
        for g in range(GROUPS):
            ls = slice(g * LANES, (g + 1) * LANES)
            mixed_ref[rs, ls] = _nn(ws_ref[g], vn_ref[rs, ls]) + bias_ref[:, ls]


def _mixer_fwd(proj, x, w, small, seq, tm, name, comm):
    t, d = x.shape
    hp_v, hn_v = _halo_specs(tm, t, d, 0)
    hp_g, hn_g = _halo_specs(tm, t, d, 1)

    def body(av, ag, u_ref, v_ref, ga_ref, gb_ref, avp, agp, avn, agn, x_ref,
             cw_ref, cb_ref, lg_ref, lb_ref, wco_ref, sg_ref, sb_ref, ws_ref, bias_ref, wso_ref,
             gba_ref, gbb_ref, wo_ref,
             c1_ref, ya_ref, yb_ref, x1_ref, cbuf, c1f, vn_ref, mixed_ref):
        i = pl.program_id(0)
        first = (i * tm) % seq == 0
        last = ((i + 1) * tm) % seq == 0
        _fill_glu_buffer(cbuf, av, ag, avp, agp, avn, agn, first, last, tm)
        _dwconv(cbuf, cw_ref, c1f, tm, flip=False, rc=CONV_ROWS_LOOP, unroll=False)
        c1 = c1f[...] + cb_ref[...]
        c1_ref[...] = c1.astype(BF16)
        c2hat, _ = _layernorm_stats(c1)
        c2 = c2hat * lg_ref[...] + lb_ref[...]
        c3 = (c2 * _sig(c2)).astype(BF16)
        ya = _nn(c3, wco_ref[...].reshape(d, d))
        ya_ref[...] = ya.astype(BF16)
        vhat, _ = _layernorm_stats(v_ref[...].astype(F32))
        vn_ref[...] = (vhat * sg_ref[...] + sb_ref[...]).astype(BF16)
        _spatial_mix(ws_ref, vn_ref, bias_ref, mixed_ref, tm)
        gated = (u_ref[...].astype(F32) * mixed_ref[...]).astype(BF16)
        yb = _nn(gated, wso_ref[...].reshape(d, d))
        yb_ref[...] = yb.astype(BF16)
        sa = _sig(ga_ref[...].astype(F32) + gba_ref[...])
        sb = _sig(gb_ref[...].astype(F32) + gbb_ref[...])
        merged = (sa * ya + sb * yb).astype(BF16)
        x1_ref[...] = x_ref[...] + _nn(merged, wo_ref[...].reshape(d, d))

    cols = [_tile(tm, d, j) for j in range(6)]
    return _call(
        body, name=name, grid=(t // tm,),
        args=(proj,) * 10 + (x, w['conv_w'], small['conv_b'], small['conv_ln_g'], small['conv_ln_b'], w['w_conv_out'],
                             small['sgu_ln_g'], small['sgu_ln_b'], small['ws'], small['bias_full'], w['w_sgu_out'],
                             small['gba'], small['gbb'], w['w_o']),
        in_specs=cols + [hp_v, hp_g, hn_v, hn_g, _tile(tm, d),
                         _weight(w['conv_w']), _row(d), _row(d), _row(d),
                         _weight(w['w_conv_out']), _row(d), _row(d),
                         _resident(small['ws'].shape), _resident(small['bias_full'].shape),
                         _weight(w['w_sgu_out']), _row(d), _row(d), _weight(w['w_o'])],
        out_specs=[_tile(tm, d)] * 4,
        out_shape=[_sds((t, d), BF16)] * 3 + [_sds((t, d), F32)],
        scratch_shapes=[pltpu.VMEM((d // LANES, tm + 2 * HALO, LANES), F32), pltpu.VMEM((tm, d), F32),
                        pltpu.VMEM((tm, d), BF16), pltpu.VMEM((tm, d), F32)],
        semantics=("parallel",), comm=comm)


def _pair_gate_up(wg, wu, name):
    n, c, d = wg.shape

    def body(wg_ref, wu_ref, o_ref):
        o_ref[0:c, :] = wg_ref[0]
        o_ref[c:2 * c, :] = wg_ref[1]
        o_ref[2 * c:3 * c, :] = wu_ref[0]
        o_ref[3 * c:4 * c, :] = wu_ref[1]

    pair = pl.BlockSpec((2, c, d), lambda k: (k, 0, 0))
    return _call(body, name=name, grid=(n // 2,), args=(wg, wu), in_specs=[pair, pair],
                 out_specs=[pl.BlockSpec((None, 4 * c, d), lambda k: (k, 0, 0))],
                 out_shape=[_sds((n // 2, 4 * c, d), wg.dtype)], semantics=("parallel",))[0][0]


def _pair_specs(w, tm):
    d = w['w_gate_up'].shape[2]
    up = pl.BlockSpec((None, 2 * FF_PAIR, d), lambda i, k: (k, 0, 0))
    down = pl.BlockSpec((None, FF_PAIR, d), lambda i, k: (k, 0, 0))
    return [up, down]


def _ffn_fwd(x1, gain, w, tm, name, comm):
    t, d = x1.shape
    tm = min(tm, t)
    npair = NDEV // 2
    hid = NDEV * FF_PAD
    wd_pairs = w['w_ffn_down'].reshape(npair, FF_PAIR, d)

    def body(x_ref, g_ref, wgu_ref, wd_ref, h_ref, gg_ref, uu_ref, x2_ref, hb_ref, acc_ref):
        k = pl.program_id(1)

        @pl.when(k == 0)
        def _():
            xv = x_ref[...]
            r = lax.rsqrt(jnp.mean(xv * xv, axis=-1, keepdims=True) + EPS)
            h = (xv * r * g_ref[...]).astype(BF16)
            hb_ref[...] = h
            h_ref[...] = h
            acc_ref[...] = xv

        gu = _nt(hb_ref[...], wgu_ref[...])
        gk = gu[:, 0:FF_PAIR]
        uk = gu[:, FF_PAIR:2 * FF_PAIR]
        gg_ref[...] = gk.astype(BF16)
        uu_ref[...] = uk.astype(BF16)
        ak = (gk * _sig(gk) * uk).astype(BF16)
        acc_ref[...] += _nn(ak, wd_ref[...])

        @pl.when(k == npair - 1)
        def _():
            x2_ref[...] = acc_ref[...]

    pair_cols = pl.BlockSpec((tm, FF_PAIR), lambda i, k: (i, k))
    return _call(
        body, name=name, grid=(t // tm, npair),
        args=(x1, gain, w['w_gate_up'], wd_pairs),
        in_specs=[_tile(tm, d), _row(d)] + _pair_specs(w, tm),
        out_specs=[_tile(tm, d), pair_cols, pair_cols, _tile(tm, d)],
        out_shape=[_sds((t, d), BF16), _sds((t, hid), BF16), _sds((t, hid), BF16), _sds((t, d), F32)],
        scratch_shapes=[pltpu.VMEM((tm, d), BF16), pltpu.VMEM((tm, d), F32)],
        semantics=("parallel", "arbitrary"), comm=comm)


def _init_small(first, small_ref, acc_ref):
    @pl.when(first)
    def _():
        small_ref[...] = jnp.zeros(small_ref.shape, F32)
        acc_ref[...] = jnp.zeros(acc_ref.shape, F32)


def _finish_small(last, small_ref, acc_ref, nq):
    @pl.when(last)
    def _():
        for q in range(nq):
            small_ref[q:q + 1, :] = jnp.sum(acc_ref[q], axis=0, keepdims=True)


def _loss_bwd(x, gain, target, tm, name):
    t, d = x.shape
    nsteps = t // tm

    def body(x_ref, g_ref, t_ref, dx_ref, small_ref, acc_ref):
        i = pl.program_id(0)
        _init_small(i == 0, small_ref, acc_ref)
        xv = x_ref[...]
        r = lax.rsqrt(jnp.mean(xv * xv, axis=-1, keepdims=True) + EPS)
        xhat = xv * r
        diff = xhat * g_ref[...] - t_ref[...]
        dy = diff * (1.0 / d)
        acc_ref[0] += _fold(dy * xhat)
        acc_ref[1] += _fold(diff * diff)
        dxhat = dy * g_ref[...]
        dx_ref[...] = r * (dxhat - xhat * jnp.mean(dxhat * xhat, axis=-1, keepdims=True))

        @pl.when(i == nsteps - 1)
        def _():
            small_ref[0:1, :] = jnp.sum(acc_ref[0], axis=0, keepdims=True)
            total = jnp.sum(acc_ref[1]) * (0.5 / d)
            small_ref[1:2, :] = jnp.full((1, d), total, F32)

    return _call(body, name=name, grid=(nsteps,), args=(x, gain, target),
                 in_specs=[_tile(tm, d), _row(d), _tile(tm, d)],
                 out_specs=[_tile(tm, d), _resident((SUBLANES, d))],
                 out_shape=[_sds((t, d), F32), _sds((SUBLANES, d), F32)],
                 scratch_shapes=[pltpu.VMEM((2, SUBLANES, d), F32)], semantics=("arbitrary",))[0]


def _ffn_bwd(dx2, x1, gain, gg, uu, w, tm, name, comm):
    t, d = x1.shape
    tm = min(tm, t)
    npair = NDEV // 2
    hid = NDEV * FF_PAD
    nsteps = t // tm
    wd_pairs = w['w_ffn_down'].reshape(npair, FF_PAIR, d)

    def body(dx_ref, x_ref, g_ref, gg_ref, uu_ref, wgu_ref, wd_ref,
             dg_ref, du_ref, dx1_ref, small_ref, acc_ref, dxb_ref, dh_ref, dgu_ref):
        i, k = pl.program_id(0), pl.program_id(1)
        _init_small((i == 0) & (k == 0), small_ref, acc_ref)

        @pl.when(k == 0)
        def _():
            dxb_ref[...] = dx_ref[...].astype(BF16)
            dh_ref[...] = jnp.zeros(dh_ref.shape, F32)

        gk = gg_ref[...].astype(F32)
        uk = uu_ref[...].astype(F32)
        sg = _sig(gk)
        silu = gk * sg
        da = _nt(dxb_ref[...], wd_ref[...])
        dgk = (da * uk * (sg * (1.0 + gk * (1.0 - sg)))).astype(BF16)
        duk = (da * silu).astype(BF16)
        dg_ref[...] = dgk
        du_ref[...] = duk
        dgu_ref[:, 0:FF_PAIR] = dgk
        dgu_ref[:, FF_PAIR:2 * FF_PAIR] = duk
        dh_ref[...] += _nn(dgu_ref[...], wgu_ref[...])

        @pl.when(k == npair - 1)
        def _():
            xv = x_ref[...]
            dh = dh_ref[...]
            r = lax.rsqrt(jnp.mean(xv * xv, axis=-1, keepdims=True) + EPS)
            xhat = xv * r
            acc_ref[0] += _fold(dh * xhat)
            dxhat = dh * g_ref[...]
            dx1_ref[...] = dx_ref[...] + r * (dxhat - xhat * jnp.mean(dxhat * xhat, axis=-1, keepdims=True))

        _finish_small((i == nsteps - 1) & (k == npair - 1), small_ref, acc_ref, 1)

    pair_cols = pl.BlockSpec((tm, FF_PAIR), lambda i, k: (i, k))
    once = pl.BlockSpec((tm, d), lambda i, k: (i, 0), pipeline_mode=pl.Buffered(1))
    return _call(
        body, name=name, grid=(nsteps, npair),
        args=(dx2, x1, gain, gg, uu, w['w_gate_up'], wd_pairs),
        in_specs=[_tile(tm, d), once, _row(d), pair_cols, pair_cols] + _pair_specs(w, tm),
        out_specs=[pair_cols] * 2 + [_tile(tm, d), _resident((SUBLANES, d))],
        out_shape=[_sds((t, hid), BF16)] * 2 + [_sds((t, d), F32), _sds((SUBLANES, d), F32)],
        scratch_shapes=[pltpu.VMEM((1, SUBLANES, d), F32), pltpu.VMEM((tm, d), BF16), pltpu.VMEM((tm, d), F32),
                        pltpu.VMEM((tm, 2 * FF_PAIR), BF16)],
        semantics=("arbitrary", "arbitrary"), comm=comm, vmem=VMEM_LIMIT_FFN_BWD)


def _matmul_tn(a, b, a_blk, b_blk, stack, shard, tm, name, comm, swiglu_with=None):
    t, ma = a.shape
    tm = min(tm, t)
    nb_ = b.shape[1]
    na, nb = ma // a_blk, nb_ // b_blk
    nsteps = t // tm
    cw = min(TN_COLS, b_blk)
    if stack == 'b':
        per = b_blk // shard
        out_shape, out_spec = (nb_ // shard, ma, shard), pl.BlockSpec((per, a_blk, shard), lambda i, j, k: (j, 0, 0))
    elif stack == 'a':
        per = a_blk // shard
        out_shape, out_spec = (ma // shard, shard, nb_), pl.BlockSpec((per, shard, b_blk), lambda i, j, k: (i, 0, 0))
    else:
        out_shape, out_spec = (ma, nb_), pl.BlockSpec((a_blk, b_blk), lambda i, j, k: (i, j))

    extra = [] if swiglu_with is None else [swiglu_with]

    def body(a_ref, *refs):
        b_ref, o_ref, acc_ref = refs[-3:]
        k = pl.program_id(2)

        @pl.when(k == 0)
        def _():
            acc_ref[...] = jnp.zeros(acc_ref.shape, F32)

        if swiglu_with is None:
            av = a_ref[...].astype(BF16)
        else:
            gv = a_ref[...].astype(F32)
            av = (gv * _sig(gv) * refs[0][...].astype(F32)).astype(BF16)
        for c in range(0, b_blk, cw):
            acc_ref[:, c:c + cw] += _tn(av, b_ref[:, c:c + cw].astype(BF16))

        @pl.when(k == nsteps - 1)
        def _():
            if stack == 'b':
                for s in range(per):
                    o_ref[s] = acc_ref[:, s * shard:(s + 1) * shard].astype(BF16)
            elif stack == 'a':
                for s in range(per):
                    o_ref[s] = acc_ref[s * shard:(s + 1) * shard, :].astype(BF16)
            else:
                o_ref[...] = acc_ref[...].astype(BF16)

    a_spec = pl.BlockSpec((tm, a_blk), lambda i, j, k: (k, i))
    res, got = _call(
        body, name=name, grid=(na, nb, nsteps), args=(a, *extra, b),
        in_specs=[a_spec] * (1 + len(extra)) + [pl.BlockSpec((tm, b_blk), lambda i, j, k: (k, j))],
        out_specs=[out_spec], out_shape=[_sds(out_shape, BF16)],
        scratch_shapes=[pltpu.VMEM((a_blk, b_blk), F32)],
        semantics=("parallel", "parallel", "arbitrary"), comm=comm)
    return res[0], got


def _mixer_bwd(dx1, proj, c1, ya, yb, w, small, tm, name, comm):
    t, d = dx1.shape
    nsteps = t // tm
    nq = 6

    def body(dx_ref, u_ref, v_ref, ga_ref, gb_ref, c1_ref, ya_ref, yb_ref,
             lg_ref, lb_ref, wco_ref, sg_ref, sb_ref, ws_ref, wst_ref, bias_ref, wso_ref, gba_ref, gbb_ref, wo_ref,
             sel_ref,
             dp_ref, dc1_ref, mg_ref, c3_ref, gt_ref, dya_ref, dyb_ref, small_ref, dws_ref, dbs_ref,
             acc_ref, vn_ref, mixed_ref, dmix_ref, dvn_ref, dbias_ref):
        i = pl.program_id(0)
        _init_small(i == 0, small_ref, acc_ref)

        @pl.when(i == 0)
        def _():
            dws_ref[...] = jnp.zeros(dws_ref.shape, F32)
            dbs_ref[...] = jnp.zeros(dbs_ref.shape, F32)
            dbias_ref[...] = jnp.zeros(dbias_ref.shape, F32)

        dmerged = _nt(dx_ref[...].astype(BF16), wo_ref[...].reshape(d, d))
        ya = ya_ref[...].astype(F32)
        yb = yb_ref[...].astype(F32)
        sa = _sig(ga_ref[...].astype(F32) + gba_ref[...])
        sb = _sig(gb_ref[...].astype(F32) + gbb_ref[...])
        mg_ref[...] = (sa * ya + sb * yb).astype(BF16)
        dya = (dmerged * sa).astype(BF16)
        dyb = (dmerged * sb).astype(BF16)
        dya_ref[...] = dya
        dyb_ref[...] = dyb
        dga = dmerged * ya * (sa * (1.0 - sa))
        dgb = dmerged * yb * (sb * (1.0 - sb))
        acc_ref[0] += _fold(dga)
        acc_ref[1] += _fold(dgb)
        dp_ref[:, 0:2 * d] = jnp.zeros((tm, 2 * d), BF16)
        dp_ref[:, 4 * d:5 * d] = dga.astype(BF16)
        dp_ref[:, 5 * d:6 * d] = dgb.astype(BF16)
        c2hat, rstd = _layernorm_stats(c1_ref[...].astype(F32))
        c2 = c2hat * lg_ref[...] + lb_ref[...]
        s2 = _sig(c2)
        c3_ref[...] = (c2 * s2).astype(BF16)
        dc3 = _nt(dya, wco_ref[...].reshape(d, d))
        dc2 = dc3 * (s2 * (1.0 + c2 * (1.0 - s2)))
        acc_ref[2] += _fold(dc2 * c2hat)
        acc_ref[3] += _fold(dc2)
        dc2hat = dc2 * lg_ref[...]
        dc1_ref[...] = (rstd * (dc2hat - jnp.mean(dc2hat, axis=-1, keepdims=True)
                                - c2hat * jnp.mean(dc2hat * c2hat, axis=-1, keepdims=True))).astype(BF16)
        vhat, rstd_v = _layernorm_stats(v_ref[...].astype(F32))
        vn_ref[...] = (vhat * sg_ref[...] + sb_ref[...]).astype(BF16)
        _spatial_mix(ws_ref, vn_ref, bias_ref, mixed_ref, tm)
        u = u_ref[...].astype(F32)
        mixed = mixed_ref[...]
        gt_ref[...] = (u * mixed).astype(BF16)
        dgated = _nt(dyb, wso_ref[...].reshape(d, d))
        dp_ref[:, 2 * d:3 * d] = (dgated * mixed).astype(BF16)
        dmix_ref[...] = dgated * u
        for ci in range(tm // CHUNK):
            rs = slice(ci * CHUNK, (ci + 1) * CHUNK)
            dbias_ref[...] += dmix_ref[rs, :]
            for g in range(GROUPS):
                ls = slice(g * LANES, (g + 1) * LANES)
                dm = dmix_ref[rs, ls].astype(BF16)
                dws_ref[g] += _nt(dm, vn_ref[rs, ls])
                dvn_ref[rs, ls] = _nn(wst_ref[g], dm)
        dvn = dvn_ref[...]
        acc_ref[4] += _fold(dvn * vhat)
        acc_ref[5] += _fold(dvn)
        dvhat = dvn * sg_ref[...]
        dp_ref[:, 3 * d:4 * d] = (rstd_v * (dvhat - jnp.mean(dvhat, axis=-1, keepdims=True)
                                           - vhat * jnp.mean(dvhat * vhat, axis=-1, keepdims=True))).astype(BF16)
        _finish_small(i == nsteps - 1, small_ref, acc_ref, nq)

        @pl.when(i == nsteps - 1)
        def _():
            db = dbias_ref[...]
            hi = db.astype(BF16)
            lo = (db - hi.astype(F32)).astype(BF16)
            dbs_ref[...] = _nt(sel_ref[...], hi) + _nt(sel_ref[...], lo)

    cols = [_tile(tm, d, j) for j in (2, 3, 4, 5)]
    return _call(
        body, name=name, grid=(nsteps,),
        args=(dx1, proj, proj, proj, proj, c1, ya, yb,
              small['conv_ln_g'], small['conv_ln_b'], w['w_conv_out'], small['sgu_ln_g'], small['sgu_ln_b'],
              small['ws'], small['wst'], small['bias_full'], w['w_sgu_out'], small['gba'], small['gbb'], w['w_o'],
              small['group_sel']),
        in_specs=[_tile(tm, d)] + cols + [_tile(tm, d)] * 3 + [
            _row(d), _row(d), _weight(w['w_conv_out']),
            _row(d), _row(d), _resident(small['ws'].shape), _resident(small['wst'].shape),
            _resident(small['bias_full'].shape), _weight(w['w_sgu_out']), _row(d), _row(d),
            _weight(w['w_o']), _resident(small['group_sel'].shape)],
        out_specs=[_tile(tm, 6 * d)] + [_tile(tm, d)] * 6 + [
            _resident((SUBLANES, d)), _resident((GROUPS, CHUNK, CHUNK)), _resident((GROUPS, CHUNK))],
        out_shape=[_sds((t, 6 * d), BF16)] + [_sds((t, d), BF16)] * 6 + [
            _sds((SUBLANES, d), F32), _sds((GROUPS, CHUNK, CHUNK), F32), _sds((GROUPS, CHUNK), F32)],
        scratch_shapes=[pltpu.VMEM((nq, SUBLANES, d), F32), pltpu.VMEM((tm, d), BF16), pltpu.VMEM((tm, d), F32),
                        pltpu.VMEM((tm, d), F32), pltpu.VMEM((tm, d), F32), pltpu.VMEM((CHUNK, d), F32)],
        semantics=("arbitrary",), comm=comm)


def _conv_bwd(dproj, dc1, proj, w, pairs, seq, tm, name, comm):
    t, d = dc1.shape
    nsteps = t // tm
    nblk = d // LANES
    npairs = len(pairs)
    hp_v, hn_v = _halo_specs(tm, t, d, 0)
    hp_g, hn_g = _halo_specs(tm, t, d, 1)
    hp_d, hn_d = _halo_specs(tm, t, d, 0)

    def body(*refs):
        (dp_in, dc_ref, dcp, dcn, av, ag, avp, agp, avn, agn, cw_ref), refs = refs[:11], refs[11:]
        ab_refs, refs = refs[:2 * npairs], refs[2 * npairs:]
        (dp_ref, dcw_ref, small_ref), refs = refs[:3], refs[3:]
        grad_refs, refs = refs[:npairs], refs[npairs:]
        acc_ref, cbuf, dbuf, dc0f, accw, gacc = refs
        del dp_in
        i = pl.program_id(0)
        _init_small(i == 0, small_ref, acc_ref)

        @pl.when(i == 0)
        def _():
            accw[...] = jnp.zeros(accw.shape, F32)
            dcw_ref[...] = jnp.zeros(dcw_ref.shape, F32)
            gacc[...] = jnp.zeros(gacc.shape, F32)

        first = (i * tm) % seq == 0
        last = ((i + 1) * tm) % seq == 0
        _fill_glu_buffer(cbuf, av, ag, avp, agp, avn, agn, first, last, tm)
        dc1v = dc_ref[...].astype(F32)
        _fill_halo_buffer(dbuf, dcp[...].astype(F32), dc1v, dcn[...].astype(F32), first, last, tm)
        acc_ref[0] += _fold(dc1v)
        _dwconv(dbuf, cw_ref, dc0f, tm, flip=True, rc=CONV_ROWS, unroll=True)

        def rows(r0):
            for kb in range(nblk):
                dv = dbuf[kb, r0 + HALO:r0 + HALO + CONV_ROWS, :]
                for k in range(CONV_TAPS):
                    accw[kb, k] += _fold(dv * _shifted(cbuf, kb, r0, 1 + k, CONV_ROWS))

        _row_chunks(tm, CONV_ROWS, True, rows)
        for q in range(npairs):
            aq = ab_refs[2 * q][...].astype(BF16)
            for c in range(0, d, TN_COLS):
                gacc[q, :, c:c + TN_COLS] += _tn(aq, ab_refs[2 * q + 1][:, c:c + TN_COLS].astype(BF16))
        sg = _sig(ag[...].astype(F32))
        avv = av[...].astype(F32)
        dc0 = dc0f[...]
        dp_ref[:, 0:d] = (dc0 * sg).astype(BF16)
        dp_ref[:, d:2 * d] = (dc0 * avv * (sg * (1.0 - sg))).astype(BF16)
        _finish_small(i == nsteps - 1, small_ref, acc_ref, 1)

        @pl.when(i == nsteps - 1)
        def _():
            for kb in range(nblk):
                dcw_ref[kb] = jnp.sum(accw[kb], axis=1)
            for q in range(npairs):
                grad_refs[q][...] = gacc[q].astype(BF16)

    return _call(
        body, name=name, grid=(nsteps,),
        args=(dproj, dc1, dc1, dc1, proj, proj, proj, proj, proj, proj, w['conv_w']) + tuple(a for ab in pairs for a in ab),
        in_specs=[pl.BlockSpec(memory_space=pl.ANY), _tile(tm, d), hp_d, hn_d, _tile(tm, d, 0), _tile(tm, d, 1),
                  hp_v, hp_g, hn_v, hn_g, _weight(w['conv_w'])] + [_tile(tm, d)] * (2 * npairs),
        out_specs=[_tile(tm, 2 * d), _resident((nblk, CONV_TAPS_PADDED, LANES)), _resident((SUBLANES, d))]
        + [_resident((d, d))] * npairs,
        out_shape=[_sds(dproj.shape, BF16), _sds((nblk, CONV_TAPS_PADDED, LANES), F32), _sds((SUBLANES, d), F32)]
        + [_sds((d, d), BF16)] * npairs,
        scratch_shapes=[pltpu.VMEM((1, SUBLANES, d), F32), pltpu.VMEM((nblk, tm + 2 * HALO, LANES), F32),
                        pltpu.VMEM((nblk, tm + 2 * HALO, LANES), F32), pltpu.VMEM((tm, d), F32),
                        pltpu.VMEM((nblk, CONV_TAPS_PADDED, SUBLANES, LANES), F32),
                        pltpu.VMEM((npairs, d, d), F32)],
        aliases={0: 0}, semantics=("arbitrary",), comm=comm)


def _in_proj_bwd(dproj, x, dx1, gain, w, tm, name, comm):
    t, d = x.shape
    nb, _, bw = w.shape
    nsteps = t // tm

    def body(dp_ref, x_ref, dx1_ref, g_ref, w_ref, dx_ref, small_ref, acc_ref):
        i = pl.program_id(0)
        _init_small(i == 0, small_ref, acc_ref)
        dh = jnp.zeros((tm, d), F32)
        for k in range(nb):
            dh = dh + _nt(dp_ref[:, k * bw:(k + 1) * bw], w_ref[k])
        xv = x_ref[...]
        r = lax.rsqrt(jnp.mean(xv * xv, axis=-1, keepdims=True) + EPS)
        xhat = xv * r
        acc_ref[0] += _fold(dh * xhat)
        dxhat = dh * g_ref[...]
        dx_ref[...] = dx1_ref[...] + r * (dxhat - xhat * jnp.mean(dxhat * xhat, axis=-1, keepdims=True))
        _finish_small(i == nsteps - 1, small_ref, acc_ref, 1)

    return _call(body, name=name, grid=(nsteps,), args=(dproj, x, dx1, gain, w),
                 in_specs=[_tile(tm, nb * bw), _tile(tm, d), _tile(tm, d), _row(d), _weight(w)],
                 out_specs=[_tile(tm, d), _resident((SUBLANES, d))],
                 out_shape=[_sds((t, d), F32), _sds((SUBLANES, d), F32)],
                 scratch_shapes=[pltpu.VMEM((1, SUBLANES, d), F32)], semantics=("arbitrary",), comm=comm)


def _adam(wv, g, mv, vv):
    m = ADAM_B1 * mv + (1.0 - ADAM_B1) * g
    v = ADAM_B2 * vv + (1.0 - ADAM_B2) * jnp.square(g)
    m_hat = m / (1.0 - ADAM_B1 ** ADAM_STEP)
    v_hat = v / (1.0 - ADAM_B2 ** ADAM_STEP)
    delta = -ADAM_LR * (m_hat / (jnp.sqrt(v_hat) + ADAM_EPS) + ADAM_WD * wv)
    return delta, m, v


def _adamw_layer(layer, w, m, v, parts, prev, nsplit, name):
    nl, r, c = w.shape
    npart, pr, pc = parts.shape
    rt, prt = r // nsplit, pr // nsplit

    def body(w_ref, m_ref, v_ref, p_ref, *rest):
        g_ref, d_ref, nm_ref, nv_ref = rest[-4:]
        g = p_ref[0, 0:rt, 0:c].astype(F32)
        for s in range(1, npart):
            g = g + p_ref[s, 0:rt, 0:c].astype(F32)
        delta, mn, vn = _adam(w_ref[0], g, m_ref[0], v_ref[0])
        g_ref[0] = g
        d_ref[0] = delta
        nm_ref[0] = mn
        nv_ref[0] = vn

    wspec = pl.BlockSpec((1, rt, c), lambda i: (layer, i, 0))
    pspec = pl.BlockSpec((npart, prt, pc), lambda i: (0, i, 0))
    in_specs = [wspec, wspec, wspec, pspec]
    args = [w, m, v, parts]
    aliases = {}
    if prev is not None:
        in_specs += [pl.BlockSpec(memory_space=pl.ANY)] * 4
        args += list(prev)
        aliases = {4 + q: q for q in range(4)}
    return _call(body, name=name, grid=(nsplit,), args=args, in_specs=in_specs, out_specs=[wspec] * 4,
                 out_shape=[_sds(w.shape, F32)] * 4, aliases=aliases, semantics=("parallel",))[0]


VEC_ROWS = {'gate_bias': (0, 1), 'conv_ln_g': 2, 'conv_ln_b': 3, 'sgu_ln_g': 4, 'sgu_ln_b': 5,
            'conv_b': SUBLANES, 'norm_ffn': 2 * SUBLANES}
FINAL_ROW = 3 * SUBLANES
LOSS_ROW = 3 * SUBLANES + 1


def _adamw_small(gathered, params, moments_m, moments_v):
    names = list(params)
    nper = len(names)
    nl = len(gathered)

    def body(*refs):
        g_refs = [refs[4 * l:4 * l + 4] for l in range(nl)]
        rest = refs[4 * nl:]
        w_refs = dict(zip(names, rest[0:nper]))
        m_refs = dict(zip(names, rest[nper:2 * nper]))
        v_refs = dict(zip(names, rest[2 * nper:3 * nper]))
        outs = rest[3 * nper:]
        loss_ref = outs[4 * nper]
        o = {kind: dict(zip(names, outs[q * nper:(q + 1) * nper])) for q, kind in enumerate("gdmv")}

        def put(nm, idx, g):
            delta, mn, vn = _adam(w_refs[nm][idx], g, m_refs[nm][idx], v_refs[nm][idx])
            o["g"][nm][idx] = g
            o["d"][nm][idx] = delta
            o["m"][nm][idx] = mn
            o["v"][nm][idx] = vn

        def total(ref, *idx):
            g = ref[(0, *idx)]
            for s in range(1, NDEV):
                g = g + ref[(s, *idx)]
            return g

        for l, (vec_ref, dws_ref, dbs_ref, vin_ref) in enumerate(g_refs):
            dd = vec_ref.shape[2]
            put('w_spatial', (l,), total(dws_ref))
            put('b_spatial', (l,), total(dbs_ref))
            put('norm_mix', (slice(l, l + 1), slice(None)), total(vin_ref, slice(0, 1)))
            for nm, rr in VEC_ROWS.items():
                if nm == 'gate_bias':
                    put(nm, (slice(l, l + 1), slice(0, dd)), total(vec_ref, slice(rr[0], rr[0] + 1)))
                    put(nm, (slice(l, l + 1), slice(dd, 2 * dd)), total(vec_ref, slice(rr[1], rr[1] + 1)))
                else:
                    put(nm, (slice(l, l + 1), slice(None)), total(vec_ref, slice(rr, rr + 1)))
        last = g_refs[nl - 1][0]
        put('norm_final', (slice(0, 1), slice(None)), total(last, slice(FINAL_ROW, FINAL_ROW + 1)))
        loss_ref[...] = total(last, slice(LOSS_ROW, LOSS_ROW + 1))

    ins = [a for g in gathered for a in g] + [params[n] for n in names] + [moments_m[n] for n in names] + [moments_v[n] for n in names]
    out_shape = [_sds(params[n].shape, F32) for n in names] * 4 + [_sds((1, gathered[0][0].shape[2]), F32)]
    res = pl.pallas_call(body, name="adamw_small", out_shape=out_shape, compiler_params=_params())(*ins)
    return {kind: dict(zip(names, res[q * nper:(q + 1) * nper])) for q, kind in enumerate("gdmv")}, res[4 * nper]


def _hidden_major(a):
    return jnp.swapaxes(a, 1, 2)


def _prepare_weights(p):
    d = p['w_in'].shape[1]
    return {
        'w_in': _cast_pad(p['w_in'], d, p['w_in'].shape[2], "cast_w_in"),
        'w_conv_out': _cast_pad(p['w_conv_out'], p['w_conv_out'].shape[1], d, "cast_w_conv_out"),
        'w_sgu_out': _cast_pad(p['w_sgu_out'], p['w_sgu_out'].shape[1], d, "cast_w_sgu_out"),
        'w_o': _cast_pad(p['w_o'], p['w_o'].shape[1], d, "cast_w_o"),
        'w_ffn_gate': _cast_pad(_hidden_major(p['w_ffn_gate']), FF_PAD, d, "cast_w_ffn_gate"),
        'w_ffn_up': _cast_pad(_hidden_major(p['w_ffn_up']), FF_PAD, d, "cast_w_ffn_up"),
        'w_ffn_down': _cast_pad(p['w_ffn_down'], FF_PAD, d, "cast_w_ffn_down"),
        'conv_w': jnp.pad(p['conv_w'][:, :, 0, :], ((0, 0), (0, CONV_TAPS_PADDED - CONV_TAPS), (0, 0))),
    }


def _gather_of(shards, names, layer):
    return _Gather([shards[n] for n in names], [layer] * len(names))


def _layer_small(p, layer):
    d = p['norm_mix'].shape[1]
    ws = p['w_spatial'][layer]
    rows = {n: p[n][layer:layer + 1] for n in ('norm_mix', 'norm_ffn', 'conv_b', 'conv_ln_g', 'conv_ln_b',
                                               'sgu_ln_g', 'sgu_ln_b')}
    return {
        **rows,
        'ws': ws.astype(BF16), 'wst': jnp.swapaxes(ws, 1, 2).astype(BF16),
        'bias_full': jnp.repeat(p['b_spatial'][layer].T, LANES, axis=1),
        'gba': p['gate_bias'][layer:layer + 1, 0:d], 'gbb': p['gate_bias'][layer:layer + 1, d:2 * d],
        'group_sel': (jnp.arange(d)[None, :] // LANES == jnp.arange(GROUPS)[:, None]).astype(BF16),
    }


class _GradQueue:
    def __init__(self):
        self.pending = []
        self.done = {}

    def push(self, key, array):
        self.pending.append((key, array))

    def take(self):
        keys = [k for k, _ in self.pending]
        comm = _Scatter([a for _, a in self.pending]) if self.pending else None
        self.pending = []
        return keys, comm

    def put(self, keys, arrays):
        self.done.update(zip(keys, arrays))


def _forward_backward(p, shards, x, target, seq):
    nl = p['norm_mix'].shape[0]
    d = x.shape[1]
    smalls = [_layer_small(p, l) for l in range(nl)]
    w_in = _exchange_alone(_gather_of(shards, ['w_in'], 0), "gather_w_in_0")[0]
    saved = []
    for l in range(nl):
        (h, proj), got = _in_proj(x, smalls[l]['norm_mix'], w_in, TILE_IN, f"in_proj_{l}",
                                  _gather_of(shards, MIXER_WEIGHTS, l))
        w = dict(zip(MIXER_WEIGHTS, got), w_in=w_in)
        (c1, ya, yb, x1), got = _mixer_fwd(proj, x, w, smalls[l], seq, TILE_MIX, f"mixer_fwd_{l}",
                                           _gather_of(shards, FFN_WEIGHTS, l))
        w.update(zip(FFN_WEIGHTS, got))
        w['w_gate_up'] = _pair_gate_up(w['w_ffn_gate'], w['w_ffn_up'], f"pair_gate_up_{l}")
        nxt = _gather_of(shards, ['w_in'], l + 1) if l + 1 < nl else None
        (h2, gg, uu, x2), got = _ffn_fwd(x1, smalls[l]['norm_ffn'], w, TILE_FFN_FWD, f"ffn_fwd_{l}", nxt)
        saved.append(dict(x=x, h=h, proj=proj, c1=c1, ya=ya, yb=yb, x1=x1, h2=h2, gg=gg, uu=uu, w=w))
        x = x2
        if got:
            w_in = got[0]
    dx, small_loss = _loss_bwd(x, p['norm_final'][None, :], target, TILE_LOSS, "loss_bwd")
    queue = _GradQueue()
    small_gathered = [None] * nl
    small_pending = None
    rows = d // NDEV
    hid = NDEV * FF_PAD
    for l in reversed(range(nl)):
        s = saved[l]
        w = s['w']

        def hosted(fn, *args, extra=None):
            keys, comm = queue.take()
            res, got = fn(*args, _together(comm, extra))
            queue.put(keys, got[:len(keys)])
            return res, got[len(keys):]

        def tn(key, a, b, a_blk, b_blk, stack, shard, reshape=None, host=False, extra=None, swiglu_with=None):
            keys, comm = queue.take() if host else ([], None)
            tile = TILE_TN if swiglu_with is None else TILE_TN_SWIGLU
            g, got = _matmul_tn(a, b, a_blk, b_blk, stack, shard, tile, f"dw_{key}_{l}", _together(comm, extra),
                                swiglu_with)
            queue.put(keys, got[:len(keys)])
            queue.push((l, key), g if reshape is None else g.reshape(reshape))
            return got[len(keys):]

        (dgg, duu, dx1, small_ffn), _ = hosted(_ffn_bwd, dx, s['x1'], smalls[l]['norm_ffn'], s['gg'], s['uu'], w,
                                               TILE_FFN_BWD, f"ffn_bwd_{l}")
        tn('w_ffn_gate', dgg, s['h2'], hid, d, 'a', FF_PAD)
        tn('w_ffn_up', duu, s['h2'], hid, d, 'a', FF_PAD)
        (dproj, dc1, merged, c3, gated, dya, dyb, small_mix, dws, dbs), _ = hosted(
            _mixer_bwd, dx1, s['proj'], s['c1'], s['ya'], s['yb'], w, smalls[l], TILE_MIX, f"mixer_bwd_{l}")
        tn('w_ffn_down', s['gg'], dx, hid, d, 'a', FF_PAD, swiglu_with=s['uu'])
        extra = small_pending[1] if small_pending else None
        (dproj, g_conv, small_conv, g_o, g_co, g_so), got = hosted(
            _conv_bwd, dproj, dc1, s['proj'], w, [(merged, dx1), (c3, dya), (gated, dyb)], seq, TILE_MIX,
            f"conv_bwd_{l}", extra=extra)
        if small_pending:
            small_gathered[small_pending[0]] = got
            small_pending = None
        queue.push((l, 'conv_w'), g_conv)
        queue.push((l, 'w_o'), g_o.reshape(NDEV, rows, d))
        queue.push((l, 'w_conv_out'), g_co.reshape(NDEV, rows, d))
        queue.push((l, 'w_sgu_out'), g_so.reshape(NDEV, rows, d))
        blocks = [small_mix, small_conv, small_ffn] + ([small_loss] if l == nl - 1 else [])
        small_main = [jnp.concatenate(blocks, axis=0), dws, dbs]
        small_main_gathered = tn('w_in', s['h'], dproj, d, hid, 'b', w['w_in'].shape[2], host=True,
                                 extra=_Gather(small_main) if l == 0 else None)
        if l == 0:
            (dx, small_in), _ = hosted(_in_proj_bwd, dproj, s['x'], dx1, smalls[l]['norm_mix'], w['w_in'], TILE_IN,
                                       f"in_proj_bwd_{l}")
        else:
            (dx, small_in), _ = _in_proj_bwd(dproj, s['x'], dx1, smalls[l]['norm_mix'], w['w_in'], TILE_IN,
                                             f"in_proj_bwd_{l}", None)
        small_pending = (l, _Gather(small_main + [small_in]))
    keys, comm = queue.take()
    last = _exchange_alone(_together(comm, _Gather([small_in])), "exchange_last_grads")
    queue.put(keys, last[:len(keys)])
    small_gathered[0] = small_main_gathered + last[len(keys):]
    return small_gathered, dx, queue.done


def _train_step(p, m, v, x3, target3):
    nl = p['norm_mix'].shape[0]
    bsz, seq, d = x3.shape
    x = x3.reshape(bsz * seq, d)
    target = target3.reshape(bsz * seq, d)
    small_gathered, dx, exchanged = _forward_backward(p, _prepare_weights(p), x, target, seq)

    out = {kind: {} for kind in "gdmv"}
    splits = {'w_in': 4, 'w_conv_out': 1, 'w_sgu_out': 1, 'w_o': 1, 'w_ffn_gate': 1, 'w_ffn_up': 1, 'w_ffn_down': 1, 'conv_w': 1}
    for n in splits:
        if n == 'conv_w':
            pad = ((0, 0), (0, CONV_TAPS_PADDED - CONV_TAPS), (0, 0))
            wl, ml, vl = (jnp.pad(a[n][:, :, 0, :], pad) for a in (p, m, v))
        elif n in ('w_ffn_gate', 'w_ffn_up'):
            wl, ml, vl = (_hidden_major(a[n]) for a in (p, m, v))
        else:
            wl, ml, vl = p[n], m[n], v[n]
        prev = None
        for l in range(nl):
            prev = _adamw_layer(l, wl, ml, vl, exchanged[(l, n)], prev, splits[n], f"adamw_{n}_{l}")
        for kind, arr in zip("gdmv", prev):
            if n == 'conv_w':
                arr = arr[:, 0:CONV_TAPS, None, :]
            elif n in ('w_ffn_gate', 'w_ffn_up'):
                arr = _hidden_major(arr)
            out[kind][n] = arr
    small_names = ['norm_mix', 'gate_bias', 'conv_b', 'conv_ln_g', 'conv_ln_b', 'sgu_ln_g', 'sgu_ln_b', 'w_spatial',
                   'b_spatial', 'norm_ffn', 'norm_final']

    def two_d(a):
        return a[None, :] if a.ndim == 1 else a

    res, loss_row = _adamw_small(small_gathered, {n: two_d(p[n]) for n in small_names},
                                 {n: two_d(m[n]) for n in small_names}, {n: two_d(v[n]) for n in small_names})
    loss = loss_row[0, 0]
    for kind in "gdmv":
        for n in small_names:
            out[kind][n] = res[kind][n].reshape(p[n].shape)
    grad_x = dx.reshape(bsz, seq, d)
    return (loss, grad_x, *[out[kind][n] for kind in "gdmv" for n in WEIGHT_NAMES])


def kernel(x, norm_mix, w_in, gate_bias, conv_w, conv_b, conv_ln_g, conv_ln_b, w_conv_out, sgu_ln_g, sgu_ln_b, w_spatial, b_spatial, w_sgu_out, w_o, norm_ffn, w_ffn_gate, w_ffn_up, w_ffn_down, norm_final, loss_target, m_norm_mix, m_w_in, m_gate_bias, m_conv_w, m_conv_b, m_conv_ln_g, m_conv_ln_b, m_w_conv_out, m_sgu_ln_g, m_sgu_ln_b, m_w_spatial, m_b_spatial, m_w_sgu_out, m_w_o, m_norm_ffn, m_w_ffn_gate, m_w_ffn_up, m_w_ffn_down, m_norm_final, v_norm_mix, v_w_in, v_gate_bias, v_conv_w, v_conv_b, v_conv_ln_g, v_conv_ln_b, v_w_conv_out, v_sgu_ln_g, v_sgu_ln_b, v_w_spatial, v_b_spatial, v_w_sgu_out, v_w_o, v_norm_ffn, v_w_ffn_gate, v_w_ffn_up, v_w_ffn_down, v_norm_final):
    p = dict(zip(WEIGHT_NAMES, (norm_mix, w_in, gate_bias, conv_w, conv_b, conv_ln_g, conv_ln_b, w_conv_out, sgu_ln_g, sgu_ln_b, w_spatial, b_spatial, w_sgu_out, w_o, norm_ffn, w_ffn_gate, w_ffn_up, w_ffn_down, norm_final)))
    m = dict(zip(WEIGHT_NAMES, (m_norm_mix, m_w_in, m_gate_bias, m_conv_w, m_conv_b, m_conv_ln_g, m_conv_ln_b, m_w_conv_out, m_sgu_ln_g, m_sgu_ln_b, m_w_spatial, m_b_spatial, m_w_sgu_out, m_w_o, m_norm_ffn, m_w_ffn_gate, m_w_ffn_up, m_w_ffn_down, m_norm_final)))
    v = dict(zip(WEIGHT_NAMES, (v_norm_mix, v_w_in, v_gate_bias, v_conv_w, v_conv_b, v_conv_ln_g, v_conv_ln_b, v_w_conv_out, v_sgu_ln_g, v_sgu_ln_b, v_w_spatial, v_b_spatial, v_w_sgu_out, v_w_o, v_norm_ffn, v_w_ffn_gate, v_w_ffn_up, v_w_ffn_down, v_norm_final)))
    return _train_step(p, m, v, x, loss_target)
```

```python
import math

import jax
import jax.numpy as jnp
from jax import lax
from jax.experimental import pallas as pl
from jax.experimental.pallas import tpu as pltpu

F32 = jnp.float32
BF16 = jnp.bfloat16
MESH_ID = pl.DeviceIdType.MESH

NDEV = 8
EPS = 1e-6
CONV_TAPS = 31
CONV_TAPS_PADDED = 32
HALO = 16
CONV_ROWS = 128
CONV_ROWS_LOOP = 64
LANES = 128
SUBLANES = 8
CHUNK = 128
GROUPS = 8
FF_PAD = 384
FF_PAIR = 2 * FF_PAD
TN_COLS = 512
VMEM_LIMIT_BYTES = 56 * 1024 * 1024
VMEM_LIMIT_FFN_BWD = 62 * 1024 * 1024

ADAM_LR = 0.001
ADAM_B1 = 0.9
ADAM_B2 = 0.999
ADAM_EPS = 1e-08
ADAM_WD = 0.01
ADAM_STEP = 10

TILE_IN = 512
TILE_IN_FWD = 1024
TILE_MIX = 256
TILE_MIX_FWD = 512
TILE_FFN_FWD = 1024
TILE_FFN_BWD = 1024
TILE_TN = 1024
TILE_TN_SWIGLU = 512

WEIGHT_NAMES = ['norm_mix', 'w_in', 'gate_bias', 'conv_w', 'conv_b', 'conv_ln_g', 'conv_ln_b', 'w_conv_out',
                'sgu_ln_g', 'sgu_ln_b', 'w_spatial', 'b_spatial', 'w_sgu_out', 'w_o', 'norm_ffn', 'w_ffn_gate',
                'w_ffn_up', 'w_ffn_down', 'norm_final']
MIXER_WEIGHTS = ['w_conv_out', 'w_sgu_out', 'w_o', 'conv_w']
FFN_WEIGHTS = ['w_ffn_gate', 'w_ffn_up', 'w_ffn_down']


def _sds(shape, dtype):
    return jax.ShapeDtypeStruct(tuple(shape), dtype)


def _params(*sem, vmem=VMEM_LIMIT_BYTES):
    return pltpu.CompilerParams(dimension_semantics=sem or None, vmem_limit_bytes=vmem)


def _nn(a, b):
    return jnp.dot(a, b, preferred_element_type=F32)


def _nt(a, b):
    return lax.dot_general(a, b, (((1,), (1,)), ((), ())), preferred_element_type=F32)


def _tn(a, b):
    return lax.dot_general(a, b, (((0,), (0,)), ((), ())), preferred_element_type=F32)


def _sig(v):
    return jax.nn.sigmoid(v)


def _fold(v):
    r, c = v.shape
    return jnp.sum(v.reshape(r // SUBLANES, SUBLANES, c), axis=0)


def _tile(tm, n, j=0):
    return pl.BlockSpec((tm, n), lambda i, *_: (i, j))


def _row(n):
    return pl.BlockSpec((1, n), lambda *_: (0, 0))


def _resident(shape):
    nd = len(shape)
    return pl.BlockSpec(tuple(shape), lambda *_: (0,) * nd)


def _weight(w):
    nd = w.ndim
    return pl.BlockSpec(tuple(w.shape), lambda *_: (0,) * nd, pipeline_mode=pl.Buffered(1))


def _peer(rel):
    x, y, c = lax.axis_index("x"), lax.axis_index("y"), lax.axis_index("c")
    return (1 - x if rel & 4 else x, 1 - y if rel & 2 else y, 1 - c if rel & 1 else c)


def _slot(pos):
    return 4 * pos[0] + 2 * pos[1] + pos[2]


class _Exchange:
    def __init__(self, arrays, layers=None):
        self.arrays = list(arrays)
        self.layers = list(layers) if layers is not None else [None] * len(self.arrays)

    def scratch(self):
        n = len(self.arrays)
        return [pltpu.SemaphoreType.DMA((n, NDEV)), pltpu.SemaphoreType.DMA((n, NDEV)), pltpu.SemaphoreType.DMA((n,))]

    def _src(self, ins, j):
        return ins[j] if self.layers[j] is None else ins[j].at[self.layers[j]]

    def _block_shape(self, j):
        a = self.arrays[j]
        return a.shape if self.layers[j] is None else a.shape[1:]


class _Gather(_Exchange):
    chips = (4, 2, 6)

    def out_shape(self):
        return [_sds((NDEV,) + tuple(self._block_shape(j)), a.dtype) for j, a in enumerate(self.arrays)]

    @staticmethod
    def _copy(outs, sems, j, sem, block_rel, to_rel, src=None):
        blk = outs[j].at[_slot(_peer(block_rel))]
        return pltpu.make_async_remote_copy(
            src_ref=blk if src is None else src, dst_ref=blk,
            send_sem=sems[0].at[j, sem], recv_sem=sems[1].at[j, sem],
            device_id=_peer(to_rel), device_id_type=MESH_ID)

    def _local(self, ins, outs, sems, j):
        return pltpu.make_async_copy(self._src(ins, j), outs[j].at[_slot(_peer(0))], sems[2].at[j])

    def start(self, ins, outs, sems):
        for j in range(len(self.arrays)):
            self._local(ins, outs, sems, j).start()
            for rel in (1,) + self.chips:
                self._copy(outs, sems, j, rel, 0, rel, src=self._src(ins, j)).start()

    def forward(self, ins, outs, sems):
        for j in range(len(self.arrays)):
            for rel in self.chips:
                self._copy(outs, sems, j, rel, rel, 0).wait_recv()
                self._copy(outs, sems, j, rel ^ 1, rel, 1).start()

    def finish(self, ins, outs, sems):
        for j in range(len(self.arrays)):
            self._copy(outs, sems, j, 1, 1, 0).wait_recv()
            for rel in self.chips:
                self._copy(outs, sems, j, rel ^ 1, rel ^ 1, 0).wait_recv()
        for j in range(len(self.arrays)):
            for rel in (1,) + self.chips:
                self._copy(outs, sems, j, rel, 0, rel, src=self._src(ins, j)).wait_send()
            for rel in self.chips:
                self._copy(outs, sems, j, rel ^ 1, rel, 1).wait_send()
            self._local(ins, outs, sems, j).wait()


class _Scatter(_Exchange):
    def out_shape(self):
        return [_sds(a.shape, a.dtype) for a in self.arrays]

    @staticmethod
    def _copy(ins, outs, sems, j, rel):
        return pltpu.make_async_remote_copy(
            src_ref=ins[j].at[_slot(_peer(rel))], dst_ref=outs[j].at[_slot(_peer(0))],
            send_sem=sems[0].at[j, rel], recv_sem=sems[1].at[j, rel],
            device_id=_peer(rel), device_id_type=MESH_ID)

    @staticmethod
    def _arrival(outs, sems, j, rel):
        blk = outs[j].at[_slot(_peer(rel))]
        return pltpu.make_async_remote_copy(
            src_ref=blk, dst_ref=blk, send_sem=sems[0].at[j, rel], recv_sem=sems[1].at[j, rel],
            device_id=_peer(rel), device_id_type=MESH_ID)

    @staticmethod
    def _local(ins, outs, sems, j):
        me = _slot(_peer(0))
        return pltpu.make_async_copy(ins[j].at[me], outs[j].at[me], sems[2].at[j])

    def start(self, ins, outs, sems):
        for j in range(len(self.arrays)):
            self._local(ins, outs, sems, j).start()
            for rel in range(1, NDEV):
                self._copy(ins, outs, sems, j, rel).start()

    def forward(self, ins, outs, sems):
        pass

    def finish(self, ins, outs, sems):
        for j in range(len(self.arrays)):
            for rel in range(1, NDEV):
                self._arrival(outs, sems, j, rel).wait_recv()
        for j in range(len(self.arrays)):
            for rel in range(1, NDEV):
                self._copy(ins, outs, sems, j, rel).wait_send()
            self._local(ins, outs, sems, j).wait()


class _Together:
    def __init__(self, parts):
        self.parts = [c for c in parts if c is not None]
        self.arrays = [a for c in self.parts for a in c.arrays]

    def out_shape(self):
        return [s for c in self.parts for s in c.out_shape()]

    def scratch(self):
        return [s for c in self.parts for s in c.scratch()]

    def _each(self, method, ins, outs, sems):
        at = 0
        for q, c in enumerate(self.parts):
            n = len(c.arrays)
            getattr(c, method)(ins[at:at + n], outs[at:at + n], sems[3 * q:3 * q + 3])
            at += n

    def start(self, ins, outs, sems):
        self._each("start", ins, outs, sems)

    def forward(self, ins, outs, sems):
        self._each("forward", ins, outs, sems)

    def finish(self, ins, outs, sems):
        self._each("finish", ins, outs, sems)


def _together(*parts):
    parts = [c for c in parts if c is not None]
    return _Together(parts) if parts else None


def _call(body, *, name, args, in_specs, out_specs, out_shape, grid=(), scratch_shapes=(), semantics=(),
          aliases=None, comm=None, vmem=VMEM_LIMIT_BYTES):
    in_specs, out_specs, out_shape = list(in_specs), list(out_specs), list(out_shape)
    scratch_shapes = list(scratch_shapes)
    if comm is None:
        res = pl.pallas_call(
            body, name=name, grid=grid, in_specs=in_specs, out_specs=out_specs, out_shape=out_shape,
            scratch_shapes=scratch_shapes, input_output_aliases=aliases or {},
            compiler_params=_params(*semantics, vmem=vmem))(*args)
        return list(res), []
    n_in, n_out, n_scr, nc = len(in_specs), len(out_specs), len(scratch_shapes), len(comm.arrays)
    total = math.prod(grid)
    middle = min((total * 5) // 8, total - 1)

    def hosted(*refs):
        ins, cins = refs[:n_in], refs[n_in:n_in + nc]
        o0 = n_in + nc
        outs, couts = refs[o0:o0 + n_out], refs[o0 + n_out:o0 + n_out + nc]
        s0 = o0 + n_out + nc
        scr, sems = refs[s0:s0 + n_scr], refs[s0 + n_scr:]
        if total == 1:
            comm.start(cins, couts, sems)
            body(*ins, *outs, *scr)
            comm.forward(cins, couts, sems)
            comm.finish(cins, couts, sems)
            return
        step = 0
        for axis, size in enumerate(grid):
            step = step * size + pl.program_id(axis)
        pl.when(step == 0)(lambda: comm.start(cins, couts, sems))
        body(*ins, *outs, *scr)
        pl.when(step == middle)(lambda: comm.forward(cins, couts, sems))
        pl.when(step == total - 1)(lambda: comm.finish(cins, couts, sems))

    any_spec = pl.BlockSpec(memory_space=pl.ANY)
    res = pl.pallas_call(
        hosted, name=name, grid=grid,
        in_specs=in_specs + [any_spec] * nc, out_specs=out_specs + [any_spec] * nc,
        out_shape=out_shape + comm.out_shape(), scratch_shapes=scratch_shapes + comm.scratch(),
        input_output_aliases=aliases or {}, compiler_params=_params(*(("arbitrary",) * len(grid)), vmem=vmem),
    )(*args, *comm.arrays)
    return list(res[:n_out]), list(res[n_out:])


def _exchange_alone(comm, name):
    return _call(lambda: None, name=name, args=(), in_specs=(), out_specs=(), out_shape=(), comm=comm)[1]


def _cast_pad(w, rows, cols, name):
    nl, r, c = w.shape

    def body(w_ref, o_ref):
        if (rows, cols) != (r, c):
            o_ref[...] = jnp.zeros(o_ref.shape, BF16)
        o_ref[0, 0:r, 0:c] = w_ref[0].astype(BF16)

    return _call(body, name=name, grid=(nl,), args=(w,),
                 in_specs=[pl.BlockSpec((1, r, c), lambda i: (i, 0, 0))],
                 out_specs=[pl.BlockSpec((1, rows, cols), lambda i: (i, 0, 0))],
                 out_shape=[_sds((nl, rows, cols), BF16)], semantics=("parallel",))[0][0]


def _in_proj(x, gain, w, tm, name, comm):
    t, d = x.shape
    nb, _, bw = w.shape

    def body(x_ref, g_ref, w_ref, h_ref, p_ref):
        xv = x_ref[...]
        r = lax.rsqrt(jnp.mean(xv * xv, axis=-1, keepdims=True) + EPS)
        h = (xv * r * g_ref[...]).astype(BF16)
        h_ref[...] = h
        for k in range(nb):
            p_ref[:, k * bw:(k + 1) * bw] = _nn(h, w_ref[k]).astype(BF16)

    return _call(body, name=name, grid=(t // tm,), args=(x, gain, w),
                 in_specs=[_tile(tm, d), _row(d), _weight(w)],
                 out_specs=[_tile(tm, d), _tile(tm, nb * bw)],
                 out_shape=[_sds((t, d), BF16), _sds((t, nb * bw), BF16)],
                 semantics=("parallel",), comm=comm)


def _halo_specs(tm, t, d, col):
    nh, nhb = tm // HALO, t // HALO
    prev = pl.BlockSpec((HALO, d), lambda i: (jnp.maximum(i * nh - 1, 0), col))
    nxt = pl.BlockSpec((HALO, d), lambda i: (jnp.minimum((i + 1) * nh, nhb - 1), col))
    return prev, nxt


def _row_chunks(tm, rc, unroll, rows):
    n = tm // rc
    if unroll:
        for j in range(n):
            rows(j * rc)
    else:
        def step(j, carry):
            rows(pl.multiple_of(j * rc, rc))
            return carry

        lax.fori_loop(0, n, step, 0)


def _shifted(buf_ref, kb, r0, off, rc):
    return buf_ref[kb, pl.ds(r0 + off, rc), :]


def _dwconv(buf_ref, w_ref, out_ref, tm, flip, rc, unroll):
    nblk = out_ref.shape[1] // LANES

    def rows(r0):
        for kb in range(nblk):
            acc = jnp.zeros((rc, LANES), F32)
            for k in range(CONV_TAPS):
                off = (CONV_TAPS - k) if flip else (1 + k)
                acc = acc + w_ref[kb, k:k + 1, :] * _shifted(buf_ref, kb, r0, off, rc)
            out_ref[pl.ds(r0, rc), kb * LANES:(kb + 1) * LANES] = acc

    _row_chunks(tm, rc, unroll, rows)


def _fill_halo_buffer(buf, prev, body, nxt, first, last, tm):
    prev = jnp.where(first, 0.0, prev)
    nxt = jnp.where(last, 0.0, nxt)
    for kb in range(buf.shape[0]):
        lanes = slice(kb * LANES, (kb + 1) * LANES)
        buf[kb, 0:HALO, :] = prev[:, lanes]
        buf[kb, HALO:HALO + tm, :] = body[:, lanes]
        buf[kb, HALO + tm:HALO + tm + HALO, :] = nxt[:, lanes]


def _fill_glu_buffer(cbuf, av, ag, avp, agp, avn, agn, first, last, tm):
    c0p = avp[...].astype(F32) * _sig(agp[...].astype(F32))
    c0n = avn[...].astype(F32) * _sig(agn[...].astype(F32))
    c0 = av[...].astype(F32) * _sig(ag[...].astype(F32))
    _fill_halo_buffer(cbuf, c0p, c0, c0n, first, last, tm)


def _layernorm_stats(v):
    mu = jnp.mean(v, axis=-1, keepdims=True)
    cen = v - mu
    rstd = lax.rsqrt(jnp.mean(cen * cen, axis=-1, keepdims=True) + EPS)
    return cen * rstd, rstd


def _spatial_mix(ws_ref, vn_ref, bias_ref, mixed_ref, tm):
    for ci in range(tm // CHUNK):
        rs = slice(ci * CHUNK, (ci + 1) * CHUNK)
        for g in range(GROUPS):
            ls = slice(g * LANES, (g + 1) * LANES)
            mixed_ref[rs, ls] = _nn(ws_ref[g], vn_ref[rs, ls]) + bias_ref[:, ls]


def _mixer_fwd(proj, x, w, small, seq, tm, name, comm):
    t, d = x.shape
    hp_v, hn_v = _halo_specs(tm, t, d, 0)
    hp_g, hn_g = _halo_specs(tm, t, d, 1)

    def body(av, ag, u_ref, v_ref, ga_ref, gb_ref, avp, agp, avn, agn, x_ref,
             cw_ref, cb_ref, lg_ref, lb_ref, wco_ref, sg_ref, sb_ref, ws_ref, bias_ref, wso_ref,
             gba_ref, gbb_ref, wo_ref,
             c1_ref, ya_ref, yb_ref, x1_ref, cbuf, c1f, vn_ref, mixed_ref):
        i = pl.program_id(0)
        first = (i * tm) % seq == 0
        last = ((i + 1) * tm) % seq == 0
        _fill_glu_buffer(cbuf, av, ag, avp, agp, avn, agn, first, last, tm)
        _dwconv(cbuf, cw_ref, c1f, tm, flip=False, rc=CONV_ROWS_LOOP, unroll=False)
        c1 = c1f[...] + cb_ref[...]
        c1_ref[...] = c1.astype(BF16)
        c2hat, _ = _layernorm_stats(c1)
        c2 = c2hat * lg_ref[...] + lb_ref[...]
        c3 = (c2 * _sig(c2)).astype(BF16)
        ya = _nn(c3, wco_ref[...].reshape(d, d))
        ya_ref[...] = ya.astype(BF16)
        vhat, _ = _layernorm_stats(v_ref[...].astype(F32))
        vn_ref[...] = (vhat * sg_ref[...] + sb_ref[...]).astype(BF16)
        _spatial_mix(ws_ref, vn_ref, bias_ref, mixed_ref, tm)
        gated = (u_ref[...].astype(F32) * mixed_ref[...]).astype(BF16)
        yb = _nn(gated, wso_ref[...].reshape(d, d))
        yb_ref[...] = yb.astype(BF16)
        sa = _sig(ga_ref[...].astype(F32) + gba_ref[...])
        sb = _sig(gb_ref[...].astype(F32) + gbb_ref[...])
        merged = (sa * ya + sb * yb).astype(BF16)
        x1_ref[...] = x_ref[...] + _nn(merged, wo_ref[...].reshape(d, d))

    cols = [_tile(tm, d, j) for j in range(6)]
    return _call(
        body, name=name, grid=(t // tm,),
        args=(proj,) * 10 + (x, w['conv_w'], small['conv_b'], small['conv_ln_g'], small['conv_ln_b'], w['w_conv_out'],
                             small['sgu_ln_g'], small['sgu_ln_b'], small['ws'], small['bias_full'], w['w_sgu_out'],
                             small['gba'], small['gbb'], w['w_o']),
        in_specs=cols + [hp_v, hp_g, hn_v, hn_g, _tile(tm, d),
                         _weight(w['conv_w']), _row(d), _row(d), _row(d),
                         _weight(w['w_conv_out']), _row(d), _row(d),
                         _resident(small['ws'].shape), _resident(small['bias_full'].shape),
                         _weight(w['w_sgu_out']), _row(d), _row(d), _weight(w['w_o'])],
        out_specs=[_tile(tm, d)] * 4,
        out_shape=[_sds((t, d), BF16)] * 3 + [_sds((t, d), F32)],
        scratch_shapes=[pltpu.VMEM((d // LANES, tm + 2 * HALO, LANES), F32), pltpu.VMEM((tm, d), F32),
                        pltpu.VMEM((tm, d), BF16), pltpu.VMEM((tm, d), F32)],
        semantics=("parallel",), comm=comm)


def _pair_gate_up(wg, wu, name):
    n, c, d = wg.shape

    def body(wg_ref, wu_ref, o_ref):
        o_ref[0:c, :] = wg_ref[0]
        o_ref[c:2 * c, :] = wg_ref[1]
        o_ref[2 * c:3 * c, :] = wu_ref[0]
        o_ref[3 * c:4 * c, :] = wu_ref[1]

    pair = pl.BlockSpec((2, c, d), lambda k: (k, 0, 0))
    return _call(body, name=name, grid=(n // 2,), args=(wg, wu), in_specs=[pair, pair],
                 out_specs=[pl.BlockSpec((None, 4 * c, d), lambda k: (k, 0, 0))],
                 out_shape=[_sds((n // 2, 4 * c, d), wg.dtype)], semantics=("parallel",))[0][0]


def _pair_specs(w, tm):
    d = w['w_gate_up'].shape[2]
    up = pl.BlockSpec((None, 2 * FF_PAIR, d), lambda i, k: (k, 0, 0))
    down = pl.BlockSpec((None, FF_PAIR, d), lambda i, k: (k, 0, 0))
    return [up, down]


def _ffn_fwd(x1, gain, w, tm, name, comm, loss=None):
    t, d = x1.shape
    tm = min(tm, t)
    nsteps = t // tm
    npair = NDEV // 2
    hid = NDEV * FF_PAD
    wd_pairs = w['w_ffn_down'].reshape(npair, FF_PAIR, d)

    def body(x_ref, g_ref, wgu_ref, wd_ref, *refs):
        if loss is None:
            h_ref, gg_ref, uu_ref, x2_ref, hb_ref, acc_ref = refs
        else:
            gf_ref, t_ref, h_ref, gg_ref, uu_ref, x2_ref, small_ref, hb_ref, acc_ref, lacc_ref = refs
        i, k = pl.program_id(0), pl.program_id(1)

        @pl.when(k == 0)
        def _():
            xv = x_ref[...]
            r = lax.rsqrt(jnp.mean(xv * xv, axis=-1, keepdims=True) + EPS)
            h = (xv * r * g_ref[...]).astype(BF16)
            hb_ref[...] = h
            h_ref[...] = h
            acc_ref[...] = xv

        gu = _nt(hb_ref[...], wgu_ref[...])
        gk = gu[:, 0:FF_PAIR]
        uk = gu[:, FF_PAIR:2 * FF_PAIR]
        gg_ref[...] = gk.astype(BF16)
        uu_ref[...] = uk.astype(BF16)
        ak = (gk * _sig(gk) * uk).astype(BF16)
        acc_ref[...] += _nn(ak, wd_ref[...])

        if loss is None:
            @pl.when(k == npair - 1)
            def _():
                x2_ref[...] = acc_ref[...]
            return

        _init_small((i == 0) & (k == 0), small_ref, lacc_ref)

        @pl.when(k == npair - 1)
        def _():
            xv = acc_ref[...]
            r = lax.rsqrt(jnp.mean(xv * xv, axis=-1, keepdims=True) + EPS)
            xhat = xv * r
            diff = xhat * gf_ref[...] - t_ref[...]
            dy = diff * (1.0 / d)
            lacc_ref[0] += _fold(dy * xhat)
            lacc_ref[1] += _fold(diff * diff)
            dxhat = dy * gf_ref[...]
            x2_ref[...] = r * (dxhat - xhat * jnp.mean(dxhat * xhat, axis=-1, keepdims=True))

        @pl.when((i == nsteps - 1) & (k == npair - 1))
        def _():
            small_ref[0:1, :] = jnp.sum(lacc_ref[0], axis=0, keepdims=True)
            small_ref[1:2, :] = jnp.full((1, d), jnp.sum(lacc_ref[1]) * (0.5 / d), F32)

    pair_cols = pl.BlockSpec((tm, FF_PAIR), lambda i, k: (i, k))
    args = (x1, gain, w['w_gate_up'], wd_pairs)
    in_specs = [_tile(tm, d), _row(d)] + _pair_specs(w, tm)
    out_specs = [_tile(tm, d), pair_cols, pair_cols, _tile(tm, d)]
    out_shape = [_sds((t, d), BF16), _sds((t, hid), BF16), _sds((t, hid), BF16), _sds((t, d), F32)]
    scratch = [pltpu.VMEM((tm, d), BF16), pltpu.VMEM((tm, d), F32)]
    if loss is not None:
        args += tuple(loss)
        in_specs += [_row(d), pl.BlockSpec((tm, d), lambda i, k: (i, 0), pipeline_mode=pl.Buffered(1))]
        out_specs += [_resident((SUBLANES, d))]
        out_shape += [_sds((SUBLANES, d), F32)]
        scratch += [pltpu.VMEM((2, SUBLANES, d), F32)]
    return _call(
        body, name=name, grid=(nsteps, npair), args=args, in_specs=in_specs, out_specs=out_specs,
        out_shape=out_shape, scratch_shapes=scratch,
        semantics=("parallel" if loss is None else "arbitrary", "arbitrary"), comm=comm)


def _init_small(first, small_ref, acc_ref):
    @pl.when(first)
    def _():
        small_ref[...] = jnp.zeros(small_ref.shape, F32)
        acc_ref[...] = jnp.zeros(acc_ref.shape, F32)


def _finish_small(last, small_ref, acc_ref, nq):
    @pl.when(last)
    def _():
        for q in range(nq):
            small_ref[q:q + 1, :] = jnp.sum(acc_ref[q], axis=0, keepdims=True)


def _ffn_bwd(dx2, x1, gain, gg, uu, w, tm, name, comm):
    t, d = x1.shape
    tm = min(tm, t)
    npair = NDEV // 2
    hid = NDEV * FF_PAD
    nsteps = t // tm
    wd_pairs = w['w_ffn_down'].reshape(npair, FF_PAIR, d)

    def body(dx_ref, x_ref, g_ref, gg_ref, uu_ref, wgu_ref, wd_ref,
             dg_ref, du_ref, dx1_ref, small_ref, acc_ref, dxb_ref, dh_ref, dgu_ref):
        i, k = pl.program_id(0), pl.program_id(1)
        _init_small((i == 0) & (k == 0), small_ref, acc_ref)

        @pl.when(k == 0)
        def _():
            dxb_ref[...] = dx_ref[...].astype(BF16)
            dh_ref[...] = jnp.zeros(dh_ref.shape, F32)

        gk = gg_ref[...].astype(F32)
        uk = uu_ref[...].astype(F32)
        sg = _sig(gk)
        silu = gk * sg
        da = _nt(dxb_ref[...], wd_ref[...])
        dgk = (da * uk * (sg * (1.0 + gk * (1.0 - sg)))).astype(BF16)
        duk = (da * silu).astype(BF16)
        dg_ref[...] = dgk
        du_ref[...] = duk
        dgu_ref[:, 0:FF_PAIR] = dgk
        dgu_ref[:, FF_PAIR:2 * FF_PAIR] = duk
        dh_ref[...] += _nn(dgu_ref[...], wgu_ref[...])

        @pl.when(k == npair - 1)
        def _():
            xv = x_ref[...]
            dh = dh_ref[...]
            r = lax.rsqrt(jnp.mean(xv * xv, axis=-1, keepdims=True) + EPS)
            xhat = xv * r
            acc_ref[0] += _fold(dh * xhat)
            dxhat = dh * g_ref[...]
            dx1_ref[...] = dx_ref[...] + r * (dxhat - xhat * jnp.mean(dxhat * xhat, axis=-1, keepdims=True))

        _finish_small((i == nsteps - 1) & (k == npair - 1), small_ref, acc_ref, 1)

    pair_cols = pl.BlockSpec((tm, FF_PAIR), lambda i, k: (i, k))
    once = pl.BlockSpec((tm, d), lambda i, k: (i, 0), pipeline_mode=pl.Buffered(1))
    return _call(
        body, name=name, grid=(nsteps, npair),
        args=(dx2, x1, gain, gg, uu, w['w_gate_up'], wd_pairs),
        in_specs=[_tile(tm, d), once, _row(d), pair_cols, pair_cols] + _pair_specs(w, tm),
        out_specs=[pair_cols] * 2 + [_tile(tm, d), _resident((SUBLANES, d))],
        out_shape=[_sds((t, hid), BF16)] * 2 + [_sds((t, d), F32), _sds((SUBLANES, d), F32)],
        scratch_shapes=[pltpu.VMEM((1, SUBLANES, d), F32), pltpu.VMEM((tm, d), BF16), pltpu.VMEM((tm, d), F32),
                        pltpu.VMEM((tm, 2 * FF_PAIR), BF16)],
        semantics=("arbitrary", "arbitrary"), comm=comm, vmem=VMEM_LIMIT_FFN_BWD)


def _matmul_tn(a, b, a_blk, b_blk, stack, shard, tm, name, comm, swiglu_with=None):
    t, ma = a.shape
    tm = min(tm, t)
    nb_ = b.shape[1]
    na, nb = ma // a_blk, nb_ // b_blk
    nsteps = t // tm
    cw = min(TN_COLS, b_blk)
    if stack == 'b':
        per = b_blk // shard
        out_shape, out_spec = (nb_ // shard, ma, shard), pl.BlockSpec((per, a_blk, shard), lambda i, j, k: (j, 0, 0))
    elif stack == 'a':
        per = a_blk // shard
        out_shape, out_spec = (ma // shard, shard, nb_), pl.BlockSpec((per, shard, b_blk), lambda i, j, k: (i, 0, 0))
    else:
        out_shape, out_spec = (ma, nb_), pl.BlockSpec((a_blk, b_blk), lambda i, j, k: (i, j))

    extra = [] if swiglu_with is None else [swiglu_with]

    def body(a_ref, *refs):
        b_ref, o_ref, acc_ref = refs[-3:]
        k = pl.program_id(2)

        @pl.when(k == 0)
        def _():
            acc_ref[...] = jnp.zeros(acc_ref.shape, F32)

        if swiglu_with is None:
            av = a_ref[...].astype(BF16)
        else:
            gv = a_ref[...].astype(F32)
            av = (gv * _sig(gv) * refs[0][...].astype(F32)).astype(BF16)
        for c in range(0, b_blk, cw):
            acc_ref[:, c:c + cw] += _tn(av, b_ref[:, c:c + cw].astype(BF16))

        @pl.when(k == nsteps - 1)
        def _():
            if stack == 'b':
                for s in range(per):
                    o_ref[s] = acc_ref[:, s * shard:(s + 1) * shard].astype(BF16)
            elif stack == 'a':
                for s in range(per):
                    o_ref[s] = acc_ref[s * shard:(s + 1) * shard, :].astype(BF16)
            else:
                o_ref[...] = acc_ref[...].astype(BF16)

    a_spec = pl.BlockSpec((tm, a_blk), lambda i, j, k: (k, i))
    res, got = _call(
        body, name=name, grid=(na, nb, nsteps), args=(a, *extra, b),
        in_specs=[a_spec] * (1 + len(extra)) + [pl.BlockSpec((tm, b_blk), lambda i, j, k: (k, j))],
        out_specs=[out_spec], out_shape=[_sds(out_shape, BF16)],
        scratch_shapes=[pltpu.VMEM((a_blk, b_blk), F32)],
        semantics=("parallel", "parallel", "arbitrary"), comm=comm)
    return res[0], got


def _mixer_bwd(dx1, proj, c1, ya, yb, w, small, tm, name, comm):
    t, d = dx1.shape
    nsteps = t // tm
    nq = 6

    def body(dx_ref, u_ref, v_ref, ga_ref, gb_ref, c1_ref, ya_ref, yb_ref,
             lg_ref, lb_ref, wco_ref, sg_ref, sb_ref, ws_ref, wst_ref, bias_ref, wso_ref, gba_ref, gbb_ref, wo_ref,
             sel_ref,
             dp_ref, dc1_ref, mg_ref, c3_ref, gt_ref, dya_ref, dyb_ref, small_ref, dws_ref, dbs_ref,
             acc_ref, vn_ref, mixed_ref, dmix_ref, dvn_ref, dbias_ref):
        i = pl.program_id(0)
        _init_small(i == 0, small_ref, acc_ref)

        @pl.when(i == 0)
        def _():
            dws_ref[...] = jnp.zeros(dws_ref.shape, F32)
            dbs_ref[...] = jnp.zeros(dbs_ref.shape, F32)
            dbias_ref[...] = jnp.zeros(dbias_ref.shape, F32)

        dmerged = _nt(dx_ref[...].astype(BF16), wo_ref[...].reshape(d, d))
        ya = ya_ref[...].astype(F32)
        yb = yb_ref[...].astype(F32)
        sa = _sig(ga_ref[...].astype(F32) + gba_ref[...])
        sb = _sig(gb_ref[...].astype(F32) + gbb_ref[...])
        mg_ref[...] = (sa * ya + sb * yb).astype(BF16)
        dya = (dmerged * sa).astype(BF16)
        dyb = (dmerged * sb).astype(BF16)
        dya_ref[...] = dya
        dyb_ref[...] = dyb
        dga = dmerged * ya * (sa * (1.0 - sa))
        dgb = dmerged * yb * (sb * (1.0 - sb))
        acc_ref[0] += _fold(dga)
        acc_ref[1] += _fold(dgb)
        dp_ref[:, 0:2 * d] = jnp.zeros((tm, 2 * d), BF16)
        dp_ref[:, 4 * d:5 * d] = dga.astype(BF16)
        dp_ref[:, 5 * d:6 * d] = dgb.astype(BF16)
        c2hat, rstd = _layernorm_stats(c1_ref[...].astype(F32))
        c2 = c2hat * lg_ref[...] + lb_ref[...]
        s2 = _sig(c2)
        c3_ref[...] = (c2 * s2).astype(BF16)
        dc3 = _nt(dya, wco_ref[...].reshape(d, d))
        dc2 = dc3 * (s2 * (1.0 + c2 * (1.0 - s2)))
        acc_ref[2] += _fold(dc2 * c2hat)
        acc_ref[3] += _fold(dc2)
        dc2hat = dc2 * lg_ref[...]
        dc1_ref[...] = (rstd * (dc2hat - jnp.mean(dc2hat, axis=-1, keepdims=True)
                                - c2hat * jnp.mean(dc2hat * c2hat, axis=-1, keepdims=True))).astype(BF16)
        vhat, rstd_v = _layernorm_stats(v_ref[...].astype(F32))
        vn_ref[...] = (vhat * sg_ref[...] + sb_ref[...]).astype(BF16)
        _spatial_mix(ws_ref, vn_ref, bias_ref, mixed_ref, tm)
        u = u_ref[...].astype(F32)
        mixed = mixed_ref[...]
        gt_ref[...] = (u * mixed).astype(BF16)
        dgated = _nt(dyb, wso_ref[...].reshape(d, d))
        dp_ref[:, 2 * d:3 * d] = (dgated * mixed).astype(BF16)
        dmix_ref[...] = dgated * u
        for ci in range(tm // CHUNK):
            rs = slice(ci * CHUNK, (ci + 1) * CHUNK)
            dbias_ref[...] += dmix_ref[rs, :]
            for g in range(GROUPS):
                ls = slice(g * LANES, (g + 1) * LANES)
                dm = dmix_ref[rs, ls].astype(BF16)
                dws_ref[g] += _nt(dm, vn_ref[rs, ls])
                dvn_ref[rs, ls] = _nn(wst_ref[g], dm)
        dvn = dvn_ref[...]
        acc_ref[4] += _fold(dvn * vhat)
        acc_ref[5] += _fold(dvn)
        dvhat = dvn * sg_ref[...]
        dp_ref[:, 3 * d:4 * d] = (rstd_v * (dvhat - jnp.mean(dvhat, axis=-1, keepdims=True)
                                           - vhat * jnp.mean(dvhat * vhat, axis=-1, keepdims=True))).astype(BF16)
        _finish_small(i == nsteps - 1, small_ref, acc_ref, nq)

        @pl.when(i == nsteps - 1)
        def _():
            db = dbias_ref[...]
            hi = db.astype(BF16)
            lo = (db - hi.astype(F32)).astype(BF16)
            dbs_ref[...] = _nt(sel_ref[...], hi) + _nt(sel_ref[...], lo)

    cols = [_tile(tm, d, j) for j in (2, 3, 4, 5)]
    return _call(
        body, name=name, grid=(nsteps,),
        args=(dx1, proj, proj, proj, proj, c1, ya, yb,
              small['conv_ln_g'], small['conv_ln_b'], w['w_conv_out'], small['sgu_ln_g'], small['sgu_ln_b'],
              small['ws'], small['wst'], small['bias_full'], w['w_sgu_out'], small['gba'], small['gbb'], w['w_o'],
              small['group_sel']),
        in_specs=[_tile(tm, d)] + cols + [_tile(tm, d)] * 3 + [
            _row(d), _row(d), _weight(w['w_conv_out']),
            _row(d), _row(d), _resident(small['ws'].shape), _resident(small['wst'].shape),
            _resident(small['bias_full'].shape), _weight(w['w_sgu_out']), _row(d), _row(d),
            _weight(w['w_o']), _resident(small['group_sel'].shape)],
        out_specs=[_tile(tm, 6 * d)] + [_tile(tm, d)] * 6 + [
            _resident((SUBLANES, d)), _resident((GROUPS, CHUNK, CHUNK)), _resident((GROUPS, CHUNK))],
        out_shape=[_sds((t, 6 * d), BF16)] + [_sds((t, d), BF16)] * 6 + [
            _sds((SUBLANES, d), F32), _sds((GROUPS, CHUNK, CHUNK), F32), _sds((GROUPS, CHUNK), F32)],
        scratch_shapes=[pltpu.VMEM((nq, SUBLANES, d), F32), pltpu.VMEM((tm, d), BF16), pltpu.VMEM((tm, d), F32),
                        pltpu.VMEM((tm, d), F32), pltpu.VMEM((tm, d), F32), pltpu.VMEM((CHUNK, d), F32)],
        semantics=("arbitrary",), comm=comm)


def _conv_bwd(dproj, dc1, proj, w, pairs, seq, tm, name, comm):
    t, d = dc1.shape
    nsteps = t // tm
    nblk = d // LANES
    npairs = len(pairs)
    hp_v, hn_v = _halo_specs(tm, t, d, 0)
    hp_g, hn_g = _halo_specs(tm, t, d, 1)
    hp_d, hn_d = _halo_specs(tm, t, d, 0)

    def body(*refs):
        (dp_in, dc_ref, dcp, dcn, av, ag, avp, agp, avn, agn, cw_ref), refs = refs[:11], refs[11:]
        ab_refs, refs = refs[:2 * npairs], refs[2 * npairs:]
        (dp_ref, dcw_ref, small_ref), refs = refs[:3], refs[3:]
        grad_refs, refs = refs[:npairs], refs[npairs:]
        acc_ref, cbuf, dbuf, dc0f, accw, gacc = refs
        del dp_in
        i = pl.program_id(0)
        _init_small(i == 0, small_ref, acc_ref)

        @pl.when(i == 0)
        def _():
            accw[...] = jnp.zeros(accw.shape, F32)
            dcw_ref[...] = jnp.zeros(dcw_ref.shape, F32)
            gacc[...] = jnp.zeros(gacc.shape, F32)

        first = (i * tm) % seq == 0
        last = ((i + 1) * tm) % seq == 0
        _fill_glu_buffer(cbuf, av, ag, avp, agp, avn, agn, first, last, tm)
        dc1v = dc_ref[...].astype(F32)
        _fill_halo_buffer(dbuf, dcp[...].astype(F32), dc1v, dcn[...].astype(F32), first, last, tm)
        acc_ref[0] += _fold(dc1v)
        _dwconv(dbuf, cw_ref, dc0f, tm, flip=True, rc=CONV_ROWS, unroll=True)

        def rows(r0):
            for kb in range(nblk):
                dv = dbuf[kb, r0 + HALO:r0 + HALO + CONV_ROWS, :]
                for k in range(CONV_TAPS):
                    accw[kb, k] += _fold(dv * _shifted(cbuf, kb, r0, 1 + k, CONV_ROWS))

        _row_chunks(tm, CONV_ROWS, True, rows)
        for q in range(npairs):
            aq = ab_refs[2 * q][...].astype(BF16)
            for c in range(0, d, TN_COLS):
                gacc[q, :, c:c + TN_COLS] += _tn(aq, ab_refs[2 * q + 1][:, c:c + TN_COLS].astype(BF16))
        sg = _sig(ag[...].astype(F32))
        avv = av[...].astype(F32)
        dc0 = dc0f[...]
        dp_ref[:, 0:d] = (dc0 * sg).astype(BF16)
        dp_ref[:, d:2 * d] = (dc0 * avv * (sg * (1.0 - sg))).astype(BF16)
        _finish_small(i == nsteps - 1, small_ref, acc_ref, 1)

        @pl.when(i == nsteps - 1)
        def _():
            for kb in range(nblk):
                dcw_ref[kb] = jnp.sum(accw[kb], axis=1)
            for q in range(npairs):
                grad_refs[q][...] = gacc[q].astype(BF16)

    return _call(
        body, name=name, grid=(nsteps,),
        args=(dproj, dc1, dc1, dc1, proj, proj, proj, proj, proj, proj, w['conv_w']) + tuple(a for ab in pairs for a in ab),
        in_specs=[pl.BlockSpec(memory_space=pl.ANY), _tile(tm, d), hp_d, hn_d, _tile(tm, d, 0), _tile(tm, d, 1),
                  hp_v, hp_g, hn_v, hn_g, _weight(w['conv_w'])] + [_tile(tm, d)] * (2 * npairs),
        out_specs=[_tile(tm, 2 * d), _resident((nblk, CONV_TAPS_PADDED, LANES)), _resident((SUBLANES, d))]
        + [_resident((d, d))] * npairs,
        out_shape=[_sds(dproj.shape, BF16), _sds((nblk, CONV_TAPS_PADDED, LANES), F32), _sds((SUBLANES, d), F32)]
        + [_sds((d, d), BF16)] * npairs,
        scratch_shapes=[pltpu.VMEM((1, SUBLANES, d), F32), pltpu.VMEM((nblk, tm + 2 * HALO, LANES), F32),
                        pltpu.VMEM((nblk, tm + 2 * HALO, LANES), F32), pltpu.VMEM((tm, d), F32),
                        pltpu.VMEM((nblk, CONV_TAPS_PADDED, SUBLANES, LANES), F32),
                        pltpu.VMEM((npairs, d, d), F32)],
        aliases={0: 0}, semantics=("arbitrary",), comm=comm)


def _in_proj_bwd(dproj, x, dx1, gain, w, tm, name, comm):
    t, d = x.shape
    nb, _, bw = w.shape
    nsteps = t // tm

    def body(dp_ref, x_ref, dx1_ref, g_ref, w_ref, dx_ref, small_ref, acc_ref):
        i = pl.program_id(0)
        _init_small(i == 0, small_ref, acc_ref)
        dh = jnp.zeros((tm, d), F32)
        for k in range(nb):
            dh = dh + _nt(dp_ref[:, k * bw:(k + 1) * bw], w_ref[k])
        xv = x_ref[...]
        r = lax.rsqrt(jnp.mean(xv * xv, axis=-1, keepdims=True) + EPS)
        xhat = xv * r
        acc_ref[0] += _fold(dh * xhat)
        dxhat = dh * g_ref[...]
        dx_ref[...] = dx1_ref[...] + r * (dxhat - xhat * jnp.mean(dxhat * xhat, axis=-1, keepdims=True))
        _finish_small(i == nsteps - 1, small_ref, acc_ref, 1)

    return _call(body, name=name, grid=(nsteps,), args=(dproj, x, dx1, gain, w),
                 in_specs=[_tile(tm, nb * bw), _tile(tm, d), _tile(tm, d), _row(d), _weight(w)],
                 out_specs=[_tile(tm, d), _resident((SUBLANES, d))],
                 out_shape=[_sds((t, d), F32), _sds((SUBLANES, d), F32)],
                 scratch_shapes=[pltpu.VMEM((1, SUBLANES, d), F32)], semantics=("arbitrary",), comm=comm)


def _adam(wv, g, mv, vv):
    m = ADAM_B1 * mv + (1.0 - ADAM_B1) * g
    v = ADAM_B2 * vv + (1.0 - ADAM_B2) * jnp.square(g)
    m_hat = m / (1.0 - ADAM_B1 ** ADAM_STEP)
    v_hat = v / (1.0 - ADAM_B2 ** ADAM_STEP)
    delta = -ADAM_LR * (m_hat / (jnp.sqrt(v_hat) + ADAM_EPS) + ADAM_WD * wv)
    return delta, m, v


def _adamw_layer(layer, w, m, v, parts, prev, nsplit, name):
    nl, r, c = w.shape
    npart, pr, pc = parts.shape
    rt, prt = r // nsplit, pr // nsplit

    def body(w_ref, m_ref, v_ref, p_ref, *rest):
        g_ref, d_ref, nm_ref, nv_ref = rest[-4:]
        g = p_ref[0, 0:rt, 0:c].astype(F32)
        for s in range(1, npart):
            g = g + p_ref[s, 0:rt, 0:c].astype(F32)
        delta, mn, vn = _adam(w_ref[0], g, m_ref[0], v_ref[0])
        g_ref[0] = g
        d_ref[0] = delta
        nm_ref[0] = mn
        nv_ref[0] = vn

    wspec = pl.BlockSpec((1, rt, c), lambda i: (layer, i, 0))
    pspec = pl.BlockSpec((npart, prt, pc), lambda i: (0, i, 0))
    in_specs = [wspec, wspec, wspec, pspec]
    args = [w, m, v, parts]
    aliases = {}
    if prev is not None:
        in_specs += [pl.BlockSpec(memory_space=pl.ANY)] * 4
        args += list(prev)
        aliases = {4 + q: q for q in range(4)}
    return _call(body, name=name, grid=(nsplit,), args=args, in_specs=in_specs, out_specs=[wspec] * 4,
                 out_shape=[_sds(w.shape, F32)] * 4, aliases=aliases, semantics=("parallel",))[0]


VEC_ROWS = {'gate_bias': (0, 1), 'conv_ln_g': 2, 'conv_ln_b': 3, 'sgu_ln_g': 4, 'sgu_ln_b': 5,
            'conv_b': SUBLANES, 'norm_ffn': 2 * SUBLANES}
FINAL_ROW = 3 * SUBLANES
LOSS_ROW = 3 * SUBLANES + 1


def _adamw_small(gathered, params, moments_m, moments_v):
    names = list(params)
    nper = len(names)
    nl = len(gathered)

    def body(*refs):
        g_refs = [refs[4 * l:4 * l + 4] for l in range(nl)]
        rest = refs[4 * nl:]
        w_refs = dict(zip(names, rest[0:nper]))
        m_refs = dict(zip(names, rest[nper:2 * nper]))
        v_refs = dict(zip(names, rest[2 * nper:3 * nper]))
        outs = rest[3 * nper:]
        loss_ref = outs[4 * nper]
        o = {kind: dict(zip(names, outs[q * nper:(q + 1) * nper])) for q, kind in enumerate("gdmv")}

        def put(nm, idx, g):
            delta, mn, vn = _adam(w_refs[nm][idx], g, m_refs[nm][idx], v_refs[nm][idx])
            o["g"][nm][idx] = g
            o["d"][nm][idx] = delta
            o["m"][nm][idx] = mn
            o["v"][nm][idx] = vn

        def total(ref, *idx):
            g = ref[(0, *idx)]
            for s in range(1, NDEV):
                g = g + ref[(s, *idx)]
            return g

        for l, (vec_ref, dws_ref, dbs_ref, vin_ref) in enumerate(g_refs):
            dd = vec_ref.shape[2]
            put('w_spatial', (l,), total(dws_ref))
            put('b_spatial', (l,), total(dbs_ref))
            put('norm_mix', (slice(l, l + 1), slice(None)), total(vin_ref, slice(0, 1)))
            for nm, rr in VEC_ROWS.items():
                if nm == 'gate_bias':
                    put(nm, (slice(l, l + 1), slice(0, dd)), total(vec_ref, slice(rr[0], rr[0] + 1)))
                    put(nm, (slice(l, l + 1), slice(dd, 2 * dd)), total(vec_ref, slice(rr[1], rr[1] + 1)))
                else:
                    put(nm, (slice(l, l + 1), slice(None)), total(vec_ref, slice(rr, rr + 1)))
        last = g_refs[nl - 1][0]
        put('norm_final', (slice(0, 1), slice(None)), total(last, slice(FINAL_ROW, FINAL_ROW + 1)))
        loss_ref[...] = total(last, slice(LOSS_ROW, LOSS_ROW + 1))

    ins = [a for g in gathered for a in g] + [params[n] for n in names] + [moments_m[n] for n in names] + [moments_v[n] for n in names]
    out_shape = [_sds(params[n].shape, F32) for n in names] * 4 + [_sds((1, gathered[0][0].shape[2]), F32)]
    res = pl.pallas_call(body, name="adamw_small", out_shape=out_shape, compiler_params=_params())(*ins)
    return {kind: dict(zip(names, res[q * nper:(q + 1) * nper])) for q, kind in enumerate("gdmv")}, res[4 * nper]


def _hidden_major(a):
    return jnp.swapaxes(a, 1, 2)


def _prepare_weights(p):
    d = p['w_in'].shape[1]
    return {
        'w_in': _cast_pad(p['w_in'], d, p['w_in'].shape[2], "cast_w_in"),
        'w_conv_out': _cast_pad(p['w_conv_out'], p['w_conv_out'].shape[1], d, "cast_w_conv_out"),
        'w_sgu_out': _cast_pad(p['w_sgu_out'], p['w_sgu_out'].shape[1], d, "cast_w_sgu_out"),
        'w_o': _cast_pad(p['w_o'], p['w_o'].shape[1], d, "cast_w_o"),
        'w_ffn_gate': _cast_pad(_hidden_major(p['w_ffn_gate']), FF_PAD, d, "cast_w_ffn_gate"),
        'w_ffn_up': _cast_pad(_hidden_major(p['w_ffn_up']), FF_PAD, d, "cast_w_ffn_up"),
        'w_ffn_down': _cast_pad(p['w_ffn_down'], FF_PAD, d, "cast_w_ffn_down"),
        'conv_w': jnp.pad(p['conv_w'][:, :, 0, :], ((0, 0), (0, CONV_TAPS_PADDED - CONV_TAPS), (0, 0))),
    }


def _gather_of(shards, names, layer):
    return _Gather([shards[n] for n in names], [layer] * len(names))


def _layer_small(p, layer):
    d = p['norm_mix'].shape[1]
    ws = p['w_spatial'][layer]
    rows = {n: p[n][layer:layer + 1] for n in ('norm_mix', 'norm_ffn', 'conv_b', 'conv_ln_g', 'conv_ln_b',
                                               'sgu_ln_g', 'sgu_ln_b')}
    return {
        **rows,
        'ws': ws.astype(BF16), 'wst': jnp.swapaxes(ws, 1, 2).astype(BF16),
        'bias_full': jnp.repeat(p['b_spatial'][layer].T, LANES, axis=1),
        'gba': p['gate_bias'][layer:layer + 1, 0:d], 'gbb': p['gate_bias'][layer:layer + 1, d:2 * d],
        'group_sel': (jnp.arange(d)[None, :] // LANES == jnp.arange(GROUPS)[:, None]).astype(BF16),
    }


class _GradQueue:
    def __init__(self):
        self.pending = []
        self.done = {}

    def push(self, key, array):
        self.pending.append((key, array))

    def take(self):
        keys = [k for k, _ in self.pending]
        comm = _Scatter([a for _, a in self.pending]) if self.pending else None
        self.pending = []
        return keys, comm

    def put(self, keys, arrays):
        self.done.update(zip(keys, arrays))


def _forward_backward(p, shards, x, target, seq):
    nl = p['norm_mix'].shape[0]
    d = x.shape[1]
    smalls = [_layer_small(p, l) for l in range(nl)]
    w_in = _exchange_alone(_gather_of(shards, ['w_in'], 0), "gather_w_in_0")[0]
    saved = []
    for l in range(nl):
        (h, proj), got = _in_proj(x, smalls[l]['norm_mix'], w_in, min(TILE_IN_FWD, x.shape[0]), f"in_proj_{l}",
                                  _gather_of(shards, MIXER_WEIGHTS, l))
        w = dict(zip(MIXER_WEIGHTS, got), w_in=w_in)
        (c1, ya, yb, x1), got = _mixer_fwd(proj, x, w, smalls[l], seq, min(TILE_MIX_FWD, seq), f"mixer_fwd_{l}",
                                           _gather_of(shards, FFN_WEIGHTS, l))
        w.update(zip(FFN_WEIGHTS, got))
        w['w_gate_up'] = _pair_gate_up(w['w_ffn_gate'], w['w_ffn_up'], f"pair_gate_up_{l}")
        saved.append(dict(x=x, h=h, proj=proj, c1=c1, ya=ya, yb=yb, x1=x1, w=w))
        if l + 1 < nl:
            (h2, gg, uu, x), got = _ffn_fwd(x1, smalls[l]['norm_ffn'], w, TILE_FFN_FWD, f"ffn_fwd_{l}",
                                            _gather_of(shards, ['w_in'], l + 1))
            w_in = got[0]
        else:
            (h2, gg, uu, dx, small_loss), _ = _ffn_fwd(x1, smalls[l]['norm_ffn'], w, TILE_FFN_FWD, f"ffn_fwd_{l}", None,
                                                       loss=(p['norm_final'][None, :], target))
        saved[l].update(h2=h2, gg=gg, uu=uu)
    queue = _GradQueue()
    small_gathered = [None] * nl
    small_pending = None
    rows = d // NDEV
    hid = NDEV * FF_PAD
    for l in reversed(range(nl)):
        s = saved[l]
        w = s['w']

        def hosted(fn, *args, extra=None):
            keys, comm = queue.take()
            res, got = fn(*args, _together(comm, extra))
            queue.put(keys, got[:len(keys)])
            return res, got[len(keys):]

        def tn(key, a, b, a_blk, b_blk, stack, shard, reshape=None, host=False, extra=None, swiglu_with=None):
            keys, comm = queue.take() if host else ([], None)
            tile = TILE_TN if swiglu_with is None else TILE_TN_SWIGLU
            g, got = _matmul_tn(a, b, a_blk, b_blk, stack, shard, tile, f"dw_{key}_{l}", _together(comm, extra),
                                swiglu_with)
            queue.put(keys, got[:len(keys)])
            queue.push((l, key), g if reshape is None else g.reshape(reshape))
            return got[len(keys):]

        (dgg, duu, dx1, small_ffn), _ = hosted(_ffn_bwd, dx, s['x1'], smalls[l]['norm_ffn'], s['gg'], s['uu'], w,
                                               TILE_FFN_BWD, f"ffn_bwd_{l}")
        tn('w_ffn_gate', dgg, s['h2'], hid, d, 'a', FF_PAD)
        tn('w_ffn_up', duu, s['h2'], hid, d, 'a', FF_PAD)
        (dproj, dc1, merged, c3, gated, dya, dyb, small_mix, dws, dbs), _ = hosted(
            _mixer_bwd, dx1, s['proj'], s['c1'], s['ya'], s['yb'], w, smalls[l], TILE_MIX, f"mixer_bwd_{l}")
        tn('w_ffn_down', s['gg'], dx, hid, d, 'a', FF_PAD, swiglu_with=s['uu'])
        extra = small_pending[1] if small_pending else None
        (dproj, g_conv, small_conv, g_o, g_co, g_so), got = hosted(
            _conv_bwd, dproj, dc1, s['proj'], w, [(merged, dx1), (c3, dya), (gated, dyb)], seq, TILE_MIX,
            f"conv_bwd_{l}", extra=extra)
        if small_pending:
            small_gathered[small_pending[0]] = got
            small_pending = None
        queue.push((l, 'conv_w'), g_conv)
        queue.push((l, 'w_o'), g_o.reshape(NDEV, rows, d))
        queue.push((l, 'w_conv_out'), g_co.reshape(NDEV, rows, d))
        queue.push((l, 'w_sgu_out'), g_so.reshape(NDEV, rows, d))
        blocks = [small_mix, small_conv, small_ffn] + ([small_loss] if l == nl - 1 else [])
        small_main = [jnp.concatenate(blocks, axis=0), dws, dbs]
        small_main_gathered = tn('w_in', s['h'], dproj, d, hid, 'b', w['w_in'].shape[2], host=True,
                                 extra=_Gather(small_main) if l == 0 else None)
        if l == 0:
            (dx, small_in), _ = hosted(_in_proj_bwd, dproj, s['x'], dx1, smalls[l]['norm_mix'], w['w_in'], TILE_IN,
                                       f"in_proj_bwd_{l}")
        else:
            (dx, small_in), _ = _in_proj_bwd(dproj, s['x'], dx1, smalls[l]['norm_mix'], w['w_in'], TILE_IN,
                                             f"in_proj_bwd_{l}", None)
        small_pending = (l, _Gather(small_main + [small_in]))
    keys, comm = queue.take()
    last = _exchange_alone(_together(comm, _Gather([small_in])), "exchange_last_grads")
    queue.put(keys, last[:len(keys)])
    small_gathered[0] = small_main_gathered + last[len(keys):]
    return small_gathered, dx, queue.done


def _train_step(p, m, v, x3, target3):
    nl = p['norm_mix'].shape[0]
    bsz, seq, d = x3.shape
    x = x3.reshape(bsz * seq, d)
    target = target3.reshape(bsz * seq, d)
    small_gathered, dx, exchanged = _forward_backward(p, _prepare_weights(p), x, target, seq)

    out = {kind: {} for kind in "gdmv"}
    splits = {'w_in': 4, 'w_conv_out': 1, 'w_sgu_out': 1, 'w_o': 1, 'w_ffn_gate': 1, 'w_ffn_up': 1, 'w_ffn_down': 1, 'conv_w': 1}
    for n in splits:
        if n == 'conv_w':
            pad = ((0, 0), (0, CONV_TAPS_PADDED - CONV_TAPS), (0, 0))
            wl, ml, vl = (jnp.pad(a[n][:, :, 0, :], pad) for a in (p, m, v))
        elif n in ('w_ffn_gate', 'w_ffn_up'):
            wl, ml, vl = (_hidden_major(a[n]) for a in (p, m, v))
        else:
            wl, ml, vl = p[n], m[n], v[n]
        prev = None
        for l in range(nl):
            prev = _adamw_layer(l, wl, ml, vl, exchanged[(l, n)], prev, splits[n], f"adamw_{n}_{l}")
        for kind, arr in zip("gdmv", prev):
            if n == 'conv_w':
                arr = arr[:, 0:CONV_TAPS, None, :]
            elif n in ('w_ffn_gate', 'w_ffn_up'):
                arr = _hidden_major(arr)
            out[kind][n] = arr
    small_names = ['norm_mix', 'gate_bias', 'conv_b', 'conv_ln_g', 'conv_ln_b', 'sgu_ln_g', 'sgu_ln_b', 'w_spatial',
                   'b_spatial', 'norm_ffn', 'norm_final']

    def two_d(a):
        return a[None, :] if a.ndim == 1 else a

    res, loss_row = _adamw_small(small_gathered, {n: two_d(p[n]) for n in small_names},
                                 {n: two_d(m[n]) for n in small_names}, {n: two_d(v[n]) for n in small_names})
    loss = loss_row[0, 0]
    for kind in "gdmv":
        for n in small_names:
            out[kind][n] = res[kind][n].reshape(p[n].shape)
    grad_x = dx.reshape(bsz, seq, d)
    return (loss, grad_x, *[out[kind][n] for kind in "gdmv" for n in WEIGHT_NAMES])


def kernel(x, norm_mix, w_in, gate_bias, conv_w, conv_b, conv_ln_g, conv_ln_b, w_conv_out, sgu_ln_g, sgu_ln_b, w_spatial, b_spatial, w_sgu_out, w_o, norm_ffn, w_ffn_gate, w_ffn_up, w_ffn_down, norm_final, loss_target, m_norm_mix, m_w_in, m_gate_bias, m_conv_w, m_conv_b, m_conv_ln_g, m_conv_ln_b, m_w_conv_out, m_sgu_ln_g, m_sgu_ln_b, m_w_spatial, m_b_spatial, m_w_sgu_out, m_w_o, m_norm_ffn, m_w_ffn_gate, m_w_ffn_up, m_w_ffn_down, m_norm_final, v_norm_mix, v_w_in, v_gate_bias, v_conv_w, v_conv_b, v_conv_ln_g, v_conv_ln_b, v_w_conv_out, v_sgu_ln_g, v_sgu_ln_b, v_w_spatial, v_b_spatial, v_w_sgu_out, v_w_o, v_norm_ffn, v_w_ffn_gate, v_w_ffn_up, v_w_ffn_down, v_norm_final):
    p = dict(zip(WEIGHT_NAMES, (norm_mix, w_in, gate_bias, conv_w, conv_b, conv_ln_g, conv_ln_b, w_conv_out, sgu_ln_g, sgu_ln_b, w_spatial, b_spatial, w_sgu_out, w_o, norm_ffn, w_ffn_gate, w_ffn_up, w_ffn_down, norm_final)))
    m = dict(zip(WEIGHT_NAMES, (m_norm_mix, m_w_in, m_gate_bias, m_conv_w, m_conv_b, m_conv_ln_g, m_conv_ln_b, m_w_conv_out, m_sgu_ln_g, m_sgu_ln_b, m_w_spatial, m_b_spatial, m_w_sgu_out, m_w_o, m_norm_ffn, m_w_ffn_gate, m_w_ffn_up, m_w_ffn_down, m_norm_final)))
    v = dict(zip(WEIGHT_NAMES, (v_norm_mix, v_w_in, v_gate_bias, v_conv_w, v_conv_b, v_conv_ln_g, v_conv_ln_b, v_w_conv_out, v_sgu_ln_g, v_sgu_ln_b, v_w_spatial, v_b_spatial, v_w_sgu_out, v_w_o, v_norm_ffn, v_w_ffn_gate, v_w_ffn_up, v_w_ffn_down, v_norm_final)))
    return _train_step(p, m, v, x, loss_target)
```

```python
import math

import jax
import jax.numpy as jnp
from jax import lax
from jax.experimental import pallas as pl
from jax.experimental.pallas import tpu as pltpu

F32 = jnp.float32
BF16 = jnp.bfloat16
MESH_ID = pl.DeviceIdType.MESH

NDEV = 8
EPS = 1e-6
CONV_TAPS = 31
CONV_TAPS_PADDED = 32
HALO = 16
CONV_ROWS = 128
CONV_ROWS_LOOP = 64
LANES = 128
SUBLANES = 8
CHUNK = 128
GROUPS = 8
FF_PAD = 384
FF_PAIR = 2 * FF_PAD
TN_COLS = 512
VMEM_LIMIT_BYTES = 56 * 1024 * 1024

ADAM_LR = 0.001
ADAM_B1 = 0.9
ADAM_B2 = 0.999
ADAM_EPS = 1e-08
ADAM_WD = 0.01
ADAM_STEP = 10

TILE_IN = 512
TILE_IN_FWD = 1024
TILE_MIX = 256
TILE_MIX_FWD = 512
TILE_FFN_FWD = 1024
TILE_FFN_BWD = 512
TILE_TN = 1024
TILE_LOSS = 512

WEIGHT_NAMES = ['norm_mix', 'w_in', 'gate_bias', 'conv_w', 'conv_b', 'conv_ln_g', 'conv_ln_b', 'w_conv_out',
                'sgu_ln_g', 'sgu_ln_b', 'w_spatial', 'b_spatial', 'w_sgu_out', 'w_o', 'norm_ffn', 'w_ffn_gate',
                'w_ffn_up', 'w_ffn_down', 'norm_final']
MIXER_WEIGHTS = ['w_conv_out', 'w_sgu_out', 'w_o', 'conv_w']
FFN_WEIGHTS = ['w_ffn_gate', 'w_ffn_up', 'w_ffn_down']


def _sds(shape, dtype):
    return jax.ShapeDtypeStruct(tuple(shape), dtype)


def _params(*sem):
    return pltpu.CompilerParams(dimension_semantics=sem or None, vmem_limit_bytes=VMEM_LIMIT_BYTES)


def _nn(a, b):
    return jnp.dot(a, b, preferred_element_type=F32)


def _nt(a, b):
    return lax.dot_general(a, b, (((1,), (1,)), ((), ())), preferred_element_type=F32)


def _tn(a, b):
    return lax.dot_general(a, b, (((0,), (0,)), ((), ())), preferred_element_type=F32)


def _sig(v):
    return jax.nn.sigmoid(v)


def _fold(v):
    r, c = v.shape
    return jnp.sum(v.reshape(r // SUBLANES, SUBLANES, c), axis=0)


def _tile(tm, n, j=0):
    return pl.BlockSpec((tm, n), lambda i, *_: (i, j))


def _row(n):
    return pl.BlockSpec((1, n), lambda *_: (0, 0))


def _resident(shape):
    nd = len(shape)
    return pl.BlockSpec(tuple(shape), lambda *_: (0,) * nd)


def _weight(w):
    nd = w.ndim
    return pl.BlockSpec(tuple(w.shape), lambda *_: (0,) * nd, pipeline_mode=pl.Buffered(1))


def _peer(rel):
    x, y, c = lax.axis_index("x"), lax.axis_index("y"), lax.axis_index("c")
    return (1 - x if rel & 4 else x, 1 - y if rel & 2 else y, 1 - c if rel & 1 else c)


def _slot(pos):
    return 4 * pos[0] + 2 * pos[1] + pos[2]


class _Exchange:
    def __init__(self, arrays, layers=None):
        self.arrays = list(arrays)
        self.layers = list(layers) if layers is not None else [None] * len(self.arrays)

    def scratch(self):
        n = len(self.arrays)
        return [pltpu.SemaphoreType.DMA((n, NDEV)), pltpu.SemaphoreType.DMA((n, NDEV)), pltpu.SemaphoreType.DMA((n,))]

    def _src(self, ins, j):
        return ins[j] if self.layers[j] is None else ins[j].at[self.layers[j]]

    def _block_shape(self, j):
        a = self.arrays[j]
        return a.shape if self.layers[j] is None else a.shape[1:]


class _Gather(_Exchange):
    chips = (4, 2, 6)

    def out_shape(self):
        return [_sds((NDEV,) + tuple(self._block_shape(j)), a.dtype) for j, a in enumerate(self.arrays)]

    @staticmethod
    def _copy(outs, sems, j, sem, block_rel, to_rel, src=None):
        blk = outs[j].at[_slot(_peer(block_rel))]
        return pltpu.make_async_remote_copy(
            src_ref=blk if src is None else src, dst_ref=blk,
            send_sem=sems[0].at[j, sem], recv_sem=sems[1].at[j, sem],
            device_id=_peer(to_rel), device_id_type=MESH_ID)

    def _local(self, ins, outs, sems, j):
        return pltpu.make_async_copy(self._src(ins, j), outs[j].at[_slot(_peer(0))], sems[2].at[j])

    def start(self, ins, outs, sems):
        for j in range(len(self.arrays)):
            self._local(ins, outs, sems, j).start()
            for rel in (1,) + self.chips:
                self._copy(outs, sems, j, rel, 0, rel, src=self._src(ins, j)).start()

    def forward(self, ins, outs, sems):
        for j in range(len(self.arrays)):
            for rel in self.chips:
                self._copy(outs, sems, j, rel, rel, 0).wait_recv()
                self._copy(outs, sems, j, rel ^ 1, rel, 1).start()

    def finish(self, ins, outs, sems):
        for j in range(len(self.arrays)):
            self._copy(outs, sems, j, 1, 1, 0).wait_recv()
            for rel in self.chips:
                self._copy(outs, sems, j, rel ^ 1, rel ^ 1, 0).wait_recv()
        for j in range(len(self.arrays)):
            for rel in (1,) + self.chips:
                self._copy(outs, sems, j, rel, 0, rel, src=self._src(ins, j)).wait_send()
            for rel in self.chips:
                self._copy(outs, sems, j, rel ^ 1, rel, 1).wait_send()
            self._local(ins, outs, sems, j).wait()


class _Scatter(_Exchange):
    def out_shape(self):
        return [_sds(a.shape, a.dtype) for a in self.arrays]

    @staticmethod
    def _copy(ins, outs, sems, j, rel):
        return pltpu.make_async_remote_copy(
            src_ref=ins[j].at[_slot(_peer(rel))], dst_ref=outs[j].at[_slot(_peer(0))],
            send_sem=sems[0].at[j, rel], recv_sem=sems[1].at[j, rel],
            device_id=_peer(rel), device_id_type=MESH_ID)

    @staticmethod
    def _arrival(outs, sems, j, rel):
        blk = outs[j].at[_slot(_peer(rel))]
        return pltpu.make_async_remote_copy(
            src_ref=blk, dst_ref=blk, send_sem=sems[0].at[j, rel], recv_sem=sems[1].at[j, rel],
            device_id=_peer(rel), device_id_type=MESH_ID)

    @staticmethod
    def _local(ins, outs, sems, j):
        me = _slot(_peer(0))
        return pltpu.make_async_copy(ins[j].at[me], outs[j].at[me], sems[2].at[j])

    def start(self, ins, outs, sems):
        for j in range(len(self.arrays)):
            self._local(ins, outs, sems, j).start()
            for rel in range(1, NDEV):
                self._copy(ins, outs, sems, j, rel).start()

    def forward(self, ins, outs, sems):
        pass

    def finish(self, ins, outs, sems):
        for j in range(len(self.arrays)):
            for rel in range(1, NDEV):
                self._arrival(outs, sems, j, rel).wait_recv()
        for j in range(len(self.arrays)):
            for rel in range(1, NDEV):
                self._copy(ins, outs, sems, j, rel).wait_send()
            self._local(ins, outs, sems, j).wait()


class _Together:
    def __init__(self, parts):
        self.parts = [c for c in parts if c is not None]
        self.arrays = [a for c in self.parts for a in c.arrays]

    def out_shape(self):
        return [s for c in self.parts for s in c.out_shape()]

    def scratch(self):
        return [s for c in self.parts for s in c.scratch()]

    def _each(self, method, ins, outs, sems):
        at = 0
        for q, c in enumerate(self.parts):
            n = len(c.arrays)
            getattr(c, method)(ins[at:at + n], outs[at:at + n], sems[3 * q:3 * q + 3])
            at += n

    def start(self, ins, outs, sems):
        self._each("start", ins, outs, sems)

    def forward(self, ins, outs, sems):
        self._each("forward", ins, outs, sems)

    def finish(self, ins, outs, sems):
        self._each("finish", ins, outs, sems)


def _together(*parts):
    parts = [c for c in parts if c is not None]
    return _Together(parts) if parts else None


def _call(body, *, name, args, in_specs, out_specs, out_shape, grid=(), scratch_shapes=(), semantics=(),
          aliases=None, comm=None):
    in_specs, out_specs, out_shape = list(in_specs), list(out_specs), list(out_shape)
    scratch_shapes = list(scratch_shapes)
    if comm is None:
        res = pl.pallas_call(
            body, name=name, grid=grid, in_specs=in_specs, out_specs=out_specs, out_shape=out_shape,
            scratch_shapes=scratch_shapes, input_output_aliases=aliases or {},
            compiler_params=_params(*semantics))(*args)
        return list(res), []
    n_in, n_out, n_scr, nc = len(in_specs), len(out_specs), len(scratch_shapes), len(comm.arrays)
    total = math.prod(grid)
    middle = min((total * 5) // 8, total - 1)

    def hosted(*refs):
        ins, cins = refs[:n_in], refs[n_in:n_in + nc]
        o0 = n_in + nc
        outs, couts = refs[o0:o0 + n_out], refs[o0 + n_out:o0 + n_out + nc]
        s0 = o0 + n_out + nc
        scr, sems = refs[s0:s0 + n_scr], refs[s0 + n_scr:]
        if total == 1:
            comm.start(cins, couts, sems)
            body(*ins, *outs, *scr)
            comm.forward(cins, couts, sems)
            comm.finish(cins, couts, sems)
            return
        step = 0
        for axis, size in enumerate(grid):
            step = step * size + pl.program_id(axis)
        pl.when(step == 0)(lambda: comm.start(cins, couts, sems))
        body(*ins, *outs, *scr)
        pl.when(step == middle)(lambda: comm.forward(cins, couts, sems))
        pl.when(step == total - 1)(lambda: comm.finish(cins, couts, sems))

    any_spec = pl.BlockSpec(memory_space=pl.ANY)
    res = pl.pallas_call(
        hosted, name=name, grid=grid,
        in_specs=in_specs + [any_spec] * nc, out_specs=out_specs + [any_spec] * nc,
        out_shape=out_shape + comm.out_shape(), scratch_shapes=scratch_shapes + comm.scratch(),
        input_output_aliases=aliases or {}, compiler_params=_params(*(("arbitrary",) * len(grid))),
    )(*args, *comm.arrays)
    return list(res[:n_out]), list(res[n_out:])


def _exchange_alone(comm, name):
    return _call(lambda: None, name=name, args=(), in_specs=(), out_specs=(), out_shape=(), comm=comm)[1]


def _cast_pad(w, rows, cols, name):
    nl, r, c = w.shape

    def body(w_ref, o_ref):
        if (rows, cols) != (r, c):
            o_ref[...] = jnp.zeros(o_ref.shape, BF16)
        o_ref[0, 0:r, 0:c] = w_ref[0].astype(BF16)

    return _call(body, name=name, grid=(nl,), args=(w,),
                 in_specs=[pl.BlockSpec((1, r, c), lambda i: (i, 0, 0))],
                 out_specs=[pl.BlockSpec((1, rows, cols), lambda i: (i, 0, 0))],
                 out_shape=[_sds((nl, rows, cols), BF16)], semantics=("parallel",))[0][0]


def _in_proj(x, gain, w, tm, name, comm):
    t, d = x.shape
    nb, _, bw = w.shape

    def body(x_ref, g_ref, w_ref, h_ref, p_ref):
        xv = x_ref[...]
        r = lax.rsqrt(jnp.mean(xv * xv, axis=-1, keepdims=True) + EPS)
        h = (xv * r * g_ref[...]).astype(BF16)
        h_ref[...] = h
        for k in range(nb):
            p_ref[:, k * bw:(k + 1) * bw] = _nn(h, w_ref[k]).astype(BF16)

    return _call(body, name=name, grid=(t // tm,), args=(x, gain, w),
                 in_specs=[_tile(tm, d), _row(d), _weight(w)],
                 out_specs=[_tile(tm, d), _tile(tm, nb * bw)],
                 out_shape=[_sds((t, d), BF16), _sds((t, nb * bw), BF16)],
                 semantics=("parallel",), comm=comm)


def _halo_specs(tm, t, d, col):
    nh, nhb = tm // HALO, t // HALO
    prev = pl.BlockSpec((HALO, d), lambda i: (jnp.maximum(i * nh - 1, 0), col))
    nxt = pl.BlockSpec((HALO, d), lambda i: (jnp.minimum((i + 1) * nh, nhb - 1), col))
    return prev, nxt


def _row_chunks(tm, rc, unroll, rows):
    n = tm // rc
    if unroll:
        for j in range(n):
            rows(j * rc)
    else:
        def step(j, carry):
            rows(pl.multiple_of(j * rc, rc))
            return carry

        lax.fori_loop(0, n, step, 0)


def _shifted(buf_ref, kb, r0, off, rc):
    return buf_ref[kb, pl.ds(r0 + off, rc), :]


def _dwconv(buf_ref, w_ref, out_ref, tm, flip, rc, unroll):
    nblk = out_ref.shape[1] // LANES

    def rows(r0):
        for kb in range(nblk):
            acc = jnp.zeros((rc, LANES), F32)
            for k in range(CONV_TAPS):
                off = (CONV_TAPS - k) if flip else (1 + k)
                acc = acc + w_ref[kb, k:k + 1, :] * _shifted(buf_ref, kb, r0, off, rc)
            out_ref[pl.ds(r0, rc), kb * LANES:(kb + 1) * LANES] = acc

    _row_chunks(tm, rc, unroll, rows)


def _fill_halo_buffer(buf, prev, body, nxt, first, last, tm):
    prev = jnp.where(first, 0.0, prev)
    nxt = jnp.where(last, 0.0, nxt)
    for kb in range(buf.shape[0]):
        lanes = slice(kb * LANES, (kb + 1) * LANES)
        buf[kb, 0:HALO, :] = prev[:, lanes]
        buf[kb, HALO:HALO + tm, :] = body[:, lanes]
        buf[kb, HALO + tm:HALO + tm + HALO, :] = nxt[:, lanes]


def _fill_glu_buffer(cbuf, av, ag, avp, agp, avn, agn, first, last, tm):
    c0p = avp[...].astype(F32) * _sig(agp[...].astype(F32))
    c0n = avn[...].astype(F32) * _sig(agn[...].astype(F32))
    c0 = av[...].astype(F32) * _sig(ag[...].astype(F32))
    _fill_halo_buffer(cbuf, c0p, c0, c0n, first, last, tm)


def _layernorm_stats(v):
    mu = jnp.mean(v, axis=-1, keepdims=True)
    cen = v - mu
    rstd = lax.rsqrt(jnp.mean(cen * cen, axis=-1, keepdims=True) + EPS)
    return cen * rstd, rstd


def _spatial_mix(ws_ref, vn_ref, bias_ref, mixed_ref, tm):
    for ci in range(tm // CHUNK):
        rs = slice(ci * CHUNK, (ci + 1) * CHUNK)
        for g in range(GROUPS):
            ls = slice(g * LANES, (g + 1) * LANES)
            mixed_ref[rs, ls] = _nn(ws_ref[g], vn_ref[rs, ls]) + bias_ref[:, ls]


def _mixer_fwd(proj, x, w, small, seq, tm, name, comm):
    t, d = x.shape
    hp_v, hn_v = _halo_specs(tm, t, d, 0)
    hp_g, hn_g = _halo_specs(tm, t, d, 1)

    def body(av, ag, u_ref, v_ref, ga_ref, gb_ref, avp, agp, avn, agn, x_ref,
             cw_ref, cb_ref, lg_ref, lb_ref, wco_ref, sg_ref, sb_ref, ws_ref, bias_ref, wso_ref,
             gba_ref, gbb_ref, wo_ref,
             c1_ref, ya_ref, yb_ref, x1_ref, cbuf, c1f, vn_ref, mixed_ref):
        i = pl.program_id(0)
        first = (i * tm) % seq == 0
        last = ((i + 1) * tm) % seq == 0
        _fill_glu_buffer(cbuf, av, ag, avp, agp, avn, agn, first, last, tm)
        _dwconv(cbuf, cw_ref, c1f, tm, flip=False, rc=CONV_ROWS_LOOP, unroll=False)
        c1 = c1f[...] + cb_ref[...]
        c1_ref[...] = c1.astype(BF16)
        c2hat, _ = _layernorm_stats(c1)
        c2 = c2hat * lg_ref[...] + lb_ref[...]
        c3 = (c2 * _sig(c2)).astype(BF16)
        ya = _nn(c3, wco_ref[...].reshape(d, d))
        ya_ref[...] = ya.astype(BF16)
        vhat, _ = _layernorm_stats(v_ref[...].astype(F32))
        vn_ref[...] = (vhat * sg_ref[...] + sb_ref[...]).astype(BF16)
        _spatial_mix(ws_ref, vn_ref, bias_ref, mixed_ref, tm)
        gated = (u_ref[...].astype(F32) * mixed_ref[...]).astype(BF16)
        yb = _nn(gated, wso_ref[...].reshape(d, d))
        yb_ref[...] = yb.astype(BF16)
        sa = _sig(ga_ref[...].astype(F32) + gba_ref[...])
        sb = _sig(gb_ref[...].astype(F32) + gbb_ref[...])
        merged = (sa * ya + sb * yb).astype(BF16)
        x1_ref[...] = x_ref[...] + _nn(merged, wo_ref[...].reshape(d, d))

    cols = [_tile(tm, d, j) for j in range(6)]
    return _call(
        body, name=name, grid=(t // tm,),
        args=(proj,) * 10 + (x, w['conv_w'], small['conv_b'], small['conv_ln_g'], small['conv_ln_b'], w['w_conv_out'],
                             small['sgu_ln_g'], small['sgu_ln_b'], small['ws'], small['bias_full'], w['w_sgu_out'],
                             small['gba'], small['gbb'], w['w_o']),
        in_specs=cols + [hp_v, hp_g, hn_v, hn_g, _tile(tm, d),
                         _weight(w['conv_w']), _row(d), _row(d), _row(d),
                         _weight(w['w_conv_out']), _row(d), _row(d),
                         _resident(small['ws'].shape), _resident(small['bias_full'].shape),
                         _weight(w['w_sgu_out']), _row(d), _row(d), _weight(w['w_o'])],
        out_specs=[_tile(tm, d)] * 4,
        out_shape=[_sds((t, d), BF16)] * 3 + [_sds((t, d), F32)],
        scratch_shapes=[pltpu.VMEM((d // LANES, tm + 2 * HALO, LANES), F32), pltpu.VMEM((tm, d), F32),
                        pltpu.VMEM((tm, d), BF16), pltpu.VMEM((tm, d), F32)],
        semantics=("parallel",), comm=comm)


def _pair_gate_up(wg, wu, name):
    n, c, d = wg.shape

    def body(wg_ref, wu_ref, o_ref):
        o_ref[0:c, :] = wg_ref[0]
        o_ref[c:2 * c, :] = wg_ref[1]
        o_ref[2 * c:3 * c, :] = wu_ref[0]
        o_ref[3 * c:4 * c, :] = wu_ref[1]

    pair = pl.BlockSpec((2, c, d), lambda k: (k, 0, 0))
    return _call(body, name=name, grid=(n // 2,), args=(wg, wu), in_specs=[pair, pair],
                 out_specs=[pl.BlockSpec((None, 4 * c, d), lambda k: (k, 0, 0))],
                 out_shape=[_sds((n // 2, 4 * c, d), wg.dtype)], semantics=("parallel",))[0][0]


def _pair_specs(w, tm):
    d = w['w_gate_up'].shape[2]
    up = pl.BlockSpec((None, 2 * FF_PAIR, d), lambda i, k: (k, 0, 0))
    down = pl.BlockSpec((None, FF_PAIR, d), lambda i, k: (k, 0, 0))
    return [up, down]


def _ffn_fwd(x1, gain, w, tm, name, comm):
    t, d = x1.shape
    tm = min(tm, t)
    npair = NDEV // 2
    hid = NDEV * FF_PAD
    wd_pairs = w['w_ffn_down'].reshape(npair, FF_PAIR, d)

    def body(x_ref, g_ref, wgu_ref, wd_ref, h_ref, gg_ref, uu_ref, x2_ref, hb_ref, acc_ref):
        k = pl.program_id(1)

        @pl.when(k == 0)
        def _():
            xv = x_ref[...]
            r = lax.rsqrt(jnp.mean(xv * xv, axis=-1, keepdims=True) + EPS)
            h = (xv * r * g_ref[...]).astype(BF16)
            hb_ref[...] = h
            h_ref[...] = h
            acc_ref[...] = xv

        gu = _nt(hb_ref[...], wgu_ref[...])
        gk = gu[:, 0:FF_PAIR]
        uk = gu[:, FF_PAIR:2 * FF_PAIR]
        gg_ref[...] = gk.astype(BF16)
        uu_ref[...] = uk.astype(BF16)
        ak = (gk * _sig(gk) * uk).astype(BF16)
        acc_ref[...] += _nn(ak, wd_ref[...])

        @pl.when(k == npair - 1)
        def _():
            x2_ref[...] = acc_ref[...]

    pair_cols = pl.BlockSpec((tm, FF_PAIR), lambda i, k: (i, k))
    return _call(
        body, name=name, grid=(t // tm, npair),
        args=(x1, gain, w['w_gate_up'], wd_pairs),
        in_specs=[_tile(tm, d), _row(d)] + _pair_specs(w, tm),
        out_specs=[_tile(tm, d), pair_cols, pair_cols, _tile(tm, d)],
        out_shape=[_sds((t, d), BF16), _sds((t, hid), BF16), _sds((t, hid), BF16), _sds((t, d), F32)],
        scratch_shapes=[pltpu.VMEM((tm, d), BF16), pltpu.VMEM((tm, d), F32)],
        semantics=("parallel", "arbitrary"), comm=comm)


def _init_small(first, small_ref, acc_ref):
    @pl.when(first)
    def _():
        small_ref[...] = jnp.zeros(small_ref.shape, F32)
        acc_ref[...] = jnp.zeros(acc_ref.shape, F32)


def _finish_small(last, small_ref, acc_ref, nq):
    @pl.when(last)
    def _():
        for q in range(nq):
            small_ref[q:q + 1, :] = jnp.sum(acc_ref[q], axis=0, keepdims=True)


def _loss_bwd(x, gain, target, tm, name):
    t, d = x.shape
    tm = min(tm, t)
    nsteps = t // tm

    def body(x_ref, g_ref, t_ref, dx_ref, small_ref, acc_ref):
        i = pl.program_id(0)
        _init_small(i == 0, small_ref, acc_ref)
        xv = x_ref[...]
        r = lax.rsqrt(jnp.mean(xv * xv, axis=-1, keepdims=True) + EPS)
        xhat = xv * r
        diff = xhat * g_ref[...] - t_ref[...]
        dy = diff * (1.0 / d)
        acc_ref[0] += _fold(dy * xhat)
        acc_ref[1] += _fold(diff * diff)
        dxhat = dy * g_ref[...]
        dx_ref[...] = r * (dxhat - xhat * jnp.mean(dxhat * xhat, axis=-1, keepdims=True))

        @pl.when(i == nsteps - 1)
        def _():
            small_ref[0:1, :] = jnp.sum(acc_ref[0], axis=0, keepdims=True)
            small_ref[1:2, :] = jnp.full((1, d), jnp.sum(acc_ref[1]) * (0.5 / d), F32)

    return _call(body, name=name, grid=(nsteps,), args=(x, gain, target),
                 in_specs=[_tile(tm, d), _row(d), _tile(tm, d)],
                 out_specs=[_tile(tm, d), _resident((SUBLANES, d))],
                 out_shape=[_sds((t, d), F32), _sds((SUBLANES, d), F32)],
                 scratch_shapes=[pltpu.VMEM((2, SUBLANES, d), F32)], semantics=("arbitrary",))[0]


def _ffn_bwd(dx2, x1, gain, gg, uu, w, tm, name, comm):
    t, d = x1.shape
    tm = min(tm, t)
    npair = NDEV // 2
    hid = NDEV * FF_PAD
    nsteps = t // tm
    wd_pairs = w['w_ffn_down'].reshape(npair, FF_PAIR, d)

    def body(dx_ref, x_ref, g_ref, gg_ref, uu_ref, wgu_ref, wd_ref,
             a_ref, dg_ref, du_ref, dx1_ref, small_ref, acc_ref, dxb_ref, dh_ref, dgu_ref):
        i, k = pl.program_id(0), pl.program_id(1)
        _init_small((i == 0) & (k == 0), small_ref, acc_ref)

        @pl.when(k == 0)
        def _():
            dxb_ref[...] = dx_ref[...].astype(BF16)
            dh_ref[...] = jnp.zeros(dh_ref.shape, F32)

        gk = gg_ref[...].astype(F32)
        uk = uu_ref[...].astype(F32)
        sg = _sig(gk)
        silu = gk * sg
        a_ref[...] = (silu * uk).astype(BF16)
        da = _nt(dxb_ref[...], wd_ref[...])
        dgk = (da * uk * (sg * (1.0 + gk * (1.0 - sg)))).astype(BF16)
        duk = (da * silu).astype(BF16)
        dg_ref[...] = dgk
        du_ref[...] = duk
        dgu_ref[:, 0:FF_PAIR] = dgk
        dgu_ref[:, FF_PAIR:2 * FF_PAIR] = duk
        dh_ref[...] += _nn(dgu_ref[...], wgu_ref[...])

        @pl.when(k == npair - 1)
        def _():
            xv = x_ref[...]
            dh = dh_ref[...]
            r = lax.rsqrt(jnp.mean(xv * xv, axis=-1, keepdims=True) + EPS)
            xhat = xv * r
            acc_ref[0] += _fold(dh * xhat)
            dxhat = dh * g_ref[...]
            dx1_ref[...] = dx_ref[...] + r * (dxhat - xhat * jnp.mean(dxhat * xhat, axis=-1, keepdims=True))

        _finish_small((i == nsteps - 1) & (k == npair - 1), small_ref, acc_ref, 1)

    pair_cols = pl.BlockSpec((tm, FF_PAIR), lambda i, k: (i, k))
    return _call(
        body, name=name, grid=(nsteps, npair),
        args=(dx2, x1, gain, gg, uu, w['w_gate_up'], wd_pairs),
        in_specs=[_tile(tm, d), _tile(tm, d), _row(d), pair_cols, pair_cols] + _pair_specs(w, tm),
        out_specs=[pair_cols] * 3 + [_tile(tm, d), _resident((SUBLANES, d))],
        out_shape=[_sds((t, hid), BF16)] * 3 + [_sds((t, d), F32), _sds((SUBLANES, d), F32)],
        scratch_shapes=[pltpu.VMEM((1, SUBLANES, d), F32), pltpu.VMEM((tm, d), BF16), pltpu.VMEM((tm, d), F32),
                        pltpu.VMEM((tm, 2 * FF_PAIR), BF16)],
        semantics=("arbitrary", "arbitrary"), comm=comm)


def _matmul_tn(a, b, a_blk, b_blk, stack, shard, tm, name, comm):
    t, ma = a.shape
    tm = min(tm, t)
    nb_ = b.shape[1]
    na, nb = ma // a_blk, nb_ // b_blk
    nsteps = t // tm
    cw = min(TN_COLS, b_blk)
    if stack == 'b':
        per = b_blk // shard
        out_shape, out_spec = (nb_ // shard, ma, shard), pl.BlockSpec((per, a_blk, shard), lambda i, j, k: (j, 0, 0))
    elif stack == 'a':
        per = a_blk // shard
        out_shape, out_spec = (ma // shard, shard, nb_), pl.BlockSpec((per, shard, b_blk), lambda i, j, k: (i, 0, 0))
    else:
        out_shape, out_spec = (ma, nb_), pl.BlockSpec((a_blk, b_blk), lambda i, j, k: (i, j))

    def body(a_ref, b_ref, o_ref, acc_ref):
        k = pl.program_id(2)

        @pl.when(k == 0)
        def _():
            acc_ref[...] = jnp.zeros(acc_ref.shape, F32)

        av = a_ref[...].astype(BF16)
        for c in range(0, b_blk, cw):
            acc_ref[:, c:c + cw] += _tn(av, b_ref[:, c:c + cw].astype(BF16))

        @pl.when(k == nsteps - 1)
        def _():
            if stack == 'b':
                for s in range(per):
                    o_ref[s] = acc_ref[:, s * shard:(s + 1) * shard].astype(BF16)
            elif stack == 'a':
                for s in range(per):
                    o_ref[s] = acc_ref[s * shard:(s + 1) * shard, :].astype(BF16)
            else:
                o_ref[...] = acc_ref[...].astype(BF16)

    res, got = _call(
        body, name=name, grid=(na, nb, nsteps), args=(a, b),
        in_specs=[pl.BlockSpec((tm, a_blk), lambda i, j, k: (k, i)), pl.BlockSpec((tm, b_blk), lambda i, j, k: (k, j))],
        out_specs=[out_spec], out_shape=[_sds(out_shape, BF16)],
        scratch_shapes=[pltpu.VMEM((a_blk, b_blk), F32)],
        semantics=("parallel", "parallel", "arbitrary"), comm=comm)
    return res[0], got


def _mixer_bwd(dx1, proj, c1, ya, yb, w, small, tm, name, comm):
    t, d = dx1.shape
    nsteps = t // tm
    nq = 6

    def body(dx_ref, u_ref, v_ref, ga_ref, gb_ref, c1_ref, ya_ref, yb_ref,
             lg_ref, lb_ref, wco_ref, sg_ref, sb_ref, ws_ref, wst_ref, bias_ref, wso_ref, gba_ref, gbb_ref, wo_ref,
             sel_ref,
             dp_ref, dc1_ref, mg_ref, c3_ref, gt_ref, dya_ref, dyb_ref, small_ref, dws_ref, dbs_ref,
             acc_ref, vn_ref, mixed_ref, dmix_ref, dvn_ref, dbias_ref):
        i = pl.program_id(0)
        _init_small(i == 0, small_ref, acc_ref)

        @pl.when(i == 0)
        def _():
            dws_ref[...] = jnp.zeros(dws_ref.shape, F32)
            dbs_ref[...] = jnp.zeros(dbs_ref.shape, F32)
            dbias_ref[...] = jnp.zeros(dbias_ref.shape, F32)

        dmerged = _nt(dx_ref[...].astype(BF16), wo_ref[...].reshape(d, d))
        ya = ya_ref[...].astype(F32)
        yb = yb_ref[...].astype(F32)
        sa = _sig(ga_ref[...].astype(F32) + gba_ref[...])
        sb = _sig(gb_ref[...].astype(F32) + gbb_ref[...])
        mg_ref[...] = (sa * ya + sb * yb).astype(BF16)
        dya = (dmerged * sa).astype(BF16)
        dyb = (dmerged * sb).astype(BF16)
        dya_ref[...] = dya
        dyb_ref[...] = dyb
        dga = dmerged * ya * (sa * (1.0 - sa))
        dgb = dmerged * yb * (sb * (1.0 - sb))
        acc_ref[0] += _fold(dga)
        acc_ref[1] += _fold(dgb)
        dp_ref[:, 0:2 * d] = jnp.zeros((tm, 2 * d), BF16)
        dp_ref[:, 4 * d:5 * d] = dga.astype(BF16)
        dp_ref[:, 5 * d:6 * d] = dgb.astype(BF16)
        c2hat, rstd = _layernorm_stats(c1_ref[...].astype(F32))
        c2 = c2hat * lg_ref[...] + lb_ref[...]
        s2 = _sig(c2)
        c3_ref[...] = (c2 * s2).astype(BF16)
        dc3 = _nt(dya, wco_ref[...].reshape(d, d))
        dc2 = dc3 * (s2 * (1.0 + c2 * (1.0 - s2)))
        acc_ref[2] += _fold(dc2 * c2hat)
        acc_ref[3] += _fold(dc2)
        dc2hat = dc2 * lg_ref[...]
        dc1_ref[...] = (rstd * (dc2hat - jnp.mean(dc2hat, axis=-1, keepdims=True)
                                - c2hat * jnp.mean(dc2hat * c2hat, axis=-1, keepdims=True))).astype(BF16)
        vhat, rstd_v = _layernorm_stats(v_ref[...].astype(F32))
        vn_ref[...] = (vhat * sg_ref[...] + sb_ref[...]).astype(BF16)
        _spatial_mix(ws_ref, vn_ref, bias_ref, mixed_ref, tm)
        u = u_ref[...].astype(F32)
        mixed = mixed_ref[...]
        gt_ref[...] = (u * mixed).astype(BF16)
        dgated = _nt(dyb, wso_ref[...].reshape(d, d))
        dp_ref[:, 2 * d:3 * d] = (dgated * mixed).astype(BF16)
        dmix_ref[...] = dgated * u
        for ci in range(tm // CHUNK):
            rs = slice(ci * CHUNK, (ci + 1) * CHUNK)
            dbias_ref[...] += dmix_ref[rs, :]
            for g in range(GROUPS):
                ls = slice(g * LANES, (g + 1) * LANES)
                dm = dmix_ref[rs, ls].astype(BF16)
                dws_ref[g] += _nt(dm, vn_ref[rs, ls])
                dvn_ref[rs, ls] = _nn(wst_ref[g], dm)
        dvn = dvn_ref[...]
        acc_ref[4] += _fold(dvn * vhat)
        acc_ref[5] += _fold(dvn)
        dvhat = dvn * sg_ref[...]
        dp_ref[:, 3 * d:4 * d] = (rstd_v * (dvhat - jnp.mean(dvhat, axis=-1, keepdims=True)
                                           - vhat * jnp.mean(dvhat * vhat, axis=-1, keepdims=True))).astype(BF16)
        _finish_small(i == nsteps - 1, small_ref, acc_ref, nq)

        @pl.when(i == nsteps - 1)
        def _():
            db = dbias_ref[...]
            hi = db.astype(BF16)
            lo = (db - hi.astype(F32)).astype(BF16)
            dbs_ref[...] = _nt(sel_ref[...], hi) + _nt(sel_ref[...], lo)

    cols = [_tile(tm, d, j) for j in (2, 3, 4, 5)]
    return _call(
        body, name=name, grid=(nsteps,),
        args=(dx1, proj, proj, proj, proj, c1, ya, yb,
              small['conv_ln_g'], small['conv_ln_b'], w['w_conv_out'], small['sgu_ln_g'], small['sgu_ln_b'],
              small['ws'], small['wst'], small['bias_full'], w['w_sgu_out'], small['gba'], small['gbb'], w['w_o'],
              small['group_sel']),
        in_specs=[_tile(tm, d)] + cols + [_tile(tm, d)] * 3 + [
            _row(d), _row(d), _weight(w['w_conv_out']),
            _row(d), _row(d), _resident(small['ws'].shape), _resident(small['wst'].shape),
            _resident(small['bias_full'].shape), _weight(w['w_sgu_out']), _row(d), _row(d),
            _weight(w['w_o']), _resident(small['group_sel'].shape)],
        out_specs=[_tile(tm, 6 * d)] + [_tile(tm, d)] * 6 + [
            _resident((SUBLANES, d)), _resident((GROUPS, CHUNK, CHUNK)), _resident((GROUPS, CHUNK))],
        out_shape=[_sds((t, 6 * d), BF16)] + [_sds((t, d), BF16)] * 6 + [
            _sds((SUBLANES, d), F32), _sds((GROUPS, CHUNK, CHUNK), F32), _sds((GROUPS, CHUNK), F32)],
        scratch_shapes=[pltpu.VMEM((nq, SUBLANES, d), F32), pltpu.VMEM((tm, d), BF16), pltpu.VMEM((tm, d), F32),
                        pltpu.VMEM((tm, d), F32), pltpu.VMEM((tm, d), F32), pltpu.VMEM((CHUNK, d), F32)],
        semantics=("arbitrary",), comm=comm)


def _conv_bwd(dproj, dc1, proj, w, pairs, seq, tm, name, comm):
    t, d = dc1.shape
    nsteps = t // tm
    nblk = d // LANES
    npairs = len(pairs)
    hp_v, hn_v = _halo_specs(tm, t, d, 0)
    hp_g, hn_g = _halo_specs(tm, t, d, 1)
    hp_d, hn_d = _halo_specs(tm, t, d, 0)

    def body(*refs):
        (dp_in, dc_ref, dcp, dcn, av, ag, avp, agp, avn, agn, cw_ref), refs = refs[:11], refs[11:]
        ab_refs, refs = refs[:2 * npairs], refs[2 * npairs:]
        (dp_ref, dcw_ref, small_ref), refs = refs[:3], refs[3:]
        grad_refs, refs = refs[:npairs], refs[npairs:]
        acc_ref, cbuf, dbuf, dc0f, accw, gacc = refs
        del dp_in
        i = pl.program_id(0)
        _init_small(i == 0, small_ref, acc_ref)

        @pl.when(i == 0)
        def _():
            accw[...] = jnp.zeros(accw.shape, F32)
            dcw_ref[...] = jnp.zeros(dcw_ref.shape, F32)
            gacc[...] = jnp.zeros(gacc.shape, F32)

        first = (i * tm) % seq == 0
        last = ((i + 1) * tm) % seq == 0
        _fill_glu_buffer(cbuf, av, ag, avp, agp, avn, agn, first, last, tm)
        dc1v = dc_ref[...].astype(F32)
        _fill_halo_buffer(dbuf, dcp[...].astype(F32), dc1v, dcn[...].astype(F32), first, last, tm)
        acc_ref[0] += _fold(dc1v)
        _dwconv(dbuf, cw_ref, dc0f, tm, flip=True, rc=CONV_ROWS, unroll=True)

        def rows(r0):
            for kb in range(nblk):
                dv = dbuf[kb, r0 + HALO:r0 + HALO + CONV_ROWS, :]
                for k in range(CONV_TAPS):
                    accw[kb, k] += _fold(dv * _shifted(cbuf, kb, r0, 1 + k, CONV_ROWS))

        _row_chunks(tm, CONV_ROWS, True, rows)
        for q in range(npairs):
            aq = ab_refs[2 * q][...].astype(BF16)
            for c in range(0, d, TN_COLS):
                gacc[q, :, c:c + TN_COLS] += _tn(aq, ab_refs[2 * q + 1][:, c:c + TN_COLS].astype(BF16))
        sg = _sig(ag[...].astype(F32))
        avv = av[...].astype(F32)
        dc0 = dc0f[...]
        dp_ref[:, 0:d] = (dc0 * sg).astype(BF16)
        dp_ref[:, d:2 * d] = (dc0 * avv * (sg * (1.0 - sg))).astype(BF16)
        _finish_small(i == nsteps - 1, small_ref, acc_ref, 1)

        @pl.when(i == nsteps - 1)
        def _():
            for kb in range(nblk):
                dcw_ref[kb] = jnp.sum(accw[kb], axis=1)
            for q in range(npairs):
                grad_refs[q][...] = gacc[q].astype(BF16)

    return _call(
        body, name=name, grid=(nsteps,),
        args=(dproj, dc1, dc1, dc1, proj, proj, proj, proj, proj, proj, w['conv_w']) + tuple(a for ab in pairs for a in ab),
        in_specs=[pl.BlockSpec(memory_space=pl.ANY), _tile(tm, d), hp_d, hn_d, _tile(tm, d, 0), _tile(tm, d, 1),
                  hp_v, hp_g, hn_v, hn_g, _weight(w['conv_w'])] + [_tile(tm, d)] * (2 * npairs),
        out_specs=[_tile(tm, 2 * d), _resident((nblk, CONV_TAPS_PADDED, LANES)), _resident((SUBLANES, d))]
        + [_resident((d, d))] * npairs,
        out_shape=[_sds(dproj.shape, BF16), _sds((nblk, CONV_TAPS_PADDED, LANES), F32), _sds((SUBLANES, d), F32)]
        + [_sds((d, d), BF16)] * npairs,
        scratch_shapes=[pltpu.VMEM((1, SUBLANES, d), F32), pltpu.VMEM((nblk, tm + 2 * HALO, LANES), F32),
                        pltpu.VMEM((nblk, tm + 2 * HALO, LANES), F32), pltpu.VMEM((tm, d), F32),
                        pltpu.VMEM((nblk, CONV_TAPS_PADDED, SUBLANES, LANES), F32),
                        pltpu.VMEM((npairs, d, d), F32)],
        aliases={0: 0}, semantics=("arbitrary",), comm=comm)


def _in_proj_bwd(dproj, x, dx1, gain, w, tm, name, comm):
    t, d = x.shape
    nb, _, bw = w.shape
    nsteps = t // tm

    def body(dp_ref, x_ref, dx1_ref, g_ref, w_ref, dx_ref, small_ref, acc_ref):
        i = pl.program_id(0)
        _init_small(i == 0, small_ref, acc_ref)
        dh = jnp.zeros((tm, d), F32)
        for k in range(nb):
            dh = dh + _nt(dp_ref[:, k * bw:(k + 1) * bw], w_ref[k])
        xv = x_ref[...]
        r = lax.rsqrt(jnp.mean(xv * xv, axis=-1, keepdims=True) + EPS)
        xhat = xv * r
        acc_ref[0] += _fold(dh * xhat)
        dxhat = dh * g_ref[...]
        dx_ref[...] = dx1_ref[...] + r * (dxhat - xhat * jnp.mean(dxhat * xhat, axis=-1, keepdims=True))
        _finish_small(i == nsteps - 1, small_ref, acc_ref, 1)

    return _call(body, name=name, grid=(nsteps,), args=(dproj, x, dx1, gain, w),
                 in_specs=[_tile(tm, nb * bw), _tile(tm, d), _tile(tm, d), _row(d), _weight(w)],
                 out_specs=[_tile(tm, d), _resident((SUBLANES, d))],
                 out_shape=[_sds((t, d), F32), _sds((SUBLANES, d), F32)],
                 scratch_shapes=[pltpu.VMEM((1, SUBLANES, d), F32)], semantics=("arbitrary",), comm=comm)


def _adam(wv, g, mv, vv):
    m = ADAM_B1 * mv + (1.0 - ADAM_B1) * g
    v = ADAM_B2 * vv + (1.0 - ADAM_B2) * jnp.square(g)
    m_hat = m / (1.0 - ADAM_B1 ** ADAM_STEP)
    v_hat = v / (1.0 - ADAM_B2 ** ADAM_STEP)
    delta = -ADAM_LR * (m_hat / (jnp.sqrt(v_hat) + ADAM_EPS) + ADAM_WD * wv)
    return delta, m, v


def _adamw_layer(layer, w, m, v, parts, prev, nsplit, name):
    nl, r, c = w.shape
    npart, pr, pc = parts.shape
    rt, prt = r // nsplit, pr // nsplit

    def body(w_ref, m_ref, v_ref, p_ref, *rest):
        g_ref, d_ref, nm_ref, nv_ref = rest[-4:]
        g = p_ref[0, 0:rt, 0:c].astype(F32)
        for s in range(1, npart):
            g = g + p_ref[s, 0:rt, 0:c].astype(F32)
        delta, mn, vn = _adam(w_ref[0], g, m_ref[0], v_ref[0])
        g_ref[0] = g
        d_ref[0] = delta
        nm_ref[0] = mn
        nv_ref[0] = vn

    wspec = pl.BlockSpec((1, rt, c), lambda i: (layer, i, 0))
    pspec = pl.BlockSpec((npart, prt, pc), lambda i: (0, i, 0))
    in_specs = [wspec, wspec, wspec, pspec]
    args = [w, m, v, parts]
    aliases = {}
    if prev is not None:
        in_specs += [pl.BlockSpec(memory_space=pl.ANY)] * 4
        args += list(prev)
        aliases = {4 + q: q for q in range(4)}
    return _call(body, name=name, grid=(nsplit,), args=args, in_specs=in_specs, out_specs=[wspec] * 4,
                 out_shape=[_sds(w.shape, F32)] * 4, aliases=aliases, semantics=("parallel",))[0]


VEC_ROWS = {'gate_bias': (0, 1), 'conv_ln_g': 2, 'conv_ln_b': 3, 'sgu_ln_g': 4, 'sgu_ln_b': 5,
            'conv_b': SUBLANES, 'norm_ffn': 2 * SUBLANES}
FINAL_ROW = 3 * SUBLANES
LOSS_ROW = 3 * SUBLANES + 1


def _adamw_small(gathered, params, moments_m, moments_v):
    names = list(params)
    nper = len(names)
    nl = len(gathered)

    def body(*refs):
        g_refs = [refs[4 * l:4 * l + 4] for l in range(nl)]
        rest = refs[4 * nl:]
        w_refs = dict(zip(names, rest[0:nper]))
        m_refs = dict(zip(names, rest[nper:2 * nper]))
        v_refs = dict(zip(names, rest[2 * nper:3 * nper]))
        outs = rest[3 * nper:]
        loss_ref = outs[4 * nper]
        o = {kind: dict(zip(names, outs[q * nper:(q + 1) * nper])) for q, kind in enumerate("gdmv")}

        def put(nm, idx, g):
            delta, mn, vn = _adam(w_refs[nm][idx], g, m_refs[nm][idx], v_refs[nm][idx])
            o["g"][nm][idx] = g
            o["d"][nm][idx] = delta
            o["m"][nm][idx] = mn
            o["v"][nm][idx] = vn

        def total(ref, *idx):
            g = ref[(0, *idx)]
            for s in range(1, NDEV):
                g = g + ref[(s, *idx)]
            return g

        for l, (vec_ref, dws_ref, dbs_ref, vin_ref) in enumerate(g_refs):
            dd = vec_ref.shape[2]
            put('w_spatial', (l,), total(dws_ref))
            put('b_spatial', (l,), total(dbs_ref))
            put('norm_mix', (slice(l, l + 1), slice(None)), total(vin_ref, slice(0, 1)))
            for nm, rr in VEC_ROWS.items():
                if nm == 'gate_bias':
                    put(nm, (slice(l, l + 1), slice(0, dd)), total(vec_ref, slice(rr[0], rr[0] + 1)))
                    put(nm, (slice(l, l + 1), slice(dd, 2 * dd)), total(vec_ref, slice(rr[1], rr[1] + 1)))
                else:
                    put(nm, (slice(l, l + 1), slice(None)), total(vec_ref, slice(rr, rr + 1)))
        last = g_refs[nl - 1][0]
        put('norm_final', (slice(0, 1), slice(None)), total(last, slice(FINAL_ROW, FINAL_ROW + 1)))
        loss_ref[...] = total(last, slice(LOSS_ROW, LOSS_ROW + 1))

    ins = [a for g in gathered for a in g] + [params[n] for n in names] + [moments_m[n] for n in names] + [moments_v[n] for n in names]
    out_shape = [_sds(params[n].shape, F32) for n in names] * 4 + [_sds((1, gathered[0][0].shape[2]), F32)]
    res = pl.pallas_call(body, name="adamw_small", out_shape=out_shape, compiler_params=_params())(*ins)
    return {kind: dict(zip(names, res[q * nper:(q + 1) * nper])) for q, kind in enumerate("gdmv")}, res[4 * nper]


def _hidden_major(a):
    return jnp.swapaxes(a, 1, 2)


def _prepare_weights(p):
    d = p['w_in'].shape[1]
    return {
        'w_in': _cast_pad(p['w_in'], d, p['w_in'].shape[2], "cast_w_in"),
        'w_conv_out': _cast_pad(p['w_conv_out'], p['w_conv_out'].shape[1], d, "cast_w_conv_out"),
        'w_sgu_out': _cast_pad(p['w_sgu_out'], p['w_sgu_out'].shape[1], d, "cast_w_sgu_out"),
        'w_o': _cast_pad(p['w_o'], p['w_o'].shape[1], d, "cast_w_o"),
        'w_ffn_gate': _cast_pad(_hidden_major(p['w_ffn_gate']), FF_PAD, d, "cast_w_ffn_gate"),
        'w_ffn_up': _cast_pad(_hidden_major(p['w_ffn_up']), FF_PAD, d, "cast_w_ffn_up"),
        'w_ffn_down': _cast_pad(p['w_ffn_down'], FF_PAD, d, "cast_w_ffn_down"),
        'conv_w': jnp.pad(p['conv_w'][:, :, 0, :], ((0, 0), (0, CONV_TAPS_PADDED - CONV_TAPS), (0, 0))),
    }


def _gather_of(shards, names, layer):
    return _Gather([shards[n] for n in names], [layer] * len(names))


def _layer_small(p, layer):
    d = p['norm_mix'].shape[1]
    ws = p['w_spatial'][layer]
    rows = {n: p[n][layer:layer + 1] for n in ('norm_mix', 'norm_ffn', 'conv_b', 'conv_ln_g', 'conv_ln_b',
                                               'sgu_ln_g', 'sgu_ln_b')}
    return {
        **rows,
        'ws': ws.astype(BF16), 'wst': jnp.swapaxes(ws, 1, 2).astype(BF16),
        'bias_full': jnp.repeat(p['b_spatial'][layer].T, LANES, axis=1),
        'gba': p['gate_bias'][layer:layer + 1, 0:d], 'gbb': p['gate_bias'][layer:layer + 1, d:2 * d],
        'group_sel': (jnp.arange(d)[None, :] // LANES == jnp.arange(GROUPS)[:, None]).astype(BF16),
    }


class _GradQueue:
    def __init__(self):
        self.pending = []
        self.done = {}

    def push(self, key, array):
        self.pending.append((key, array))

    def take(self):
        keys = [k for k, _ in self.pending]
        comm = _Scatter([a for _, a in self.pending]) if self.pending else None
        self.pending = []
        return keys, comm

    def put(self, keys, arrays):
        self.done.update(zip(keys, arrays))


def _forward_backward(p, shards, x, target, seq):
    nl = p['norm_mix'].shape[0]
    d = x.shape[1]
    smalls = [_layer_small(p, l) for l in range(nl)]
    w_in = _exchange_alone(_gather_of(shards, ['w_in'], 0), "gather_w_in_0")[0]
    saved = []
    for l in range(nl):
        (h, proj), got = _in_proj(x, smalls[l]['norm_mix'], w_in, min(TILE_IN_FWD, x.shape[0]), f"in_proj_{l}",
                                  _gather_of(shards, MIXER_WEIGHTS, l))
        w = dict(zip(MIXER_WEIGHTS, got), w_in=w_in)
        (c1, ya, yb, x1), got = _mixer_fwd(proj, x, w, smalls[l], seq, min(TILE_MIX_FWD, seq), f"mixer_fwd_{l}",
                                           _gather_of(shards, FFN_WEIGHTS, l))
        w.update(zip(FFN_WEIGHTS, got))
        w['w_gate_up'] = _pair_gate_up(w['w_ffn_gate'], w['w_ffn_up'], f"pair_gate_up_{l}")
        nxt = _gather_of(shards, ['w_in'], l + 1) if l + 1 < nl else None
        (h2, gg, uu, x2), got = _ffn_fwd(x1, smalls[l]['norm_ffn'], w, TILE_FFN_FWD, f"ffn_fwd_{l}", nxt)
        saved.append(dict(x=x, h=h, proj=proj, c1=c1, ya=ya, yb=yb, x1=x1, h2=h2, gg=gg, uu=uu, w=w))
        x = x2
        if got:
            w_in = got[0]
    dx, small_loss = _loss_bwd(x, p['norm_final'][None, :], target, TILE_LOSS, "loss_bwd")
    queue = _GradQueue()
    small_gathered = [None] * nl
    small_pending = None
    rows = d // NDEV
    hid = NDEV * FF_PAD
    for l in reversed(range(nl)):
        s = saved[l]
        w = s['w']

        def hosted(fn, *args, extra=None):
            keys, comm = queue.take()
            res, got = fn(*args, _together(comm, extra))
            queue.put(keys, got[:len(keys)])
            return res, got[len(keys):]

        def tn(key, a, b, a_blk, b_blk, stack, shard, reshape=None, host=False, extra=None):
            keys, comm = queue.take() if host else ([], None)
            g, got = _matmul_tn(a, b, a_blk, b_blk, stack, shard, TILE_TN, f"dw_{key}_{l}", _together(comm, extra))
            queue.put(keys, got[:len(keys)])
            queue.push((l, key), g if reshape is None else g.reshape(reshape))
            return got[len(keys):]

        (act, dgg, duu, dx1, small_ffn), _ = hosted(_ffn_bwd, dx, s['x1'], smalls[l]['norm_ffn'], s['gg'], s['uu'], w,
                                                    TILE_FFN_BWD, f"ffn_bwd_{l}")
        tn('w_ffn_gate', dgg, s['h2'], hid, d, 'a', FF_PAD)
        tn('w_ffn_up', duu, s['h2'], hid, d, 'a', FF_PAD)
        (dproj, dc1, merged, c3, gated, dya, dyb, small_mix, dws, dbs), _ = hosted(
            _mixer_bwd, dx1, s['proj'], s['c1'], s['ya'], s['yb'], w, smalls[l], TILE_MIX, f"mixer_bwd_{l}")
        tn('w_ffn_down', act, dx, hid, d, 'a', FF_PAD)
        extra = small_pending[1] if small_pending else None
        (dproj, g_conv, small_conv, g_o, g_co, g_so), got = hosted(
            _conv_bwd, dproj, dc1, s['proj'], w, [(merged, dx1), (c3, dya), (gated, dyb)], seq, TILE_MIX,
            f"conv_bwd_{l}", extra=extra)
        if small_pending:
            small_gathered[small_pending[0]] = got
            small_pending = None
        queue.push((l, 'conv_w'), g_conv)
        queue.push((l, 'w_o'), g_o.reshape(NDEV, rows, d))
        queue.push((l, 'w_conv_out'), g_co.reshape(NDEV, rows, d))
        queue.push((l, 'w_sgu_out'), g_so.reshape(NDEV, rows, d))
        blocks = [small_mix, small_conv, small_ffn] + ([small_loss] if l == nl - 1 else [])
        small_main = [jnp.concatenate(blocks, axis=0), dws, dbs]
        small_main_gathered = tn('w_in', s['h'], dproj, d, hid, 'b', w['w_in'].shape[2], host=True,
                                 extra=_Gather(small_main) if l == 0 else None)
        if l == 0:
            (dx, small_in), _ = hosted(_in_proj_bwd, dproj, s['x'], dx1, smalls[l]['norm_mix'], w['w_in'], TILE_IN,
                                       f"in_proj_bwd_{l}")
        else:
            (dx, small_in), _ = _in_proj_bwd(dproj, s['x'], dx1, smalls[l]['norm_mix'], w['w_in'], TILE_IN,
                                             f"in_proj_bwd_{l}", None)
        small_pending = (l, _Gather(small_main + [small_in]))
    keys, comm = queue.take()
    last = _exchange_alone(_together(comm, _Gather([small_in])), "exchange_last_grads")
    queue.put(keys, last[:len(keys)])
    small_gathered[0] = small_main_gathered + last[len(keys):]
    return small_gathered, dx, queue.done


def _train_step(p, m, v, x3, target3):
    nl = p['norm_mix'].shape[0]
    bsz, seq, d = x3.shape
    x = x3.reshape(bsz * seq, d)
    target = target3.reshape(bsz * seq, d)
    small_gathered, dx, exchanged = _forward_backward(p, _prepare_weights(p), x, target, seq)

    out = {kind: {} for kind in "gdmv"}
    splits = {'w_in': 4, 'w_conv_out': 1, 'w_sgu_out': 1, 'w_o': 1, 'w_ffn_gate': 1, 'w_ffn_up': 1, 'w_ffn_down': 1, 'conv_w': 1}
    for n in splits:
        if n == 'conv_w':
            pad = ((0, 0), (0, CONV_TAPS_PADDED - CONV_TAPS), (0, 0))
            wl, ml, vl = (jnp.pad(a[n][:, :, 0, :], pad) for a in (p, m, v))
        elif n in ('w_ffn_gate', 'w_ffn_up'):
            wl, ml, vl = (_hidden_major(a[n]) for a in (p, m, v))
        else:
            wl, ml, vl = p[n], m[n], v[n]
        prev = None
        for l in range(nl):
            prev = _adamw_layer(l, wl, ml, vl, exchanged[(l, n)], prev, splits[n], f"adamw_{n}_{l}")
        for kind, arr in zip("gdmv", prev):
            if n == 'conv_w':
                arr = arr[:, 0:CONV_TAPS, None, :]
            elif n in ('w_ffn_gate', 'w_ffn_up'):
                arr = _hidden_major(arr)
            out[kind][n] = arr
    small_names = ['norm_mix', 'gate_bias', 'conv_b', 'conv_ln_g', 'conv_ln_b', 'sgu_ln_g', 'sgu_ln_b', 'w_spatial',
                   'b_spatial', 'norm_ffn', 'norm_final']

    def two_d(a):
        return a[None, :] if a.ndim == 1 else a

    res, loss_row = _adamw_small(small_gathered, {n: two_d(p[n]) for n in small_names},
                                 {n: two_d(m[n]) for n in small_names}, {n: two_d(v[n]) for n in small_names})
    loss = loss_row[0, 0]
    for kind in "gdmv":
        for n in small_names:
            out[kind][n] = res[kind][n].reshape(p[n].shape)
    grad_x = dx.reshape(bsz, seq, d)
    return (loss, grad_x, *[out[kind][n] for kind in "gdmv" for n in WEIGHT_NAMES])


def kernel(x, norm_mix, w_in, gate_bias, conv_w, conv_b, conv_ln_g, conv_ln_b, w_conv_out, sgu_ln_g, sgu_ln_b, w_spatial, b_spatial, w_sgu_out, w_o, norm_ffn, w_ffn_gate, w_ffn_up, w_ffn_down, norm_final, loss_target, m_norm_mix, m_w_in, m_gate_bias, m_conv_w, m_conv_b, m_conv_ln_g, m_conv_ln_b, m_w_conv_out, m_sgu_ln_g, m_sgu_ln_b, m_w_spatial, m_b_spatial, m_w_sgu_out, m_w_o, m_norm_ffn, m_w_ffn_gate, m_w_ffn_up, m_w_ffn_down, m_norm_final, v_norm_mix, v_w_in, v_gate_bias, v_conv_w, v_conv_b, v_conv_ln_g, v_conv_ln_b, v_w_conv_out, v_sgu_ln_g, v_sgu_ln_b, v_w_spatial, v_b_spatial, v_w_sgu_out, v_w_o, v_norm_ffn, v_w_ffn_gate, v_w_ffn_up, v_w_ffn_down, v_norm_final):
    p = dict(zip(WEIGHT_NAMES, (norm_mix, w_in, gate_bias, conv_w, conv_b, conv_ln_g, conv_ln_b, w_conv_out, sgu_ln_g, sgu_ln_b, w_spatial, b_spatial, w_sgu_out, w_o, norm_ffn, w_ffn_gate, w_ffn_up, w_ffn_down, norm_final)))
    m = dict(zip(WEIGHT_NAMES, (m_norm_mix, m_w_in, m_gate_bias, m_conv_w, m_conv_b, m_conv_ln_g, m_conv_ln_b, m_w_conv_out, m_sgu_ln_g, m_sgu_ln_b, m_w_spatial, m_b_spatial, m_w_sgu_out, m_w_o, m_norm_ffn, m_w_ffn_gate, m_w_ffn_up, m_w_ffn_down, m_norm_final)))
    v = dict(zip(WEIGHT_NAMES, (v_norm_mix, v_w_in, v_gate_bias, v_conv_w, v_conv_b, v_conv_ln_g, v_conv_ln_b, v_w_conv_out, v_sgu_ln_g, v_sgu_ln_b, v_w_spatial, v_b_spatial, v_w_sgu_out, v_w_o, v_norm_ffn, v_w_ffn_gate, v_w_ffn_up, v_w_ffn_down, v_norm_final)))
    return _train_step(p, m, v, x, loss_target)
```

```python
import math

import jax
import jax.numpy as jnp
from jax import lax
from jax.experimental import pallas as pl
from jax.experimental.pallas import tpu as pltpu

F32 = jnp.float32
BF16 = jnp.bfloat16
MESH_ID = pl.DeviceIdType.MESH

NDEV = 8
EPS = 1e-6
CONV_TAPS = 31
CONV_TAPS_PADDED = 32
HALO = 16
CONV_ROWS = 128
CONV_ROWS_LOOP = 64
LANES = 128
SUBLANES = 8
CHUNK = 128
GROUPS = 8
FF_PAD = 384
FF_PAIR = 2 * FF_PAD
TN_COLS = 512
VMEM_LIMIT_BYTES = 56 * 1024 * 1024

ADAM_LR = 0.001
ADAM_B1 = 0.9
ADAM_B2 = 0.999
ADAM_EPS = 1e-08
ADAM_WD = 0.01
ADAM_STEP = 10

TILE_IN = 512
TILE_IN_FWD = 1024
TILE_MIX = 256
TILE_MIX_FWD = 512
TILE_FFN_FWD = 1024
TILE_FFN_BWD = 512
TILE_TN = 1024
TILE_LOSS = 512

WEIGHT_NAMES = ['norm_mix', 'w_in', 'gate_bias', 'conv_w', 'conv_b', 'conv_ln_g', 'conv_ln_b', 'w_conv_out',
                'sgu_ln_g', 'sgu_ln_b', 'w_spatial', 'b_spatial', 'w_sgu_out', 'w_o', 'norm_ffn', 'w_ffn_gate',
                'w_ffn_up', 'w_ffn_down', 'norm_final']
MIXER_WEIGHTS = ['w_conv_out', 'w_sgu_out', 'w_o', 'conv_w']
FFN_WEIGHTS = ['w_ffn_gate', 'w_ffn_up', 'w_ffn_down']


def _sds(shape, dtype):
    return jax.ShapeDtypeStruct(tuple(shape), dtype)


def _params(*sem):
    return pltpu.CompilerParams(dimension_semantics=sem or None, vmem_limit_bytes=VMEM_LIMIT_BYTES)


def _nn(a, b):
    return jnp.dot(a, b, preferred_element_type=F32)


def _nt(a, b):
    return lax.dot_general(a, b, (((1,), (1,)), ((), ())), preferred_element_type=F32)


def _tn(a, b):
    return lax.dot_general(a, b, (((0,), (0,)), ((), ())), preferred_element_type=F32)


def _sig(v):
    return jax.nn.sigmoid(v)


def _fold(v):
    r, c = v.shape
    return jnp.sum(v.reshape(r // SUBLANES, SUBLANES, c), axis=0)


def _tile(tm, n, j=0):
    return pl.BlockSpec((tm, n), lambda i, *_: (i, j))


def _row(n):
    return pl.BlockSpec((1, n), lambda *_: (0, 0))


def _resident(shape):
    nd = len(shape)
    return pl.BlockSpec(tuple(shape), lambda *_: (0,) * nd)


def _weight(w):
    nd = w.ndim
    return pl.BlockSpec(tuple(w.shape), lambda *_: (0,) * nd, pipeline_mode=pl.Buffered(1))


def _peer(rel):
    x, y, c = lax.axis_index("x"), lax.axis_index("y"), lax.axis_index("c")
    return (1 - x if rel & 4 else x, 1 - y if rel & 2 else y, 1 - c if rel & 1 else c)


def _slot(pos):
    return 4 * pos[0] + 2 * pos[1] + pos[2]


class _Exchange:
    def __init__(self, arrays, layers=None):
        self.arrays = list(arrays)
        self.layers = list(layers) if layers is not None else [None] * len(self.arrays)

    def scratch(self):
        n = len(self.arrays)
        return [pltpu.SemaphoreType.DMA((n, NDEV)), pltpu.SemaphoreType.DMA((n, NDEV)), pltpu.SemaphoreType.DMA((n,))]

    def _src(self, ins, j):
        return ins[j] if self.layers[j] is None else ins[j].at[self.layers[j]]

    def _block_shape(self, j):
        a = self.arrays[j]
        return a.shape if self.layers[j] is None else a.shape[1:]


class _Gather(_Exchange):
    chips = (4, 2, 6)

    def out_shape(self):
        return [_sds((NDEV,) + tuple(self._block_shape(j)), a.dtype) for j, a in enumerate(self.arrays)]

    @staticmethod
    def _copy(outs, sems, j, sem, block_rel, to_rel, src=None):
        blk = outs[j].at[_slot(_peer(block_rel))]
        return pltpu.make_async_remote_copy(
            src_ref=blk if src is None else src, dst_ref=blk,
            send_sem=sems[0].at[j, sem], recv_sem=sems[1].at[j, sem],
            device_id=_peer(to_rel), device_id_type=MESH_ID)

    def _local(self, ins, outs, sems, j):
        return pltpu.make_async_copy(self._src(ins, j), outs[j].at[_slot(_peer(0))], sems[2].at[j])

    def start(self, ins, outs, sems):
        for j in range(len(self.arrays)):
            self._local(ins, outs, sems, j).start()
            for rel in (1,) + self.chips:
                self._copy(outs, sems, j, rel, 0, rel, src=self._src(ins, j)).start()

    def forward(self, ins, outs, sems):
        for j in range(len(self.arrays)):
            for rel in self.chips:
                self._copy(outs, sems, j, rel, rel, 0).wait_recv()
                self._copy(outs, sems, j, rel ^ 1, rel, 1).start()

    def finish(self, ins, outs, sems):
        for j in range(len(self.arrays)):
            self._copy(outs, sems, j, 1, 1, 0).wait_recv()
            for rel in self.chips:
                self._copy(outs, sems, j, rel ^ 1, rel ^ 1, 0).wait_recv()
        for j in range(len(self.arrays)):
            for rel in (1,) + self.chips:
                self._copy(outs, sems, j, rel, 0, rel, src=self._src(ins, j)).wait_send()
            for rel in self.chips:
                self._copy(outs, sems, j, rel ^ 1, rel, 1).wait_send()
            self._local(ins, outs, sems, j).wait()


class _Scatter(_Exchange):
    def out_shape(self):
        return [_sds(a.shape, a.dtype) for a in self.arrays]

    @staticmethod
    def _copy(ins, outs, sems, j, rel):
        return pltpu.make_async_remote_copy(
            src_ref=ins[j].at[_slot(_peer(rel))], dst_ref=outs[j].at[_slot(_peer(0))],
            send_sem=sems[0].at[j, rel], recv_sem=sems[1].at[j, rel],
            device_id=_peer(rel), device_id_type=MESH_ID)

    @staticmethod
    def _arrival(outs, sems, j, rel):
        blk = outs[j].at[_slot(_peer(rel))]
        return pltpu.make_async_remote_copy(
            src_ref=blk, dst_ref=blk, send_sem=sems[0].at[j, rel], recv_sem=sems[1].at[j, rel],
            device_id=_peer(rel), device_id_type=MESH_ID)

    @staticmethod
    def _local(ins, outs, sems, j):
        me = _slot(_peer(0))
        return pltpu.make_async_copy(ins[j].at[me], outs[j].at[me], sems[2].at[j])

    def start(self, ins, outs, sems):
        for j in range(len(self.arrays)):
            self._local(ins, outs, sems, j).start()
            for rel in range(1, NDEV):
                self._copy(ins, outs, sems, j, rel).start()

    def forward(self, ins, outs, sems):
        pass

    def finish(self, ins, outs, sems):
        for j in range(len(self.arrays)):
            for rel in range(1, NDEV):
                self._arrival(outs, sems, j, rel).wait_recv()
        for j in range(len(self.arrays)):
            for rel in range(1, NDEV):
                self._copy(ins, outs, sems, j, rel).wait_send()
            self._local(ins, outs, sems, j).wait()


class _Together:
    def __init__(self, parts):
        self.parts = [c for c in parts if c is not None]
        self.arrays = [a for c in self.parts for a in c.arrays]

    def out_shape(self):
        return [s for c in self.parts for s in c.out_shape()]

    def scratch(self):
        return [s for c in self.parts for s in c.scratch()]

    def _each(self, method, ins, outs, sems):
        at = 0
        for q, c in enumerate(self.parts):
            n = len(c.arrays)
            getattr(c, method)(ins[at:at + n], outs[at:at + n], sems[3 * q:3 * q + 3])
            at += n

    def start(self, ins, outs, sems):
        self._each("start", ins, outs, sems)

    def forward(self, ins, outs, sems):
        self._each("forward", ins, outs, sems)

    def finish(self, ins, outs, sems):
        self._each("finish", ins, outs, sems)


def _together(*parts):
    parts = [c for c in parts if c is not None]
    return _Together(parts) if parts else None


def _call(body, *, name, args, in_specs, out_specs, out_shape, grid=(), scratch_shapes=(), semantics=(),
          aliases=None, comm=None):
    in_specs, out_specs, out_shape = list(in_specs), list(out_specs), list(out_shape)
    scratch_shapes = list(scratch_shapes)
    if comm is None:
        res = pl.pallas_call(
            body, name=name, grid=grid, in_specs=in_specs, out_specs=out_specs, out_shape=out_shape,
            scratch_shapes=scratch_shapes, input_output_aliases=aliases or {},
            compiler_params=_params(*semantics))(*args)
        return list(res), []
    n_in, n_out, n_scr, nc = len(in_specs), len(out_specs), len(scratch_shapes), len(comm.arrays)
    total = math.prod(grid)
    middle = min((total * 5) // 8, total - 1)

    def hosted(*refs):
        ins, cins = refs[:n_in], refs[n_in:n_in + nc]
        o0 = n_in + nc
        outs, couts = refs[o0:o0 + n_out], refs[o0 + n_out:o0 + n_out + nc]
        s0 = o0 + n_out + nc
        scr, sems = refs[s0:s0 + n_scr], refs[s0 + n_scr:]
        if total == 1:
            comm.start(cins, couts, sems)
            body(*ins, *outs, *scr)
            comm.forward(cins, couts, sems)
            comm.finish(cins, couts, sems)
            return
        step = 0
        for axis, size in enumerate(grid):
            step = step * size + pl.program_id(axis)
        pl.when(step == 0)(lambda: comm.start(cins, couts, sems))
        body(*ins, *outs, *scr)
        pl.when(step == middle)(lambda: comm.forward(cins, couts, sems))
        pl.when(step == total - 1)(lambda: comm.finish(cins, couts, sems))

    any_spec = pl.BlockSpec(memory_space=pl.ANY)
    res = pl.pallas_call(
        hosted, name=name, grid=grid,
        in_specs=in_specs + [any_spec] * nc, out_specs=out_specs + [any_spec] * nc,
        out_shape=out_shape + comm.out_shape(), scratch_shapes=scratch_shapes + comm.scratch(),
        input_output_aliases=aliases or {}, compiler_params=_params(*(("arbitrary",) * len(grid))),
    )(*args, *comm.arrays)
    return list(res[:n_out]), list(res[n_out:])


def _exchange_alone(comm, name):
    return _call(lambda: None, name=name, args=(), in_specs=(), out_specs=(), out_shape=(), comm=comm)[1]


def _cast_pad(ws, shapes, name, comm):
    n = len(ws)
    nl = ws[0].shape[0]

    def body(*refs):
        for w, (rows, cols), w_ref, o_ref in zip(ws, shapes, refs[:n], refs[n:]):
            r, c = w.shape[1:]
            if (rows, cols) != (r, c):
                o_ref[...] = jnp.zeros(o_ref.shape, BF16)
            o_ref[0, 0:r, 0:c] = w_ref[0].astype(BF16)

    return _call(body, name=name, grid=(nl,), args=tuple(ws),
                 in_specs=[pl.BlockSpec((1,) + w.shape[1:], lambda i: (i, 0, 0)) for w in ws],
                 out_specs=[pl.BlockSpec((1,) + tuple(s), lambda i: (i, 0, 0)) for s in shapes],
                 out_shape=[_sds((nl,) + tuple(s), BF16) for s in shapes], semantics=("parallel",), comm=comm)


def _in_proj(x, gain, w, tm, name, comm):
    t, d = x.shape
    nb, _, bw = w.shape

    def body(x_ref, g_ref, w_ref, h_ref, p_ref):
        xv = x_ref[...]
        r = lax.rsqrt(jnp.mean(xv * xv, axis=-1, keepdims=True) + EPS)
        h = (xv * r * g_ref[...]).astype(BF16)
        h_ref[...] = h
        for k in range(nb):
            p_ref[:, k * bw:(k + 1) * bw] = _nn(h, w_ref[k]).astype(BF16)

    return _call(body, name=name, grid=(t // tm,), args=(x, gain, w),
                 in_specs=[_tile(tm, d), _row(d), _weight(w)],
                 out_specs=[_tile(tm, d), _tile(tm, nb * bw)],
                 out_shape=[_sds((t, d), BF16), _sds((t, nb * bw), BF16)],
                 semantics=("parallel",), comm=comm)


def _halo_specs(tm, t, d, col):
    nh, nhb = tm // HALO, t // HALO
    prev = pl.BlockSpec((HALO, d), lambda i: (jnp.maximum(i * nh - 1, 0), col))
    nxt = pl.BlockSpec((HALO, d), lambda i: (jnp.minimum((i + 1) * nh, nhb - 1), col))
    return prev, nxt


def _row_chunks(tm, rc, unroll, rows):
    n = tm // rc
    if unroll:
        for j in range(n):
            rows(j * rc)
    else:
        def step(j, carry):
            rows(pl.multiple_of(j * rc, rc))
            return carry

        lax.fori_loop(0, n, step, 0)


def _shifted(buf_ref, kb, r0, off, rc):
    return buf_ref[kb, pl.ds(r0 + off, rc), :]


def _dwconv(buf_ref, w_ref, out_ref, tm, flip, rc, unroll):
    nblk = out_ref.shape[1] // LANES

    def rows(r0):
        for kb in range(nblk):
            acc = jnp.zeros((rc, LANES), F32)
            for k in range(CONV_TAPS):
                off = (CONV_TAPS - k) if flip else (1 + k)
                acc = acc + w_ref[kb, k:k + 1, :] * _shifted(buf_ref, kb, r0, off, rc)
            out_ref[pl.ds(r0, rc), kb * LANES:(kb + 1) * LANES] = acc

    _row_chunks(tm, rc, unroll, rows)


def _fill_halo_buffer(buf, prev, body, nxt, first, last, tm):
    prev = jnp.where(first, 0.0, prev)
    nxt = jnp.where(last, 0.0, nxt)
    for kb in range(buf.shape[0]):
        lanes = slice(kb * LANES, (kb + 1) * LANES)
        buf[kb, 0:HALO, :] = prev[:, lanes]
        buf[kb, HALO:HALO + tm, :] = body[:, lanes]
        buf[kb, HALO + tm:HALO + tm + HALO, :] = nxt[:, lanes]


def _fill_glu_buffer(cbuf, av, ag, avp, agp, avn, agn, first, last, tm):
    c0p = avp[...].astype(F32) * _sig(agp[...].astype(F32))
    c0n = avn[...].astype(F32) * _sig(agn[...].astype(F32))
    c0 = av[...].astype(F32) * _sig(ag[...].astype(F32))
    _fill_halo_buffer(cbuf, c0p, c0, c0n, first, last, tm)


def _layernorm_stats(v):
    mu = jnp.mean(v, axis=-1, keepdims=True)
    cen = v - mu
    rstd = lax.rsqrt(jnp.mean(cen * cen, axis=-1, keepdims=True) + EPS)
    return cen * rstd, rstd


def _spatial_mix(ws_ref, vn_ref, bias_ref, mixed_ref, tm):
    for ci in range(tm // CHUNK):
        rs = slice(ci * CHUNK, (ci + 1) * CHUNK)
        for g in range(GROUPS):
            ls = slice(g * LANES, (g + 1) * LANES)
            mixed_ref[rs, ls] = _nn(ws_ref[g], vn_ref[rs, ls]) + bias_ref[:, ls]


def _mixer_fwd(proj, x, w, small, seq, tm, name, comm):
    t, d = x.shape
    hp_v, hn_v = _halo_specs(tm, t, d, 0)
    hp_g, hn_g = _halo_specs(tm, t, d, 1)

    def body(av, ag, u_ref, v_ref, ga_ref, gb_ref, avp, agp, avn, agn, x_ref,
             cw_ref, cb_ref, lg_ref, lb_ref, wco_ref, sg_ref, sb_ref, ws_ref, bias_ref, wso_ref,
             gba_ref, gbb_ref, wo_ref,
             c1_ref, ya_ref, yb_ref, x1_ref, cbuf, c1f, vn_ref, mixed_ref):
        i = pl.program_id(0)
        first = (i * tm) % seq == 0
        last = ((i + 1) * tm) % seq == 0
        _fill_glu_buffer(cbuf, av, ag, avp, agp, avn, agn, first, last, tm)
        _dwconv(cbuf, cw_ref, c1f, tm, flip=False, rc=CONV_ROWS_LOOP, unroll=False)
        c1 = c1f[...] + cb_ref[...]
        c1_ref[...] = c1.astype(BF16)
        c2hat, _ = _layernorm_stats(c1)
        c2 = c2hat * lg_ref[...] + lb_ref[...]
        c3 = (c2 * _sig(c2)).astype(BF16)
        ya = _nn(c3, wco_ref[...].reshape(d, d))
        ya_ref[...] = ya.astype(BF16)
        vhat, _ = _layernorm_stats(v_ref[...].astype(F32))
        vn_ref[...] = (vhat * sg_ref[...] + sb_ref[...]).astype(BF16)
        _spatial_mix(ws_ref, vn_ref, bias_ref, mixed_ref, tm)
        gated = (u_ref[...].astype(F32) * mixed_ref[...]).astype(BF16)
        yb = _nn(gated, wso_ref[...].reshape(d, d))
        yb_ref[...] = yb.astype(BF16)
        sa = _sig(ga_ref[...].astype(F32) + gba_ref[...])
        sb = _sig(gb_ref[...].astype(F32) + gbb_ref[...])
        merged = (sa * ya + sb * yb).astype(BF16)
        x1_ref[...] = x_ref[...] + _nn(merged, wo_ref[...].reshape(d, d))

    cols = [_tile(tm, d, j) for j in range(6)]
    return _call(
        body, name=name, grid=(t // tm,),
        args=(proj,) * 10 + (x, w['conv_w'], small['conv_b'], small['conv_ln_g'], small['conv_ln_b'], w['w_conv_out'],
                             small['sgu_ln_g'], small['sgu_ln_b'], small['ws'], small['bias_full'], w['w_sgu_out'],
                             small['gba'], small['gbb'], w['w_o']),
        in_specs=cols + [hp_v, hp_g, hn_v, hn_g, _tile(tm, d),
                         _weight(w['conv_w']), _row(d), _row(d), _row(d),
                         _weight(w['w_conv_out']), _row(d), _row(d),
                         _resident(small['ws'].shape), _resident(small['bias_full'].shape),
                         _weight(w['w_sgu_out']), _row(d), _row(d), _weight(w['w_o'])],
        out_specs=[_tile(tm, d)] * 4,
        out_shape=[_sds((t, d), BF16)] * 3 + [_sds((t, d), F32)],
        scratch_shapes=[pltpu.VMEM((d // LANES, tm + 2 * HALO, LANES), F32), pltpu.VMEM((tm, d), F32),
                        pltpu.VMEM((tm, d), BF16), pltpu.VMEM((tm, d), F32)],
        semantics=("parallel",), comm=comm)


def _pair_gate_up(wg, wu, name):
    n, c, d = wg.shape

    def body(wg_ref, wu_ref, o_ref):
        o_ref[0:c, :] = wg_ref[0]
        o_ref[c:2 * c, :] = wg_ref[1]
        o_ref[2 * c:3 * c, :] = wu_ref[0]
        o_ref[3 * c:4 * c, :] = wu_ref[1]

    pair = pl.BlockSpec((2, c, d), lambda k: (k, 0, 0))
    return _call(body, name=name, grid=(n // 2,), args=(wg, wu), in_specs=[pair, pair],
                 out_specs=[pl.BlockSpec((None, 4 * c, d), lambda k: (k, 0, 0))],
                 out_shape=[_sds((n // 2, 4 * c, d), wg.dtype)], semantics=("parallel",))[0][0]


def _pair_specs(w, tm):
    d = w['w_gate_up'].shape[2]
    up = pl.BlockSpec((None, 2 * FF_PAIR, d), lambda i, k: (k, 0, 0))
    down = pl.BlockSpec((None, FF_PAIR, d), lambda i, k: (k, 0, 0))
    return [up, down]


def _ffn_fwd(x1, gain, w, tm, name, comm):
    t, d = x1.shape
    tm = min(tm, t)
    npair = NDEV // 2
    hid = NDEV * FF_PAD
    wd_pairs = w['w_ffn_down'].reshape(npair, FF_PAIR, d)

    def body(x_ref, g_ref, wgu_ref, wd_ref, h_ref, gg_ref, uu_ref, x2_ref, hb_ref, acc_ref):
        k = pl.program_id(1)

        @pl.when(k == 0)
        def _():
            xv = x_ref[...]
            r = lax.rsqrt(jnp.mean(xv * xv, axis=-1, keepdims=True) + EPS)
            h = (xv * r * g_ref[...]).astype(BF16)
            hb_ref[...] = h
            h_ref[...] = h
            acc_ref[...] = xv

        gu = _nt(hb_ref[...], wgu_ref[...])
        gk = gu[:, 0:FF_PAIR]
        uk = gu[:, FF_PAIR:2 * FF_PAIR]
        gg_ref[...] = gk.astype(BF16)
        uu_ref[...] = uk.astype(BF16)
        ak = (gk * _sig(gk) * uk).astype(BF16)
        acc_ref[...] += _nn(ak, wd_ref[...])

        @pl.when(k == npair - 1)
        def _():
            x2_ref[...] = acc_ref[...]

    pair_cols = pl.BlockSpec((tm, FF_PAIR), lambda i, k: (i, k))
    return _call(
        body, name=name, grid=(t // tm, npair),
        args=(x1, gain, w['w_gate_up'], wd_pairs),
        in_specs=[_tile(tm, d), _row(d)] + _pair_specs(w, tm),
        out_specs=[_tile(tm, d), pair_cols, pair_cols, _tile(tm, d)],
        out_shape=[_sds((t, d), BF16), _sds((t, hid), BF16), _sds((t, hid), BF16), _sds((t, d), F32)],
        scratch_shapes=[pltpu.VMEM((tm, d), BF16), pltpu.VMEM((tm, d), F32)],
        semantics=("parallel", "arbitrary"), comm=comm)


def _init_small(first, small_ref, acc_ref):
    @pl.when(first)
    def _():
        small_ref[...] = jnp.zeros(small_ref.shape, F32)
        acc_ref[...] = jnp.zeros(acc_ref.shape, F32)


def _finish_small(last, small_ref, acc_ref, nq):
    @pl.when(last)
    def _():
        for q in range(nq):
            small_ref[q:q + 1, :] = jnp.sum(acc_ref[q], axis=0, keepdims=True)


def _loss_bwd(x, gain, target, tm, name):
    t, d = x.shape
    tm = min(tm, t)
    nsteps = t // tm

    def body(x_ref, g_ref, t_ref, dx_ref, small_ref, acc_ref):
        i = pl.program_id(0)
        _init_small(i == 0, small_ref, acc_ref)
        xv = x_ref[...]
        r = lax.rsqrt(jnp.mean(xv * xv, axis=-1, keepdims=True) + EPS)
        xhat = xv * r
        diff = xhat * g_ref[...] - t_ref[...]
        dy = diff * (1.0 / d)
        acc_ref[0] += _fold(dy * xhat)
        acc_ref[1] += _fold(diff * diff)
        dxhat = dy * g_ref[...]
        dx_ref[...] = r * (dxhat - xhat * jnp.mean(dxhat * xhat, axis=-1, keepdims=True))

        @pl.when(i == nsteps - 1)
        def _():
            small_ref[0:1, :] = jnp.sum(acc_ref[0], axis=0, keepdims=True)
            small_ref[1:2, :] = jnp.full((1, d), jnp.sum(acc_ref[1]) * (0.5 / d), F32)

    return _call(body, name=name, grid=(nsteps,), args=(x, gain, target),
                 in_specs=[_tile(tm, d), _row(d), _tile(tm, d)],
                 out_specs=[_tile(tm, d), _resident((SUBLANES, d))],
                 out_shape=[_sds((t, d), F32), _sds((SUBLANES, d), F32)],
                 scratch_shapes=[pltpu.VMEM((2, SUBLANES, d), F32)], semantics=("arbitrary",))[0]


def _ffn_bwd(dx2, x1, gain, gg, uu, w, tm, name, comm):
    t, d = x1.shape
    tm = min(tm, t)
    npair = NDEV // 2
    hid = NDEV * FF_PAD
    nsteps = t // tm
    wd_pairs = w['w_ffn_down'].reshape(npair, FF_PAIR, d)

    def body(dx_ref, x_ref, g_ref, gg_ref, uu_ref, wgu_ref, wd_ref,
             a_ref, dg_ref, du_ref, dx1_ref, small_ref, acc_ref, dxb_ref, dh_ref, dgu_ref):
        i, k = pl.program_id(0), pl.program_id(1)
        _init_small((i == 0) & (k == 0), small_ref, acc_ref)

        @pl.when(k == 0)
        def _():
            dxb_ref[...] = dx_ref[...].astype(BF16)
            dh_ref[...] = jnp.zeros(dh_ref.shape, F32)

        gk = gg_ref[...].astype(F32)
        uk = uu_ref[...].astype(F32)
        sg = _sig(gk)
        silu = gk * sg
        a_ref[...] = (silu * uk).astype(BF16)
        da = _nt(dxb_ref[...], wd_ref[...])
        dgk = (da * uk * (sg * (1.0 + gk * (1.0 - sg)))).astype(BF16)
        duk = (da * silu).astype(BF16)
        dg_ref[...] = dgk
        du_ref[...] = duk
        dgu_ref[:, 0:FF_PAIR] = dgk
        dgu_ref[:, FF_PAIR:2 * FF_PAIR] = duk
        dh_ref[...] += _nn(dgu_ref[...], wgu_ref[...])

        @pl.when(k == npair - 1)
        def _():
            xv = x_ref[...]
            dh = dh_ref[...]
            r = lax.rsqrt(jnp.mean(xv * xv, axis=-1, keepdims=True) + EPS)
            xhat = xv * r
            acc_ref[0] += _fold(dh * xhat)
            dxhat = dh * g_ref[...]
            dx1_ref[...] = dx_ref[...] + r * (dxhat - xhat * jnp.mean(dxhat * xhat, axis=-1, keepdims=True))

        _finish_small((i == nsteps - 1) & (k == npair - 1), small_ref, acc_ref, 1)

    pair_cols = pl.BlockSpec((tm, FF_PAIR), lambda i, k: (i, k))
    return _call(
        body, name=name, grid=(nsteps, npair),
        args=(dx2, x1, gain, gg, uu, w['w_gate_up'], wd_pairs),
        in_specs=[_tile(tm, d), _tile(tm, d), _row(d), pair_cols, pair_cols] + _pair_specs(w, tm),
        out_specs=[pair_cols] * 3 + [_tile(tm, d), _resident((SUBLANES, d))],
        out_shape=[_sds((t, hid), BF16)] * 3 + [_sds((t, d), F32), _sds((SUBLANES, d), F32)],
        scratch_shapes=[pltpu.VMEM((1, SUBLANES, d), F32), pltpu.VMEM((tm, d), BF16), pltpu.VMEM((tm, d), F32),
                        pltpu.VMEM((tm, 2 * FF_PAIR), BF16)],
        semantics=("arbitrary", "arbitrary"), comm=comm)


def _matmul_tn(a, b, a_blk, b_blk, stack, shard, tm, name, comm):
    t, ma = a.shape
    tm = min(tm, t)
    nb_ = b.shape[1]
    na, nb = ma // a_blk, nb_ // b_blk
    nsteps = t // tm
    cw = min(TN_COLS, b_blk)
    if stack == 'b':
        per = b_blk // shard
        out_shape, out_spec = (nb_ // shard, ma, shard), pl.BlockSpec((per, a_blk, shard), lambda i, j, k: (j, 0, 0))
    elif stack == 'a':
        per = a_blk // shard
        out_shape, out_spec = (ma // shard, shard, nb_), pl.BlockSpec((per, shard, b_blk), lambda i, j, k: (i, 0, 0))
    else:
        out_shape, out_spec = (ma, nb_), pl.BlockSpec((a_blk, b_blk), lambda i, j, k: (i, j))

    def body(a_ref, b_ref, o_ref, acc_ref):
        k = pl.program_id(2)

        @pl.when(k == 0)
        def _():
            acc_ref[...] = jnp.zeros(acc_ref.shape, F32)

        av = a_ref[...].astype(BF16)
        for c in range(0, b_blk, cw):
            acc_ref[:, c:c + cw] += _tn(av, b_ref[:, c:c + cw].astype(BF16))

        @pl.when(k == nsteps - 1)
        def _():
            if stack == 'b':
                for s in range(per):
                    o_ref[s] = acc_ref[:, s * shard:(s + 1) * shard].astype(BF16)
            elif stack == 'a':
                for s in range(per):
                    o_ref[s] = acc_ref[s * shard:(s + 1) * shard, :].astype(BF16)
            else:
                o_ref[...] = acc_ref[...].astype(BF16)

    res, got = _call(
        body, name=name, grid=(na, nb, nsteps), args=(a, b),
        in_specs=[pl.BlockSpec((tm, a_blk), lambda i, j, k: (k, i)), pl.BlockSpec((tm, b_blk), lambda i, j, k: (k, j))],
        out_specs=[out_spec], out_shape=[_sds(out_shape, BF16)],
        scratch_shapes=[pltpu.VMEM((a_blk, b_blk), F32)],
        semantics=("parallel", "parallel", "arbitrary"), comm=comm)
    return res[0], got


def _mixer_bwd(dx1, proj, c1, ya, yb, w, small, tm, name, comm):
    t, d = dx1.shape
    nsteps = t // tm
    nq = 6

    def body(dx_ref, u_ref, v_ref, ga_ref, gb_ref, c1_ref, ya_ref, yb_ref,
             lg_ref, lb_ref, wco_ref, sg_ref, sb_ref, ws_ref, wst_ref, bias_ref, wso_ref, gba_ref, gbb_ref, wo_ref,
             sel_ref,
             dp_ref, dc1_ref, mg_ref, c3_ref, gt_ref, dya_ref, dyb_ref, small_ref, dws_ref, dbs_ref,
             acc_ref, vn_ref, mixed_ref, dmix_ref, dvn_ref, dbias_ref):
        i = pl.program_id(0)
        _init_small(i == 0, small_ref, acc_ref)

        @pl.when(i == 0)
        def _():
            dws_ref[...] = jnp.zeros(dws_ref.shape, F32)
            dbs_ref[...] = jnp.zeros(dbs_ref.shape, F32)
            dbias_ref[...] = jnp.zeros(dbias_ref.shape, F32)

        dmerged = _nt(dx_ref[...].astype(BF16), wo_ref[...].reshape(d, d))
        ya = ya_ref[...].astype(F32)
        yb = yb_ref[...].astype(F32)
        sa = _sig(ga_ref[...].astype(F32) + gba_ref[...])
        sb = _sig(gb_ref[...].astype(F32) + gbb_ref[...])
        mg_ref[...] = (sa * ya + sb * yb).astype(BF16)
        dya = (dmerged * sa).astype(BF16)
        dyb = (dmerged * sb).astype(BF16)
        dya_ref[...] = dya
        dyb_ref[...] = dyb
        dga = dmerged * ya * (sa * (1.0 - sa))
        dgb = dmerged * yb * (sb * (1.0 - sb))
        acc_ref[0] += _fold(dga)
        acc_ref[1] += _fold(dgb)
        dp_ref[:, 0:2 * d] = jnp.zeros((tm, 2 * d), BF16)
        dp_ref[:, 4 * d:5 * d] = dga.astype(BF16)
        dp_ref[:, 5 * d:6 * d] = dgb.astype(BF16)
        c2hat, rstd = _layernorm_stats(c1_ref[...].astype(F32))
        c2 = c2hat * lg_ref[...] + lb_ref[...]
        s2 = _sig(c2)
        c3_ref[...] = (c2 * s2).astype(BF16)
        dc3 = _nt(dya, wco_ref[...].reshape(d, d))
        dc2 = dc3 * (s2 * (1.0 + c2 * (1.0 - s2)))
        acc_ref[2] += _fold(dc2 * c2hat)
        acc_ref[3] += _fold(dc2)
        dc2hat = dc2 * lg_ref[...]
        dc1_ref[...] = (rstd * (dc2hat - jnp.mean(dc2hat, axis=-1, keepdims=True)
                                - c2hat * jnp.mean(dc2hat * c2hat, axis=-1, keepdims=True))).astype(BF16)
        vhat, rstd_v = _layernorm_stats(v_ref[...].astype(F32))
        vn_ref[...] = (vhat * sg_ref[...] + sb_ref[...]).astype(BF16)
        _spatial_mix(ws_ref, vn_ref, bias_ref, mixed_ref, tm)
        u = u_ref[...].astype(F32)
        mixed = mixed_ref[...]
        gt_ref[...] = (u * mixed).astype(BF16)
        dgated = _nt(dyb, wso_ref[...].reshape(d, d))
        dp_ref[:, 2 * d:3 * d] = (dgated * mixed).astype(BF16)
        dmix_ref[...] = dgated * u
        for ci in range(tm // CHUNK):
            rs = slice(ci * CHUNK, (ci + 1) * CHUNK)
            dbias_ref[...] += dmix_ref[rs, :]
            for g in range(GROUPS):
                ls = slice(g * LANES, (g + 1) * LANES)
                dm = dmix_ref[rs, ls].astype(BF16)
                dws_ref[g] += _nt(dm, vn_ref[rs, ls])
                dvn_ref[rs, ls] = _nn(wst_ref[g], dm)
        dvn = dvn_ref[...]
        acc_ref[4] += _fold(dvn * vhat)
        acc_ref[5] += _fold(dvn)
        dvhat = dvn * sg_ref[...]
        dp_ref[:, 3 * d:4 * d] = (rstd_v * (dvhat - jnp.mean(dvhat, axis=-1, keepdims=True)
                                           - vhat * jnp.mean(dvhat * vhat, axis=-1, keepdims=True))).astype(BF16)
        _finish_small(i == nsteps - 1, small_ref, acc_ref, nq)

        @pl.when(i == nsteps - 1)
        def _():
            db = dbias_ref[...]
            hi = db.astype(BF16)
            lo = (db - hi.astype(F32)).astype(BF16)
            dbs_ref[...] = _nt(sel_ref[...], hi) + _nt(sel_ref[...], lo)

    cols = [_tile(tm, d, j) for j in (2, 3, 4, 5)]
    return _call(
        body, name=name, grid=(nsteps,),
        args=(dx1, proj, proj, proj, proj, c1, ya, yb,
              small['conv_ln_g'], small['conv_ln_b'], w['w_conv_out'], small['sgu_ln_g'], small['sgu_ln_b'],
              small['ws'], small['wst'], small['bias_full'], w['w_sgu_out'], small['gba'], small['gbb'], w['w_o'],
              small['group_sel']),
        in_specs=[_tile(tm, d)] + cols + [_tile(tm, d)] * 3 + [
            _row(d), _row(d), _weight(w['w_conv_out']),
            _row(d), _row(d), _resident(small['ws'].shape), _resident(small['wst'].shape),
            _resident(small['bias_full'].shape), _weight(w['w_sgu_out']), _row(d), _row(d),
            _weight(w['w_o']), _resident(small['group_sel'].shape)],
        out_specs=[_tile(tm, 6 * d)] + [_tile(tm, d)] * 6 + [
            _resident((SUBLANES, d)), _resident((GROUPS, CHUNK, CHUNK)), _resident((GROUPS, CHUNK))],
        out_shape=[_sds((t, 6 * d), BF16)] + [_sds((t, d), BF16)] * 6 + [
            _sds((SUBLANES, d), F32), _sds((GROUPS, CHUNK, CHUNK), F32), _sds((GROUPS, CHUNK), F32)],
        scratch_shapes=[pltpu.VMEM((nq, SUBLANES, d), F32), pltpu.VMEM((tm, d), BF16), pltpu.VMEM((tm, d), F32),
                        pltpu.VMEM((tm, d), F32), pltpu.VMEM((tm, d), F32), pltpu.VMEM((CHUNK, d), F32)],
        semantics=("arbitrary",), comm=comm)


def _conv_bwd(dproj, dc1, proj, w, pairs, seq, tm, name, comm):
    t, d = dc1.shape
    nsteps = t // tm
    nblk = d // LANES
    npairs = len(pairs)
    hp_v, hn_v = _halo_specs(tm, t, d, 0)
    hp_g, hn_g = _halo_specs(tm, t, d, 1)
    hp_d, hn_d = _halo_specs(tm, t, d, 0)

    def body(*refs):
        (dp_in, dc_ref, dcp, dcn, av, ag, avp, agp, avn, agn, cw_ref), refs = refs[:11], refs[11:]
        ab_refs, refs = refs[:2 * npairs], refs[2 * npairs:]
        (dp_ref, dcw_ref, small_ref), refs = refs[:3], refs[3:]
        grad_refs, refs = refs[:npairs], refs[npairs:]
        acc_ref, cbuf, dbuf, dc0f, accw, gacc = refs
        del dp_in
        i = pl.program_id(0)
        _init_small(i == 0, small_ref, acc_ref)

        @pl.when(i == 0)
        def _():
            accw[...] = jnp.zeros(accw.shape, F32)
            dcw_ref[...] = jnp.zeros(dcw_ref.shape, F32)
            gacc[...] = jnp.zeros(gacc.shape, F32)

        first = (i * tm) % seq == 0
        last = ((i + 1) * tm) % seq == 0
        _fill_glu_buffer(cbuf, av, ag, avp, agp, avn, agn, first, last, tm)
        dc1v = dc_ref[...].astype(F32)
        _fill_halo_buffer(dbuf, dcp[...].astype(F32), dc1v, dcn[...].astype(F32), first, last, tm)
        acc_ref[0] += _fold(dc1v)
        _dwconv(dbuf, cw_ref, dc0f, tm, flip=True, rc=CONV_ROWS, unroll=True)

        def rows(r0):
            for kb in range(nblk):
                dv = dbuf[kb, r0 + HALO:r0 + HALO + CONV_ROWS, :]
                for k in range(CONV_TAPS):
                    accw[kb, k] += _fold(dv * _shifted(cbuf, kb, r0, 1 + k, CONV_ROWS))

        _row_chunks(tm, CONV_ROWS, True, rows)
        for q in range(npairs):
            aq = ab_refs[2 * q][...].astype(BF16)
            for c in range(0, d, TN_COLS):
                gacc[q, :, c:c + TN_COLS] += _tn(aq, ab_refs[2 * q + 1][:, c:c + TN_COLS].astype(BF16))
        sg = _sig(ag[...].astype(F32))
        avv = av[...].astype(F32)
        dc0 = dc0f[...]
        dp_ref[:, 0:d] = (dc0 * sg).astype(BF16)
        dp_ref[:, d:2 * d] = (dc0 * avv * (sg * (1.0 - sg))).astype(BF16)
        _finish_small(i == nsteps - 1, small_ref, acc_ref, 1)

        @pl.when(i == nsteps - 1)
        def _():
            for kb in range(nblk):
                dcw_ref[kb] = jnp.sum(accw[kb], axis=1)
            for q in range(npairs):
                grad_refs[q][...] = gacc[q].astype(BF16)

    return _call(
        body, name=name, grid=(nsteps,),
        args=(dproj, dc1, dc1, dc1, proj, proj, proj, proj, proj, proj, w['conv_w']) + tuple(a for ab in pairs for a in ab),
        in_specs=[pl.BlockSpec(memory_space=pl.ANY), _tile(tm, d), hp_d, hn_d, _tile(tm, d, 0), _tile(tm, d, 1),
                  hp_v, hp_g, hn_v, hn_g, _weight(w['conv_w'])] + [_tile(tm, d)] * (2 * npairs),
        out_specs=[_tile(tm, 2 * d), _resident((nblk, CONV_TAPS_PADDED, LANES)), _resident((SUBLANES, d))]
        + [_resident((d, d))] * npairs,
        out_shape=[_sds(dproj.shape, BF16), _sds((nblk, CONV_TAPS_PADDED, LANES), F32), _sds((SUBLANES, d), F32)]
        + [_sds((d, d), BF16)] * npairs,
        scratch_shapes=[pltpu.VMEM((1, SUBLANES, d), F32), pltpu.VMEM((nblk, tm + 2 * HALO, LANES), F32),
                        pltpu.VMEM((nblk, tm + 2 * HALO, LANES), F32), pltpu.VMEM((tm, d), F32),
                        pltpu.VMEM((nblk, CONV_TAPS_PADDED, SUBLANES, LANES), F32),
                        pltpu.VMEM((npairs, d, d), F32)],
        aliases={0: 0}, semantics=("arbitrary",), comm=comm)


def _in_proj_bwd(dproj, x, dx1, gain, w, tm, name, comm):
    t, d = x.shape
    nb, _, bw = w.shape
    nsteps = t // tm

    def body(dp_ref, x_ref, dx1_ref, g_ref, w_ref, dx_ref, small_ref, acc_ref):
        i = pl.program_id(0)
        _init_small(i == 0, small_ref, acc_ref)
        dh = jnp.zeros((tm, d), F32)
        for k in range(nb):
            dh = dh + _nt(dp_ref[:, k * bw:(k + 1) * bw], w_ref[k])
        xv = x_ref[...]
        r = lax.rsqrt(jnp.mean(xv * xv, axis=-1, keepdims=True) + EPS)
        xhat = xv * r
        acc_ref[0] += _fold(dh * xhat)
        dxhat = dh * g_ref[...]
        dx_ref[...] = dx1_ref[...] + r * (dxhat - xhat * jnp.mean(dxhat * xhat, axis=-1, keepdims=True))
        _finish_small(i == nsteps - 1, small_ref, acc_ref, 1)

    return _call(body, name=name, grid=(nsteps,), args=(dproj, x, dx1, gain, w),
                 in_specs=[_tile(tm, nb * bw), _tile(tm, d), _tile(tm, d), _row(d), _weight(w)],
                 out_specs=[_tile(tm, d), _resident((SUBLANES, d))],
                 out_shape=[_sds((t, d), F32), _sds((SUBLANES, d), F32)],
                 scratch_shapes=[pltpu.VMEM((1, SUBLANES, d), F32)], semantics=("arbitrary",), comm=comm)


def _adam(wv, g, mv, vv):
    m = ADAM_B1 * mv + (1.0 - ADAM_B1) * g
    v = ADAM_B2 * vv + (1.0 - ADAM_B2) * jnp.square(g)
    m_hat = m / (1.0 - ADAM_B1 ** ADAM_STEP)
    v_hat = v / (1.0 - ADAM_B2 ** ADAM_STEP)
    delta = -ADAM_LR * (m_hat / (jnp.sqrt(v_hat) + ADAM_EPS) + ADAM_WD * wv)
    return delta, m, v


def _adamw_layer(layer, w, m, v, parts, prev, nsplit, name):
    nl, r, c = w.shape
    npart, pr, pc = parts.shape
    rt, prt = r // nsplit, pr // nsplit

    def body(w_ref, m_ref, v_ref, p_ref, *rest):
        g_ref, d_ref, nm_ref, nv_ref = rest[-4:]
        g = p_ref[0, 0:rt, 0:c].astype(F32)
        for s in range(1, npart):
            g = g + p_ref[s, 0:rt, 0:c].astype(F32)
        delta, mn, vn = _adam(w_ref[0], g, m_ref[0], v_ref[0])
        g_ref[0] = g
        d_ref[0] = delta
        nm_ref[0] = mn
        nv_ref[0] = vn

    wspec = pl.BlockSpec((1, rt, c), lambda i: (layer, i, 0))
    pspec = pl.BlockSpec((npart, prt, pc), lambda i: (0, i, 0))
    in_specs = [wspec, wspec, wspec, pspec]
    args = [w, m, v, parts]
    aliases = {}
    if prev is not None:
        in_specs += [pl.BlockSpec(memory_space=pl.ANY)] * 4
        args += list(prev)
        aliases = {4 + q: q for q in range(4)}
    return _call(body, name=name, grid=(nsplit,), args=args, in_specs=in_specs, out_specs=[wspec] * 4,
                 out_shape=[_sds(w.shape, F32)] * 4, aliases=aliases, semantics=("parallel",))[0]


VEC_ROWS = {'gate_bias': (0, 1), 'conv_ln_g': 2, 'conv_ln_b': 3, 'sgu_ln_g': 4, 'sgu_ln_b': 5,
            'conv_b': SUBLANES, 'norm_ffn': 2 * SUBLANES}
FINAL_ROW = 3 * SUBLANES
LOSS_ROW = 3 * SUBLANES + 1


def _adamw_small(gathered, params, moments_m, moments_v):
    names = list(params)
    nper = len(names)
    nl = len(gathered)

    def body(*refs):
        g_refs = [refs[4 * l:4 * l + 4] for l in range(nl)]
        rest = refs[4 * nl:]
        w_refs = dict(zip(names, rest[0:nper]))
        m_refs = dict(zip(names, rest[nper:2 * nper]))
        v_refs = dict(zip(names, rest[2 * nper:3 * nper]))
        outs = rest[3 * nper:]
        loss_ref = outs[4 * nper]
        o = {kind: dict(zip(names, outs[q * nper:(q + 1) * nper])) for q, kind in enumerate("gdmv")}

        def put(nm, idx, g):
            delta, mn, vn = _adam(w_refs[nm][idx], g, m_refs[nm][idx], v_refs[nm][idx])
            o["g"][nm][idx] = g
            o["d"][nm][idx] = delta
            o["m"][nm][idx] = mn
            o["v"][nm][idx] = vn

        def total(ref, *idx):
            g = ref[(0, *idx)]
            for s in range(1, NDEV):
                g = g + ref[(s, *idx)]
            return g

        for l, (vec_ref, dws_ref, dbs_ref, vin_ref) in enumerate(g_refs):
            dd = vec_ref.shape[2]
            put('w_spatial', (l,), total(dws_ref))
            put('b_spatial', (l,), total(dbs_ref))
            put('norm_mix', (slice(l, l + 1), slice(None)), total(vin_ref, slice(0, 1)))
            for nm, rr in VEC_ROWS.items():
                if nm == 'gate_bias':
                    put(nm, (slice(l, l + 1), slice(0, dd)), total(vec_ref, slice(rr[0], rr[0] + 1)))
                    put(nm, (slice(l, l + 1), slice(dd, 2 * dd)), total(vec_ref, slice(rr[1], rr[1] + 1)))
                else:
                    put(nm, (slice(l, l + 1), slice(None)), total(vec_ref, slice(rr, rr + 1)))
        last = g_refs[nl - 1][0]
        put('norm_final', (slice(0, 1), slice(None)), total(last, slice(FINAL_ROW, FINAL_ROW + 1)))
        loss_ref[...] = total(last, slice(LOSS_ROW, LOSS_ROW + 1))

    ins = [a for g in gathered for a in g] + [params[n] for n in names] + [moments_m[n] for n in names] + [moments_v[n] for n in names]
    out_shape = [_sds(params[n].shape, F32) for n in names] * 4 + [_sds((1, gathered[0][0].shape[2]), F32)]
    res = pl.pallas_call(body, name="adamw_small", out_shape=out_shape, compiler_params=_params())(*ins)
    return {kind: dict(zip(names, res[q * nper:(q + 1) * nper])) for q, kind in enumerate("gdmv")}, res[4 * nper]


def _hidden_major(a):
    return jnp.swapaxes(a, 1, 2)


def _prepare_weights(p):
    d = p['w_in'].shape[1]
    (w_in,), _ = _cast_pad([p['w_in']], [(d, p['w_in'].shape[2])], "cast_w_in", None)
    rows = p['w_o'].shape[1]
    others = {'w_conv_out': (p['w_conv_out'], (rows, d)), 'w_sgu_out': (p['w_sgu_out'], (rows, d)),
              'w_o': (p['w_o'], (rows, d)),
              'w_ffn_gate': (_hidden_major(p['w_ffn_gate']), (FF_PAD, d)),
              'w_ffn_up': (_hidden_major(p['w_ffn_up']), (FF_PAD, d)),
              'w_ffn_down': (p['w_ffn_down'], (FF_PAD, d))}
    cast, got = _cast_pad([a for a, _ in others.values()], [s for _, s in others.values()], "cast_weights",
                          _Gather([w_in], [0]))
    shards = dict(zip(others, cast), w_in=w_in)
    shards['conv_w'] = jnp.pad(p['conv_w'][:, :, 0, :], ((0, 0), (0, CONV_TAPS_PADDED - CONV_TAPS), (0, 0)))
    return shards, got[0]


def _gather_of(shards, names, layer):
    return _Gather([shards[n] for n in names], [layer] * len(names))


def _layer_small(p, layer):
    d = p['norm_mix'].shape[1]
    ws = p['w_spatial'][layer]
    rows = {n: p[n][layer:layer + 1] for n in ('norm_mix', 'norm_ffn', 'conv_b', 'conv_ln_g', 'conv_ln_b',
                                               'sgu_ln_g', 'sgu_ln_b')}
    return {
        **rows,
        'ws': ws.astype(BF16), 'wst': jnp.swapaxes(ws, 1, 2).astype(BF16),
        'bias_full': jnp.repeat(p['b_spatial'][layer].T, LANES, axis=1),
        'gba': p['gate_bias'][layer:layer + 1, 0:d], 'gbb': p['gate_bias'][layer:layer + 1, d:2 * d],
        'group_sel': (jnp.arange(d)[None, :] // LANES == jnp.arange(GROUPS)[:, None]).astype(BF16),
    }


class _GradQueue:
    def __init__(self):
        self.pending = []
        self.done = {}

    def push(self, key, array):
        self.pending.append((key, array))

    def take(self):
        keys = [k for k, _ in self.pending]
        comm = _Scatter([a for _, a in self.pending]) if self.pending else None
        self.pending = []
        return keys, comm

    def put(self, keys, arrays):
        self.done.update(zip(keys, arrays))


def _forward_backward(p, shards, w_in, x, target, seq):
    nl = p['norm_mix'].shape[0]
    d = x.shape[1]
    smalls = [_layer_small(p, l) for l in range(nl)]
    saved = []
    for l in range(nl):
        (h, proj), got = _in_proj(x, smalls[l]['norm_mix'], w_in, min(TILE_IN_FWD, x.shape[0]), f"in_proj_{l}",
                                  _gather_of(shards, MIXER_WEIGHTS, l))
        w = dict(zip(MIXER_WEIGHTS, got), w_in=w_in)
        (c1, ya, yb, x1), got = _mixer_fwd(proj, x, w, smalls[l], seq, min(TILE_MIX_FWD, seq), f"mixer_fwd_{l}",
                                           _gather_of(shards, FFN_WEIGHTS, l))
        w.update(zip(FFN_WEIGHTS, got))
        w['w_gate_up'] = _pair_gate_up(w['w_ffn_gate'], w['w_ffn_up'], f"pair_gate_up_{l}")
        nxt = _gather_of(shards, ['w_in'], l + 1) if l + 1 < nl else None
        (h2, gg, uu, x2), got = _ffn_fwd(x1, smalls[l]['norm_ffn'], w, TILE_FFN_FWD, f"ffn_fwd_{l}", nxt)
        saved.append(dict(x=x, h=h, proj=proj, c1=c1, ya=ya, yb=yb, x1=x1, h2=h2, gg=gg, uu=uu, w=w))
        x = x2
        if got:
            w_in = got[0]
    dx, small_loss = _loss_bwd(x, p['norm_final'][None, :], target, TILE_LOSS, "loss_bwd")
    queue = _GradQueue()
    small_gathered = [None] * nl
    small_pending = None
    rows = d // NDEV
    hid = NDEV * FF_PAD
    for l in reversed(range(nl)):
        s = saved[l]
        w = s['w']

        def hosted(fn, *args, extra=None):
            keys, comm = queue.take()
            res, got = fn(*args, _together(comm, extra))
            queue.put(keys, got[:len(keys)])
            return res, got[len(keys):]

        def tn(key, a, b, a_blk, b_blk, stack, shard, reshape=None, host=False, extra=None):
            keys, comm = queue.take() if host else ([], None)
            g, got = _matmul_tn(a, b, a_blk, b_blk, stack, shard, TILE_TN, f"dw_{key}_{l}", _together(comm, extra))
            queue.put(keys, got[:len(keys)])
            queue.push((l, key), g if reshape is None else g.reshape(reshape))
            return got[len(keys):]

        (act, dgg, duu, dx1, small_ffn), _ = hosted(_ffn_bwd, dx, s['x1'], smalls[l]['norm_ffn'], s['gg'], s['uu'], w,
                                                    TILE_FFN_BWD, f"ffn_bwd_{l}")
        tn('w_ffn_gate', dgg, s['h2'], hid, d, 'a', FF_PAD)
        tn('w_ffn_up', duu, s['h2'], hid, d, 'a', FF_PAD)
        (dproj, dc1, merged, c3, gated, dya, dyb, small_mix, dws, dbs), _ = hosted(
            _mixer_bwd, dx1, s['proj'], s['c1'], s['ya'], s['yb'], w, smalls[l], TILE_MIX, f"mixer_bwd_{l}")
        tn('w_ffn_down', act, dx, hid, d, 'a', FF_PAD)
        extra = small_pending[1] if small_pending else None
        (dproj, g_conv, small_conv, g_o, g_co, g_so), got = hosted(
            _conv_bwd, dproj, dc1, s['proj'], w, [(merged, dx1), (c3, dya), (gated, dyb)], seq, TILE_MIX,
            f"conv_bwd_{l}", extra=extra)
        if small_pending:
            small_gathered[small_pending[0]] = got
            small_pending = None
        queue.push((l, 'conv_w'), g_conv)
        queue.push((l, 'w_o'), g_o.reshape(NDEV, rows, d))
        queue.push((l, 'w_conv_out'), g_co.reshape(NDEV, rows, d))
        queue.push((l, 'w_sgu_out'), g_so.reshape(NDEV, rows, d))
        blocks = [small_mix, small_conv, small_ffn] + ([small_loss] if l == nl - 1 else [])
        small_main = [jnp.concatenate(blocks, axis=0), dws, dbs]
        small_main_gathered = tn('w_in', s['h'], dproj, d, hid, 'b', w['w_in'].shape[2], host=True,
                                 extra=_Gather(small_main) if l == 0 else None)
        if l == 0:
            (dx, small_in), _ = hosted(_in_proj_bwd, dproj, s['x'], dx1, smalls[l]['norm_mix'], w['w_in'], TILE_IN,
                                       f"in_proj_bwd_{l}")
        else:
            (dx, small_in), _ = _in_proj_bwd(dproj, s['x'], dx1, smalls[l]['norm_mix'], w['w_in'], TILE_IN,
                                             f"in_proj_bwd_{l}", None)
        small_pending = (l, _Gather(small_main + [small_in]))
    keys, comm = queue.take()
    last = _exchange_alone(_together(comm, _Gather([small_in])), "exchange_last_grads")
    queue.put(keys, last[:len(keys)])
    small_gathered[0] = small_main_gathered + last[len(keys):]
    return small_gathered, dx, queue.done


def _train_step(p, m, v, x3, target3):
    nl = p['norm_mix'].shape[0]
    bsz, seq, d = x3.shape
    x = x3.reshape(bsz * seq, d)
    target = target3.reshape(bsz * seq, d)
    shards, w_in = _prepare_weights(p)
    small_gathered, dx, exchanged = _forward_backward(p, shards, w_in, x, target, seq)

    out = {kind: {} for kind in "gdmv"}
    splits = {'w_in': 4, 'w_conv_out': 1, 'w_sgu_out': 1, 'w_o': 1, 'w_ffn_gate': 1, 'w_ffn_up': 1, 'w_ffn_down': 1, 'conv_w': 1}
    for n in splits:
        if n == 'conv_w':
            pad = ((0, 0), (0, CONV_TAPS_PADDED - CONV_TAPS), (0, 0))
            wl, ml, vl = (jnp.pad(a[n][:, :, 0, :], pad) for a in (p, m, v))
        elif n in ('w_ffn_gate', 'w_ffn_up'):
            wl, ml, vl = (_hidden_major(a[n]) for a in (p, m, v))
        else:
            wl, ml, vl = p[n], m[n], v[n]
        prev = None
        for l in range(nl):
            prev = _adamw_layer(l, wl, ml, vl, exchanged[(l, n)], prev, splits[n], f"adamw_{n}_{l}")
        for kind, arr in zip("gdmv", prev):
            if n == 'conv_w':
                arr = arr[:, 0:CONV_TAPS, None, :]
            elif n in ('w_ffn_gate', 'w_ffn_up'):
                arr = _hidden_major(arr)
            out[kind][n] = arr
    small_names = ['norm_mix', 'gate_bias', 'conv_b', 'conv_ln_g', 'conv_ln_b', 'sgu_ln_g', 'sgu_ln_b', 'w_spatial',
                   'b_spatial', 'norm_ffn', 'norm_final']

    def two_d(a):
        return a[None, :] if a.ndim == 1 else a

    res, loss_row = _adamw_small(small_gathered, {n: two_d(p[n]) for n in small_names},
                                 {n: two_d(m[n]) for n in small_names}, {n: two_d(v[n]) for n in small_names})
    loss = loss_row[0, 0]
    for kind in "gdmv":
        for n in small_names:
            out[kind][n] = res[kind][n].reshape(p[n].shape)
    grad_x = dx.reshape(bsz, seq, d)
    return (loss, grad_x, *[out[kind][n] for kind in "gdmv" for n in WEIGHT_NAMES])


def kernel(x, norm_mix, w_in, gate_bias, conv_w, conv_b, conv_ln_g, conv_ln_b, w_conv_out, sgu_ln_g, sgu_ln_b, w_spatial, b_spatial, w_sgu_out, w_o, norm_ffn, w_ffn_gate, w_ffn_up, w_ffn_down, norm_final, loss_target, m_norm_mix, m_w_in, m_gate_bias, m_conv_w, m_conv_b, m_conv_ln_g, m_conv_ln_b, m_w_conv_out, m_sgu_ln_g, m_sgu_ln_b, m_w_spatial, m_b_spatial, m_w_sgu_out, m_w_o, m_norm_ffn, m_w_ffn_gate, m_w_ffn_up, m_w_ffn_down, m_norm_final, v_norm_mix, v_w_in, v_gate_bias, v_conv_w, v_conv_b, v_conv_ln_g, v_conv_ln_b, v_w_conv_out, v_sgu_ln_g, v_sgu_ln_b, v_w_spatial, v_b_spatial, v_w_sgu_out, v_w_o, v_norm_ffn, v_w_ffn_gate, v_w_ffn_up, v_w_ffn_down, v_norm_final):
    p = dict(zip(WEIGHT_NAMES, (norm_mix, w_in, gate_bias, conv_w, conv_b, conv_ln_g, conv_ln_b, w_conv_out, sgu_ln_g, sgu_ln_b, w_spatial, b_spatial, w_sgu_out, w_o, norm_ffn, w_ffn_gate, w_ffn_up, w_ffn_down, norm_final)))
    m = dict(zip(WEIGHT_NAMES, (m_norm_mix, m_w_in, m_gate_bias, m_conv_w, m_conv_b, m_conv_ln_g, m_conv_ln_b, m_w_conv_out, m_sgu_ln_g, m_sgu_ln_b, m_w_spatial, m_b_spatial, m_w_sgu_out, m_w_o, m_norm_ffn, m_w_ffn_gate, m_w_ffn_up, m_w_ffn_down, m_norm_final)))
    v = dict(zip(WEIGHT_NAMES, (v_norm_mix, v_w_in, v_gate_bias, v_conv_w, v_conv_b, v_conv_ln_g, v_conv_ln_b, v_w_conv_out, v_sgu_ln_g, v_sgu_ln_b, v_w_spatial, v_b_spatial, v_w_sgu_out, v_w_o, v_norm_ffn, v_w_ffn_gate, v_w_ffn_up, v_w_ffn_down, v_norm_final)))
    return _train_step(p, m, v, x, loss_target)
```

```python
import math

import jax
import jax.numpy as jnp
from jax import lax
from jax.experimental import pallas as pl
from jax.experimental.pallas import tpu as pltpu

F32 = jnp.float32
BF16 = jnp.bfloat16
MESH_ID = pl.DeviceIdType.MESH

NDEV = 8
EPS = 1e-6
CONV_TAPS = 31
CONV_TAPS_PADDED = 32
HALO = 16
CONV_ROWS = 128
CONV_ROWS_LOOP = 64
LANES = 128
SUBLANES = 8
CHUNK = 128
GROUPS = 8
FF_PAD = 384
FF_PAIR = 2 * FF_PAD
TN_COLS = 512
VMEM_LIMIT_BYTES = 56 * 1024 * 1024

ADAM_LR = 0.001
ADAM_B1 = 0.9
ADAM_B2 = 0.999
ADAM_EPS = 1e-08
ADAM_WD = 0.01
ADAM_STEP = 10

TILE_IN = 512
TILE_IN_FWD = 1024
TILE_MIX = 256
TILE_MIX_FWD = 512
TILE_FFN_FWD = 1024
TILE_FFN_BWD = 512
TILE_TN = 1024
TILE_LOSS = 512

WEIGHT_NAMES = ['norm_mix', 'w_in', 'gate_bias', 'conv_w', 'conv_b', 'conv_ln_g', 'conv_ln_b', 'w_conv_out',
                'sgu_ln_g', 'sgu_ln_b', 'w_spatial', 'b_spatial', 'w_sgu_out', 'w_o', 'norm_ffn', 'w_ffn_gate',
                'w_ffn_up', 'w_ffn_down', 'norm_final']
MIXER_WEIGHTS = ['w_conv_out', 'w_sgu_out', 'w_o', 'conv_w']
FFN_WEIGHTS = ['w_ffn_gate', 'w_ffn_up', 'w_ffn_down']


def _sds(shape, dtype):
    return jax.ShapeDtypeStruct(tuple(shape), dtype)


def _params(*sem):
    return pltpu.CompilerParams(dimension_semantics=sem or None, vmem_limit_bytes=VMEM_LIMIT_BYTES)


def _nn(a, b):
    return jnp.dot(a, b, preferred_element_type=F32)


def _nt(a, b):
    return lax.dot_general(a, b, (((1,), (1,)), ((), ())), preferred_element_type=F32)


def _tn(a, b):
    return lax.dot_general(a, b, (((0,), (0,)), ((), ())), preferred_element_type=F32)


def _sig(v):
    return jax.nn.sigmoid(v)


def _fold(v):
    r, c = v.shape
    return jnp.sum(v.reshape(r // SUBLANES, SUBLANES, c), axis=0)


def _tile(tm, n, j=0):
    return pl.BlockSpec((tm, n), lambda i, *_: (i, j))


def _row(n):
    return pl.BlockSpec((1, n), lambda *_: (0, 0))


def _resident(shape):
    nd = len(shape)
    return pl.BlockSpec(tuple(shape), lambda *_: (0,) * nd)


def _weight(w):
    nd = w.ndim
    return pl.BlockSpec(tuple(w.shape), lambda *_: (0,) * nd, pipeline_mode=pl.Buffered(1))


def _peer(rel):
    x, y, c = lax.axis_index("x"), lax.axis_index("y"), lax.axis_index("c")
    return (1 - x if rel & 4 else x, 1 - y if rel & 2 else y, 1 - c if rel & 1 else c)


def _slot(pos):
    return 4 * pos[0] + 2 * pos[1] + pos[2]


class _Exchange:
    def __init__(self, arrays, layers=None):
        self.arrays = list(arrays)
        self.layers = list(layers) if layers is not None else [None] * len(self.arrays)

    def scratch(self):
        n = len(self.arrays)
        return [pltpu.SemaphoreType.DMA((n, NDEV)), pltpu.SemaphoreType.DMA((n, NDEV)), pltpu.SemaphoreType.DMA((n,))]

    def _src(self, ins, j):
        return ins[j] if self.layers[j] is None else ins[j].at[self.layers[j]]

    def _block_shape(self, j):
        a = self.arrays[j]
        return a.shape if self.layers[j] is None else a.shape[1:]


class _Gather(_Exchange):
    chips = (4, 2, 6)

    def out_shape(self):
        return [_sds((NDEV,) + tuple(self._block_shape(j)), a.dtype) for j, a in enumerate(self.arrays)]

    @staticmethod
    def _copy(outs, sems, j, sem, block_rel, to_rel, src=None):
        blk = outs[j].at[_slot(_peer(block_rel))]
        return pltpu.make_async_remote_copy(
            src_ref=blk if src is None else src, dst_ref=blk,
            send_sem=sems[0].at[j, sem], recv_sem=sems[1].at[j, sem],
            device_id=_peer(to_rel), device_id_type=MESH_ID)

    def _local(self, ins, outs, sems, j):
        return pltpu.make_async_copy(self._src(ins, j), outs[j].at[_slot(_peer(0))], sems[2].at[j])

    def start(self, ins, outs, sems):
        for j in range(len(self.arrays)):
            self._local(ins, outs, sems, j).start()
            for rel in (1,) + self.chips:
                self._copy(outs, sems, j, rel, 0, rel, src=self._src(ins, j)).start()

    def forward(self, ins, outs, sems):
        for j in range(len(self.arrays)):
            for rel in self.chips:
                self._copy(outs, sems, j, rel, rel, 0).wait_recv()
                self._copy(outs, sems, j, rel ^ 1, rel, 1).start()

    def finish(self, ins, outs, sems):
        for j in range(len(self.arrays)):
            self._copy(outs, sems, j, 1, 1, 0).wait_recv()
            for rel in self.chips:
                self._copy(outs, sems, j, rel ^ 1, rel ^ 1, 0).wait_recv()
        for j in range(len(self.arrays)):
            for rel in (1,) + self.chips:
                self._copy(outs, sems, j, rel, 0, rel, src=self._src(ins, j)).wait_send()
            for rel in self.chips:
                self._copy(outs, sems, j, rel ^ 1, rel, 1).wait_send()
            self._local(ins, outs, sems, j).wait()


class _Scatter(_Exchange):
    def out_shape(self):
        return [_sds(a.shape, a.dtype) for a in self.arrays]

    @staticmethod
    def _copy(ins, outs, sems, j, rel):
        return pltpu.make_async_remote_copy(
            src_ref=ins[j].at[_slot(_peer(rel))], dst_ref=outs[j].at[_slot(_peer(0))],
            send_sem=sems[0].at[j, rel], recv_sem=sems[1].at[j, rel],
            device_id=_peer(rel), device_id_type=MESH_ID)

    @staticmethod
    def _arrival(outs, sems, j, rel):
        blk = outs[j].at[_slot(_peer(rel))]
        return pltpu.make_async_remote_copy(
            src_ref=blk, dst_ref=blk, send_sem=sems[0].at[j, rel], recv_sem=sems[1].at[j, rel],
            device_id=_peer(rel), device_id_type=MESH_ID)

    @staticmethod
    def _local(ins, outs, sems, j):
        me = _slot(_peer(0))
        return pltpu.make_async_copy(ins[j].at[me], outs[j].at[me], sems[2].at[j])

    def start(self, ins, outs, sems):
        for j in range(len(self.arrays)):
            self._local(ins, outs, sems, j).start()
            for rel in range(1, NDEV):
                self._copy(ins, outs, sems, j, rel).start()

    def forward(self, ins, outs, sems):
        pass

    def finish(self, ins, outs, sems):
        for j in range(len(self.arrays)):
            for rel in range(1, NDEV):
                self._arrival(outs, sems, j, rel).wait_recv()
        for j in range(len(self.arrays)):
            for rel in range(1, NDEV):
                self._copy(ins, outs, sems, j, rel).wait_send()
            self._local(ins, outs, sems, j).wait()


class _Together:
    def __init__(self, parts):
        self.parts = [c for c in parts if c is not None]
        self.arrays = [a for c in self.parts for a in c.arrays]

    def out_shape(self):
        return [s for c in self.parts for s in c.out_shape()]

    def scratch(self):
        return [s for c in self.parts for s in c.scratch()]

    def _each(self, method, ins, outs, sems):
        at = 0
        for q, c in enumerate(self.parts):
            n = len(c.arrays)
            getattr(c, method)(ins[at:at + n], outs[at:at + n], sems[3 * q:3 * q + 3])
            at += n

    def start(self, ins, outs, sems):
        self._each("start", ins, outs, sems)

    def forward(self, ins, outs, sems):
        self._each("forward", ins, outs, sems)

    def finish(self, ins, outs, sems):
        self._each("finish", ins, outs, sems)


def _together(*parts):
    flat = []
    for c in parts:
        if c is not None:
            flat.extend(c.parts if isinstance(c, _Together) else [c])
    return _Together(flat) if flat else None


def _call(body, *, name, args, in_specs, out_specs, out_shape, grid=(), scratch_shapes=(), semantics=(),
          aliases=None, comm=None):
    in_specs, out_specs, out_shape = list(in_specs), list(out_specs), list(out_shape)
    scratch_shapes = list(scratch_shapes)
    if comm is None:
        res = pl.pallas_call(
            body, name=name, grid=grid, in_specs=in_specs, out_specs=out_specs, out_shape=out_shape,
            scratch_shapes=scratch_shapes, input_output_aliases=aliases or {},
            compiler_params=_params(*semantics))(*args)
        return list(res), []
    n_in, n_out, n_scr, nc = len(in_specs), len(out_specs), len(scratch_shapes), len(comm.arrays)
    total = math.prod(grid)
    middle = min((total * 5) // 8, total - 1)

    def hosted(*refs):
        ins, cins = refs[:n_in], refs[n_in:n_in + nc]
        o0 = n_in + nc
        outs, couts = refs[o0:o0 + n_out], refs[o0 + n_out:o0 + n_out + nc]
        s0 = o0 + n_out + nc
        scr, sems = refs[s0:s0 + n_scr], refs[s0 + n_scr:]
        if total == 1:
            comm.start(cins, couts, sems)
            body(*ins, *outs, *scr)
            comm.forward(cins, couts, sems)
            comm.finish(cins, couts, sems)
            return
        step = 0
        for axis, size in enumerate(grid):
            step = step * size + pl.program_id(axis)
        pl.when(step == 0)(lambda: comm.start(cins, couts, sems))
        body(*ins, *outs, *scr)
        pl.when(step == middle)(lambda: comm.forward(cins, couts, sems))
        pl.when(step == total - 1)(lambda: comm.finish(cins, couts, sems))

    any_spec = pl.BlockSpec(memory_space=pl.ANY)
    res = pl.pallas_call(
        hosted, name=name, grid=grid,
        in_specs=in_specs + [any_spec] * nc, out_specs=out_specs + [any_spec] * nc,
        out_shape=out_shape + comm.out_shape(), scratch_shapes=scratch_shapes + comm.scratch(),
        input_output_aliases=aliases or {}, compiler_params=_params(*(("arbitrary",) * len(grid))),
    )(*args, *comm.arrays)
    return list(res[:n_out]), list(res[n_out:])


def _exchange_alone(comm, name):
    return _call(lambda: None, name=name, args=(), in_specs=(), out_specs=(), out_shape=(), comm=comm)[1]


def _cast_pad(ws, shapes, name, comm):
    n = len(ws)
    nl = ws[0].shape[0]

    def body(*refs):
        for w, (rows, cols), w_ref, o_ref in zip(ws, shapes, refs[:n], refs[n:]):
            r, c = w.shape[1:]
            if (rows, cols) != (r, c):
                o_ref[...] = jnp.zeros(o_ref.shape, BF16)
            o_ref[0, 0:r, 0:c] = w_ref[0].astype(BF16)

    return _call(body, name=name, grid=(nl,), args=tuple(ws),
                 in_specs=[pl.BlockSpec((1,) + w.shape[1:], lambda i: (i, 0, 0)) for w in ws],
                 out_specs=[pl.BlockSpec((1,) + tuple(s), lambda i: (i, 0, 0)) for s in shapes],
                 out_shape=[_sds((nl,) + tuple(s), BF16) for s in shapes], semantics=("parallel",), comm=comm)


def _in_proj(x, gain, w, tm, name, comm):
    t, d = x.shape
    nb, _, bw = w.shape

    def body(x_ref, g_ref, w_ref, h_ref, p_ref):
        xv = x_ref[...]
        r = lax.rsqrt(jnp.mean(xv * xv, axis=-1, keepdims=True) + EPS)
        h = (xv * r * g_ref[...]).astype(BF16)
        h_ref[...] = h
        for k in range(nb):
            p_ref[:, k * bw:(k + 1) * bw] = _nn(h, w_ref[k]).astype(BF16)

    return _call(body, name=name, grid=(t // tm,), args=(x, gain, w),
                 in_specs=[_tile(tm, d), _row(d), _weight(w)],
                 out_specs=[_tile(tm, d), _tile(tm, nb * bw)],
                 out_shape=[_sds((t, d), BF16), _sds((t, nb * bw), BF16)],
                 semantics=("parallel",), comm=comm)


def _halo_specs(tm, t, d, col):
    nh, nhb = tm // HALO, t // HALO
    prev = pl.BlockSpec((HALO, d), lambda i: (jnp.maximum(i * nh - 1, 0), col))
    nxt = pl.BlockSpec((HALO, d), lambda i: (jnp.minimum((i + 1) * nh, nhb - 1), col))
    return prev, nxt


def _row_chunks(tm, rc, unroll, rows):
    n = tm // rc
    if unroll:
        for j in range(n):
            rows(j * rc)
    else:
        def step(j, carry):
            rows(pl.multiple_of(j * rc, rc))
            return carry

        lax.fori_loop(0, n, step, 0)


def _shifted(buf_ref, kb, r0, off, rc):
    return buf_ref[kb, pl.ds(r0 + off, rc), :]


def _dwconv(buf_ref, w_ref, out_ref, tm, flip, rc, unroll):
    nblk = out_ref.shape[1] // LANES

    def rows(r0):
        for kb in range(nblk):
            acc = jnp.zeros((rc, LANES), F32)
            for k in range(CONV_TAPS):
                off = (CONV_TAPS - k) if flip else (1 + k)
                acc = acc + w_ref[kb, k:k + 1, :] * _shifted(buf_ref, kb, r0, off, rc)
            out_ref[pl.ds(r0, rc), kb * LANES:(kb + 1) * LANES] = acc

    _row_chunks(tm, rc, unroll, rows)


def _fill_halo_buffer(buf, prev, body, nxt, first, last, tm):
    prev = jnp.where(first, 0.0, prev)
    nxt = jnp.where(last, 0.0, nxt)
    for kb in range(buf.shape[0]):
        lanes = slice(kb * LANES, (kb + 1) * LANES)
        buf[kb, 0:HALO, :] = prev[:, lanes]
        buf[kb, HALO:HALO + tm, :] = body[:, lanes]
        buf[kb, HALO + tm:HALO + tm + HALO, :] = nxt[:, lanes]


def _fill_glu_buffer(cbuf, av, ag, avp, agp, avn, agn, first, last, tm):
    c0p = avp[...].astype(F32) * _sig(agp[...].astype(F32))
    c0n = avn[...].astype(F32) * _sig(agn[...].astype(F32))
    c0 = av[...].astype(F32) * _sig(ag[...].astype(F32))
    _fill_halo_buffer(cbuf, c0p, c0, c0n, first, last, tm)


def _layernorm_stats(v):
    mu = jnp.mean(v, axis=-1, keepdims=True)
    cen = v - mu
    rstd = lax.rsqrt(jnp.mean(cen * cen, axis=-1, keepdims=True) + EPS)
    return cen * rstd, rstd


def _spatial_mix(ws_ref, vn_ref, bias_ref, mixed_ref, tm):
    for ci in range(tm // CHUNK):
        rs = slice(ci * CHUNK, (ci + 1) * CHUNK)
        for g in range(GROUPS):
            ls = slice(g * LANES, (g + 1) * LANES)
            mixed_ref[rs, ls] = _nn(ws_ref[g], vn_ref[rs, ls]) + bias_ref[:, ls]


def _mixer_fwd(proj, x, w, small, seq, tm, name, comm):
    t, d = x.shape
    hp_v, hn_v = _halo_specs(tm, t, d, 0)
    hp_g, hn_g = _halo_specs(tm, t, d, 1)

    def body(av, ag, u_ref, v_ref, ga_ref, gb_ref, avp, agp, avn, agn, x_ref,
             cw_ref, cb_ref, lg_ref, lb_ref, wco_ref, sg_ref, sb_ref, ws_ref, bias_ref, wso_ref,
             gba_ref, gbb_ref, wo_ref,
             c1_ref, ya_ref, yb_ref, x1_ref, cbuf, c1f, vn_ref, mixed_ref):
        i = pl.program_id(0)
        first = (i * tm) % seq == 0
        last = ((i + 1) * tm) % seq == 0
        _fill_glu_buffer(cbuf, av, ag, avp, agp, avn, agn, first, last, tm)
        _dwconv(cbuf, cw_ref, c1f, tm, flip=False, rc=CONV_ROWS_LOOP, unroll=False)
        c1 = c1f[...] + cb_ref[...]
        c1_ref[...] = c1.astype(BF16)
        c2hat, _ = _layernorm_stats(c1)
        c2 = c2hat * lg_ref[...] + lb_ref[...]
        c3 = (c2 * _sig(c2)).astype(BF16)
        ya = _nn(c3, wco_ref[...].reshape(d, d))
        ya_ref[...] = ya.astype(BF16)
        vhat, _ = _layernorm_stats(v_ref[...].astype(F32))
        vn_ref[...] = (vhat * sg_ref[...] + sb_ref[...]).astype(BF16)
        _spatial_mix(ws_ref, vn_ref, bias_ref, mixed_ref, tm)
        gated = (u_ref[...].astype(F32) * mixed_ref[...]).astype(BF16)
        yb = _nn(gated, wso_ref[...].reshape(d, d))
        yb_ref[...] = yb.astype(BF16)
        sa = _sig(ga_ref[...].astype(F32) + gba_ref[...])
        sb = _sig(gb_ref[...].astype(F32) + gbb_ref[...])
        merged = (sa * ya + sb * yb).astype(BF16)
        x1_ref[...] = x_ref[...] + _nn(merged, wo_ref[...].reshape(d, d))

    cols = [_tile(tm, d, j) for j in range(6)]
    return _call(
        body, name=name, grid=(t // tm,),
        args=(proj,) * 10 + (x, w['conv_w'], small['conv_b'], small['conv_ln_g'], small['conv_ln_b'], w['w_conv_out'],
                             small['sgu_ln_g'], small['sgu_ln_b'], small['ws'], small['bias_full'], w['w_sgu_out'],
                             small['gba'], small['gbb'], w['w_o']),
        in_specs=cols + [hp_v, hp_g, hn_v, hn_g, _tile(tm, d),
                         _weight(w['conv_w']), _row(d), _row(d), _row(d),
                         _weight(w['w_conv_out']), _row(d), _row(d),
                         _resident(small['ws'].shape), _resident(small['bias_full'].shape),
                         _weight(w['w_sgu_out']), _row(d), _row(d), _weight(w['w_o'])],
        out_specs=[_tile(tm, d)] * 4,
        out_shape=[_sds((t, d), BF16)] * 3 + [_sds((t, d), F32)],
        scratch_shapes=[pltpu.VMEM((d // LANES, tm + 2 * HALO, LANES), F32), pltpu.VMEM((tm, d), F32),
                        pltpu.VMEM((tm, d), BF16), pltpu.VMEM((tm, d), F32)],
        semantics=("parallel",), comm=comm)


def _pair_gate_up(wg, wu, name):
    n, c, d = wg.shape

    def body(wg_ref, wu_ref, o_ref):
        o_ref[0:c, :] = wg_ref[0]
        o_ref[c:2 * c, :] = wg_ref[1]
        o_ref[2 * c:3 * c, :] = wu_ref[0]
        o_ref[3 * c:4 * c, :] = wu_ref[1]

    pair = pl.BlockSpec((2, c, d), lambda k: (k, 0, 0))
    return _call(body, name=name, grid=(n // 2,), args=(wg, wu), in_specs=[pair, pair],
                 out_specs=[pl.BlockSpec((None, 4 * c, d), lambda k: (k, 0, 0))],
                 out_shape=[_sds((n // 2, 4 * c, d), wg.dtype)], semantics=("parallel",))[0][0]


def _pair_specs(w, tm):
    d = w['w_gate_up'].shape[2]
    up = pl.BlockSpec((None, 2 * FF_PAIR, d), lambda i, k: (k, 0, 0))
    down = pl.BlockSpec((None, FF_PAIR, d), lambda i, k: (k, 0, 0))
    return [up, down]


def _ffn_fwd(x1, gain, w, tm, name, comm):
    t, d = x1.shape
    tm = min(tm, t)
    npair = NDEV // 2
    hid = NDEV * FF_PAD
    wd_pairs = w['w_ffn_down'].reshape(npair, FF_PAIR, d)

    def body(x_ref, g_ref, wgu_ref, wd_ref, h_ref, gg_ref, uu_ref, x2_ref, hb_ref, acc_ref):
        k = pl.program_id(1)

        @pl.when(k == 0)
        def _():
            xv = x_ref[...]
            r = lax.rsqrt(jnp.mean(xv * xv, axis=-1, keepdims=True) + EPS)
            h = (xv * r * g_ref[...]).astype(BF16)
            hb_ref[...] = h
            h_ref[...] = h
            acc_ref[...] = xv

        gu = _nt(hb_ref[...], wgu_ref[...])
        gk = gu[:, 0:FF_PAIR]
        uk = gu[:, FF_PAIR:2 * FF_PAIR]
        gg_ref[...] = gk.astype(BF16)
        uu_ref[...] = uk.astype(BF16)
        ak = (gk * _sig(gk) * uk).astype(BF16)
        acc_ref[...] += _nn(ak, wd_ref[...])

        @pl.when(k == npair - 1)
        def _():
            x2_ref[...] = acc_ref[...]

    pair_cols = pl.BlockSpec((tm, FF_PAIR), lambda i, k: (i, k))
    return _call(
        body, name=name, grid=(t // tm, npair),
        args=(x1, gain, w['w_gate_up'], wd_pairs),
        in_specs=[_tile(tm, d), _row(d)] + _pair_specs(w, tm),
        out_specs=[_tile(tm, d), pair_cols, pair_cols, _tile(tm, d)],
        out_shape=[_sds((t, d), BF16), _sds((t, hid), BF16), _sds((t, hid), BF16), _sds((t, d), F32)],
        scratch_shapes=[pltpu.VMEM((tm, d), BF16), pltpu.VMEM((tm, d), F32)],
        semantics=("parallel", "arbitrary"), comm=comm)


def _init_small(first, small_ref, acc_ref):
    @pl.when(first)
    def _():
        small_ref[...] = jnp.zeros(small_ref.shape, F32)
        acc_ref[...] = jnp.zeros(acc_ref.shape, F32)


def _finish_small(last, small_ref, acc_ref, nq):
    @pl.when(last)
    def _():
        for q in range(nq):
            small_ref[q:q + 1, :] = jnp.sum(acc_ref[q], axis=0, keepdims=True)


def _loss_bwd(x, gain, target, tm, name):
    t, d = x.shape
    tm = min(tm, t)
    nsteps = t // tm

    def body(x_ref, g_ref, t_ref, dx_ref, small_ref, acc_ref):
        i = pl.program_id(0)
        _init_small(i == 0, small_ref, acc_ref)
        xv = x_ref[...]
        r = lax.rsqrt(jnp.mean(xv * xv, axis=-1, keepdims=True) + EPS)
        xhat = xv * r
        diff = xhat * g_ref[...] - t_ref[...]
        dy = diff * (1.0 / d)
        acc_ref[0] += _fold(dy * xhat)
        acc_ref[1] += _fold(diff * diff)
        dxhat = dy * g_ref[...]
        dx_ref[...] = r * (dxhat - xhat * jnp.mean(dxhat * xhat, axis=-1, keepdims=True))

        @pl.when(i == nsteps - 1)
        def _():
            small_ref[0:1, :] = jnp.sum(acc_ref[0], axis=0, keepdims=True)
            small_ref[1:2, :] = jnp.full((1, d), jnp.sum(acc_ref[1]) * (0.5 / d), F32)

    return _call(body, name=name, grid=(nsteps,), args=(x, gain, target),
                 in_specs=[_tile(tm, d), _row(d), _tile(tm, d)],
                 out_specs=[_tile(tm, d), _resident((SUBLANES, d))],
                 out_shape=[_sds((t, d), F32), _sds((SUBLANES, d), F32)],
                 scratch_shapes=[pltpu.VMEM((2, SUBLANES, d), F32)], semantics=("arbitrary",))[0]


def _ffn_bwd(dx2, x1, gain, gg, uu, w, tm, name, comm):
    t, d = x1.shape
    tm = min(tm, t)
    npair = NDEV // 2
    hid = NDEV * FF_PAD
    nsteps = t // tm
    wd_pairs = w['w_ffn_down'].reshape(npair, FF_PAIR, d)

    def body(dx_ref, x_ref, g_ref, gg_ref, uu_ref, wgu_ref, wd_ref,
             a_ref, dg_ref, du_ref, dx1_ref, small_ref, acc_ref, dxb_ref, dh_ref, dgu_ref):
        i, k = pl.program_id(0), pl.program_id(1)
        _init_small((i == 0) & (k == 0), small_ref, acc_ref)

        @pl.when(k == 0)
        def _():
            dxb_ref[...] = dx_ref[...].astype(BF16)
            dh_ref[...] = jnp.zeros(dh_ref.shape, F32)

        gk = gg_ref[...].astype(F32)
        uk = uu_ref[...].astype(F32)
        sg = _sig(gk)
        silu = gk * sg
        a_ref[...] = (silu * uk).astype(BF16)
        da = _nt(dxb_ref[...], wd_ref[...])
        dgk = (da * uk * (sg * (1.0 + gk * (1.0 - sg)))).astype(BF16)
        duk = (da * silu).astype(BF16)
        dg_ref[...] = dgk
        du_ref[...] = duk
        dgu_ref[:, 0:FF_PAIR] = dgk
        dgu_ref[:, FF_PAIR:2 * FF_PAIR] = duk
        dh_ref[...] += _nn(dgu_ref[...], wgu_ref[...])

        @pl.when(k == npair - 1)
        def _():
            xv = x_ref[...]
            dh = dh_ref[...]
            r = lax.rsqrt(jnp.mean(xv * xv, axis=-1, keepdims=True) + EPS)
            xhat = xv * r
            acc_ref[0] += _fold(dh * xhat)
            dxhat = dh * g_ref[...]
            dx1_ref[...] = dx_ref[...] + r * (dxhat - xhat * jnp.mean(dxhat * xhat, axis=-1, keepdims=True))

        _finish_small((i == nsteps - 1) & (k == npair - 1), small_ref, acc_ref, 1)

    pair_cols = pl.BlockSpec((tm, FF_PAIR), lambda i, k: (i, k))
    return _call(
        body, name=name, grid=(nsteps, npair),
        args=(dx2, x1, gain, gg, uu, w['w_gate_up'], wd_pairs),
        in_specs=[_tile(tm, d), _tile(tm, d), _row(d), pair_cols, pair_cols] + _pair_specs(w, tm),
        out_specs=[pair_cols] * 3 + [_tile(tm, d), _resident((SUBLANES, d))],
        out_shape=[_sds((t, hid), BF16)] * 3 + [_sds((t, d), F32), _sds((SUBLANES, d), F32)],
        scratch_shapes=[pltpu.VMEM((1, SUBLANES, d), F32), pltpu.VMEM((tm, d), BF16), pltpu.VMEM((tm, d), F32),
                        pltpu.VMEM((tm, 2 * FF_PAIR), BF16)],
        semantics=("arbitrary", "arbitrary"), comm=comm)


def _matmul_tn(a, b, a_blk, b_blk, stack, shard, tm, name, comm):
    t, ma = a.shape
    tm = min(tm, t)
    nb_ = b.shape[1]
    na, nb = ma // a_blk, nb_ // b_blk
    nsteps = t // tm
    cw = min(TN_COLS, b_blk)
    if stack == 'b':
        per = b_blk // shard
        out_shape, out_spec = (nb_ // shard, ma, shard), pl.BlockSpec((per, a_blk, shard), lambda i, j, k: (j, 0, 0))
    elif stack == 'a':
        per = a_blk // shard
        out_shape, out_spec = (ma // shard, shard, nb_), pl.BlockSpec((per, shard, b_blk), lambda i, j, k: (i, 0, 0))
    else:
        out_shape, out_spec = (ma, nb_), pl.BlockSpec((a_blk, b_blk), lambda i, j, k: (i, j))

    def body(a_ref, b_ref, o_ref, acc_ref):
        k = pl.program_id(2)

        @pl.when(k == 0)
        def _():
            acc_ref[...] = jnp.zeros(acc_ref.shape, F32)

        av = a_ref[...].astype(BF16)
        for c in range(0, b_blk, cw):
            acc_ref[:, c:c + cw] += _tn(av, b_ref[:, c:c + cw].astype(BF16))

        @pl.when(k == nsteps - 1)
        def _():
            if stack == 'b':
                for s in range(per):
                    o_ref[s] = acc_ref[:, s * shard:(s + 1) * shard].astype(BF16)
            elif stack == 'a':
                for s in range(per):
                    o_ref[s] = acc_ref[s * shard:(s + 1) * shard, :].astype(BF16)
            else:
                o_ref[...] = acc_ref[...].astype(BF16)

    res, got = _call(
        body, name=name, grid=(na, nb, nsteps), args=(a, b),
        in_specs=[pl.BlockSpec((tm, a_blk), lambda i, j, k: (k, i)), pl.BlockSpec((tm, b_blk), lambda i, j, k: (k, j))],
        out_specs=[out_spec], out_shape=[_sds(out_shape, BF16)],
        scratch_shapes=[pltpu.VMEM((a_blk, b_blk), F32)],
        semantics=("parallel", "parallel", "arbitrary"), comm=comm)
    return res[0], got


def _mixer_bwd(dx1, proj, c1, ya, yb, w, small, tm, name, comm):
    t, d = dx1.shape
    nsteps = t // tm
    nq = 6

    def body(dx_ref, u_ref, v_ref, ga_ref, gb_ref, c1_ref, ya_ref, yb_ref,
             lg_ref, lb_ref, wco_ref, sg_ref, sb_ref, ws_ref, wst_ref, bias_ref, wso_ref, gba_ref, gbb_ref, wo_ref,
             sel_ref,
             dp_ref, dc1_ref, mg_ref, c3_ref, gt_ref, dya_ref, dyb_ref, small_ref, dws_ref, dbs_ref,
             acc_ref, vn_ref, mixed_ref, dmix_ref, dvn_ref, dbias_ref):
        i = pl.program_id(0)
        _init_small(i == 0, small_ref, acc_ref)

        @pl.when(i == 0)
        def _():
            dws_ref[...] = jnp.zeros(dws_ref.shape, F32)
            dbs_ref[...] = jnp.zeros(dbs_ref.shape, F32)
            dbias_ref[...] = jnp.zeros(dbias_ref.shape, F32)

        dmerged = _nt(dx_ref[...].astype(BF16), wo_ref[...].reshape(d, d))
        ya = ya_ref[...].astype(F32)
        yb = yb_ref[...].astype(F32)
        sa = _sig(ga_ref[...].astype(F32) + gba_ref[...])
        sb = _sig(gb_ref[...].astype(F32) + gbb_ref[...])
        mg_ref[...] = (sa * ya + sb * yb).astype(BF16)
        dya = (dmerged * sa).astype(BF16)
        dyb = (dmerged * sb).astype(BF16)
        dya_ref[...] = dya
        dyb_ref[...] = dyb
        dga = dmerged * ya * (sa * (1.0 - sa))
        dgb = dmerged * yb * (sb * (1.0 - sb))
        acc_ref[0] += _fold(dga)
        acc_ref[1] += _fold(dgb)
        dp_ref[:, 0:2 * d] = jnp.zeros((tm, 2 * d), BF16)
        dp_ref[:, 4 * d:5 * d] = dga.astype(BF16)
        dp_ref[:, 5 * d:6 * d] = dgb.astype(BF16)
        c2hat, rstd = _layernorm_stats(c1_ref[...].astype(F32))
        c2 = c2hat * lg_ref[...] + lb_ref[...]
        s2 = _sig(c2)
        c3_ref[...] = (c2 * s2).astype(BF16)
        dc3 = _nt(dya, wco_ref[...].reshape(d, d))
        dc2 = dc3 * (s2 * (1.0 + c2 * (1.0 - s2)))
        acc_ref[2] += _fold(dc2 * c2hat)
        acc_ref[3] += _fold(dc2)
        dc2hat = dc2 * lg_ref[...]
        dc1_ref[...] = (rstd * (dc2hat - jnp.mean(dc2hat, axis=-1, keepdims=True)
                                - c2hat * jnp.mean(dc2hat * c2hat, axis=-1, keepdims=True))).astype(BF16)
        vhat, rstd_v = _layernorm_stats(v_ref[...].astype(F32))
        vn_ref[...] = (vhat * sg_ref[...] + sb_ref[...]).astype(BF16)
        _spatial_mix(ws_ref, vn_ref, bias_ref, mixed_ref, tm)
        u = u_ref[...].astype(F32)
        mixed = mixed_ref[...]
        gt_ref[...] = (u * mixed).astype(BF16)
        dgated = _nt(dyb, wso_ref[...].reshape(d, d))
        dp_ref[:, 2 * d:3 * d] = (dgated * mixed).astype(BF16)
        dmix_ref[...] = dgated * u
        for ci in range(tm // CHUNK):
            rs = slice(ci * CHUNK, (ci + 1) * CHUNK)
            dbias_ref[...] += dmix_ref[rs, :]
            for g in range(GROUPS):
                ls = slice(g * LANES, (g + 1) * LANES)
                dm = dmix_ref[rs, ls].astype(BF16)
                dws_ref[g] += _nt(dm, vn_ref[rs, ls])
                dvn_ref[rs, ls] = _nn(wst_ref[g], dm)
        dvn = dvn_ref[...]
        acc_ref[4] += _fold(dvn * vhat)
        acc_ref[5] += _fold(dvn)
        dvhat = dvn * sg_ref[...]
        dp_ref[:, 3 * d:4 * d] = (rstd_v * (dvhat - jnp.mean(dvhat, axis=-1, keepdims=True)
                                           - vhat * jnp.mean(dvhat * vhat, axis=-1, keepdims=True))).astype(BF16)
        _finish_small(i == nsteps - 1, small_ref, acc_ref, nq)

        @pl.when(i == nsteps - 1)
        def _():
            db = dbias_ref[...]
            hi = db.astype(BF16)
            lo = (db - hi.astype(F32)).astype(BF16)
            dbs_ref[...] = _nt(sel_ref[...], hi) + _nt(sel_ref[...], lo)

    cols = [_tile(tm, d, j) for j in (2, 3, 4, 5)]
    return _call(
        body, name=name, grid=(nsteps,),
        args=(dx1, proj, proj, proj, proj, c1, ya, yb,
              small['conv_ln_g'], small['conv_ln_b'], w['w_conv_out'], small['sgu_ln_g'], small['sgu_ln_b'],
              small['ws'], small['wst'], small['bias_full'], w['w_sgu_out'], small['gba'], small['gbb'], w['w_o'],
              small['group_sel']),
        in_specs=[_tile(tm, d)] + cols + [_tile(tm, d)] * 3 + [
            _row(d), _row(d), _weight(w['w_conv_out']),
            _row(d), _row(d), _resident(small['ws'].shape), _resident(small['wst'].shape),
            _resident(small['bias_full'].shape), _weight(w['w_sgu_out']), _row(d), _row(d),
            _weight(w['w_o']), _resident(small['group_sel'].shape)],
        out_specs=[_tile(tm, 6 * d)] + [_tile(tm, d)] * 6 + [
            _resident((SUBLANES, d)), _resident((GROUPS, CHUNK, CHUNK)), _resident((GROUPS, CHUNK))],
        out_shape=[_sds((t, 6 * d), BF16)] + [_sds((t, d), BF16)] * 6 + [
            _sds((SUBLANES, d), F32), _sds((GROUPS, CHUNK, CHUNK), F32), _sds((GROUPS, CHUNK), F32)],
        scratch_shapes=[pltpu.VMEM((nq, SUBLANES, d), F32), pltpu.VMEM((tm, d), BF16), pltpu.VMEM((tm, d), F32),
                        pltpu.VMEM((tm, d), F32), pltpu.VMEM((tm, d), F32), pltpu.VMEM((CHUNK, d), F32)],
        semantics=("arbitrary",), comm=comm)


def _conv_bwd(dproj, dc1, proj, w, pairs, seq, tm, name, comm):
    t, d = dc1.shape
    nsteps = t // tm
    nblk = d // LANES
    npairs = len(pairs)
    hp_v, hn_v = _halo_specs(tm, t, d, 0)
    hp_g, hn_g = _halo_specs(tm, t, d, 1)
    hp_d, hn_d = _halo_specs(tm, t, d, 0)

    def body(*refs):
        (dp_in, dc_ref, dcp, dcn, av, ag, avp, agp, avn, agn, cw_ref), refs = refs[:11], refs[11:]
        ab_refs, refs = refs[:2 * npairs], refs[2 * npairs:]
        (dp_ref, dcw_ref, small_ref), refs = refs[:3], refs[3:]
        grad_refs, refs = refs[:npairs], refs[npairs:]
        acc_ref, cbuf, dbuf, dc0f, accw, gacc = refs
        del dp_in
        i = pl.program_id(0)
        _init_small(i == 0, small_ref, acc_ref)

        @pl.when(i == 0)
        def _():
            accw[...] = jnp.zeros(accw.shape, F32)
            dcw_ref[...] = jnp.zeros(dcw_ref.shape, F32)
            gacc[...] = jnp.zeros(gacc.shape, F32)

        first = (i * tm) % seq == 0
        last = ((i + 1) * tm) % seq == 0
        _fill_glu_buffer(cbuf, av, ag, avp, agp, avn, agn, first, last, tm)
        dc1v = dc_ref[...].astype(F32)
        _fill_halo_buffer(dbuf, dcp[...].astype(F32), dc1v, dcn[...].astype(F32), first, last, tm)
        acc_ref[0] += _fold(dc1v)
        _dwconv(dbuf, cw_ref, dc0f, tm, flip=True, rc=CONV_ROWS, unroll=True)

        def rows(r0):
            for kb in range(nblk):
                dv = dbuf[kb, r0 + HALO:r0 + HALO + CONV_ROWS, :]
                for k in range(CONV_TAPS):
                    accw[kb, k] += _fold(dv * _shifted(cbuf, kb, r0, 1 + k, CONV_ROWS))

        _row_chunks(tm, CONV_ROWS, True, rows)
        for q in range(npairs):
            aq = ab_refs[2 * q][...].astype(BF16)
            for c in range(0, d, TN_COLS):
                gacc[q, :, c:c + TN_COLS] += _tn(aq, ab_refs[2 * q + 1][:, c:c + TN_COLS].astype(BF16))
        sg = _sig(ag[...].astype(F32))
        avv = av[...].astype(F32)
        dc0 = dc0f[...]
        dp_ref[:, 0:d] = (dc0 * sg).astype(BF16)
        dp_ref[:, d:2 * d] = (dc0 * avv * (sg * (1.0 - sg))).astype(BF16)
        _finish_small(i == nsteps - 1, small_ref, acc_ref, 1)

        @pl.when(i == nsteps - 1)
        def _():
            for kb in range(nblk):
                dcw_ref[kb] = jnp.sum(accw[kb], axis=1)
            for q in range(npairs):
                grad_refs[q][...] = gacc[q].astype(BF16)

    return _call(
        body, name=name, grid=(nsteps,),
        args=(dproj, dc1, dc1, dc1, proj, proj, proj, proj, proj, proj, w['conv_w']) + tuple(a for ab in pairs for a in ab),
        in_specs=[pl.BlockSpec(memory_space=pl.ANY), _tile(tm, d), hp_d, hn_d, _tile(tm, d, 0), _tile(tm, d, 1),
                  hp_v, hp_g, hn_v, hn_g, _weight(w['conv_w'])] + [_tile(tm, d)] * (2 * npairs),
        out_specs=[_tile(tm, 2 * d), _resident((nblk, CONV_TAPS_PADDED, LANES)), _resident((SUBLANES, d))]
        + [_resident((d, d))] * npairs,
        out_shape=[_sds(dproj.shape, BF16), _sds((nblk, CONV_TAPS_PADDED, LANES), F32), _sds((SUBLANES, d), F32)]
        + [_sds((d, d), BF16)] * npairs,
        scratch_shapes=[pltpu.VMEM((1, SUBLANES, d), F32), pltpu.VMEM((nblk, tm + 2 * HALO, LANES), F32),
                        pltpu.VMEM((nblk, tm + 2 * HALO, LANES), F32), pltpu.VMEM((tm, d), F32),
                        pltpu.VMEM((nblk, CONV_TAPS_PADDED, SUBLANES, LANES), F32),
                        pltpu.VMEM((npairs, d, d), F32)],
        aliases={0: 0}, semantics=("arbitrary",), comm=comm)


def _in_proj_bwd(dproj, x, dx1, gain, w, tm, name, comm):
    t, d = x.shape
    nb, _, bw = w.shape
    nsteps = t // tm

    def body(dp_ref, x_ref, dx1_ref, g_ref, w_ref, dx_ref, small_ref, acc_ref):
        i = pl.program_id(0)
        _init_small(i == 0, small_ref, acc_ref)
        dh = jnp.zeros((tm, d), F32)
        for k in range(nb):
            dh = dh + _nt(dp_ref[:, k * bw:(k + 1) * bw], w_ref[k])
        xv = x_ref[...]
        r = lax.rsqrt(jnp.mean(xv * xv, axis=-1, keepdims=True) + EPS)
        xhat = xv * r
        acc_ref[0] += _fold(dh * xhat)
        dxhat = dh * g_ref[...]
        dx_ref[...] = dx1_ref[...] + r * (dxhat - xhat * jnp.mean(dxhat * xhat, axis=-1, keepdims=True))
        _finish_small(i == nsteps - 1, small_ref, acc_ref, 1)

    return _call(body, name=name, grid=(nsteps,), args=(dproj, x, dx1, gain, w),
                 in_specs=[_tile(tm, nb * bw), _tile(tm, d), _tile(tm, d), _row(d), _weight(w)],
                 out_specs=[_tile(tm, d), _resident((SUBLANES, d))],
                 out_shape=[_sds((t, d), F32), _sds((SUBLANES, d), F32)],
                 scratch_shapes=[pltpu.VMEM((1, SUBLANES, d), F32)], semantics=("arbitrary",), comm=comm)


def _adam(wv, g, mv, vv):
    m = ADAM_B1 * mv + (1.0 - ADAM_B1) * g
    v = ADAM_B2 * vv + (1.0 - ADAM_B2) * jnp.square(g)
    m_hat = m / (1.0 - ADAM_B1 ** ADAM_STEP)
    v_hat = v / (1.0 - ADAM_B2 ** ADAM_STEP)
    delta = -ADAM_LR * (m_hat / (jnp.sqrt(v_hat) + ADAM_EPS) + ADAM_WD * wv)
    return delta, m, v


def _adamw_layer(layer, w, m, v, parts, prev, nsplit, name):
    nl, r, c = w.shape
    npart, pr, pc = parts.shape
    rt, prt = r // nsplit, pr // nsplit

    def body(w_ref, m_ref, v_ref, p_ref, *rest):
        g_ref, d_ref, nm_ref, nv_ref = rest[-4:]
        g = p_ref[0, 0:rt, 0:c].astype(F32)
        for s in range(1, npart):
            g = g + p_ref[s, 0:rt, 0:c].astype(F32)
        delta, mn, vn = _adam(w_ref[0], g, m_ref[0], v_ref[0])
        g_ref[0] = g
        d_ref[0] = delta
        nm_ref[0] = mn
        nv_ref[0] = vn

    wspec = pl.BlockSpec((1, rt, c), lambda i: (layer, i, 0))
    pspec = pl.BlockSpec((npart, prt, pc), lambda i: (0, i, 0))
    in_specs = [wspec, wspec, wspec, pspec]
    args = [w, m, v, parts]
    aliases = {}
    if prev is not None:
        in_specs += [pl.BlockSpec(memory_space=pl.ANY)] * 4
        args += list(prev)
        aliases = {4 + q: q for q in range(4)}
    return _call(body, name=name, grid=(nsplit,), args=args, in_specs=in_specs, out_specs=[wspec] * 4,
                 out_shape=[_sds(w.shape, F32)] * 4, aliases=aliases, semantics=("parallel",))[0]


VEC_ROWS = {'gate_bias': (0, 1), 'conv_ln_g': 2, 'conv_ln_b': 3, 'sgu_ln_g': 4, 'sgu_ln_b': 5, 'norm_ffn': SUBLANES}
FINAL_ROW = 2 * SUBLANES
LOSS_ROW = 2 * SUBLANES + 1
LATE_ROWS = {'norm_mix': 0, 'conv_b': SUBLANES}


def _adamw_small(gathered, params, moments_m, moments_v):
    names = list(params)
    nper = len(names)
    nl = len(gathered)

    def body(*refs):
        g_refs = [refs[4 * l:4 * l + 4] for l in range(nl)]
        rest = refs[4 * nl:]
        w_refs = dict(zip(names, rest[0:nper]))
        m_refs = dict(zip(names, rest[nper:2 * nper]))
        v_refs = dict(zip(names, rest[2 * nper:3 * nper]))
        outs = rest[3 * nper:]
        loss_ref = outs[4 * nper]
        o = {kind: dict(zip(names, outs[q * nper:(q + 1) * nper])) for q, kind in enumerate("gdmv")}

        def put(nm, idx, g):
            delta, mn, vn = _adam(w_refs[nm][idx], g, m_refs[nm][idx], v_refs[nm][idx])
            o["g"][nm][idx] = g
            o["d"][nm][idx] = delta
            o["m"][nm][idx] = mn
            o["v"][nm][idx] = vn

        def total(ref, *idx):
            g = ref[(0, *idx)]
            for s in range(1, NDEV):
                g = g + ref[(s, *idx)]
            return g

        for l, (vec_ref, dws_ref, dbs_ref, late_ref) in enumerate(g_refs):
            dd = vec_ref.shape[2]
            put('w_spatial', (l,), total(dws_ref))
            put('b_spatial', (l,), total(dbs_ref))
            for nm, rr in LATE_ROWS.items():
                put(nm, (slice(l, l + 1), slice(None)), total(late_ref, slice(rr, rr + 1)))
            for nm, rr in VEC_ROWS.items():
                if nm == 'gate_bias':
                    put(nm, (slice(l, l + 1), slice(0, dd)), total(vec_ref, slice(rr[0], rr[0] + 1)))
                    put(nm, (slice(l, l + 1), slice(dd, 2 * dd)), total(vec_ref, slice(rr[1], rr[1] + 1)))
                else:
                    put(nm, (slice(l, l + 1), slice(None)), total(vec_ref, slice(rr, rr + 1)))
        last = g_refs[nl - 1][0]
        put('norm_final', (slice(0, 1), slice(None)), total(last, slice(FINAL_ROW, FINAL_ROW + 1)))
        loss_ref[...] = total(last, slice(LOSS_ROW, LOSS_ROW + 1))

    ins = [a for g in gathered for a in g] + [params[n] for n in names] + [moments_m[n] for n in names] + [moments_v[n] for n in names]
    out_shape = [_sds(params[n].shape, F32) for n in names] * 4 + [_sds((1, gathered[0][0].shape[2]), F32)]
    res = pl.pallas_call(body, name="adamw_small", out_shape=out_shape, compiler_params=_params())(*ins)
    return {kind: dict(zip(names, res[q * nper:(q + 1) * nper])) for q, kind in enumerate("gdmv")}, res[4 * nper]


def _hidden_major(a):
    return jnp.swapaxes(a, 1, 2)


def _prepare_weights(p):
    d = p['w_in'].shape[1]
    (w_in,), _ = _cast_pad([p['w_in']], [(d, p['w_in'].shape[2])], "cast_w_in", None)
    rows = p['w_o'].shape[1]
    others = {'w_conv_out': (p['w_conv_out'], (rows, d)), 'w_sgu_out': (p['w_sgu_out'], (rows, d)),
              'w_o': (p['w_o'], (rows, d)),
              'w_ffn_gate': (_hidden_major(p['w_ffn_gate']), (FF_PAD, d)),
              'w_ffn_up': (_hidden_major(p['w_ffn_up']), (FF_PAD, d)),
              'w_ffn_down': (p['w_ffn_down'], (FF_PAD, d))}
    cast, got = _cast_pad([a for a, _ in others.values()], [s for _, s in others.values()], "cast_weights",
                          _Gather([w_in], [0]))
    shards = dict(zip(others, cast), w_in=w_in)
    shards['conv_w'] = jnp.pad(p['conv_w'][:, :, 0, :], ((0, 0), (0, CONV_TAPS_PADDED - CONV_TAPS), (0, 0)))
    return shards, got[0]


def _gather_of(shards, names, layer):
    return _Gather([shards[n] for n in names], [layer] * len(names))


def _layer_small(p, layer):
    d = p['norm_mix'].shape[1]
    ws = p['w_spatial'][layer]
    rows = {n: p[n][layer:layer + 1] for n in ('norm_mix', 'norm_ffn', 'conv_b', 'conv_ln_g', 'conv_ln_b',
                                               'sgu_ln_g', 'sgu_ln_b')}
    return {
        **rows,
        'ws': ws.astype(BF16), 'wst': jnp.swapaxes(ws, 1, 2).astype(BF16),
        'bias_full': jnp.repeat(p['b_spatial'][layer].T, LANES, axis=1),
        'gba': p['gate_bias'][layer:layer + 1, 0:d], 'gbb': p['gate_bias'][layer:layer + 1, d:2 * d],
        'group_sel': (jnp.arange(d)[None, :] // LANES == jnp.arange(GROUPS)[:, None]).astype(BF16),
    }


class _GradQueue:
    def __init__(self):
        self.pending = []
        self.done = {}

    def push(self, key, array):
        self.pending.append((key, array))

    def take(self):
        keys = [k for k, _ in self.pending]
        comm = _Scatter([a for _, a in self.pending]) if self.pending else None
        self.pending = []
        return keys, comm

    def put(self, keys, arrays):
        self.done.update(zip(keys, arrays))


def _forward_backward(p, shards, w_in, x, target, seq):
    nl = p['norm_mix'].shape[0]
    d = x.shape[1]
    smalls = [_layer_small(p, l) for l in range(nl)]
    saved = []
    for l in range(nl):
        (h, proj), got = _in_proj(x, smalls[l]['norm_mix'], w_in, min(TILE_IN_FWD, x.shape[0]), f"in_proj_{l}",
                                  _gather_of(shards, MIXER_WEIGHTS, l))
        w = dict(zip(MIXER_WEIGHTS, got), w_in=w_in)
        (c1, ya, yb, x1), got = _mixer_fwd(proj, x, w, smalls[l], seq, min(TILE_MIX_FWD, seq), f"mixer_fwd_{l}",
                                           _gather_of(shards, FFN_WEIGHTS, l))
        w.update(zip(FFN_WEIGHTS, got))
        w['w_gate_up'] = _pair_gate_up(w['w_ffn_gate'], w['w_ffn_up'], f"pair_gate_up_{l}")
        nxt = _gather_of(shards, ['w_in'], l + 1) if l + 1 < nl else None
        (h2, gg, uu, x2), got = _ffn_fwd(x1, smalls[l]['norm_ffn'], w, TILE_FFN_FWD, f"ffn_fwd_{l}", nxt)
        saved.append(dict(x=x, h=h, proj=proj, c1=c1, ya=ya, yb=yb, x1=x1, h2=h2, gg=gg, uu=uu, w=w))
        x = x2
        if got:
            w_in = got[0]
    dx, small_loss = _loss_bwd(x, p['norm_final'][None, :], target, TILE_LOSS, "loss_bwd")
    queue = _GradQueue()
    small_gathered = [None] * nl
    small_pending = None
    rows = d // NDEV
    hid = NDEV * FF_PAD
    for l in reversed(range(nl)):
        s = saved[l]
        w = s['w']

        def hosted(fn, *args, extra=None):
            keys, comm = queue.take()
            res, got = fn(*args, _together(comm, extra))
            queue.put(keys, got[:len(keys)])
            return res, got[len(keys):]

        def tn(key, a, b, a_blk, b_blk, stack, shard, reshape=None, host=False):
            keys, comm = queue.take() if host else ([], None)
            g, got = _matmul_tn(a, b, a_blk, b_blk, stack, shard, TILE_TN, f"dw_{key}_{l}", comm)
            queue.put(keys, got)
            queue.push((l, key), g if reshape is None else g.reshape(reshape))

        (act, dgg, duu, dx1, small_ffn), _ = hosted(_ffn_bwd, dx, s['x1'], smalls[l]['norm_ffn'], s['gg'], s['uu'], w,
                                                    TILE_FFN_BWD, f"ffn_bwd_{l}")
        tn('w_ffn_gate', dgg, s['h2'], hid, d, 'a', FF_PAD)
        tn('w_ffn_up', duu, s['h2'], hid, d, 'a', FF_PAD)
        (dproj, dc1, merged, c3, gated, dya, dyb, small_mix, dws, dbs), _ = hosted(
            _mixer_bwd, dx1, s['proj'], s['c1'], s['ya'], s['yb'], w, smalls[l], TILE_MIX, f"mixer_bwd_{l}")
        tn('w_ffn_down', act, dx, hid, d, 'a', FF_PAD)
        blocks = [small_mix, small_ffn] + ([small_loss] if l == nl - 1 else [])
        small_main = [jnp.concatenate(blocks, axis=0), dws, dbs]
        (dproj, g_conv, small_conv, g_o, g_co, g_so), got = hosted(
            _conv_bwd, dproj, dc1, s['proj'], w, [(merged, dx1), (c3, dya), (gated, dyb)], seq, TILE_MIX,
            f"conv_bwd_{l}", extra=_together(small_pending[1] if small_pending else None,
                                             _Gather(small_main) if l == 0 else None))
        if small_pending:
            small_gathered[small_pending[0]], got = got[:4], got[4:]
            small_pending = None
        small_main_gathered = got
        queue.push((l, 'conv_w'), g_conv)
        queue.push((l, 'w_o'), g_o.reshape(NDEV, rows, d))
        queue.push((l, 'w_conv_out'), g_co.reshape(NDEV, rows, d))
        queue.push((l, 'w_sgu_out'), g_so.reshape(NDEV, rows, d))
        tn('w_in', s['h'], dproj, d, hid, 'b', w['w_in'].shape[2], host=True)
        if l == 0:
            (dx, small_in), _ = hosted(_in_proj_bwd, dproj, s['x'], dx1, smalls[l]['norm_mix'], w['w_in'], TILE_IN,
                                       f"in_proj_bwd_{l}")
        else:
            (dx, small_in), _ = _in_proj_bwd(dproj, s['x'], dx1, smalls[l]['norm_mix'], w['w_in'], TILE_IN,
                                             f"in_proj_bwd_{l}", None)
        small_late = jnp.concatenate([small_in, small_conv], axis=0)
        small_pending = (l, _Gather(small_main + [small_late]))
    keys, comm = queue.take()
    last = _exchange_alone(_together(comm, _Gather([small_late])), "exchange_last_grads")
    queue.put(keys, last[:len(keys)])
    small_gathered[0] = small_main_gathered + last[len(keys):]
    return small_gathered, dx, queue.done


def _train_step(p, m, v, x3, target3):
    nl = p['norm_mix'].shape[0]
    bsz, seq, d = x3.shape
    x = x3.reshape(bsz * seq, d)
    target = target3.reshape(bsz * seq, d)
    shards, w_in = _prepare_weights(p)
    small_gathered, dx, exchanged = _forward_backward(p, shards, w_in, x, target, seq)

    out = {kind: {} for kind in "gdmv"}
    splits = {'w_in': 4, 'w_conv_out': 1, 'w_sgu_out': 1, 'w_o': 1, 'w_ffn_gate': 1, 'w_ffn_up': 1, 'w_ffn_down': 1, 'conv_w': 1}
    for n in splits:
        if n == 'conv_w':
            pad = ((0, 0), (0, CONV_TAPS_PADDED - CONV_TAPS), (0, 0))
            wl, ml, vl = (jnp.pad(a[n][:, :, 0, :], pad) for a in (p, m, v))
        elif n in ('w_ffn_gate', 'w_ffn_up'):
            wl, ml, vl = (_hidden_major(a[n]) for a in (p, m, v))
        else:
            wl, ml, vl = p[n], m[n], v[n]
        prev = None
        for l in range(nl):
            prev = _adamw_layer(l, wl, ml, vl, exchanged[(l, n)], prev, splits[n], f"adamw_{n}_{l}")
        for kind, arr in zip("gdmv", prev):
            if n == 'conv_w':
                arr = arr[:, 0:CONV_TAPS, None, :]
            elif n in ('w_ffn_gate', 'w_ffn_up'):
                arr = _hidden_major(arr)
            out[kind][n] = arr
    small_names = ['norm_mix', 'gate_bias', 'conv_b', 'conv_ln_g', 'conv_ln_b', 'sgu_ln_g', 'sgu_ln_b', 'w_spatial',
                   'b_spatial', 'norm_ffn', 'norm_final']

    def two_d(a):
        return a[None, :] if a.ndim == 1 else a

    res, loss_row = _adamw_small(small_gathered, {n: two_d(p[n]) for n in small_names},
                                 {n: two_d(m[n]) for n in small_names}, {n: two_d(v[n]) for n in small_names})
    loss = loss_row[0, 0]
    for kind in "gdmv":
        for n in small_names:
            out[kind][n] = res[kind][n].reshape(p[n].shape)
    grad_x = dx.reshape(bsz, seq, d)
    return (loss, grad_x, *[out[kind][n] for kind in "gdmv" for n in WEIGHT_NAMES])


def kernel(x, norm_mix, w_in, gate_bias, conv_w, conv_b, conv_ln_g, conv_ln_b, w_conv_out, sgu_ln_g, sgu_ln_b, w_spatial, b_spatial, w_sgu_out, w_o, norm_ffn, w_ffn_gate, w_ffn_up, w_ffn_down, norm_final, loss_target, m_norm_mix, m_w_in, m_gate_bias, m_conv_w, m_conv_b, m_conv_ln_g, m_conv_ln_b, m_w_conv_out, m_sgu_ln_g, m_sgu_ln_b, m_w_spatial, m_b_spatial, m_w_sgu_out, m_w_o, m_norm_ffn, m_w_ffn_gate, m_w_ffn_up, m_w_ffn_down, m_norm_final, v_norm_mix, v_w_in, v_gate_bias, v_conv_w, v_conv_b, v_conv_ln_g, v_conv_ln_b, v_w_conv_out, v_sgu_ln_g, v_sgu_ln_b, v_w_spatial, v_b_spatial, v_w_sgu_out, v_w_o, v_norm_ffn, v_w_ffn_gate, v_w_ffn_up, v_w_ffn_down, v_norm_final):
    p = dict(zip(WEIGHT_NAMES, (norm_mix, w_in, gate_bias, conv_w, conv_b, conv_ln_g, conv_ln_b, w_conv_out, sgu_ln_g, sgu_ln_b, w_spatial, b_spatial, w_sgu_out, w_o, norm_ffn, w_ffn_gate, w_ffn_up, w_ffn_down, norm_final)))
    m = dict(zip(WEIGHT_NAMES, (m_norm_mix, m_w_in, m_gate_bias, m_conv_w, m_conv_b, m_conv_ln_g, m_conv_ln_b, m_w_conv_out, m_sgu_ln_g, m_sgu_ln_b, m_w_spatial, m_b_spatial, m_w_sgu_out, m_w_o, m_norm_ffn, m_w_ffn_gate, m_w_ffn_up, m_w_ffn_down, m_norm_final)))
    v = dict(zip(WEIGHT_NAMES, (v_norm_mix, v_w_in, v_gate_bias, v_conv_w, v_conv_b, v_conv_ln_g, v_conv_ln_b, v_w_conv_out, v_sgu_ln_g, v_sgu_ln_b, v_w_spatial, v_b_spatial, v_w_sgu_out, v_w_o, v_norm_ffn, v_w_ffn_gate, v_w_ffn_up, v_w_ffn_down, v_norm_final)))
    return _train_step(p, m, v, x, loss_target)
```

```python
import math

import jax
import jax.numpy as jnp
from jax import lax
from jax.experimental import pallas as pl
from jax.experimental.pallas import tpu as pltpu

F32 = jnp.float32
BF16 = jnp.bfloat16
MESH_ID = pl.DeviceIdType.MESH

NDEV = 8
EPS = 1e-6
CONV_TAPS = 31
CONV_TAPS_PADDED = 32
HALO = 16
CONV_ROWS = 128
CONV_ROWS_LOOP = 128
LANES = 128
SUBLANES = 8
CHUNK = 128
GROUPS = 8
FF_PAD = 384
FF_PAIR = 2 * FF_PAD
TN_COLS = 512
VMEM_LIMIT_BYTES = 56 * 1024 * 1024

ADAM_LR = 0.001
ADAM_B1 = 0.9
ADAM_B2 = 0.999
ADAM_EPS = 1e-08
ADAM_WD = 0.01
ADAM_STEP = 10

TILE_IN = 512
TILE_IN_FWD = 1024
TILE_MIX = 256
TILE_MIX_FWD = 512
TILE_FFN_FWD = 1024
TILE_FFN_BWD = 512
TILE_TN = 1024
TILE_LOSS = 512

WEIGHT_NAMES = ['norm_mix', 'w_in', 'gate_bias', 'conv_w', 'conv_b', 'conv_ln_g', 'conv_ln_b', 'w_conv_out',
                'sgu_ln_g', 'sgu_ln_b', 'w_spatial', 'b_spatial', 'w_sgu_out', 'w_o', 'norm_ffn', 'w_ffn_gate',
                'w_ffn_up', 'w_ffn_down', 'norm_final']
MIXER_WEIGHTS = ['w_conv_out', 'w_sgu_out', 'w_o', 'conv_w']
FFN_WEIGHTS = ['w_ffn_gate', 'w_ffn_up', 'w_ffn_down']


def _sds(shape, dtype):
    return jax.ShapeDtypeStruct(tuple(shape), dtype)


def _params(*sem):
    return pltpu.CompilerParams(dimension_semantics=sem or None, vmem_limit_bytes=VMEM_LIMIT_BYTES)


def _nn(a, b):
    return jnp.dot(a, b, preferred_element_type=F32)


def _nt(a, b):
    return lax.dot_general(a, b, (((1,), (1,)), ((), ())), preferred_element_type=F32)


def _tn(a, b):
    return lax.dot_general(a, b, (((0,), (0,)), ((), ())), preferred_element_type=F32)


def _sig(v):
    return jax.nn.sigmoid(v)


def _fold(v):
    r, c = v.shape
    return jnp.sum(v.reshape(r // SUBLANES, SUBLANES, c), axis=0)


def _tile(tm, n, j=0):
    return pl.BlockSpec((tm, n), lambda i, *_: (i, j))


def _row(n):
    return pl.BlockSpec((1, n), lambda *_: (0, 0))


def _resident(shape):
    nd = len(shape)
    return pl.BlockSpec(tuple(shape), lambda *_: (0,) * nd)


def _weight(w):
    nd = w.ndim
    return pl.BlockSpec(tuple(w.shape), lambda *_: (0,) * nd, pipeline_mode=pl.Buffered(1))


def _peer(rel):
    x, y, c = lax.axis_index("x"), lax.axis_index("y"), lax.axis_index("c")
    return (1 - x if rel & 4 else x, 1 - y if rel & 2 else y, 1 - c if rel & 1 else c)


def _slot(pos):
    return 4 * pos[0] + 2 * pos[1] + pos[2]


class _Exchange:
    def __init__(self, arrays, layers=None):
        self.arrays = list(arrays)
        self.layers = list(layers) if layers is not None else [None] * len(self.arrays)

    def scratch(self):
        n = len(self.arrays)
        return [pltpu.SemaphoreType.DMA((n, NDEV)), pltpu.SemaphoreType.DMA((n, NDEV)), pltpu.SemaphoreType.DMA((n,))]

    def _src(self, ins, j):
        return ins[j] if self.layers[j] is None else ins[j].at[self.layers[j]]

    def _block_shape(self, j):
        a = self.arrays[j]
        return a.shape if self.layers[j] is None else a.shape[1:]


class _Gather(_Exchange):
    chips = (4, 2, 6)

    def out_shape(self):
        return [_sds((NDEV,) + tuple(self._block_shape(j)), a.dtype) for j, a in enumerate(self.arrays)]

    @staticmethod
    def _copy(outs, sems, j, sem, block_rel, to_rel, src=None):
        blk = outs[j].at[_slot(_peer(block_rel))]
        return pltpu.make_async_remote_copy(
            src_ref=blk if src is None else src, dst_ref=blk,
            send_sem=sems[0].at[j, sem], recv_sem=sems[1].at[j, sem],
            device_id=_peer(to_rel), device_id_type=MESH_ID)

    def _local(self, ins, outs, sems, j):
        return pltpu.make_async_copy(self._src(ins, j), outs[j].at[_slot(_peer(0))], sems[2].at[j])

    def start(self, ins, outs, sems):
        for j in range(len(self.arrays)):
            self._local(ins, outs, sems, j).start()
            for rel in (1,) + self.chips:
                self._copy(outs, sems, j, rel, 0, rel, src=self._src(ins, j)).start()

    def forward(self, ins, outs, sems):
        for j in range(len(self.arrays)):
            for rel in self.chips:
                self._copy(outs, sems, j, rel, rel, 0).wait_recv()
                self._copy(outs, sems, j, rel ^ 1, rel, 1).start()

    def finish(self, ins, outs, sems):
        for j in range(len(self.arrays)):
            self._copy(outs, sems, j, 1, 1, 0).wait_recv()
            for rel in self.chips:
                self._copy(outs, sems, j, rel ^ 1, rel ^ 1, 0).wait_recv()
        for j in range(len(self.arrays)):
            for rel in (1,) + self.chips:
                self._copy(outs, sems, j, rel, 0, rel, src=self._src(ins, j)).wait_send()
            for rel in self.chips:
                self._copy(outs, sems, j, rel ^ 1, rel, 1).wait_send()
            self._local(ins, outs, sems, j).wait()


class _Scatter(_Exchange):
    def out_shape(self):
        return [_sds(a.shape, a.dtype) for a in self.arrays]

    @staticmethod
    def _copy(ins, outs, sems, j, rel):
        return pltpu.make_async_remote_copy(
            src_ref=ins[j].at[_slot(_peer(rel))], dst_ref=outs[j].at[_slot(_peer(0))],
            send_sem=sems[0].at[j, rel], recv_sem=sems[1].at[j, rel],
            device_id=_peer(rel), device_id_type=MESH_ID)

    @staticmethod
    def _arrival(outs, sems, j, rel):
        blk = outs[j].at[_slot(_peer(rel))]
        return pltpu.make_async_remote_copy(
            src_ref=blk, dst_ref=blk, send_sem=sems[0].at[j, rel], recv_sem=sems[1].at[j, rel],
            device_id=_peer(rel), device_id_type=MESH_ID)

    @staticmethod
    def _local(ins, outs, sems, j):
        me = _slot(_peer(0))
        return pltpu.make_async_copy(ins[j].at[me], outs[j].at[me], sems[2].at[j])

    def start(self, ins, outs, sems):
        for j in range(len(self.arrays)):
            self._local(ins, outs, sems, j).start()
            for rel in range(1, NDEV):
                self._copy(ins, outs, sems, j, rel).start()

    def forward(self, ins, outs, sems):
        pass

    def finish(self, ins, outs, sems):
        for j in range(len(self.arrays)):
            for rel in range(1, NDEV):
                self._arrival(outs, sems, j, rel).wait_recv()
        for j in range(len(self.arrays)):
            for rel in range(1, NDEV):
                self._copy(ins, outs, sems, j, rel).wait_send()
            self._local(ins, outs, sems, j).wait()


class _Together:
    def __init__(self, parts):
        self.parts = [c for c in parts if c is not None]
        self.arrays = [a for c in self.parts for a in c.arrays]

    def out_shape(self):
        return [s for c in self.parts for s in c.out_shape()]

    def scratch(self):
        return [s for c in self.parts for s in c.scratch()]

    def _each(self, method, ins, outs, sems):
        at = 0
        for q, c in enumerate(self.parts):
            n = len(c.arrays)
            getattr(c, method)(ins[at:at + n], outs[at:at + n], sems[3 * q:3 * q + 3])
            at += n

    def start(self, ins, outs, sems):
        self._each("start", ins, outs, sems)

    def forward(self, ins, outs, sems):
        self._each("forward", ins, outs, sems)

    def finish(self, ins, outs, sems):
        self._each("finish", ins, outs, sems)


def _together(*parts):
    flat = []
    for c in parts:
        if c is not None:
            flat.extend(c.parts if isinstance(c, _Together) else [c])
    return _Together(flat) if flat else None


def _call(body, *, name, args, in_specs, out_specs, out_shape, grid=(), scratch_shapes=(), semantics=(),
          aliases=None, comm=None):
    in_specs, out_specs, out_shape = list(in_specs), list(out_specs), list(out_shape)
    scratch_shapes = list(scratch_shapes)
    if comm is None:
        res = pl.pallas_call(
            body, name=name, grid=grid, in_specs=in_specs, out_specs=out_specs, out_shape=out_shape,
            scratch_shapes=scratch_shapes, input_output_aliases=aliases or {},
            compiler_params=_params(*semantics))(*args)
        return list(res), []
    n_in, n_out, n_scr, nc = len(in_specs), len(out_specs), len(scratch_shapes), len(comm.arrays)
    total = math.prod(grid)
    middle = min((total * 5) // 8, total - 1)

    def hosted(*refs):
        ins, cins = refs[:n_in], refs[n_in:n_in + nc]
        o0 = n_in + nc
        outs, couts = refs[o0:o0 + n_out], refs[o0 + n_out:o0 + n_out + nc]
        s0 = o0 + n_out + nc
        scr, sems = refs[s0:s0 + n_scr], refs[s0 + n_scr:]
        if total == 1:
            comm.start(cins, couts, sems)
            body(*ins, *outs, *scr)
            comm.forward(cins, couts, sems)
            comm.finish(cins, couts, sems)
            return
        step = 0
        for axis, size in enumerate(grid):
            step = step * size + pl.program_id(axis)
        pl.when(step == 0)(lambda: comm.start(cins, couts, sems))
        body(*ins, *outs, *scr)
        pl.when(step == middle)(lambda: comm.forward(cins, couts, sems))
        pl.when(step == total - 1)(lambda: comm.finish(cins, couts, sems))

    any_spec = pl.BlockSpec(memory_space=pl.ANY)
    res = pl.pallas_call(
        hosted, name=name, grid=grid,
        in_specs=in_specs + [any_spec] * nc, out_specs=out_specs + [any_spec] * nc,
        out_shape=out_shape + comm.out_shape(), scratch_shapes=scratch_shapes + comm.scratch(),
        input_output_aliases=aliases or {}, compiler_params=_params(*(("arbitrary",) * len(grid))),
    )(*args, *comm.arrays)
    return list(res[:n_out]), list(res[n_out:])


def _exchange_alone(comm, name):
    return _call(lambda: None, name=name, args=(), in_specs=(), out_specs=(), out_shape=(), comm=comm)[1]


def _cast_pad(ws, shapes, name, comm):
    n = len(ws)
    nl = ws[0].shape[0]

    def body(*refs):
        for w, (rows, cols), w_ref, o_ref in zip(ws, shapes, refs[:n], refs[n:]):
            r, c = w.shape[1:]
            if (rows, cols) != (r, c):
                o_ref[...] = jnp.zeros(o_ref.shape, BF16)
            o_ref[0, 0:r, 0:c] = w_ref[0].astype(BF16)

    return _call(body, name=name, grid=(nl,), args=tuple(ws),
                 in_specs=[pl.BlockSpec((1,) + w.shape[1:], lambda i: (i, 0, 0)) for w in ws],
                 out_specs=[pl.BlockSpec((1,) + tuple(s), lambda i: (i, 0, 0)) for s in shapes],
                 out_shape=[_sds((nl,) + tuple(s), BF16) for s in shapes], semantics=("parallel",), comm=comm)


def _in_proj(x, gain, w, tm, name, comm):
    t, d = x.shape
    nb, _, bw = w.shape

    def body(x_ref, g_ref, w_ref, h_ref, p_ref):
        xv = x_ref[...]
        r = lax.rsqrt(jnp.mean(xv * xv, axis=-1, keepdims=True) + EPS)
        h = (xv * r * g_ref[...]).astype(BF16)
        h_ref[...] = h
        for k in range(nb):
            p_ref[:, k * bw:(k + 1) * bw] = _nn(h, w_ref[k]).astype(BF16)

    return _call(body, name=name, grid=(t // tm,), args=(x, gain, w),
                 in_specs=[_tile(tm, d), _row(d), _weight(w)],
                 out_specs=[_tile(tm, d), _tile(tm, nb * bw)],
                 out_shape=[_sds((t, d), BF16), _sds((t, nb * bw), BF16)],
                 semantics=("parallel",), comm=comm)


def _halo_specs(tm, t, d, col):
    nh, nhb = tm // HALO, t // HALO
    prev = pl.BlockSpec((HALO, d), lambda i: (jnp.maximum(i * nh - 1, 0), col))
    nxt = pl.BlockSpec((HALO, d), lambda i: (jnp.minimum((i + 1) * nh, nhb - 1), col))
    return prev, nxt


def _row_chunks(tm, rc, unroll, rows):
    n = tm // rc
    if unroll:
        for j in range(n):
            rows(j * rc)
    else:
        def step(j, carry):
            rows(pl.multiple_of(j * rc, rc))
            return carry

        lax.fori_loop(0, n, step, 0)


def _shifted(buf_ref, kb, r0, off, rc):
    return buf_ref[kb, pl.ds(r0 + off, rc), :]


def _dwconv(buf_ref, w_ref, out_ref, tm, flip, rc, unroll):
    nblk = out_ref.shape[1] // LANES

    def rows(r0):
        for kb in range(nblk):
            acc = jnp.zeros((rc, LANES), F32)
            for k in range(CONV_TAPS):
                off = (CONV_TAPS - k) if flip else (1 + k)
                acc = acc + w_ref[kb, k:k + 1, :] * _shifted(buf_ref, kb, r0, off, rc)
            out_ref[pl.ds(r0, rc), kb * LANES:(kb + 1) * LANES] = acc

    _row_chunks(tm, rc, unroll, rows)


def _fill_halo_buffer(buf, prev, body, nxt, first, last, tm):
    prev = jnp.where(first, 0.0, prev)
    nxt = jnp.where(last, 0.0, nxt)
    for kb in range(buf.shape[0]):
        lanes = slice(kb * LANES, (kb + 1) * LANES)
        buf[kb, 0:HALO, :] = prev[:, lanes]
        buf[kb, HALO:HALO + tm, :] = body[:, lanes]
        buf[kb, HALO + tm:HALO + tm + HALO, :] = nxt[:, lanes]


def _fill_glu_buffer(cbuf, av, ag, avp, agp, avn, agn, first, last, tm):
    c0p = avp[...].astype(F32) * _sig(agp[...].astype(F32))
    c0n = avn[...].astype(F32) * _sig(agn[...].astype(F32))
    c0 = av[...].astype(F32) * _sig(ag[...].astype(F32))
    _fill_halo_buffer(cbuf, c0p, c0, c0n, first, last, tm)


def _layernorm_stats(v):
    mu = jnp.mean(v, axis=-1, keepdims=True)
    cen = v - mu
    rstd = lax.rsqrt(jnp.mean(cen * cen, axis=-1, keepdims=True) + EPS)
    return cen * rstd, rstd


def _spatial_mix(ws_ref, vn_ref, bias_ref, mixed_ref, tm):
    for ci in range(tm // CHUNK):
        rs = slice(ci * CHUNK, (ci + 1) * CHUNK)
        for g in range(GROUPS):
            ls = slice(g * LANES, (g + 1) * LANES)
            mixed_ref[rs, ls] = _nn(ws_ref[g], vn_ref[rs, ls]) + bias_ref[:, ls]


def _mixer_fwd(proj, x, w, small, seq, tm, name, comm):
    t, d = x.shape
    hp_v, hn_v = _halo_specs(tm, t, d, 0)
    hp_g, hn_g = _halo_specs(tm, t, d, 1)

    def body(av, ag, u_ref, v_ref, ga_ref, gb_ref, avp, agp, avn, agn, x_ref,
             cw_ref, cb_ref, lg_ref, lb_ref, wco_ref, sg_ref, sb_ref, ws_ref, bias_ref, wso_ref,
             gba_ref, gbb_ref, wo_ref,
             c1_ref, ya_ref, yb_ref, x1_ref, cbuf, c1f, vn_ref, mixed_ref):
        i = pl.program_id(0)
        first = (i * tm) % seq == 0
        last = ((i + 1) * tm) % seq == 0
        _fill_glu_buffer(cbuf, av, ag, avp, agp, avn, agn, first, last, tm)
        _dwconv(cbuf, cw_ref, c1f, tm, flip=False, rc=CONV_ROWS_LOOP, unroll=False)
        c1 = c1f[...] + cb_ref[...]
        c1_ref[...] = c1.astype(BF16)
        c2hat, _ = _layernorm_stats(c1)
        c2 = c2hat * lg_ref[...] + lb_ref[...]
        c3 = (c2 * _sig(c2)).astype(BF16)
        ya = _nn(c3, wco_ref[...].reshape(d, d))
        ya_ref[...] = ya.astype(BF16)
        vhat, _ = _layernorm_stats(v_ref[...].astype(F32))
        vn_ref[...] = (vhat * sg_ref[...] + sb_ref[...]).astype(BF16)
        _spatial_mix(ws_ref, vn_ref, bias_ref, mixed_ref, tm)
        gated = (u_ref[...].astype(F32) * mixed_ref[...]).astype(BF16)
        yb = _nn(gated, wso_ref[...].reshape(d, d))
        yb_ref[...] = yb.astype(BF16)
        sa = _sig(ga_ref[...].astype(F32) + gba_ref[...])
        sb = _sig(gb_ref[...].astype(F32) + gbb_ref[...])
        merged = (sa * ya + sb * yb).astype(BF16)
        x1_ref[...] = x_ref[...] + _nn(merged, wo_ref[...].reshape(d, d))

    cols = [_tile(tm, d, j) for j in range(6)]
    return _call(
        body, name=name, grid=(t // tm,),
        args=(proj,) * 10 + (x, w['conv_w'], small['conv_b'], small['conv_ln_g'], small['conv_ln_b'], w['w_conv_out'],
                             small['sgu_ln_g'], small['sgu_ln_b'], small['ws'], small['bias_full'], w['w_sgu_out'],
                             small['gba'], small['gbb'], w['w_o']),
        in_specs=cols + [hp_v, hp_g, hn_v, hn_g, _tile(tm, d),
                         _weight(w['conv_w']), _row(d), _row(d), _row(d),
                         _weight(w['w_conv_out']), _row(d), _row(d),
                         _resident(small['ws'].shape), _resident(small['bias_full'].shape),
                         _weight(w['w_sgu_out']), _row(d), _row(d), _weight(w['w_o'])],
        out_specs=[_tile(tm, d)] * 4,
        out_shape=[_sds((t, d), BF16)] * 3 + [_sds((t, d), F32)],
        scratch_shapes=[pltpu.VMEM((d // LANES, tm + 2 * HALO, LANES), F32), pltpu.VMEM((tm, d), F32),
                        pltpu.VMEM((tm, d), BF16), pltpu.VMEM((tm, d), F32)],
        semantics=("parallel",), comm=comm)


def _pair_gate_up(wg, wu, name):
    n, c, d = wg.shape

    def body(wg_ref, wu_ref, o_ref):
        o_ref[0:c, :] = wg_ref[0]
        o_ref[c:2 * c, :] = wg_ref[1]
        o_ref[2 * c:3 * c, :] = wu_ref[0]
        o_ref[3 * c:4 * c, :] = wu_ref[1]

    pair = pl.BlockSpec((2, c, d), lambda k: (k, 0, 0))
    return _call(body, name=name, grid=(n // 2,), args=(wg, wu), in_specs=[pair, pair],
                 out_specs=[pl.BlockSpec((None, 4 * c, d), lambda k: (k, 0, 0))],
                 out_shape=[_sds((n // 2, 4 * c, d), wg.dtype)], semantics=("parallel",))[0][0]


def _pair_specs(w, tm):
    d = w['w_gate_up'].shape[2]
    up = pl.BlockSpec((None, 2 * FF_PAIR, d), lambda i, k: (k, 0, 0))
    down = pl.BlockSpec((None, FF_PAIR, d), lambda i, k: (k, 0, 0))
    return [up, down]


def _ffn_fwd(x1, gain, w, tm, name, comm):
    t, d = x1.shape
    tm = min(tm, t)
    npair = NDEV // 2
    hid = NDEV * FF_PAD
    wd_pairs = w['w_ffn_down'].reshape(npair, FF_PAIR, d)

    def body(x_ref, g_ref, wgu_ref, wd_ref, h_ref, gg_ref, uu_ref, x2_ref, hb_ref, acc_ref):
        k = pl.program_id(1)

        @pl.when(k == 0)
        def _():
            xv = x_ref[...]
            r = lax.rsqrt(jnp.mean(xv * xv, axis=-1, keepdims=True) + EPS)
            h = (xv * r * g_ref[...]).astype(BF16)
            hb_ref[...] = h
            h_ref[...] = h
            acc_ref[...] = xv

        gu = _nt(hb_ref[...], wgu_ref[...])
        gk = gu[:, 0:FF_PAIR]
        uk = gu[:, FF_PAIR:2 * FF_PAIR]
        gg_ref[...] = gk.astype(BF16)
        uu_ref[...] = uk.astype(BF16)
        ak = (gk * _sig(gk) * uk).astype(BF16)
        acc_ref[...] += _nn(ak, wd_ref[...])

        @pl.when(k == npair - 1)
        def _():
            x2_ref[...] = acc_ref[...]

    pair_cols = pl.BlockSpec((tm, FF_PAIR), lambda i, k: (i, k))
    return _call(
        body, name=name, grid=(t // tm, npair),
        args=(x1, gain, w['w_gate_up'], wd_pairs),
        in_specs=[_tile(tm, d), _row(d)] + _pair_specs(w, tm),
        out_specs=[_tile(tm, d), pair_cols, pair_cols, _tile(tm, d)],
        out_shape=[_sds((t, d), BF16), _sds((t, hid), BF16), _sds((t, hid), BF16), _sds((t, d), F32)],
        scratch_shapes=[pltpu.VMEM((tm, d), BF16), pltpu.VMEM((tm, d), F32)],
        semantics=("parallel", "arbitrary"), comm=comm)


def _init_small(first, small_ref, acc_ref):
    @pl.when(first)
    def _():
        small_ref[...] = jnp.zeros(small_ref.shape, F32)
        acc_ref[...] = jnp.zeros(acc_ref.shape, F32)


def _finish_small(last, small_ref, acc_ref, nq):
    @pl.when(last)
    def _():
        for q in range(nq):
            small_ref[q:q + 1, :] = jnp.sum(acc_ref[q], axis=0, keepdims=True)


def _loss_bwd(x, gain, target, tm, name):
    t, d = x.shape
    tm = min(tm, t)
    nsteps = t // tm

    def body(x_ref, g_ref, t_ref, dx_ref, small_ref, acc_ref):
        i = pl.program_id(0)
        _init_small(i == 0, small_ref, acc_ref)
        xv = x_ref[...]
        r = lax.rsqrt(jnp.mean(xv * xv, axis=-1, keepdims=True) + EPS)
        xhat = xv * r
        diff = xhat * g_ref[...] - t_ref[...]
        dy = diff * (1.0 / d)
        acc_ref[0] += _fold(dy * xhat)
        acc_ref[1] += _fold(diff * diff)
        dxhat = dy * g_ref[...]
        dx_ref[...] = r * (dxhat - xhat * jnp.mean(dxhat * xhat, axis=-1, keepdims=True))

        @pl.when(i == nsteps - 1)
        def _():
            small_ref[0:1, :] = jnp.sum(acc_ref[0], axis=0, keepdims=True)
            small_ref[1:2, :] = jnp.full((1, d), jnp.sum(acc_ref[1]) * (0.5 / d), F32)

    return _call(body, name=name, grid=(nsteps,), args=(x, gain, target),
                 in_specs=[_tile(tm, d), _row(d), _tile(tm, d)],
                 out_specs=[_tile(tm, d), _resident((SUBLANES, d))],
                 out_shape=[_sds((t, d), F32), _sds((SUBLANES, d), F32)],
                 scratch_shapes=[pltpu.VMEM((2, SUBLANES, d), F32)], semantics=("arbitrary",))[0]


def _ffn_bwd(dx2, x1, gain, gg, uu, w, tm, name, comm):
    t, d = x1.shape
    tm = min(tm, t)
    npair = NDEV // 2
    hid = NDEV * FF_PAD
    nsteps = t // tm
    wd_pairs = w['w_ffn_down'].reshape(npair, FF_PAIR, d)

    def body(dx_ref, x_ref, g_ref, gg_ref, uu_ref, wgu_ref, wd_ref,
             a_ref, dg_ref, du_ref, dx1_ref, small_ref, acc_ref, dxb_ref, dh_ref, dgu_ref):
        i, k = pl.program_id(0), pl.program_id(1)
        _init_small((i == 0) & (k == 0), small_ref, acc_ref)

        @pl.when(k == 0)
        def _():
            dxb_ref[...] = dx_ref[...].astype(BF16)
            dh_ref[...] = jnp.zeros(dh_ref.shape, F32)

        gk = gg_ref[...].astype(F32)
        uk = uu_ref[...].astype(F32)
        sg = _sig(gk)
        silu = gk * sg
        a_ref[...] = (silu * uk).astype(BF16)
        da = _nt(dxb_ref[...], wd_ref[...])
        dgk = (da * uk * (sg * (1.0 + gk * (1.0 - sg)))).astype(BF16)
        duk = (da * silu).astype(BF16)
        dg_ref[...] = dgk
        du_ref[...] = duk
        dgu_ref[:, 0:FF_PAIR] = dgk
        dgu_ref[:, FF_PAIR:2 * FF_PAIR] = duk
        dh_ref[...] += _nn(dgu_ref[...], wgu_ref[...])

        @pl.when(k == npair - 1)
        def _():
            xv = x_ref[...]
            dh = dh_ref[...]
            r = lax.rsqrt(jnp.mean(xv * xv, axis=-1, keepdims=True) + EPS)
            xhat = xv * r
            acc_ref[0] += _fold(dh * xhat)
            dxhat = dh * g_ref[...]
            dx1_ref[...] = dx_ref[...] + r * (dxhat - xhat * jnp.mean(dxhat * xhat, axis=-1, keepdims=True))

        _finish_small((i == nsteps - 1) & (k == npair - 1), small_ref, acc_ref, 1)

    pair_cols = pl.BlockSpec((tm, FF_PAIR), lambda i, k: (i, k))
    return _call(
        body, name=name, grid=(nsteps, npair),
        args=(dx2, x1, gain, gg, uu, w['w_gate_up'], wd_pairs),
        in_specs=[_tile(tm, d), _tile(tm, d), _row(d), pair_cols, pair_cols] + _pair_specs(w, tm),
        out_specs=[pair_cols] * 3 + [_tile(tm, d), _resident((SUBLANES, d))],
        out_shape=[_sds((t, hid), BF16)] * 3 + [_sds((t, d), F32), _sds((SUBLANES, d), F32)],
        scratch_shapes=[pltpu.VMEM((1, SUBLANES, d), F32), pltpu.VMEM((tm, d), BF16), pltpu.VMEM((tm, d), F32),
                        pltpu.VMEM((tm, 2 * FF_PAIR), BF16)],
        semantics=("arbitrary", "arbitrary"), comm=comm)


def _matmul_tn(a, b, a_blk, b_blk, stack, shard, tm, name, comm):
    t, ma = a.shape
    tm = min(tm, t)
    nb_ = b.shape[1]
    na, nb = ma // a_blk, nb_ // b_blk
    nsteps = t // tm
    cw = min(TN_COLS, b_blk)
    if stack == 'b':
        per = b_blk // shard
        out_shape, out_spec = (nb_ // shard, ma, shard), pl.BlockSpec((per, a_blk, shard), lambda i, j, k: (j, 0, 0))
    elif stack == 'a':
        per = a_blk // shard
        out_shape, out_spec = (ma // shard, shard, nb_), pl.BlockSpec((per, shard, b_blk), lambda i, j, k: (i, 0, 0))
    else:
        out_shape, out_spec = (ma, nb_), pl.BlockSpec((a_blk, b_blk), lambda i, j, k: (i, j))

    def body(a_ref, b_ref, o_ref, acc_ref):
        k = pl.program_id(2)

        @pl.when(k == 0)
        def _():
            acc_ref[...] = jnp.zeros(acc_ref.shape, F32)

        av = a_ref[...].astype(BF16)
        for c in range(0, b_blk, cw):
            acc_ref[:, c:c + cw] += _tn(av, b_ref[:, c:c + cw].astype(BF16))

        @pl.when(k == nsteps - 1)
        def _():
            if stack == 'b':
                for s in range(per):
                    o_ref[s] = acc_ref[:, s * shard:(s + 1) * shard].astype(BF16)
            elif stack == 'a':
                for s in range(per):
                    o_ref[s] = acc_ref[s * shard:(s + 1) * shard, :].astype(BF16)
            else:
                o_ref[...] = acc_ref[...].astype(BF16)

    res, got = _call(
        body, name=name, grid=(na, nb, nsteps), args=(a, b),
        in_specs=[pl.BlockSpec((tm, a_blk), lambda i, j, k: (k, i)), pl.BlockSpec((tm, b_blk), lambda i, j, k: (k, j))],
        out_specs=[out_spec], out_shape=[_sds(out_shape, BF16)],
        scratch_shapes=[pltpu.VMEM((a_blk, b_blk), F32)],
        semantics=("parallel", "parallel", "arbitrary"), comm=comm)
    return res[0], got


def _mixer_bwd(dx1, proj, c1, ya, yb, w, small, tm, name, comm):
    t, d = dx1.shape
    nsteps = t // tm
    nq = 6

    def body(dx_ref, u_ref, v_ref, ga_ref, gb_ref, c1_ref, ya_ref, yb_ref,
             lg_ref, lb_ref, wco_ref, sg_ref, sb_ref, ws_ref, wst_ref, bias_ref, wso_ref, gba_ref, gbb_ref, wo_ref,
             sel_ref,
             dp_ref, dc1_ref, mg_ref, c3_ref, gt_ref, dya_ref, dyb_ref, small_ref, dws_ref, dbs_ref,
             acc_ref, vn_ref, mixed_ref, dmix_ref, dvn_ref, dbias_ref):
        i = pl.program_id(0)
        _init_small(i == 0, small_ref, acc_ref)

        @pl.when(i == 0)
        def _():
            dws_ref[...] = jnp.zeros(dws_ref.shape, F32)
            dbs_ref[...] = jnp.zeros(dbs_ref.shape, F32)
            dbias_ref[...] = jnp.zeros(dbias_ref.shape, F32)

        dmerged = _nt(dx_ref[...].astype(BF16), wo_ref[...].reshape(d, d))
        ya = ya_ref[...].astype(F32)
        yb = yb_ref[...].astype(F32)
        sa = _sig(ga_ref[...].astype(F32) + gba_ref[...])
        sb = _sig(gb_ref[...].astype(F32) + gbb_ref[...])
        mg_ref[...] = (sa * ya + sb * yb).astype(BF16)
        dya = (dmerged * sa).astype(BF16)
        dyb = (dmerged * sb).astype(BF16)
        dya_ref[...] = dya
        dyb_ref[...] = dyb
        dga = dmerged * ya * (sa * (1.0 - sa))
        dgb = dmerged * yb * (sb * (1.0 - sb))
        acc_ref[0] += _fold(dga)
        acc_ref[1] += _fold(dgb)
        dp_ref[:, 0:2 * d] = jnp.zeros((tm, 2 * d), BF16)
        dp_ref[:, 4 * d:5 * d] = dga.astype(BF16)
        dp_ref[:, 5 * d:6 * d] = dgb.astype(BF16)
        c2hat, rstd = _layernorm_stats(c1_ref[...].astype(F32))
        c2 = c2hat * lg_ref[...] + lb_ref[...]
        s2 = _sig(c2)
        c3_ref[...] = (c2 * s2).astype(BF16)
        dc3 = _nt(dya, wco_ref[...].reshape(d, d))
        dc2 = dc3 * (s2 * (1.0 + c2 * (1.0 - s2)))
        acc_ref[2] += _fold(dc2 * c2hat)
        acc_ref[3] += _fold(dc2)
        dc2hat = dc2 * lg_ref[...]
        dc1_ref[...] = (rstd * (dc2hat - jnp.mean(dc2hat, axis=-1, keepdims=True)
                                - c2hat * jnp.mean(dc2hat * c2hat, axis=-1, keepdims=True))).astype(BF16)
        vhat, rstd_v = _layernorm_stats(v_ref[...].astype(F32))
        vn_ref[...] = (vhat * sg_ref[...] + sb_ref[...]).astype(BF16)
        _spatial_mix(ws_ref, vn_ref, bias_ref, mixed_ref, tm)
        u = u_ref[...].astype(F32)
        mixed = mixed_ref[...]
        gt_ref[...] = (u * mixed).astype(BF16)
        dgated = _nt(dyb, wso_ref[...].reshape(d, d))
        dp_ref[:, 2 * d:3 * d] = (dgated * mixed).astype(BF16)
        dmix_ref[...] = dgated * u
        for ci in range(tm // CHUNK):
            rs = slice(ci * CHUNK, (ci + 1) * CHUNK)
            dbias_ref[...] += dmix_ref[rs, :]
            for g in range(GROUPS):
                ls = slice(g * LANES, (g + 1) * LANES)
                dm = dmix_ref[rs, ls].astype(BF16)
                dws_ref[g] += _nt(dm, vn_ref[rs, ls])
                dvn_ref[rs, ls] = _nn(wst_ref[g], dm)
        dvn = dvn_ref[...]
        acc_ref[4] += _fold(dvn * vhat)
        acc_ref[5] += _fold(dvn)
        dvhat = dvn * sg_ref[...]
        dp_ref[:, 3 * d:4 * d] = (rstd_v * (dvhat - jnp.mean(dvhat, axis=-1, keepdims=True)
                                           - vhat * jnp.mean(dvhat * vhat, axis=-1, keepdims=True))).astype(BF16)
        _finish_small(i == nsteps - 1, small_ref, acc_ref, nq)

        @pl.when(i == nsteps - 1)
        def _():
            db = dbias_ref[...]
            hi = db.astype(BF16)
            lo = (db - hi.astype(F32)).astype(BF16)
            dbs_ref[...] = _nt(sel_ref[...], hi) + _nt(sel_ref[...], lo)

    cols = [_tile(tm, d, j) for j in (2, 3, 4, 5)]
    return _call(
        body, name=name, grid=(nsteps,),
        args=(dx1, proj, proj, proj, proj, c1, ya, yb,
              small['conv_ln_g'], small['conv_ln_b'], w['w_conv_out'], small['sgu_ln_g'], small['sgu_ln_b'],
              small['ws'], small['wst'], small['bias_full'], w['w_sgu_out'], small['gba'], small['gbb'], w['w_o'],
              small['group_sel']),
        in_specs=[_tile(tm, d)] + cols + [_tile(tm, d)] * 3 + [
            _row(d), _row(d), _weight(w['w_conv_out']),
            _row(d), _row(d), _resident(small['ws'].shape), _resident(small['wst'].shape),
            _resident(small['bias_full'].shape), _weight(w['w_sgu_out']), _row(d), _row(d),
            _weight(w['w_o']), _resident(small['group_sel'].shape)],
        out_specs=[_tile(tm, 6 * d)] + [_tile(tm, d)] * 6 + [
            _resident((SUBLANES, d)), _resident((GROUPS, CHUNK, CHUNK)), _resident((GROUPS, CHUNK))],
        out_shape=[_sds((t, 6 * d), BF16)] + [_sds((t, d), BF16)] * 6 + [
            _sds((SUBLANES, d), F32), _sds((GROUPS, CHUNK, CHUNK), F32), _sds((GROUPS, CHUNK), F32)],
        scratch_shapes=[pltpu.VMEM((nq, SUBLANES, d), F32), pltpu.VMEM((tm, d), BF16), pltpu.VMEM((tm, d), F32),
                        pltpu.VMEM((tm, d), F32), pltpu.VMEM((tm, d), F32), pltpu.VMEM((CHUNK, d), F32)],
        semantics=("arbitrary",), comm=comm)


def _conv_bwd(dproj, dc1, proj, w, pairs, seq, tm, name, comm):
    t, d = dc1.shape
    nsteps = t // tm
    nblk = d // LANES
    npairs = len(pairs)
    hp_v, hn_v = _halo_specs(tm, t, d, 0)
    hp_g, hn_g = _halo_specs(tm, t, d, 1)
    hp_d, hn_d = _halo_specs(tm, t, d, 0)

    def body(*refs):
        (dp_in, dc_ref, dcp, dcn, av, ag, avp, agp, avn, agn, cw_ref), refs = refs[:11], refs[11:]
        ab_refs, refs = refs[:2 * npairs], refs[2 * npairs:]
        (dp_ref, dcw_ref, small_ref), refs = refs[:3], refs[3:]
        grad_refs, refs = refs[:npairs], refs[npairs:]
        acc_ref, cbuf, dbuf, dc0f, accw, gacc = refs
        del dp_in
        i = pl.program_id(0)
        _init_small(i == 0, small_ref, acc_ref)

        @pl.when(i == 0)
        def _():
            accw[...] = jnp.zeros(accw.shape, F32)
            dcw_ref[...] = jnp.zeros(dcw_ref.shape, F32)
            gacc[...] = jnp.zeros(gacc.shape, F32)

        first = (i * tm) % seq == 0
        last = ((i + 1) * tm) % seq == 0
        _fill_glu_buffer(cbuf, av, ag, avp, agp, avn, agn, first, last, tm)
        dc1v = dc_ref[...].astype(F32)
        _fill_halo_buffer(dbuf, dcp[...].astype(F32), dc1v, dcn[...].astype(F32), first, last, tm)
        acc_ref[0] += _fold(dc1v)
        _dwconv(dbuf, cw_ref, dc0f, tm, flip=True, rc=CONV_ROWS, unroll=True)

        def rows(r0):
            for kb in range(nblk):
                dv = dbuf[kb, r0 + HALO:r0 + HALO + CONV_ROWS, :]
                for k in range(CONV_TAPS):
                    accw[kb, k] += _fold(dv * _shifted(cbuf, kb, r0, 1 + k, CONV_ROWS))

        _row_chunks(tm, CONV_ROWS, True, rows)
        for q in range(npairs):
            aq = ab_refs[2 * q][...].astype(BF16)
            for c in range(0, d, TN_COLS):
                gacc[q, :, c:c + TN_COLS] += _tn(aq, ab_refs[2 * q + 1][:, c:c + TN_COLS].astype(BF16))
        sg = _sig(ag[...].astype(F32))
        avv = av[...].astype(F32)
        dc0 = dc0f[...]
        dp_ref[:, 0:d] = (dc0 * sg).astype(BF16)
        dp_ref[:, d:2 * d] = (dc0 * avv * (sg * (1.0 - sg))).astype(BF16)
        _finish_small(i == nsteps - 1, small_ref, acc_ref, 1)

        @pl.when(i == nsteps - 1)
        def _():
            for kb in range(nblk):
                dcw_ref[kb] = jnp.sum(accw[kb], axis=1)
            for q in range(npairs):
                grad_refs[q][...] = gacc[q].astype(BF16)

    return _call(
        body, name=name, grid=(nsteps,),
        args=(dproj, dc1, dc1, dc1, proj, proj, proj, proj, proj, proj, w['conv_w']) + tuple(a for ab in pairs for a in ab),
        in_specs=[pl.BlockSpec(memory_space=pl.ANY), _tile(tm, d), hp_d, hn_d, _tile(tm, d, 0), _tile(tm, d, 1),
                  hp_v, hp_g, hn_v, hn_g, _weight(w['conv_w'])] + [_tile(tm, d)] * (2 * npairs),
        out_specs=[_tile(tm, 2 * d), _resident((nblk, CONV_TAPS_PADDED, LANES)), _resident((SUBLANES, d))]
        + [_resident((d, d))] * npairs,
        out_shape=[_sds(dproj.shape, BF16), _sds((nblk, CONV_TAPS_PADDED, LANES), F32), _sds((SUBLANES, d), F32)]
        + [_sds((d, d), BF16)] * npairs,
        scratch_shapes=[pltpu.VMEM((1, SUBLANES, d), F32), pltpu.VMEM((nblk, tm + 2 * HALO, LANES), F32),
                        pltpu.VMEM((nblk, tm + 2 * HALO, LANES), F32), pltpu.VMEM((tm, d), F32),
                        pltpu.VMEM((nblk, CONV_TAPS_PADDED, SUBLANES, LANES), F32),
                        pltpu.VMEM((npairs, d, d), F32)],
        aliases={0: 0}, semantics=("arbitrary",), comm=comm)


def _in_proj_bwd(dproj, x, dx1, gain, w, tm, name, comm):
    t, d = x.shape
    nb, _, bw = w.shape
    nsteps = t // tm

    def body(dp_ref, x_ref, dx1_ref, g_ref, w_ref, dx_ref, small_ref, acc_ref):
        i = pl.program_id(0)
        _init_small(i == 0, small_ref, acc_ref)
        dh = jnp.zeros((tm, d), F32)
        for k in range(nb):
            dh = dh + _nt(dp_ref[:, k * bw:(k + 1) * bw], w_ref[k])
        xv = x_ref[...]
        r = lax.rsqrt(jnp.mean(xv * xv, axis=-1, keepdims=True) + EPS)
        xhat = xv * r
        acc_ref[0] += _fold(dh * xhat)
        dxhat = dh * g_ref[...]
        dx_ref[...] = dx1_ref[...] + r * (dxhat - xhat * jnp.mean(dxhat * xhat, axis=-1, keepdims=True))
        _finish_small(i == nsteps - 1, small_ref, acc_ref, 1)

    return _call(body, name=name, grid=(nsteps,), args=(dproj, x, dx1, gain, w),
                 in_specs=[_tile(tm, nb * bw), _tile(tm, d), _tile(tm, d), _row(d), _weight(w)],
                 out_specs=[_tile(tm, d), _resident((SUBLANES, d))],
                 out_shape=[_sds((t, d), F32), _sds((SUBLANES, d), F32)],
                 scratch_shapes=[pltpu.VMEM((1, SUBLANES, d), F32)], semantics=("arbitrary",), comm=comm)


def _adam(wv, g, mv, vv):
    m = ADAM_B1 * mv + (1.0 - ADAM_B1) * g
    v = ADAM_B2 * vv + (1.0 - ADAM_B2) * jnp.square(g)
    m_hat = m / (1.0 - ADAM_B1 ** ADAM_STEP)
    v_hat = v / (1.0 - ADAM_B2 ** ADAM_STEP)
    delta = -ADAM_LR * (m_hat / (jnp.sqrt(v_hat) + ADAM_EPS) + ADAM_WD * wv)
    return delta, m, v


def _adamw_layer(layer, w, m, v, parts, prev, nsplit, name):
    nl, r, c = w.shape
    npart, pr, pc = parts.shape
    rt, prt = r // nsplit, pr // nsplit

    def body(w_ref, m_ref, v_ref, p_ref, *rest):
        g_ref, d_ref, nm_ref, nv_ref = rest[-4:]
        g = p_ref[0, 0:rt, 0:c].astype(F32)
        for s in range(1, npart):
            g = g + p_ref[s, 0:rt, 0:c].astype(F32)
        delta, mn, vn = _adam(w_ref[0], g, m_ref[0], v_ref[0])
        g_ref[0] = g
        d_ref[0] = delta
        nm_ref[0] = mn
        nv_ref[0] = vn

    wspec = pl.BlockSpec((1, rt, c), lambda i: (layer, i, 0))
    pspec = pl.BlockSpec((npart, prt, pc), lambda i: (0, i, 0))
    in_specs = [wspec, wspec, wspec, pspec]
    args = [w, m, v, parts]
    aliases = {}
    if prev is not None:
        in_specs += [pl.BlockSpec(memory_space=pl.ANY)] * 4
        args += list(prev)
        aliases = {4 + q: q for q in range(4)}
    return _call(body, name=name, grid=(nsplit,), args=args, in_specs=in_specs, out_specs=[wspec] * 4,
                 out_shape=[_sds(w.shape, F32)] * 4, aliases=aliases, semantics=("parallel",))[0]


VEC_ROWS = {'gate_bias': (0, 1), 'conv_ln_g': 2, 'conv_ln_b': 3, 'sgu_ln_g': 4, 'sgu_ln_b': 5, 'norm_ffn': SUBLANES}
FINAL_ROW = 2 * SUBLANES
LOSS_ROW = 2 * SUBLANES + 1
LATE_ROWS = {'norm_mix': 0, 'conv_b': SUBLANES}


def _adamw_small(gathered, params, moments_m, moments_v):
    names = list(params)
    nper = len(names)
    nl = len(gathered)

    def body(*refs):
        g_refs = [refs[4 * l:4 * l + 4] for l in range(nl)]
        rest = refs[4 * nl:]
        w_refs = dict(zip(names, rest[0:nper]))
        m_refs = dict(zip(names, rest[nper:2 * nper]))
        v_refs = dict(zip(names, rest[2 * nper:3 * nper]))
        outs = rest[3 * nper:]
        loss_ref = outs[4 * nper]
        o = {kind: dict(zip(names, outs[q * nper:(q + 1) * nper])) for q, kind in enumerate("gdmv")}

        def put(nm, idx, g):
            delta, mn, vn = _adam(w_refs[nm][idx], g, m_refs[nm][idx], v_refs[nm][idx])
            o["g"][nm][idx] = g
            o["d"][nm][idx] = delta
            o["m"][nm][idx] = mn
            o["v"][nm][idx] = vn

        def total(ref, *idx):
            g = ref[(0, *idx)]
            for s in range(1, NDEV):
                g = g + ref[(s, *idx)]
            return g

        for l, (vec_ref, dws_ref, dbs_ref, late_ref) in enumerate(g_refs):
            dd = vec_ref.shape[2]
            put('w_spatial', (l,), total(dws_ref))
            put('b_spatial', (l,), total(dbs_ref))
            for nm, rr in LATE_ROWS.items():
                put(nm, (slice(l, l + 1), slice(None)), total(late_ref, slice(rr, rr + 1)))
            for nm, rr in VEC_ROWS.items():
                if nm == 'gate_bias':
                    put(nm, (slice(l, l + 1), slice(0, dd)), total(vec_ref, slice(rr[0], rr[0] + 1)))
                    put(nm, (slice(l, l + 1), slice(dd, 2 * dd)), total(vec_ref, slice(rr[1], rr[1] + 1)))
                else:
                    put(nm, (slice(l, l + 1), slice(None)), total(vec_ref, slice(rr, rr + 1)))
        last = g_refs[nl - 1][0]
        put('norm_final', (slice(0, 1), slice(None)), total(last, slice(FINAL_ROW, FINAL_ROW + 1)))
        loss_ref[...] = total(last, slice(LOSS_ROW, LOSS_ROW + 1))

    ins = [a for g in gathered for a in g] + [params[n] for n in names] + [moments_m[n] for n in names] + [moments_v[n] for n in names]
    out_shape = [_sds(params[n].shape, F32) for n in names] * 4 + [_sds((1, gathered[0][0].shape[2]), F32)]
    res = pl.pallas_call(body, name="adamw_small", out_shape=out_shape, compiler_params=_params())(*ins)
    return {kind: dict(zip(names, res[q * nper:(q + 1) * nper])) for q, kind in enumerate("gdmv")}, res[4 * nper]


def _hidden_major(a):
    return jnp.swapaxes(a, 1, 2)


def _prepare_weights(p):
    d = p['w_in'].shape[1]
    (w_in,), _ = _cast_pad([p['w_in']], [(d, p['w_in'].shape[2])], "cast_w_in", None)
    rows = p['w_o'].shape[1]
    others = {'w_conv_out': (p['w_conv_out'], (rows, d)), 'w_sgu_out': (p['w_sgu_out'], (rows, d)),
              'w_o': (p['w_o'], (rows, d)),
              'w_ffn_gate': (_hidden_major(p['w_ffn_gate']), (FF_PAD, d)),
              'w_ffn_up': (_hidden_major(p['w_ffn_up']), (FF_PAD, d)),
              'w_ffn_down': (p['w_ffn_down'], (FF_PAD, d))}
    cast, got = _cast_pad([a for a, _ in others.values()], [s for _, s in others.values()], "cast_weights",
                          _Gather([w_in], [0]))
    shards = dict(zip(others, cast), w_in=w_in)
    shards['conv_w'] = jnp.pad(p['conv_w'][:, :, 0, :], ((0, 0), (0, CONV_TAPS_PADDED - CONV_TAPS), (0, 0)))
    return shards, got[0]


def _gather_of(shards, names, layer):
    return _Gather([shards[n] for n in names], [layer] * len(names))


def _layer_small(p, layer):
    d = p['norm_mix'].shape[1]
    ws = p['w_spatial'][layer]
    rows = {n: p[n][layer:layer + 1] for n in ('norm_mix', 'norm_ffn', 'conv_b', 'conv_ln_g', 'conv_ln_b',
                                               'sgu_ln_g', 'sgu_ln_b')}
    return {
        **rows,
        'ws': ws.astype(BF16), 'wst': jnp.swapaxes(ws, 1, 2).astype(BF16),
        'bias_full': jnp.repeat(p['b_spatial'][layer].T, LANES, axis=1),
        'gba': p['gate_bias'][layer:layer + 1, 0:d], 'gbb': p['gate_bias'][layer:layer + 1, d:2 * d],
        'group_sel': (jnp.arange(d)[None, :] // LANES == jnp.arange(GROUPS)[:, None]).astype(BF16),
    }


class _GradQueue:
    def __init__(self):
        self.pending = []
        self.done = {}

    def push(self, key, array):
        self.pending.append((key, array))

    def take(self):
        keys = [k for k, _ in self.pending]
        comm = _Scatter([a for _, a in self.pending]) if self.pending else None
        self.pending = []
        return keys, comm

    def put(self, keys, arrays):
        self.done.update(zip(keys, arrays))


def _forward_backward(p, shards, w_in, x, target, seq):
    nl = p['norm_mix'].shape[0]
    d = x.shape[1]
    smalls = [_layer_small(p, l) for l in range(nl)]
    saved = []
    for l in range(nl):
        (h, proj), got = _in_proj(x, smalls[l]['norm_mix'], w_in, min(TILE_IN_FWD, x.shape[0]), f"in_proj_{l}",
                                  _gather_of(shards, MIXER_WEIGHTS, l))
        w = dict(zip(MIXER_WEIGHTS, got), w_in=w_in)
        (c1, ya, yb, x1), got = _mixer_fwd(proj, x, w, smalls[l], seq, min(TILE_MIX_FWD, seq), f"mixer_fwd_{l}",
                                           _gather_of(shards, FFN_WEIGHTS, l))
        w.update(zip(FFN_WEIGHTS, got))
        w['w_gate_up'] = _pair_gate_up(w['w_ffn_gate'], w['w_ffn_up'], f"pair_gate_up_{l}")
        nxt = _gather_of(shards, ['w_in'], l + 1) if l + 1 < nl else None
        (h2, gg, uu, x2), got = _ffn_fwd(x1, smalls[l]['norm_ffn'], w, TILE_FFN_FWD, f"ffn_fwd_{l}", nxt)
        saved.append(dict(x=x, h=h, proj=proj, c1=c1, ya=ya, yb=yb, x1=x1, h2=h2, gg=gg, uu=uu, w=w))
        x = x2
        if got:
            w_in = got[0]
    dx, small_loss = _loss_bwd(x, p['norm_final'][None, :], target, TILE_LOSS, "loss_bwd")
    queue = _GradQueue()
    small_gathered = [None] * nl
    small_pending = None
    rows = d // NDEV
    hid = NDEV * FF_PAD
    for l in reversed(range(nl)):
        s = saved[l]
        w = s['w']

        def hosted(fn, *args, extra=None):
            keys, comm = queue.take()
            res, got = fn(*args, _together(comm, extra))
            queue.put(keys, got[:len(keys)])
            return res, got[len(keys):]

        def tn(key, a, b, a_blk, b_blk, stack, shard, reshape=None, host=False):
            keys, comm = queue.take() if host else ([], None)
            g, got = _matmul_tn(a, b, a_blk, b_blk, stack, shard, TILE_TN, f"dw_{key}_{l}", comm)
            queue.put(keys, got)
            queue.push((l, key), g if reshape is None else g.reshape(reshape))

        (act, dgg, duu, dx1, small_ffn), _ = hosted(_ffn_bwd, dx, s['x1'], smalls[l]['norm_ffn'], s['gg'], s['uu'], w,
                                                    TILE_FFN_BWD, f"ffn_bwd_{l}")
        tn('w_ffn_gate', dgg, s['h2'], hid, d, 'a', FF_PAD)
        tn('w_ffn_up', duu, s['h2'], hid, d, 'a', FF_PAD)
        (dproj, dc1, merged, c3, gated, dya, dyb, small_mix, dws, dbs), _ = hosted(
            _mixer_bwd, dx1, s['proj'], s['c1'], s['ya'], s['yb'], w, smalls[l], TILE_MIX, f"mixer_bwd_{l}")
        tn('w_ffn_down', act, dx, hid, d, 'a', FF_PAD)
        blocks = [small_mix, small_ffn] + ([small_loss] if l == nl - 1 else [])
        small_main = [jnp.concatenate(blocks, axis=0), dws, dbs]
        (dproj, g_conv, small_conv, g_o, g_co, g_so), got = hosted(
            _conv_bwd, dproj, dc1, s['proj'], w, [(merged, dx1), (c3, dya), (gated, dyb)], seq, TILE_MIX,
            f"conv_bwd_{l}", extra=_together(small_pending[1] if small_pending else None,
                                             _Gather(small_main) if l == 0 else None))
        if small_pending:
            small_gathered[small_pending[0]], got = got[:4], got[4:]
            small_pending = None
        small_main_gathered = got
        queue.push((l, 'conv_w'), g_conv)
        queue.push((l, 'w_o'), g_o.reshape(NDEV, rows, d))
        queue.push((l, 'w_conv_out'), g_co.reshape(NDEV, rows, d))
        queue.push((l, 'w_sgu_out'), g_so.reshape(NDEV, rows, d))
        tn('w_in', s['h'], dproj, d, hid, 'b', w['w_in'].shape[2], host=True)
        if l == 0:
            (dx, small_in), _ = hosted(_in_proj_bwd, dproj, s['x'], dx1, smalls[l]['norm_mix'], w['w_in'], TILE_IN,
                                       f"in_proj_bwd_{l}")
        else:
            (dx, small_in), _ = _in_proj_bwd(dproj, s['x'], dx1, smalls[l]['norm_mix'], w['w_in'], TILE_IN,
                                             f"in_proj_bwd_{l}", None)
        small_late = jnp.concatenate([small_in, small_conv], axis=0)
        small_pending = (l, _Gather(small_main + [small_late]))
    keys, comm = queue.take()
    last = _exchange_alone(_together(comm, _Gather([small_late])), "exchange_last_grads")
    queue.put(keys, last[:len(keys)])
    small_gathered[0] = small_main_gathered + last[len(keys):]
    return small_gathered, dx, queue.done


def _train_step(p, m, v, x3, target3):
    nl = p['norm_mix'].shape[0]
    bsz, seq, d = x3.shape
    x = x3.reshape(bsz * seq, d)
    target = target3.reshape(bsz * seq, d)
    shards, w_in = _prepare_weights(p)
    small_gathered, dx, exchanged = _forward_backward(p, shards, w_in, x, target, seq)

    out = {kind: {} for kind in "gdmv"}
    splits = {'w_in': 4, 'w_conv_out': 1, 'w_sgu_out': 1, 'w_o': 1, 'w_ffn_gate': 1, 'w_ffn_up': 1, 'w_ffn_down': 1, 'conv_w': 1}
    for n in splits:
        if n == 'conv_w':
            pad = ((0, 0), (0, CONV_TAPS_PADDED - CONV_TAPS), (0, 0))
            wl, ml, vl = (jnp.pad(a[n][:, :, 0, :], pad) for a in (p, m, v))
        elif n in ('w_ffn_gate', 'w_ffn_up'):
            wl, ml, vl = (_hidden_major(a[n]) for a in (p, m, v))
        else:
            wl, ml, vl = p[n], m[n], v[n]
        prev = None
        for l in range(nl):
            prev = _adamw_layer(l, wl, ml, vl, exchanged[(l, n)], prev, splits[n], f"adamw_{n}_{l}")
        for kind, arr in zip("gdmv", prev):
            if n == 'conv_w':
                arr = arr[:, 0:CONV_TAPS, None, :]
            elif n in ('w_ffn_gate', 'w_ffn_up'):
                arr = _hidden_major(arr)
            out[kind][n] = arr
    small_names = ['norm_mix', 'gate_bias', 'conv_b', 'conv_ln_g', 'conv_ln_b', 'sgu_ln_g', 'sgu_ln_b', 'w_spatial',
                   'b_spatial', 'norm_ffn', 'norm_final']

    def two_d(a):
        return a[None, :] if a.ndim == 1 else a

    res, loss_row = _adamw_small(small_gathered, {n: two_d(p[n]) for n in small_names},
                                 {n: two_d(m[n]) for n in small_names}, {n: two_d(v[n]) for n in small_names})
    loss = loss_row[0, 0]
    for kind in "gdmv":
        for n in small_names:
            out[kind][n] = res[kind][n].reshape(p[n].shape)
    grad_x = dx.reshape(bsz, seq, d)
    return (loss, grad_x, *[out[kind][n] for kind in "gdmv" for n in WEIGHT_NAMES])


def kernel(x, norm_mix, w_in, gate_bias, conv_w, conv_b, conv_ln_g, conv_ln_b, w_conv_out, sgu_ln_g, sgu_ln_b, w_spatial, b_spatial, w_sgu_out, w_o, norm_ffn, w_ffn_gate, w_ffn_up, w_ffn_down, norm_final, loss_target, m_norm_mix, m_w_in, m_gate_bias, m_conv_w, m_conv_b, m_conv_ln_g, m_conv_ln_b, m_w_conv_out, m_sgu_ln_g, m_sgu_ln_b, m_w_spatial, m_b_spatial, m_w_sgu_out, m_w_o, m_norm_ffn, m_w_ffn_gate, m_w_ffn_up, m_w_ffn_down, m_norm_final, v_norm_mix, v_w_in, v_gate_bias, v_conv_w, v_conv_b, v_conv_ln_g, v_conv_ln_b, v_w_conv_out, v_sgu_ln_g, v_sgu_ln_b, v_w_spatial, v_b_spatial, v_w_sgu_out, v_w_o, v_norm_ffn, v_w_ffn_gate, v_w_ffn_up, v_w_ffn_down, v_norm_final):
    p = dict(zip(WEIGHT_NAMES, (norm_mix, w_in, gate_bias, conv_w, conv_b, conv_ln_g, conv_ln_b, w_conv_out, sgu_ln_g, sgu_ln_b, w_spatial, b_spatial, w_sgu_out, w_o, norm_ffn, w_ffn_gate, w_ffn_up, w_ffn_down, norm_final)))
    m = dict(zip(WEIGHT_NAMES, (m_norm_mix, m_w_in, m_gate_bias, m_conv_w, m_conv_b, m_conv_ln_g, m_conv_ln_b, m_w_conv_out, m_sgu_ln_g, m_sgu_ln_b, m_w_spatial, m_b_spatial, m_w_sgu_out, m_w_o, m_norm_ffn, m_w_ffn_gate, m_w_ffn_up, m_w_ffn_down, m_norm_final)))
    v = dict(zip(WEIGHT_NAMES, (v_norm_mix, v_w_in, v_gate_bias, v_conv_w, v_conv_b, v_conv_ln_g, v_conv_ln_b, v_w_conv_out, v_sgu_ln_g, v_sgu_ln_b, v_w_spatial, v_b_spatial, v_w_sgu_out, v_w_o, v_norm_ffn, v_w_ffn_gate, v_w_ffn_up, v_w_ffn_down, v_norm_final)))
    return _train_step(p, m, v, x, loss_target)
```

```python
import math

import jax
import jax.numpy as jnp
from jax import lax
from jax.experimental import pallas as pl
from jax.experimental.pallas import tpu as pltpu

F32 = jnp.float32
BF16 = jnp.bfloat16
MESH_ID = pl.DeviceIdType.MESH

NDEV = 8
EPS = 1e-6
CONV_TAPS = 31
CONV_TAPS_PADDED = 32
HALO = 16
CONV_ROWS = 128
CONV_ROWS_LOOP = 64
LANES = 128
SUBLANES = 8
CHUNK = 128
GROUPS = 8
FF_PAD = 384
FF_PAIR = 2 * FF_PAD
TN_COLS = 512
FFN_WEIGHT_SLOTS = 3
VMEM_LIMIT_BYTES = 56 * 1024 * 1024

ADAM_LR = 0.001
ADAM_B1 = 0.9
ADAM_B2 = 0.999
ADAM_EPS = 1e-08
ADAM_WD = 0.01
ADAM_STEP = 10

TILE_IN = 512
TILE_IN_FWD = 1024
TILE_MIX = 256
TILE_MIX_FWD = 512
TILE_FFN_FWD = 1024
TILE_FFN_BWD = 512
TILE_TN = 1024
TILE_LOSS = 512

WEIGHT_NAMES = ['norm_mix', 'w_in', 'gate_bias', 'conv_w', 'conv_b', 'conv_ln_g', 'conv_ln_b', 'w_conv_out',
                'sgu_ln_g', 'sgu_ln_b', 'w_spatial', 'b_spatial', 'w_sgu_out', 'w_o', 'norm_ffn', 'w_ffn_gate',
                'w_ffn_up', 'w_ffn_down', 'norm_final']
MIXER_WEIGHTS = ['w_conv_out', 'w_sgu_out', 'w_o', 'conv_w']
FFN_WEIGHTS = ['w_ffn_gate', 'w_ffn_up', 'w_ffn_down']


def _sds(shape, dtype):
    return jax.ShapeDtypeStruct(tuple(shape), dtype)


def _params(*sem):
    return pltpu.CompilerParams(dimension_semantics=sem or None, vmem_limit_bytes=VMEM_LIMIT_BYTES)


def _nn(a, b):
    return jnp.dot(a, b, preferred_element_type=F32)


def _nt(a, b):
    return lax.dot_general(a, b, (((1,), (1,)), ((), ())), preferred_element_type=F32)


def _tn(a, b):
    return lax.dot_general(a, b, (((0,), (0,)), ((), ())), preferred_element_type=F32)


def _sig(v):
    return jax.nn.sigmoid(v)


def _fold(v):
    r, c = v.shape
    return jnp.sum(v.reshape(r // SUBLANES, SUBLANES, c), axis=0)


def _tile(tm, n, j=0):
    return pl.BlockSpec((tm, n), lambda i, *_: (i, j))


def _row(n):
    return pl.BlockSpec((1, n), lambda *_: (0, 0))


def _resident(shape):
    nd = len(shape)
    return pl.BlockSpec(tuple(shape), lambda *_: (0,) * nd)


def _weight(w):
    nd = w.ndim
    return pl.BlockSpec(tuple(w.shape), lambda *_: (0,) * nd, pipeline_mode=pl.Buffered(1))


def _peer(rel):
    x, y, c = lax.axis_index("x"), lax.axis_index("y"), lax.axis_index("c")
    return (1 - x if rel & 4 else x, 1 - y if rel & 2 else y, 1 - c if rel & 1 else c)


def _slot(pos):
    return 4 * pos[0] + 2 * pos[1] + pos[2]


class _Exchange:
    def __init__(self, arrays, layers=None):
        self.arrays = list(arrays)
        self.layers = list(layers) if layers is not None else [None] * len(self.arrays)

    def scratch(self):
        n = len(self.arrays)
        return [pltpu.SemaphoreType.DMA((n, NDEV)), pltpu.SemaphoreType.DMA((n, NDEV)), pltpu.SemaphoreType.DMA((n,))]

    def _src(self, ins, j):
        return ins[j] if self.layers[j] is None else ins[j].at[self.layers[j]]

    def _block_shape(self, j):
        a = self.arrays[j]
        return a.shape if self.layers[j] is None else a.shape[1:]


class _Gather(_Exchange):
    chips = (4, 2, 6)

    def out_shape(self):
        return [_sds((NDEV,) + tuple(self._block_shape(j)), a.dtype) for j, a in enumerate(self.arrays)]

    @staticmethod
    def _copy(outs, sems, j, sem, block_rel, to_rel, src=None):
        blk = outs[j].at[_slot(_peer(block_rel))]
        return pltpu.make_async_remote_copy(
            src_ref=blk if src is None else src, dst_ref=blk,
            send_sem=sems[0].at[j, sem], recv_sem=sems[1].at[j, sem],
            device_id=_peer(to_rel), device_id_type=MESH_ID)

    def _local(self, ins, outs, sems, j):
        return pltpu.make_async_copy(self._src(ins, j), outs[j].at[_slot(_peer(0))], sems[2].at[j])

    def start(self, ins, outs, sems):
        for j in range(len(self.arrays)):
            self._local(ins, outs, sems, j).start()
            for rel in (1,) + self.chips:
                self._copy(outs, sems, j, rel, 0, rel, src=self._src(ins, j)).start()

    def forward(self, ins, outs, sems):
        for j in range(len(self.arrays)):
            for rel in self.chips:
                self._copy(outs, sems, j, rel, rel, 0).wait_recv()
                self._copy(outs, sems, j, rel ^ 1, rel, 1).start()

    def finish(self, ins, outs, sems):
        for j in range(len(self.arrays)):
            self._copy(outs, sems, j, 1, 1, 0).wait_recv()
            for rel in self.chips:
                self._copy(outs, sems, j, rel ^ 1, rel ^ 1, 0).wait_recv()
        for j in range(len(self.arrays)):
            for rel in (1,) + self.chips:
                self._copy(outs, sems, j, rel, 0, rel, src=self._src(ins, j)).wait_send()
            for rel in self.chips:
                self._copy(outs, sems, j, rel ^ 1, rel, 1).wait_send()
            self._local(ins, outs, sems, j).wait()


class _Scatter(_Exchange):
    def out_shape(self):
        return [_sds(a.shape, a.dtype) for a in self.arrays]

    @staticmethod
    def _copy(ins, outs, sems, j, rel):
        return pltpu.make_async_remote_copy(
            src_ref=ins[j].at[_slot(_peer(rel))], dst_ref=outs[j].at[_slot(_peer(0))],
            send_sem=sems[0].at[j, rel], recv_sem=sems[1].at[j, rel],
            device_id=_peer(rel), device_id_type=MESH_ID)

    @staticmethod
    def _arrival(outs, sems, j, rel):
        blk = outs[j].at[_slot(_peer(rel))]
        return pltpu.make_async_remote_copy(
            src_ref=blk, dst_ref=blk, send_sem=sems[0].at[j, rel], recv_sem=sems[1].at[j, rel],
            device_id=_peer(rel), device_id_type=MESH_ID)

    @staticmethod
    def _local(ins, outs, sems, j):
        me = _slot(_peer(0))
        return pltpu.make_async_copy(ins[j].at[me], outs[j].at[me], sems[2].at[j])

    def start(self, ins, outs, sems):
        for j in range(len(self.arrays)):
            self._local(ins, outs, sems, j).start()
            for rel in range(1, NDEV):
                self._copy(ins, outs, sems, j, rel).start()

    def forward(self, ins, outs, sems):
        pass

    def finish(self, ins, outs, sems):
        for j in range(len(self.arrays)):
            for rel in range(1, NDEV):
                self._arrival(outs, sems, j, rel).wait_recv()
        for j in range(len(self.arrays)):
            for rel in range(1, NDEV):
                self._copy(ins, outs, sems, j, rel).wait_send()
            self._local(ins, outs, sems, j).wait()


class _Together:
    def __init__(self, parts):
        self.parts = [c for c in parts if c is not None]
        self.arrays = [a for c in self.parts for a in c.arrays]

    def out_shape(self):
        return [s for c in self.parts for s in c.out_shape()]

    def scratch(self):
        return [s for c in self.parts for s in c.scratch()]

    def _each(self, method, ins, outs, sems):
        at = 0
        for q, c in enumerate(self.parts):
            n = len(c.arrays)
            getattr(c, method)(ins[at:at + n], outs[at:at + n], sems[3 * q:3 * q + 3])
            at += n

    def start(self, ins, outs, sems):
        self._each("start", ins, outs, sems)

    def forward(self, ins, outs, sems):
        self._each("forward", ins, outs, sems)

    def finish(self, ins, outs, sems):
        self._each("finish", ins, outs, sems)


def _together(*parts):
    flat = []
    for c in parts:
        if c is not None:
            flat.extend(c.parts if isinstance(c, _Together) else [c])
    return _Together(flat) if flat else None


def _call(body, *, name, args, in_specs, out_specs, out_shape, grid=(), scratch_shapes=(), semantics=(),
          aliases=None, comm=None):
    in_specs, out_specs, out_shape = list(in_specs), list(out_specs), list(out_shape)
    scratch_shapes = list(scratch_shapes)
    if comm is None:
        res = pl.pallas_call(
            body, name=name, grid=grid, in_specs=in_specs, out_specs=out_specs, out_shape=out_shape,
            scratch_shapes=scratch_shapes, input_output_aliases=aliases or {},
            compiler_params=_params(*semantics))(*args)
        return list(res), []
    n_in, n_out, n_scr, nc = len(in_specs), len(out_specs), len(scratch_shapes), len(comm.arrays)
    total = math.prod(grid)
    middle = min((total * 5) // 8, total - 1)

    def hosted(*refs):
        ins, cins = refs[:n_in], refs[n_in:n_in + nc]
        o0 = n_in + nc
        outs, couts = refs[o0:o0 + n_out], refs[o0 + n_out:o0 + n_out + nc]
        s0 = o0 + n_out + nc
        scr, sems = refs[s0:s0 + n_scr], refs[s0 + n_scr:]
        if total == 1:
            comm.start(cins, couts, sems)
            body(*ins, *outs, *scr)
            comm.forward(cins, couts, sems)
            comm.finish(cins, couts, sems)
            return
        step = 0
        for axis, size in enumerate(grid):
            step = step * size + pl.program_id(axis)
        pl.when(step == 0)(lambda: comm.start(cins, couts, sems))
        body(*ins, *outs, *scr)
        pl.when(step == middle)(lambda: comm.forward(cins, couts, sems))
        pl.when(step == total - 1)(lambda: comm.finish(cins, couts, sems))

    any_spec = pl.BlockSpec(memory_space=pl.ANY)
    res = pl.pallas_call(
        hosted, name=name, grid=grid,
        in_specs=in_specs + [any_spec] * nc, out_specs=out_specs + [any_spec] * nc,
        out_shape=out_shape + comm.out_shape(), scratch_shapes=scratch_shapes + comm.scratch(),
        input_output_aliases=aliases or {}, compiler_params=_params(*(("arbitrary",) * len(grid))),
    )(*args, *comm.arrays)
    return list(res[:n_out]), list(res[n_out:])


def _exchange_alone(comm, name):
    return _call(lambda: None, name=name, args=(), in_specs=(), out_specs=(), out_shape=(), comm=comm)[1]


def _cast_pad(ws, shapes, name, comm):
    n = len(ws)
    nl = ws[0].shape[0]

    def body(*refs):
        for w, (rows, cols), w_ref, o_ref in zip(ws, shapes, refs[:n], refs[n:]):
            r, c = w.shape[1:]
            if (rows, cols) != (r, c):
                o_ref[...] = jnp.zeros(o_ref.shape, BF16)
            o_ref[0, 0:r, 0:c] = w_ref[0].astype(BF16)

    return _call(body, name=name, grid=(nl,), args=tuple(ws),
                 in_specs=[pl.BlockSpec((1,) + w.shape[1:], lambda i: (i, 0, 0)) for w in ws],
                 out_specs=[pl.BlockSpec((1,) + tuple(s), lambda i: (i, 0, 0)) for s in shapes],
                 out_shape=[_sds((nl,) + tuple(s), BF16) for s in shapes], semantics=("parallel",), comm=comm)


def _in_proj(x, gain, w, tm, name, comm):
    t, d = x.shape
    nb, _, bw = w.shape

    def body(x_ref, g_ref, w_ref, h_ref, p_ref):
        xv = x_ref[...]
        r = lax.rsqrt(jnp.mean(xv * xv, axis=-1, keepdims=True) + EPS)
        h = (xv * r * g_ref[...]).astype(BF16)
        h_ref[...] = h
        for k in range(nb):
            p_ref[:, k * bw:(k + 1) * bw] = _nn(h, w_ref[k]).astype(BF16)

    return _call(body, name=name, grid=(t // tm,), args=(x, gain, w),
                 in_specs=[_tile(tm, d), _row(d), _weight(w)],
                 out_specs=[_tile(tm, d), _tile(tm, nb * bw)],
                 out_shape=[_sds((t, d), BF16), _sds((t, nb * bw), BF16)],
                 semantics=("parallel",), comm=comm)


def _halo_specs(tm, t, d, col):
    nh, nhb = tm // HALO, t // HALO
    prev = pl.BlockSpec((HALO, d), lambda i: (jnp.maximum(i * nh - 1, 0), col))
    nxt = pl.BlockSpec((HALO, d), lambda i: (jnp.minimum((i + 1) * nh, nhb - 1), col))
    return prev, nxt


def _row_chunks(tm, rc, unroll, rows):
    n = tm // rc
    if unroll:
        for j in range(n):
            rows(j * rc)
    else:
        def step(j, carry):
            rows(pl.multiple_of(j * rc, rc))
            return carry

        lax.fori_loop(0, n, step, 0)


def _shifted(buf_ref, kb, r0, off, rc):
    return buf_ref[kb, pl.ds(r0 + off, rc), :]


def _dwconv(buf_ref, w_ref, out_ref, tm, flip, rc, unroll):
    nblk = out_ref.shape[1] // LANES

    def rows(r0):
        for kb in range(nblk):
            acc = jnp.zeros((rc, LANES), F32)
            for k in range(CONV_TAPS):
                off = (CONV_TAPS - k) if flip else (1 + k)
                acc = acc + w_ref[kb, k:k + 1, :] * _shifted(buf_ref, kb, r0, off, rc)
            out_ref[pl.ds(r0, rc), kb * LANES:(kb + 1) * LANES] = acc

    _row_chunks(tm, rc, unroll, rows)


def _fill_halo_buffer(buf, prev, body, nxt, first, last, tm):
    prev = jnp.where(first, 0.0, prev)
    nxt = jnp.where(last, 0.0, nxt)
    for kb in range(buf.shape[0]):
        lanes = slice(kb * LANES, (kb + 1) * LANES)
        buf[kb, 0:HALO, :] = prev[:, lanes]
        buf[kb, HALO:HALO + tm, :] = body[:, lanes]
        buf[kb, HALO + tm:HALO + tm + HALO, :] = nxt[:, lanes]


def _fill_glu_buffer(cbuf, av, ag, avp, agp, avn, agn, first, last, tm):
    c0p = avp[...].astype(F32) * _sig(agp[...].astype(F32))
    c0n = avn[...].astype(F32) * _sig(agn[...].astype(F32))
    c0 = av[...].astype(F32) * _sig(ag[...].astype(F32))
    _fill_halo_buffer(cbuf, c0p, c0, c0n, first, last, tm)


def _layernorm_stats(v):
    mu = jnp.mean(v, axis=-1, keepdims=True)
    cen = v - mu
    rstd = lax.rsqrt(jnp.mean(cen * cen, axis=-1, keepdims=True) + EPS)
    return cen * rstd, rstd


def _spatial_mix(ws_ref, vn_ref, bias_ref, mixed_ref, tm):
    for ci in range(tm // CHUNK):
        rs = slice(ci * CHUNK, (ci + 1) * CHUNK)
        for g in range(GROUPS):
            ls = slice(g * LANES, (g + 1) * LANES)
            mixed_ref[rs, ls] = _nn(ws_ref[g], vn_ref[rs, ls]) + bias_ref[:, ls]


def _mixer_fwd(proj, x, w, small, seq, tm, name, comm):
    t, d = x.shape
    hp_v, hn_v = _halo_specs(tm, t, d, 0)
    hp_g, hn_g = _halo_specs(tm, t, d, 1)

    def body(av, ag, u_ref, v_ref, ga_ref, gb_ref, avp, agp, avn, agn, x_ref,
             cw_ref, cb_ref, lg_ref, lb_ref, wco_ref, sg_ref, sb_ref, ws_ref, bias_ref, wso_ref,
             gba_ref, gbb_ref, wo_ref,
             c1_ref, ya_ref, yb_ref, x1_ref, cbuf, c1f, vn_ref, mixed_ref):
        i = pl.program_id(0)
        first = (i * tm) % seq == 0
        last = ((i + 1) * tm) % seq == 0
        _fill_glu_buffer(cbuf, av, ag, avp, agp, avn, agn, first, last, tm)
        _dwconv(cbuf, cw_ref, c1f, tm, flip=False, rc=CONV_ROWS_LOOP, unroll=False)
        c1 = c1f[...] + cb_ref[...]
        c1_ref[...] = c1.astype(BF16)
        c2hat, _ = _layernorm_stats(c1)
        c2 = c2hat * lg_ref[...] + lb_ref[...]
        c3 = (c2 * _sig(c2)).astype(BF16)
        ya = _nn(c3, wco_ref[...].reshape(d, d))
        ya_ref[...] = ya.astype(BF16)
        vhat, _ = _layernorm_stats(v_ref[...].astype(F32))
        vn_ref[...] = (vhat * sg_ref[...] + sb_ref[...]).astype(BF16)
        _spatial_mix(ws_ref, vn_ref, bias_ref, mixed_ref, tm)
        gated = (u_ref[...].astype(F32) * mixed_ref[...]).astype(BF16)
        yb = _nn(gated, wso_ref[...].reshape(d, d))
        yb_ref[...] = yb.astype(BF16)
        sa = _sig(ga_ref[...].astype(F32) + gba_ref[...])
        sb = _sig(gb_ref[...].astype(F32) + gbb_ref[...])
        merged = (sa * ya + sb * yb).astype(BF16)
        x1_ref[...] = x_ref[...] + _nn(merged, wo_ref[...].reshape(d, d))

    cols = [_tile(tm, d, j) for j in range(6)]
    return _call(
        body, name=name, grid=(t // tm,),
        args=(proj,) * 10 + (x, w['conv_w'], small['conv_b'], small['conv_ln_g'], small['conv_ln_b'], w['w_conv_out'],
                             small['sgu_ln_g'], small['sgu_ln_b'], small['ws'], small['bias_full'], w['w_sgu_out'],
                             small['gba'], small['gbb'], w['w_o']),
        in_specs=cols + [hp_v, hp_g, hn_v, hn_g, _tile(tm, d),
                         _weight(w['conv_w']), _row(d), _row(d), _row(d),
                         _weight(w['w_conv_out']), _row(d), _row(d),
                         _resident(small['ws'].shape), _resident(small['bias_full'].shape),
                         _weight(w['w_sgu_out']), _row(d), _row(d), _weight(w['w_o'])],
        out_specs=[_tile(tm, d)] * 4,
        out_shape=[_sds((t, d), BF16)] * 3 + [_sds((t, d), F32)],
        scratch_shapes=[pltpu.VMEM((d // LANES, tm + 2 * HALO, LANES), F32), pltpu.VMEM((tm, d), F32),
                        pltpu.VMEM((tm, d), BF16), pltpu.VMEM((tm, d), F32)],
        semantics=("parallel",), comm=comm)


def _pair_gate_up(wg, wu, name):
    n, c, d = wg.shape

    def body(wg_ref, wu_ref, o_ref):
        o_ref[0:c, :] = wg_ref[0]
        o_ref[c:2 * c, :] = wg_ref[1]
        o_ref[2 * c:3 * c, :] = wu_ref[0]
        o_ref[3 * c:4 * c, :] = wu_ref[1]

    pair = pl.BlockSpec((2, c, d), lambda k: (k, 0, 0))
    return _call(body, name=name, grid=(n // 2,), args=(wg, wu), in_specs=[pair, pair],
                 out_specs=[pl.BlockSpec((None, 4 * c, d), lambda k: (k, 0, 0))],
                 out_shape=[_sds((n // 2, 4 * c, d), wg.dtype)], semantics=("parallel",))[0][0]


def _pair_specs(w, tm):
    d = w['w_gate_up'].shape[2]
    up = pl.BlockSpec((None, 2 * FF_PAIR, d), lambda i, k: (k, 0, 0))
    down = pl.BlockSpec((None, FF_PAIR, d), lambda i, k: (k, 0, 0))
    return [up, down]


def _ffn_fwd(x1, gain, w, tm, name, comm):
    t, d = x1.shape
    tm = min(tm, t)
    npair = NDEV // 2
    hid = NDEV * FF_PAD
    wd_pairs = w['w_ffn_down'].reshape(npair, FF_PAIR, d)

    def body(x_ref, g_ref, wgu_ref, wd_ref, h_ref, gg_ref, uu_ref, x2_ref, hb_ref, acc_ref):
        k = pl.program_id(1)

        @pl.when(k == 0)
        def _():
            xv = x_ref[...]
            r = lax.rsqrt(jnp.mean(xv * xv, axis=-1, keepdims=True) + EPS)
            h = (xv * r * g_ref[...]).astype(BF16)
            hb_ref[...] = h
            h_ref[...] = h
            acc_ref[...] = xv

        gu = _nt(hb_ref[...], wgu_ref[...])
        gk = gu[:, 0:FF_PAIR]
        uk = gu[:, FF_PAIR:2 * FF_PAIR]
        gg_ref[...] = gk.astype(BF16)
        uu_ref[...] = uk.astype(BF16)
        ak = (gk * _sig(gk) * uk).astype(BF16)
        acc_ref[...] += _nn(ak, wd_ref[...])

        @pl.when(k == npair - 1)
        def _():
            x2_ref[...] = acc_ref[...]

    pair_cols = pl.BlockSpec((tm, FF_PAIR), lambda i, k: (i, k))
    return _call(
        body, name=name, grid=(t // tm, npair),
        args=(x1, gain, w['w_gate_up'], wd_pairs),
        in_specs=[_tile(tm, d), _row(d)] + _pair_specs(w, tm),
        out_specs=[_tile(tm, d), pair_cols, pair_cols, _tile(tm, d)],
        out_shape=[_sds((t, d), BF16), _sds((t, hid), BF16), _sds((t, hid), BF16), _sds((t, d), F32)],
        scratch_shapes=[pltpu.VMEM((tm, d), BF16), pltpu.VMEM((tm, d), F32)],
        semantics=("parallel", "arbitrary"), comm=comm)


def _init_small(first, small_ref, acc_ref):
    @pl.when(first)
    def _():
        small_ref[...] = jnp.zeros(small_ref.shape, F32)
        acc_ref[...] = jnp.zeros(acc_ref.shape, F32)


def _finish_small(last, small_ref, acc_ref, nq):
    @pl.when(last)
    def _():
        for q in range(nq):
            small_ref[q:q + 1, :] = jnp.sum(acc_ref[q], axis=0, keepdims=True)


def _loss_bwd(x, gain, target, tm, name):
    t, d = x.shape
    tm = min(tm, t)
    nsteps = t // tm

    def body(x_ref, g_ref, t_ref, dx_ref, small_ref, acc_ref):
        i = pl.program_id(0)
        _init_small(i == 0, small_ref, acc_ref)
        xv = x_ref[...]
        r = lax.rsqrt(jnp.mean(xv * xv, axis=-1, keepdims=True) + EPS)
        xhat = xv * r
        diff = xhat * g_ref[...] - t_ref[...]
        dy = diff * (1.0 / d)
        acc_ref[0] += _fold(dy * xhat)
        acc_ref[1] += _fold(diff * diff)
        dxhat = dy * g_ref[...]
        dx_ref[...] = r * (dxhat - xhat * jnp.mean(dxhat * xhat, axis=-1, keepdims=True))

        @pl.when(i == nsteps - 1)
        def _():
            small_ref[0:1, :] = jnp.sum(acc_ref[0], axis=0, keepdims=True)
            small_ref[1:2, :] = jnp.full((1, d), jnp.sum(acc_ref[1]) * (0.5 / d), F32)

    return _call(body, name=name, grid=(nsteps,), args=(x, gain, target),
                 in_specs=[_tile(tm, d), _row(d), _tile(tm, d)],
                 out_specs=[_tile(tm, d), _resident((SUBLANES, d))],
                 out_shape=[_sds((t, d), F32), _sds((SUBLANES, d), F32)],
                 scratch_shapes=[pltpu.VMEM((2, SUBLANES, d), F32)], semantics=("arbitrary",))[0]


def _ffn_bwd(dx2, x1, gain, gg, uu, w, tm, name, comm):
    t, d = x1.shape
    tm = min(tm, t)
    npair = NDEV // 2
    hid = NDEV * FF_PAD
    nsteps = t // tm
    wd_pairs = w['w_ffn_down'].reshape(npair, FF_PAIR, d)

    total = nsteps * npair

    def body(dx_ref, x_ref, g_ref, gg_ref, uu_ref, wgu_hbm, wd_hbm,
             a_ref, dg_ref, du_ref, dx1_ref, small_ref, acc_ref, dxb_ref, dh_ref, dgu_ref, wgu_ring, wd_ring, wsem):
        i, k = pl.program_id(0), pl.program_id(1)
        _init_small((i == 0) & (k == 0), small_ref, acc_ref)
        step = i * npair + k

        def fetch(s):
            slot = s % FFN_WEIGHT_SLOTS
            pair = s % npair
            return (pltpu.make_async_copy(wgu_hbm.at[pair], wgu_ring.at[slot], wsem.at[0, slot]),
                    pltpu.make_async_copy(wd_hbm.at[pair], wd_ring.at[slot], wsem.at[1, slot]))

        @pl.when(step == 0)
        def _():
            for s in range(FFN_WEIGHT_SLOTS - 1):
                for cp in fetch(s):
                    cp.start()

        @pl.when(step + FFN_WEIGHT_SLOTS - 1 < total)
        def _():
            for cp in fetch(step + FFN_WEIGHT_SLOTS - 1):
                cp.start()

        for cp in fetch(step):
            cp.wait()
        wgu_ref = wgu_ring.at[step % FFN_WEIGHT_SLOTS]
        wd_ref = wd_ring.at[step % FFN_WEIGHT_SLOTS]

        @pl.when(k == 0)
        def _():
            dxb_ref[...] = dx_ref[...].astype(BF16)
            dh_ref[...] = jnp.zeros(dh_ref.shape, F32)

        gk = gg_ref[...].astype(F32)
        uk = uu_ref[...].astype(F32)
        sg = _sig(gk)
        silu = gk * sg
        a_ref[...] = (silu * uk).astype(BF16)
        da = _nt(dxb_ref[...], wd_ref[...])
        dgk = (da * uk * (sg * (1.0 + gk * (1.0 - sg)))).astype(BF16)
        duk = (da * silu).astype(BF16)
        dg_ref[...] = dgk
        du_ref[...] = duk
        dgu_ref[:, 0:FF_PAIR] = dgk
        dgu_ref[:, FF_PAIR:2 * FF_PAIR] = duk
        dh_ref[...] += _nn(dgu_ref[...], wgu_ref[...])

        @pl.when(k == npair - 1)
        def _():
            xv = x_ref[...]
            dh = dh_ref[...]
            r = lax.rsqrt(jnp.mean(xv * xv, axis=-1, keepdims=True) + EPS)
            xhat = xv * r
            acc_ref[0] += _fold(dh * xhat)
            dxhat = dh * g_ref[...]
            dx1_ref[...] = dx_ref[...] + r * (dxhat - xhat * jnp.mean(dxhat * xhat, axis=-1, keepdims=True))

        _finish_small((i == nsteps - 1) & (k == npair - 1), small_ref, acc_ref, 1)

    pair_cols = pl.BlockSpec((tm, FF_PAIR), lambda i, k: (i, k))
    return _call(
        body, name=name, grid=(nsteps, npair),
        args=(dx2, x1, gain, gg, uu, w['w_gate_up'], wd_pairs),
        in_specs=[_tile(tm, d), _tile(tm, d), _row(d), pair_cols, pair_cols] + [pl.BlockSpec(memory_space=pl.ANY)] * 2,
        out_specs=[pair_cols] * 3 + [_tile(tm, d), _resident((SUBLANES, d))],
        out_shape=[_sds((t, hid), BF16)] * 3 + [_sds((t, d), F32), _sds((SUBLANES, d), F32)],
        scratch_shapes=[pltpu.VMEM((1, SUBLANES, d), F32), pltpu.VMEM((tm, d), BF16), pltpu.VMEM((tm, d), F32),
                        pltpu.VMEM((tm, 2 * FF_PAIR), BF16),
                        pltpu.VMEM((FFN_WEIGHT_SLOTS, 2 * FF_PAIR, d), BF16), pltpu.VMEM((FFN_WEIGHT_SLOTS, FF_PAIR, d), BF16),
                        pltpu.SemaphoreType.DMA((2, FFN_WEIGHT_SLOTS))],
        semantics=("arbitrary", "arbitrary"), comm=comm)


def _matmul_tn(a, b, a_blk, b_blk, stack, shard, tm, name, comm):
    t, ma = a.shape
    tm = min(tm, t)
    nb_ = b.shape[1]
    na, nb = ma // a_blk, nb_ // b_blk
    nsteps = t // tm
    cw = min(TN_COLS, b_blk)
    if stack == 'b':
        per = b_blk // shard
        out_shape, out_spec = (nb_ // shard, ma, shard), pl.BlockSpec((per, a_blk, shard), lambda i, j, k: (j, 0, 0))
    elif stack == 'a':
        per = a_blk // shard
        out_shape, out_spec = (ma // shard, shard, nb_), pl.BlockSpec((per, shard, b_blk), lambda i, j, k: (i, 0, 0))
    else:
        out_shape, out_spec = (ma, nb_), pl.BlockSpec((a_blk, b_blk), lambda i, j, k: (i, j))

    def body(a_ref, b_ref, o_ref, acc_ref):
        k = pl.program_id(2)

        @pl.when(k == 0)
        def _():
            acc_ref[...] = jnp.zeros(acc_ref.shape, F32)

        av = a_ref[...].astype(BF16)
        for c in range(0, b_blk, cw):
            acc_ref[:, c:c + cw] += _tn(av, b_ref[:, c:c + cw].astype(BF16))

        @pl.when(k == nsteps - 1)
        def _():
            if stack == 'b':
                for s in range(per):
                    o_ref[s] = acc_ref[:, s * shard:(s + 1) * shard].astype(BF16)
            elif stack == 'a':
                for s in range(per):
                    o_ref[s] = acc_ref[s * shard:(s + 1) * shard, :].astype(BF16)
            else:
                o_ref[...] = acc_ref[...].astype(BF16)

    res, got = _call(
        body, name=name, grid=(na, nb, nsteps), args=(a, b),
        in_specs=[pl.BlockSpec((tm, a_blk), lambda i, j, k: (k, i)), pl.BlockSpec((tm, b_blk), lambda i, j, k: (k, j))],
        out_specs=[out_spec], out_shape=[_sds(out_shape, BF16)],
        scratch_shapes=[pltpu.VMEM((a_blk, b_blk), F32)],
        semantics=("parallel", "parallel", "arbitrary"), comm=comm)
    return res[0], got


def _mixer_bwd(dx1, proj, c1, ya, yb, w, small, tm, name, comm):
    t, d = dx1.shape
    nsteps = t // tm
    nq = 6

    def body(dx_ref, u_ref, v_ref, ga_ref, gb_ref, c1_ref, ya_ref, yb_ref,
             lg_ref, lb_ref, wco_ref, sg_ref, sb_ref, ws_ref, wst_ref, bias_ref, wso_ref, gba_ref, gbb_ref, wo_ref,
             sel_ref,
             dp_ref, dc1_ref, mg_ref, c3_ref, gt_ref, dya_ref, dyb_ref, small_ref, dws_ref, dbs_ref,
             acc_ref, vn_ref, mixed_ref, dmix_ref, dvn_ref, dbias_ref):
        i = pl.program_id(0)
        _init_small(i == 0, small_ref, acc_ref)

        @pl.when(i == 0)
        def _():
            dws_ref[...] = jnp.zeros(dws_ref.shape, F32)
            dbs_ref[...] = jnp.zeros(dbs_ref.shape, F32)
            dbias_ref[...] = jnp.zeros(dbias_ref.shape, F32)

        dmerged = _nt(dx_ref[...].astype(BF16), wo_ref[...].reshape(d, d))
        ya = ya_ref[...].astype(F32)
        yb = yb_ref[...].astype(F32)
        sa = _sig(ga_ref[...].astype(F32) + gba_ref[...])
        sb = _sig(gb_ref[...].astype(F32) + gbb_ref[...])
        mg_ref[...] = (sa * ya + sb * yb).astype(BF16)
        dya = (dmerged * sa).astype(BF16)
        dyb = (dmerged * sb).astype(BF16)
        dya_ref[...] = dya
        dyb_ref[...] = dyb
        dga = dmerged * ya * (sa * (1.0 - sa))
        dgb = dmerged * yb * (sb * (1.0 - sb))
        acc_ref[0] += _fold(dga)
        acc_ref[1] += _fold(dgb)
        dp_ref[:, 0:2 * d] = jnp.zeros((tm, 2 * d), BF16)
        dp_ref[:, 4 * d:5 * d] = dga.astype(BF16)
        dp_ref[:, 5 * d:6 * d] = dgb.astype(BF16)
        c2hat, rstd = _layernorm_stats(c1_ref[...].astype(F32))
        c2 = c2hat * lg_ref[...] + lb_ref[...]
        s2 = _sig(c2)
        c3_ref[...] = (c2 * s2).astype(BF16)
        dc3 = _nt(dya, wco_ref[...].reshape(d, d))
        dc2 = dc3 * (s2 * (1.0 + c2 * (1.0 - s2)))
        acc_ref[2] += _fold(dc2 * c2hat)
        acc_ref[3] += _fold(dc2)
        dc2hat = dc2 * lg_ref[...]
        dc1_ref[...] = (rstd * (dc2hat - jnp.mean(dc2hat, axis=-1, keepdims=True)
                                - c2hat * jnp.mean(dc2hat * c2hat, axis=-1, keepdims=True))).astype(BF16)
        vhat, rstd_v = _layernorm_stats(v_ref[...].astype(F32))
        vn_ref[...] = (vhat * sg_ref[...] + sb_ref[...]).astype(BF16)
        _spatial_mix(ws_ref, vn_ref, bias_ref, mixed_ref, tm)
        u = u_ref[...].astype(F32)
        mixed = mixed_ref[...]
        gt_ref[...] = (u * mixed).astype(BF16)
        dgated = _nt(dyb, wso_ref[...].reshape(d, d))
        dp_ref[:, 2 * d:3 * d] = (dgated * mixed).astype(BF16)
        dmix_ref[...] = dgated * u
        for ci in range(tm // CHUNK):
            rs = slice(ci * CHUNK, (ci + 1) * CHUNK)
            dbias_ref[...] += dmix_ref[rs, :]
            for g in range(GROUPS):
                ls = slice(g * LANES, (g + 1) * LANES)
                dm = dmix_ref[rs, ls].astype(BF16)
                dws_ref[g] += _nt(dm, vn_ref[rs, ls])
                dvn_ref[rs, ls] = _nn(wst_ref[g], dm)
        dvn = dvn_ref[...]
        acc_ref[4] += _fold(dvn * vhat)
        acc_ref[5] += _fold(dvn)
        dvhat = dvn * sg_ref[...]
        dp_ref[:, 3 * d:4 * d] = (rstd_v * (dvhat - jnp.mean(dvhat, axis=-1, keepdims=True)
                                           - vhat * jnp.mean(dvhat * vhat, axis=-1, keepdims=True))).astype(BF16)
        _finish_small(i == nsteps - 1, small_ref, acc_ref, nq)

        @pl.when(i == nsteps - 1)
        def _():
            db = dbias_ref[...]
            hi = db.astype(BF16)
            lo = (db - hi.astype(F32)).astype(BF16)
            dbs_ref[...] = _nt(sel_ref[...], hi) + _nt(sel_ref[...], lo)

    cols = [_tile(tm, d, j) for j in (2, 3, 4, 5)]
    return _call(
        body, name=name, grid=(nsteps,),
        args=(dx1, proj, proj, proj, proj, c1, ya, yb,
              small['conv_ln_g'], small['conv_ln_b'], w['w_conv_out'], small['sgu_ln_g'], small['sgu_ln_b'],
              small['ws'], small['wst'], small['bias_full'], w['w_sgu_out'], small['gba'], small['gbb'], w['w_o'],
              small['group_sel']),
        in_specs=[_tile(tm, d)] + cols + [_tile(tm, d)] * 3 + [
            _row(d), _row(d), _weight(w['w_conv_out']),
            _row(d), _row(d), _resident(small['ws'].shape), _resident(small['wst'].shape),
            _resident(small['bias_full'].shape), _weight(w['w_sgu_out']), _row(d), _row(d),
            _weight(w['w_o']), _resident(small['group_sel'].shape)],
        out_specs=[_tile(tm, 6 * d)] + [_tile(tm, d)] * 6 + [
            _resident((SUBLANES, d)), _resident((GROUPS, CHUNK, CHUNK)), _resident((GROUPS, CHUNK))],
        out_shape=[_sds((t, 6 * d), BF16)] + [_sds((t, d), BF16)] * 6 + [
            _sds((SUBLANES, d), F32), _sds((GROUPS, CHUNK, CHUNK), F32), _sds((GROUPS, CHUNK), F32)],
        scratch_shapes=[pltpu.VMEM((nq, SUBLANES, d), F32), pltpu.VMEM((tm, d), BF16), pltpu.VMEM((tm, d), F32),
                        pltpu.VMEM((tm, d), F32), pltpu.VMEM((tm, d), F32), pltpu.VMEM((CHUNK, d), F32)],
        semantics=("arbitrary",), comm=comm)


def _conv_bwd(dproj, dc1, proj, w, pairs, seq, tm, name, comm):
    t, d = dc1.shape
    nsteps = t // tm
    nblk = d // LANES
    npairs = len(pairs)
    hp_v, hn_v = _halo_specs(tm, t, d, 0)
    hp_g, hn_g = _halo_specs(tm, t, d, 1)
    hp_d, hn_d = _halo_specs(tm, t, d, 0)

    def body(*refs):
        (dp_in, dc_ref, dcp, dcn, av, ag, avp, agp, avn, agn, cw_ref), refs = refs[:11], refs[11:]
        ab_refs, refs = refs[:2 * npairs], refs[2 * npairs:]
        (dp_ref, dcw_ref, small_ref), refs = refs[:3], refs[3:]
        grad_refs, refs = refs[:npairs], refs[npairs:]
        acc_ref, cbuf, dbuf, dc0f, accw, gacc = refs
        del dp_in
        i = pl.program_id(0)
        _init_small(i == 0, small_ref, acc_ref)

        @pl.when(i == 0)
        def _():
            accw[...] = jnp.zeros(accw.shape, F32)
            dcw_ref[...] = jnp.zeros(dcw_ref.shape, F32)
            gacc[...] = jnp.zeros(gacc.shape, F32)

        first = (i * tm) % seq == 0
        last = ((i + 1) * tm) % seq == 0
        _fill_glu_buffer(cbuf, av, ag, avp, agp, avn, agn, first, last, tm)
        dc1v = dc_ref[...].astype(F32)
        _fill_halo_buffer(dbuf, dcp[...].astype(F32), dc1v, dcn[...].astype(F32), first, last, tm)
        acc_ref[0] += _fold(dc1v)
        _dwconv(dbuf, cw_ref, dc0f, tm, flip=True, rc=CONV_ROWS, unroll=True)

        def rows(r0):
            for kb in range(nblk):
                dv = dbuf[kb, r0 + HALO:r0 + HALO + CONV_ROWS, :]
                for k in range(CONV_TAPS):
                    accw[kb, k] += _fold(dv * _shifted(cbuf, kb, r0, 1 + k, CONV_ROWS))

        _row_chunks(tm, CONV_ROWS, True, rows)
        for q in range(npairs):
            aq = ab_refs[2 * q][...].astype(BF16)
            for c in range(0, d, TN_COLS):
                gacc[q, :, c:c + TN_COLS] += _tn(aq, ab_refs[2 * q + 1][:, c:c + TN_COLS].astype(BF16))
        sg = _sig(ag[...].astype(F32))
        avv = av[...].astype(F32)
        dc0 = dc0f[...]
        dp_ref[:, 0:d] = (dc0 * sg).astype(BF16)
        dp_ref[:, d:2 * d] = (dc0 * avv * (sg * (1.0 - sg))).astype(BF16)
        _finish_small(i == nsteps - 1, small_ref, acc_ref, 1)

        @pl.when(i == nsteps - 1)
        def _():
            for kb in range(nblk):
                dcw_ref[kb] = jnp.sum(accw[kb], axis=1)
            for q in range(npairs):
                grad_refs[q][...] = gacc[q].astype(BF16)

    return _call(
        body, name=name, grid=(nsteps,),
        args=(dproj, dc1, dc1, dc1, proj, proj, proj, proj, proj, proj, w['conv_w']) + tuple(a for ab in pairs for a in ab),
        in_specs=[pl.BlockSpec(memory_space=pl.ANY), _tile(tm, d), hp_d, hn_d, _tile(tm, d, 0), _tile(tm, d, 1),
                  hp_v, hp_g, hn_v, hn_g, _weight(w['conv_w'])] + [_tile(tm, d)] * (2 * npairs),
        out_specs=[_tile(tm, 2 * d), _resident((nblk, CONV_TAPS_PADDED, LANES)), _resident((SUBLANES, d))]
        + [_resident((d, d))] * npairs,
        out_shape=[_sds(dproj.shape, BF16), _sds((nblk, CONV_TAPS_PADDED, LANES), F32), _sds((SUBLANES, d), F32)]
        + [_sds((d, d), BF16)] * npairs,
        scratch_shapes=[pltpu.VMEM((1, SUBLANES, d), F32), pltpu.VMEM((nblk, tm + 2 * HALO, LANES), F32),
                        pltpu.VMEM((nblk, tm + 2 * HALO, LANES), F32), pltpu.VMEM((tm, d), F32),
                        pltpu.VMEM((nblk, CONV_TAPS_PADDED, SUBLANES, LANES), F32),
                        pltpu.VMEM((npairs, d, d), F32)],
        aliases={0: 0}, semantics=("arbitrary",), comm=comm)


def _in_proj_bwd(dproj, x, dx1, gain, w, tm, name, comm):
    t, d = x.shape
    nb, _, bw = w.shape
    nsteps = t // tm

    def body(dp_ref, x_ref, dx1_ref, g_ref, w_ref, dx_ref, small_ref, acc_ref):
        i = pl.program_id(0)
        _init_small(i == 0, small_ref, acc_ref)
        dh = jnp.zeros((tm, d), F32)
        for k in range(nb):
            dh = dh + _nt(dp_ref[:, k * bw:(k + 1) * bw], w_ref[k])
        xv = x_ref[...]
        r = lax.rsqrt(jnp.mean(xv * xv, axis=-1, keepdims=True) + EPS)
        xhat = xv * r
        acc_ref[0] += _fold(dh * xhat)
        dxhat = dh * g_ref[...]
        dx_ref[...] = dx1_ref[...] + r * (dxhat - xhat * jnp.mean(dxhat * xhat, axis=-1, keepdims=True))
        _finish_small(i == nsteps - 1, small_ref, acc_ref, 1)

    return _call(body, name=name, grid=(nsteps,), args=(dproj, x, dx1, gain, w),
                 in_specs=[_tile(tm, nb * bw), _tile(tm, d), _tile(tm, d), _row(d), _weight(w)],
                 out_specs=[_tile(tm, d), _resident((SUBLANES, d))],
                 out_shape=[_sds((t, d), F32), _sds((SUBLANES, d), F32)],
                 scratch_shapes=[pltpu.VMEM((1, SUBLANES, d), F32)], semantics=("arbitrary",), comm=comm)


def _adam(wv, g, mv, vv):
    m = ADAM_B1 * mv + (1.0 - ADAM_B1) * g
    v = ADAM_B2 * vv + (1.0 - ADAM_B2) * jnp.square(g)
    m_hat = m / (1.0 - ADAM_B1 ** ADAM_STEP)
    v_hat = v / (1.0 - ADAM_B2 ** ADAM_STEP)
    delta = -ADAM_LR * (m_hat / (jnp.sqrt(v_hat) + ADAM_EPS) + ADAM_WD * wv)
    return delta, m, v


def _adamw_layer(layer, w, m, v, parts, prev, nsplit, name):
    nl, r, c = w.shape
    npart, pr, pc = parts.shape
    rt, prt = r // nsplit, pr // nsplit

    def body(w_ref, m_ref, v_ref, p_ref, *rest):
        g_ref, d_ref, nm_ref, nv_ref = rest[-4:]
        g = p_ref[0, 0:rt, 0:c].astype(F32)
        for s in range(1, npart):
            g = g + p_ref[s, 0:rt, 0:c].astype(F32)
        delta, mn, vn = _adam(w_ref[0], g, m_ref[0], v_ref[0])
        g_ref[0] = g
        d_ref[0] = delta
        nm_ref[0] = mn
        nv_ref[0] = vn

    wspec = pl.BlockSpec((1, rt, c), lambda i: (layer, i, 0))
    pspec = pl.BlockSpec((npart, prt, pc), lambda i: (0, i, 0))
    in_specs = [wspec, wspec, wspec, pspec]
    args = [w, m, v, parts]
    aliases = {}
    if prev is not None:
        in_specs += [pl.BlockSpec(memory_space=pl.ANY)] * 4
        args += list(prev)
        aliases = {4 + q: q for q in range(4)}
    return _call(body, name=name, grid=(nsplit,), args=args, in_specs=in_specs, out_specs=[wspec] * 4,
                 out_shape=[_sds(w.shape, F32)] * 4, aliases=aliases, semantics=("parallel",))[0]


VEC_ROWS = {'gate_bias': (0, 1), 'conv_ln_g': 2, 'conv_ln_b': 3, 'sgu_ln_g': 4, 'sgu_ln_b': 5, 'norm_ffn': SUBLANES}
FINAL_ROW = 2 * SUBLANES
LOSS_ROW = 2 * SUBLANES + 1
LATE_ROWS = {'norm_mix': 0, 'conv_b': SUBLANES}


def _adamw_small(gathered, params, moments_m, moments_v):
    names = list(params)
    nper = len(names)
    nl = len(gathered)

    def body(*refs):
        g_refs = [refs[4 * l:4 * l + 4] for l in range(nl)]
        rest = refs[4 * nl:]
        w_refs = dict(zip(names, rest[0:nper]))
        m_refs = dict(zip(names, rest[nper:2 * nper]))
        v_refs = dict(zip(names, rest[2 * nper:3 * nper]))
        outs = rest[3 * nper:]
        loss_ref = outs[4 * nper]
        o = {kind: dict(zip(names, outs[q * nper:(q + 1) * nper])) for q, kind in enumerate("gdmv")}

        def put(nm, idx, g):
            delta, mn, vn = _adam(w_refs[nm][idx], g, m_refs[nm][idx], v_refs[nm][idx])
            o["g"][nm][idx] = g
            o["d"][nm][idx] = delta
            o["m"][nm][idx] = mn
            o["v"][nm][idx] = vn

        def total(ref, *idx):
            g = ref[(0, *idx)]
            for s in range(1, NDEV):
                g = g + ref[(s, *idx)]
            return g

        for l, (vec_ref, dws_ref, dbs_ref, late_ref) in enumerate(g_refs):
            dd = vec_ref.shape[2]
            put('w_spatial', (l,), total(dws_ref))
            put('b_spatial', (l,), total(dbs_ref))
            for nm, rr in LATE_ROWS.items():
                put(nm, (slice(l, l + 1), slice(None)), total(late_ref, slice(rr, rr + 1)))
            for nm, rr in VEC_ROWS.items():
                if nm == 'gate_bias':
                    put(nm, (slice(l, l + 1), slice(0, dd)), total(vec_ref, slice(rr[0], rr[0] + 1)))
                    put(nm, (slice(l, l + 1), slice(dd, 2 * dd)), total(vec_ref, slice(rr[1], rr[1] + 1)))
                else:
                    put(nm, (slice(l, l + 1), slice(None)), total(vec_ref, slice(rr, rr + 1)))
        last = g_refs[nl - 1][0]
        put('norm_final', (slice(0, 1), slice(None)), total(last, slice(FINAL_ROW, FINAL_ROW + 1)))
        loss_ref[...] = total(last, slice(LOSS_ROW, LOSS_ROW + 1))

    ins = [a for g in gathered for a in g] + [params[n] for n in names] + [moments_m[n] for n in names] + [moments_v[n] for n in names]
    out_shape = [_sds(params[n].shape, F32) for n in names] * 4 + [_sds((1, gathered[0][0].shape[2]), F32)]
    res = pl.pallas_call(body, name="adamw_small", out_shape=out_shape, compiler_params=_params())(*ins)
    return {kind: dict(zip(names, res[q * nper:(q + 1) * nper])) for q, kind in enumerate("gdmv")}, res[4 * nper]


def _hidden_major(a):
    return jnp.swapaxes(a, 1, 2)


def _prepare_weights(p):
    d = p['w_in'].shape[1]
    (w_in,), _ = _cast_pad([p['w_in']], [(d, p['w_in'].shape[2])], "cast_w_in", None)
    rows = p['w_o'].shape[1]
    others = {'w_conv_out': (p['w_conv_out'], (rows, d)), 'w_sgu_out': (p['w_sgu_out'], (rows, d)),
              'w_o': (p['w_o'], (rows, d)),
              'w_ffn_gate': (_hidden_major(p['w_ffn_gate']), (FF_PAD, d)),
              'w_ffn_up': (_hidden_major(p['w_ffn_up']), (FF_PAD, d)),
              'w_ffn_down': (p['w_ffn_down'], (FF_PAD, d))}
    cast, got = _cast_pad([a for a, _ in others.values()], [s for _, s in others.values()], "cast_weights",
                          _Gather([w_in], [0]))
    shards = dict(zip(others, cast), w_in=w_in)
    shards['conv_w'] = jnp.pad(p['conv_w'][:, :, 0, :], ((0, 0), (0, CONV_TAPS_PADDED - CONV_TAPS), (0, 0)))
    return shards, got[0]


def _gather_of(shards, names, layer):
    return _Gather([shards[n] for n in names], [layer] * len(names))


def _layer_small(p, layer):
    d = p['norm_mix'].shape[1]
    ws = p['w_spatial'][layer]
    rows = {n: p[n][layer:layer + 1] for n in ('norm_mix', 'norm_ffn', 'conv_b', 'conv_ln_g', 'conv_ln_b',
                                               'sgu_ln_g', 'sgu_ln_b')}
    return {
        **rows,
        'ws': ws.astype(BF16), 'wst': jnp.swapaxes(ws, 1, 2).astype(BF16),
        'bias_full': jnp.repeat(p['b_spatial'][layer].T, LANES, axis=1),
        'gba': p['gate_bias'][layer:layer + 1, 0:d], 'gbb': p['gate_bias'][layer:layer + 1, d:2 * d],
        'group_sel': (jnp.arange(d)[None, :] // LANES == jnp.arange(GROUPS)[:, None]).astype(BF16),
    }


class _GradQueue:
    def __init__(self):
        self.pending = []
        self.done = {}

    def push(self, key, array):
        self.pending.append((key, array))

    def take(self):
        keys = [k for k, _ in self.pending]
        comm = _Scatter([a for _, a in self.pending]) if self.pending else None
        self.pending = []
        return keys, comm

    def put(self, keys, arrays):
        self.done.update(zip(keys, arrays))


def _forward_backward(p, shards, w_in, x, target, seq):
    nl = p['norm_mix'].shape[0]
    d = x.shape[1]
    smalls = [_layer_small(p, l) for l in range(nl)]
    saved = []
    for l in range(nl):
        (h, proj), got = _in_proj(x, smalls[l]['norm_mix'], w_in, min(TILE_IN_FWD, x.shape[0]), f"in_proj_{l}",
                                  _gather_of(shards, MIXER_WEIGHTS, l))
        w = dict(zip(MIXER_WEIGHTS, got), w_in=w_in)
        (c1, ya, yb, x1), got = _mixer_fwd(proj, x, w, smalls[l], seq, min(TILE_MIX_FWD, seq), f"mixer_fwd_{l}",
                                           _gather_of(shards, FFN_WEIGHTS, l))
        w.update(zip(FFN_WEIGHTS, got))
        w['w_gate_up'] = _pair_gate_up(w['w_ffn_gate'], w['w_ffn_up'], f"pair_gate_up_{l}")
        nxt = _gather_of(shards, ['w_in'], l + 1) if l + 1 < nl else None
        (h2, gg, uu, x2), got = _ffn_fwd(x1, smalls[l]['norm_ffn'], w, TILE_FFN_FWD, f"ffn_fwd_{l}", nxt)
        saved.append(dict(x=x, h=h, proj=proj, c1=c1, ya=ya, yb=yb, x1=x1, h2=h2, gg=gg, uu=uu, w=w))
        x = x2
        if got:
            w_in = got[0]
    dx, small_loss = _loss_bwd(x, p['norm_final'][None, :], target, TILE_LOSS, "loss_bwd")
    queue = _GradQueue()
    small_gathered = [None] * nl
    small_pending = None
    rows = d // NDEV
    hid = NDEV * FF_PAD
    for l in reversed(range(nl)):
        s = saved[l]
        w = s['w']

        def hosted(fn, *args, extra=None):
            keys, comm = queue.take()
            res, got = fn(*args, _together(comm, extra))
            queue.put(keys, got[:len(keys)])
            return res, got[len(keys):]

        def tn(key, a, b, a_blk, b_blk, stack, shard, reshape=None, host=False):
            keys, comm = queue.take() if host else ([], None)
            g, got = _matmul_tn(a, b, a_blk, b_blk, stack, shard, TILE_TN, f"dw_{key}_{l}", comm)
            queue.put(keys, got)
            queue.push((l, key), g if reshape is None else g.reshape(reshape))

        (act, dgg, duu, dx1, small_ffn), _ = hosted(_ffn_bwd, dx, s['x1'], smalls[l]['norm_ffn'], s['gg'], s['uu'], w,
                                                    TILE_FFN_BWD, f"ffn_bwd_{l}")
        tn('w_ffn_gate', dgg, s['h2'], hid, d, 'a', FF_PAD)
        tn('w_ffn_up', duu, s['h2'], hid, d, 'a', FF_PAD)
        (dproj, dc1, merged, c3, gated, dya, dyb, small_mix, dws, dbs), _ = hosted(
            _mixer_bwd, dx1, s['proj'], s['c1'], s['ya'], s['yb'], w, smalls[l], TILE_MIX, f"mixer_bwd_{l}")
        tn('w_ffn_down', act, dx, hid, d, 'a', FF_PAD)
        blocks = [small_mix, small_ffn] + ([small_loss] if l == nl - 1 else [])
        small_main = [jnp.concatenate(blocks, axis=0), dws, dbs]
        (dproj, g_conv, small_conv, g_o, g_co, g_so), got = hosted(
            _conv_bwd, dproj, dc1, s['proj'], w, [(merged, dx1), (c3, dya), (gated, dyb)], seq, TILE_MIX,
            f"conv_bwd_{l}", extra=_together(small_pending[1] if small_pending else None,
                                             _Gather(small_main) if l == 0 else None))
        if small_pending:
            small_gathered[small_pending[0]], got = got[:4], got[4:]
            small_pending = None
        small_main_gathered = got
        queue.push((l, 'conv_w'), g_conv)
        queue.push((l, 'w_o'), g_o.reshape(NDEV, rows, d))
        queue.push((l, 'w_conv_out'), g_co.reshape(NDEV, rows, d))
        queue.push((l, 'w_sgu_out'), g_so.reshape(NDEV, rows, d))
        tn('w_in', s['h'], dproj, d, hid, 'b', w['w_in'].shape[2], host=True)
        if l == 0:
            (dx, small_in), _ = hosted(_in_proj_bwd, dproj, s['x'], dx1, smalls[l]['norm_mix'], w['w_in'], TILE_IN,
                                       f"in_proj_bwd_{l}")
        else:
            (dx, small_in), _ = _in_proj_bwd(dproj, s['x'], dx1, smalls[l]['norm_mix'], w['w_in'], TILE_IN,
                                             f"in_proj_bwd_{l}", None)
        small_late = jnp.concatenate([small_in, small_conv], axis=0)
        small_pending = (l, _Gather(small_main + [small_late]))
    keys, comm = queue.take()
    last = _exchange_alone(_together(comm, _Gather([small_late])), "exchange_last_grads")
    queue.put(keys, last[:len(keys)])
    small_gathered[0] = small_main_gathered + last[len(keys):]
    return small_gathered, dx, queue.done


def _train_step(p, m, v, x3, target3):
    nl = p['norm_mix'].shape[0]
    bsz, seq, d = x3.shape
    x = x3.reshape(bsz * seq, d)
    target = target3.reshape(bsz * seq, d)
    shards, w_in = _prepare_weights(p)
    small_gathered, dx, exchanged = _forward_backward(p, shards, w_in, x, target, seq)

    out = {kind: {} for kind in "gdmv"}
    splits = {'w_in': 4, 'w_conv_out': 1, 'w_sgu_out': 1, 'w_o': 1, 'w_ffn_gate': 1, 'w_ffn_up': 1, 'w_ffn_down': 1, 'conv_w': 1}
    for n in splits:
        if n == 'conv_w':
            pad = ((0, 0), (0, CONV_TAPS_PADDED - CONV_TAPS), (0, 0))
            wl, ml, vl = (jnp.pad(a[n][:, :, 0, :], pad) for a in (p, m, v))
        elif n in ('w_ffn_gate', 'w_ffn_up'):
            wl, ml, vl = (_hidden_major(a[n]) for a in (p, m, v))
        else:
            wl, ml, vl = p[n], m[n], v[n]
        prev = None
        for l in range(nl):
            prev = _adamw_layer(l, wl, ml, vl, exchanged[(l, n)], prev, splits[n], f"adamw_{n}_{l}")
        for kind, arr in zip("gdmv", prev):
            if n == 'conv_w':
                arr = arr[:, 0:CONV_TAPS, None, :]
            elif n in ('w_ffn_gate', 'w_ffn_up'):
                arr = _hidden_major(arr)
            out[kind][n] = arr
    small_names = ['norm_mix', 'gate_bias', 'conv_b', 'conv_ln_g', 'conv_ln_b', 'sgu_ln_g', 'sgu_ln_b', 'w_spatial',
                   'b_spatial', 'norm_ffn', 'norm_final']

    def two_d(a):
        return a[None, :] if a.ndim == 1 else a

    res, loss_row = _adamw_small(small_gathered, {n: two_d(p[n]) for n in small_names},
                                 {n: two_d(m[n]) for n in small_names}, {n: two_d(v[n]) for n in small_names})
    loss = loss_row[0, 0]
    for kind in "gdmv":
        for n in small_names:
            out[kind][n] = res[kind][n].reshape(p[n].shape)
    grad_x = dx.reshape(bsz, seq, d)
    return (loss, grad_x, *[out[kind][n] for kind in "gdmv" for n in WEIGHT_NAMES])


def kernel(x, norm_mix, w_in, gate_bias, conv_w, conv_b, conv_ln_g, conv_ln_b, w_conv_out, sgu_ln_g, sgu_ln_b, w_spatial, b_spatial, w_sgu_out, w_o, norm_ffn, w_ffn_gate, w_ffn_up, w_ffn_down, norm_final, loss_target, m_norm_mix, m_w_in, m_gate_bias, m_conv_w, m_conv_b, m_conv_ln_g, m_conv_ln_b, m_w_conv_out, m_sgu_ln_g, m_sgu_ln_b, m_w_spatial, m_b_spatial, m_w_sgu_out, m_w_o, m_norm_ffn, m_w_ffn_gate, m_w_ffn_up, m_w_ffn_down, m_norm_final, v_norm_mix, v_w_in, v_gate_bias, v_conv_w, v_conv_b, v_conv_ln_g, v_conv_ln_b, v_w_conv_out, v_sgu_ln_g, v_sgu_ln_b, v_w_spatial, v_b_spatial, v_w_sgu_out, v_w_o, v_norm_ffn, v_w_ffn_gate, v_w_ffn_up, v_w_ffn_down, v_norm_final):
    p = dict(zip(WEIGHT_NAMES, (norm_mix, w_in, gate_bias, conv_w, conv_b, conv_ln_g, conv_ln_b, w_conv_out, sgu_ln_g, sgu_ln_b, w_spatial, b_spatial, w_sgu_out, w_o, norm_ffn, w_ffn_gate, w_ffn_up, w_ffn_down, norm_final)))
    m = dict(zip(WEIGHT_NAMES, (m_norm_mix, m_w_in, m_gate_bias, m_conv_w, m_conv_b, m_conv_ln_g, m_conv_ln_b, m_w_conv_out, m_sgu_ln_g, m_sgu_ln_b, m_w_spatial, m_b_spatial, m_w_sgu_out, m_w_o, m_norm_ffn, m_w_ffn_gate, m_w_ffn_up, m_w_ffn_down, m_norm_final)))
    v = dict(zip(WEIGHT_NAMES, (v_norm_mix, v_w_in, v_gate_bias, v_conv_w, v_conv_b, v_conv_ln_g, v_conv_ln_b, v_w_conv_out, v_sgu_ln_g, v_sgu_ln_b, v_w_spatial, v_b_spatial, v_w_sgu_out, v_w_o, v_norm_ffn, v_w_ffn_gate, v_w_ffn_up, v_w_ffn_down, v_norm_final)))
    return _train_step(p, m, v, x, loss_target)
```
